```python
import math
import jax, jax.numpy as jnp
from jax import lax
import numpy as np

D_MODEL = 2048
BATCH = 4
SEQ = 4096
DEPTH = 4

GRID_W = 64
CTX_LEN = 256
N_MIXERS = 2
N_DN_LAYERS = (DEPTH + 1) // 2
N_FN_LAYERS = DEPTH // 2
DN_HEADS = 16
DN_HEAD_DIM = D_MODEL // DN_HEADS
DN_WIDTH = DN_HEADS * DN_HEAD_DIM
DN_IN = 4 * DN_WIDTH + 4 * DN_HEADS
CONV_W = 5
CHUNK = 64
FN_GROUPS = 4
FN_WIDTH = D_MODEL
FN_GROUP_DIM = FN_WIDTH // FN_GROUPS
N_EXPERTS = 16
N_GROUPS = 4
EXPERTS_PER_GROUP = N_EXPERTS // N_GROUPS
GROUP_SCORE_K = 2
TOP_K = 2
D_FF = 512
N_MOD = 6
EPS = 1e-6

kernel_name = "hybrid_deltanet_fnet_grouped_moe_dit"


def _rmsnorm(x, w):
    x32 = x.astype(jnp.float32)
    y = x32 * lax.rsqrt(jnp.mean(x32 * x32, axis=-1, keepdims=True) + EPS)
    return (y * w.astype(jnp.float32)).astype(x.dtype)


def _modulate(x, w, shift, scale):
    return _rmsnorm(x, w) * (1 + scale) + shift


def _l2norm(x):
    return x * lax.rsqrt(jnp.sum(x * x, axis=-1, keepdims=True) + EPS)


def _dwconv_centred(u, w):
    L = u.shape[-2]
    pad = CONV_W // 2
    up = jnp.pad(u, [(0, 0)] * (u.ndim - 2) + [(pad, pad), (0, 0)])
    return sum(up[..., j:j + L, :] * w[j] for j in range(CONV_W))


def _gated_delta(q, k, v, log_alpha, beta, s0):
    B, T, H, dk = q.shape
    dv = v.shape[-1]
    n = T // CHUNK

    def chunks(t):
        t = t.reshape((B, n, CHUNK, H) + t.shape[3:])
        return jnp.moveaxis(t, 3, 1)

    q, k, v, beta = chunks(q), chunks(k), chunks(v), chunks(beta)
    g = jnp.cumsum(chunks(log_alpha), axis=-1)
    causal = jnp.tril(jnp.ones((CHUNK, CHUNK), bool))
    strict = jnp.tril(jnp.ones((CHUNK, CHUNK), bool), -1)
    decay = jnp.exp(jnp.where(causal, g[..., :, None] - g[..., None, :], -jnp.inf))
    kb = k * beta[..., None]
    lower = jnp.where(strict, jnp.einsum("bhncd,bhnsd->bhncs", kb, k) * decay, 0.0)
    eye = jnp.eye(CHUNK, dtype=q.dtype)
    t_mat = lax.linalg.triangular_solve(eye + lower, jnp.broadcast_to(eye, lower.shape),
                                        left_side=True, lower=True)
    u = jnp.einsum("bhncs,bhnsv->bhncv", t_mat, v * beta[..., None])
    w = jnp.einsum("bhncs,bhnsk->bhnck", t_mat, kb * jnp.exp(g)[..., None])
    intra = jnp.einsum("bhncd,bhnsd->bhncs", q, k) * decay
    q_g = q * jnp.exp(g)[..., None]
    k_g = k * jnp.exp(g[..., -1:] - g)[..., None]
    g_last = jnp.exp(g[..., -1])

    def step(s, xs):
        u_c, w_c, a_c, qg_c, kg_c, gl_c = xs
        v_new = u_c - jnp.einsum("bhck,bhkv->bhcv", w_c, s)
        o_c = jnp.einsum("bhck,bhkv->bhcv", qg_c, s) + jnp.einsum("bhcs,bhsv->bhcv", a_c, v_new)
        s = s * gl_c[..., None, None] + jnp.einsum("bhck,bhcv->bhkv", kg_c, v_new)
        return s, o_c

    xs = tuple(jnp.moveaxis(t, 2, 0) for t in (u, w, intra, q_g, k_g, g_last))
    s_final, o = lax.scan(step, s0, xs)
    o = jnp.transpose(o, (1, 0, 3, 2, 4)).reshape(B, T, H, dv)
    return o, s_final


def _dn_streams(p, conv_w, A_log, dt_bias, on_grid):
    B, T, _ = p.shape
    qkv = p[..., :3 * DN_WIDTH]
    if on_grid:
        rows = T // GRID_W
        qkv = _dwconv_centred(qkv.reshape(B, rows, GRID_W, 3 * DN_WIDTH), conv_w)
        qkv = qkv.reshape(B, T, 3 * DN_WIDTH)
    else:
        qkv = _dwconv_centred(qkv, conv_w)
    qkv = jax.nn.silu(qkv).astype(jnp.float32).reshape(B, T, 3, DN_HEADS, DN_HEAD_DIM)
    q = _l2norm(qkv[:, :, 0]) * (DN_HEAD_DIM ** -0.5)
    k = _l2norm(qkv[:, :, 1])
    v = qkv[:, :, 2]
    gate = p[..., 3 * DN_WIDTH:4 * DN_WIDTH]
    a = p[..., 4 * DN_WIDTH:4 * DN_WIDTH + 2 * DN_HEADS].astype(jnp.float32).reshape(B, T, 2, DN_HEADS)
    b = p[..., 4 * DN_WIDTH + 2 * DN_HEADS:].astype(jnp.float32).reshape(B, T, 2, DN_HEADS)
    log_alpha = -jnp.exp(A_log.astype(jnp.float32)) * jax.nn.softplus(a + dt_bias.astype(jnp.float32))
    beta = jax.nn.sigmoid(b)
    return q, k, v, log_alpha, beta, gate


def _directional_scan(streams, d, s0):
    q, k, v, log_alpha, beta, _ = streams
    la, bt = log_alpha[:, :, d], beta[:, :, d]
    if d == 1:
        q, k, v, la, bt = (jnp.flip(t, axis=1) for t in (q, k, v, la, bt))
    o, s = _gated_delta(q, k, v, la, bt, s0)
    if d == 1:
        o = jnp.flip(o, axis=1)
    return o, s


def _dn_output(o, gate, norm_w, w_out):
    B, T = o.shape[:2]
    g = jax.nn.silu(gate.astype(jnp.float32)).reshape(B, T, DN_HEADS, DN_HEAD_DIM)
    y = (_rmsnorm(o, norm_w) * g).reshape(B, T, DN_WIDTH).astype(gate.dtype)
    return y @ w_out


def _deltanet_mixer(h_ctx, h_lat, w_in, conv_w, A_log, dt_bias, norm_w, w_out, ctx_out):
    B = h_lat.shape[0]
    ctx_s = _dn_streams(h_ctx @ w_in, conv_w, A_log, dt_bias, on_grid=False)
    lat_s = _dn_streams(h_lat @ w_in, conv_w, A_log, dt_bias, on_grid=True)
    o_ctx, o_lat = 0.0, 0.0
    for d in range(2):
        s0 = jnp.zeros((B, DN_HEADS, DN_HEAD_DIM, DN_HEAD_DIM), jnp.float32)
        oc, s_ctx = _directional_scan(ctx_s, d, s0)
        ol, _ = _directional_scan(lat_s, d, s_ctx)
        o_lat = o_lat + ol
        if ctx_out:
            o_ctx = o_ctx + oc
    y_lat = _dn_output(o_lat, lat_s[5], norm_w, w_out)
    y_ctx = _dn_output(o_ctx, ctx_s[5], norm_w, w_out) if ctx_out else None
    return y_ctx, y_lat


def _fourier_mixer(h, w_in, w_out):
    u = h @ w_in
    B, T, _ = u.shape
    u = u.astype(jnp.float32).reshape(B, T, FN_GROUPS, FN_GROUP_DIM)
    mixed = jnp.fft.fft2(u, axes=(1, 3), norm="ortho").real
    return mixed.reshape(B, T, FN_WIDTH).astype(h.dtype) @ w_out


def _moe(h, router_w, router_bias, w_gate, w_up, w_down):
    s = jax.nn.sigmoid(h.astype(jnp.float32) @ router_w.astype(jnp.float32))
    sel = (s + router_bias.astype(jnp.float32)).reshape(s.shape[:-1] + (N_GROUPS, EXPERTS_PER_GROUP))
    group_score = jnp.sum(lax.top_k(sel, GROUP_SCORE_K)[0], axis=-1)
    g_idx = jnp.argmax(group_score, axis=-1)
    in_group = jnp.arange(N_GROUPS) == g_idx[..., None]
    masked = jnp.where(in_group[..., None], sel, -jnp.inf).reshape(s.shape)
    _, e_idx = lax.top_k(masked, TOP_K)
    wts = jnp.take_along_axis(s, e_idx, axis=-1)
    wts = wts / jnp.sum(wts, axis=-1, keepdims=True)
    combine = jnp.sum(jax.nn.one_hot(e_idx, N_EXPERTS, dtype=jnp.float32) * wts[..., None],
                      axis=-2).astype(h.dtype)
    y = jnp.zeros_like(h)
    for e in range(N_EXPERTS):
        act = jax.nn.silu(h @ w_gate[e]) * (h @ w_up[e])
        y = y + combine[..., e:e + 1] * (act @ w_down[e])
    return y


def setup_inputs(seed: int = 0) -> dict:
    key = jax.random.key(seed)
    ks = jax.random.split(key, 22)
    f32 = jnp.float32

    def nrm(k, shape, scale):
        return jax.random.normal(k, shape, f32) * scale

    dt = jnp.exp(jax.random.uniform(ks[11], (N_DN_LAYERS, 2, DN_HEADS), f32,
                                    math.log(1e-3), math.log(1e-1)))
    return {
        "x": nrm(ks[0], (BATCH, SEQ, D_MODEL), 1.0),
        "c": nrm(ks[1], (BATCH, D_MODEL), 1.0),
        "ctx": nrm(ks[2], (BATCH, CTX_LEN, D_MODEL), 1.0),
        "c_ctx": nrm(ks[3], (D_MODEL,), 1.0),
        "ada_w": nrm(ks[4], (DEPTH, D_MODEL, N_MOD * D_MODEL), 0.5 * D_MODEL ** -0.5),
        "ada_b": nrm(ks[5], (DEPTH, N_MOD * D_MODEL), 0.02),
        "norm_mix_w": 1.0 + nrm(ks[6], (DEPTH, D_MODEL), 0.02),
        "norm_ffn_w": 1.0 + nrm(ks[7], (DEPTH, D_MODEL), 0.02),
        "dn_w_in": nrm(ks[8], (N_DN_LAYERS, D_MODEL, DN_IN), D_MODEL ** -0.5),
        "dn_conv_w": nrm(ks[9], (N_DN_LAYERS, CONV_W, 3 * DN_WIDTH), CONV_W ** -0.5),
        "dn_A_log": jnp.log(jax.random.uniform(ks[10], (N_DN_LAYERS, 2, DN_HEADS), f32, 1.0, 16.0)),
        "dn_dt_bias": dt + jnp.log(-jnp.expm1(-dt)),
        "dn_norm_w": 1.0 + nrm(ks[12], (N_DN_LAYERS, DN_HEAD_DIM), 0.02),
        "dn_w_out": nrm(ks[13], (N_DN_LAYERS, DN_WIDTH, D_MODEL), DN_WIDTH ** -0.5),
        "fn_w_in": nrm(ks[14], (N_FN_LAYERS, D_MODEL, FN_WIDTH), D_MODEL ** -0.5),
        "fn_w_out": nrm(ks[15], (N_FN_LAYERS, FN_WIDTH, D_MODEL), FN_WIDTH ** -0.5),
        "router_w": nrm(ks[16], (D_MODEL, N_EXPERTS), D_MODEL ** -0.5),
        "router_bias": nrm(ks[17], (N_EXPERTS,), 0.01),
        "moe_w_gate": nrm(ks[18], (DEPTH, N_EXPERTS, D_MODEL, D_FF), D_MODEL ** -0.5),
        "moe_w_up": nrm(ks[19], (DEPTH, N_EXPERTS, D_MODEL, D_FF), D_MODEL ** -0.5),
        "moe_w_down": nrm(ks[20], (DEPTH, N_EXPERTS, D_FF, D_MODEL), D_FF ** -0.5),
        "final_norm_w": 1.0 + nrm(ks[21], (D_MODEL,), 0.02),
    }


def reference(x, c, ctx, c_ctx, ada_w, ada_b, norm_mix_w, norm_ffn_w, dn_w_in, dn_conv_w,
              dn_A_log, dn_dt_bias, dn_norm_w, dn_w_out, fn_w_in, fn_w_out, router_w,
              router_bias, moe_w_gate, moe_w_up, moe_w_down, final_norm_w):
    z = ctx
    silu_c = jax.nn.silu(c)
    silu_cc = jax.nn.silu(c_ctx)
    for i in range(DEPTH):
        use_dn = i % N_MIXERS == 0
        j = i // N_MIXERS
        ctx_out = i < DEPTH - 1
        ctx_live = use_dn or ctx_out
        mod = (silu_c @ ada_w[i] + ada_b[i])[:, None, :]
        sh_m, sc_m, g_m, sh_f, sc_f, g_f = jnp.split(mod, N_MOD, axis=-1)
        if ctx_live:
            mod_c = silu_cc @ ada_w[i] + ada_b[i]
            ch_m, cs_m, cg_m, ch_f, cs_f, cg_f = jnp.split(mod_c, N_MOD, axis=-1)

        h_lat = _modulate(x, norm_mix_w[i], sh_m, sc_m)
        if use_dn:
            h_ctx = _modulate(z, norm_mix_w[i], ch_m, cs_m)
            y_ctx, y_lat = _deltanet_mixer(h_ctx, h_lat, dn_w_in[j], dn_conv_w[j], dn_A_log[j],
                                           dn_dt_bias[j], dn_norm_w[j], dn_w_out[j], ctx_out)
        else:
            y_lat = _fourier_mixer(h_lat, fn_w_in[j], fn_w_out[j])
            y_ctx = (_fourier_mixer(_modulate(z, norm_mix_w[i], ch_m, cs_m), fn_w_in[j], fn_w_out[j])
                     if ctx_out else None)
        x = x + g_m * y_lat

        h_lat = _modulate(x, norm_ffn_w[i], sh_f, sc_f)
        x = x + g_f * _moe(h_lat, router_w, router_bias, moe_w_gate[i], moe_w_up[i], moe_w_down[i])
        if ctx_out:
            z = z + cg_m * y_ctx
            h_ctx = _modulate(z, norm_ffn_w[i], ch_f, cs_f)
            z = z + cg_f * _moe(h_ctx, router_w, router_bias, moe_w_gate[i], moe_w_up[i], moe_w_down[i])
    return _rmsnorm(x, final_norm_w)
```

```python
import functools
import math

import numpy as np
import jax
import jax.numpy as jnp
from jax import lax
from jax.experimental import pallas as pl
from jax.experimental.pallas import tpu as pltpu

F32 = jnp.float32
BF16 = jnp.bfloat16

GRID_W = 64
CHUNK = 64
CONV_W = 5
DN_HEADS = 16
FN_GROUPS = 4
N_GROUPS = 4
GROUP_SCORE_K = 2
TOP_K = 2
N_MOD = 6
N_MIXERS = 2
EPS = 1e-6

VMEM_LIMIT = 56 * 1024 * 1024
SCAN_HEADS = 4


def _cparams(*sem):
    return pltpu.CompilerParams(dimension_semantics=sem, vmem_limit_bytes=VMEM_LIMIT)


def _dot(a, b):
    return jnp.dot(a, b, preferred_element_type=F32)


def _dot_nt(a, b):
    return lax.dot_general(a, b, (((1,), (1,)), ((), ())), preferred_element_type=F32)


def _dot_tn(a, b):
    return lax.dot_general(a, b, (((0,), (0,)), ((), ())), preferred_element_type=F32)


def _split2(x):
    hi = x.astype(BF16)
    lo = (x - hi.astype(F32)).astype(BF16)
    return hi, lo


def _split3(x):
    hi = x.astype(BF16)
    r = x - hi.astype(F32)
    mid = r.astype(BF16)
    lo = (r - mid.astype(F32)).astype(BF16)
    return hi, mid, lo


def _silu(x):
    return x / (1.0 + jnp.exp(-x))


def _sigmoid(x):
    return 1.0 / (1.0 + jnp.exp(-x))


def _tile(n, pref, mult=8):
    if n <= pref:
        return n
    t = (pref // mult) * mult
    while t >= mult:
        if n % t == 0:
            return t
        t -= mult
    return n


class _Stream:
    def __init__(self, rows_per_mod, mod_base, seq_len, conv_len):
        self.rows_per_mod = rows_per_mod
        self.mod_base = mod_base
        self.seq_len = seq_len
        self.conv_len = conv_len


def _log2(n):
    assert n & (n - 1) == 0, n
    return n.bit_length() - 1


def _ada_kernel(c_ref, w_ref, b_ref, o_ref):
    s_hi, s_lo = _split2(_silu(c_ref[...]))
    w_hi, w_lo = _split2(w_ref[0])
    acc = _dot(s_hi, w_hi) + _dot(s_lo, w_hi) + _dot(s_hi, w_lo)
    o_ref[0] = acc + b_ref[0]


def _ada_table(cond, ada_w, ada_b):
    depth, d, n = ada_w.shape
    r = cond.shape[0]
    tn = _tile(n, 512, 128)
    return pl.pallas_call(
        _ada_kernel,
        grid=(depth, n // tn),
        in_specs=[
            pl.BlockSpec((r, d), lambda l, j: (0, 0)),
            pl.BlockSpec((1, d, tn), lambda l, j: (l, 0, j)),
            pl.BlockSpec((1, 1, tn), lambda l, j: (l, 0, j)),
        ],
        out_specs=pl.BlockSpec((1, r, tn), lambda l, j: (l, 0, j)),
        out_shape=jax.ShapeDtypeStruct((depth, r, n), F32),
        compiler_params=_cparams("parallel", "parallel"),
        name="ada_table",
    )(cond, ada_w, ada_b.reshape(depth, 1, n))


def _modulated(x, nw, shift, scale):
    ms = jnp.mean(x * x, axis=-1, keepdims=True)
    return (x * lax.rsqrt(ms + EPS) * nw) * (1.0 + scale) + shift


def _modulate_kernel(x_ref, nw_ref, sh_ref, sc_ref, o_ref):
    o_ref[...] = _modulated(x_ref[...], nw_ref[...], sh_ref[0], sc_ref[0]).astype(o_ref.dtype)


class _Mod:
    def __init__(self, table3, mod_rows, d):
        self.table = table3
        self.mod_rows = mod_rows
        self.d = d

    def spec(self, stream, layer, k, tm, width=None, col=None):
        width = self.d if width is None else width
        rows, base, mr = stream.rows_per_mod, stream.mod_base, self.mod_rows

        def index_map(*g):
            row = base + (g[0] * tm) // rows
            return ((layer * mr + row) * N_MOD + k, 0, 0 if col is None else col(*g))
        return pl.BlockSpec((1, 1, width), index_map)


def _modulate(x, nw, mod, stream, layer, k_shift, k_scale):
    m, d = x.shape
    tm = _tile(min(m, stream.rows_per_mod), 512)
    return pl.pallas_call(
        _modulate_kernel,
        grid=(m // tm,),
        in_specs=[
            pl.BlockSpec((tm, d), lambda i: (i, 0)),
            pl.BlockSpec((1, d), lambda i: (0, 0)),
            mod.spec(stream, layer, k_shift, tm),
            mod.spec(stream, layer, k_scale, tm),
        ],
        out_specs=pl.BlockSpec((tm, d), lambda i: (i, 0)),
        out_shape=jax.ShapeDtypeStruct((m, d), BF16),
        compiler_params=_cparams("parallel"),
        name="modulate",
    )(x, nw.reshape(1, d), mod.table, mod.table)


def _final_norm_kernel(x_ref, nw_ref, o_ref):
    x = x_ref[...]
    ms = jnp.mean(x * x, axis=-1, keepdims=True)
    o_ref[...] = x * lax.rsqrt(ms + EPS) * nw_ref[...]


def _final_norm(x, nw):
    m, d = x.shape
    tm = _tile(m, 512)
    return pl.pallas_call(
        _final_norm_kernel,
        grid=(m // tm,),
        in_specs=[pl.BlockSpec((tm, d), lambda i: (i, 0)), pl.BlockSpec((1, d), lambda i: (0, 0))],
        out_specs=pl.BlockSpec((tm, d), lambda i: (i, 0)),
        out_shape=jax.ShapeDtypeStruct((m, d), F32),
        compiler_params=_cparams("parallel"),
        name="final_norm",
    )(x, nw.reshape(1, d))


def _matmul_kernel(a_ref, w_ref, o_ref):
    o_ref[...] = _dot(a_ref[...], w_ref[...]).astype(o_ref.dtype)


def _matmul(a, w, out_dtype, tm_pref=1024, tn_pref=512):
    m, k = a.shape
    n = w.shape[1]
    tm, tn = _tile(m, tm_pref), _tile(n, tn_pref, 128)
    return pl.pallas_call(
        _matmul_kernel,
        grid=(m // tm, n // tn),
        in_specs=[pl.BlockSpec((tm, k), lambda i, j: (i, 0)), pl.BlockSpec((k, tn), lambda i, j: (0, j))],
        out_specs=pl.BlockSpec((tm, tn), lambda i, j: (i, j)),
        out_shape=jax.ShapeDtypeStruct((m, n), out_dtype),
        compiler_params=_cparams("parallel", "parallel"),
        name="matmul",
    )(a, w)


def _matmul_res_kernel(a_ref, w_ref, x_ref, g_ref, o_ref):
    o_ref[...] = x_ref[...] + g_ref[0] * _dot(a_ref[...], w_ref[...])


def _matmul_gated_residual(a, w, x, mod, stream, layer, k_gate, tm_pref=1024, tn_pref=512):
    m, k = a.shape
    n = w.shape[1]
    tm = _tile(min(m, stream.rows_per_mod), tm_pref)
    tn = _tile(n, tn_pref, 128)
    return pl.pallas_call(
        _matmul_res_kernel,
        grid=(m // tm, n // tn),
        in_specs=[
            pl.BlockSpec((tm, k), lambda i, j: (i, 0)),
            pl.BlockSpec((k, tn), lambda i, j: (0, j)),
            pl.BlockSpec((tm, tn), lambda i, j: (i, j)),
            mod.spec(stream, layer, k_gate, tm, width=tn, col=lambda i, j: j),
        ],
        out_specs=pl.BlockSpec((tm, tn), lambda i, j: (i, j)),
        out_shape=jax.ShapeDtypeStruct((m, n), F32),
        compiler_params=_cparams("parallel", "parallel"),
        name="matmul_gated_residual",
    )(a, w, x, mod.table)


def _head_sumsq(y, ones_bd):
    sq_hi, sq_lo = _split2(y * y)
    return _dot(sq_hi, ones_bd) + _dot(sq_lo, ones_bd)


def _dn_conv_kernel(u_ref, cw_ref, e_ref, o_ref, *, conv_len, mode, head_dim):
    u = u_ref[...].astype(F32)
    tm = u.shape[0]
    pos = lax.broadcasted_iota(jnp.int32, u.shape, 0) & (conv_len - 1)
    pad = CONV_W // 2
    acc = u * cw_ref[pad:pad + 1, :]
    for j in range(CONV_W):
        d = j - pad
        if d == 0:
            continue
        shifted = pltpu.roll(u, (-d) % tm, 0)
        ok = (pos + d >= 0) & (pos + d < conv_len)
        acc = acc + jnp.where(ok, shifted, 0.0) * cw_ref[j:j + 1, :]
    y = _silu(acc)
    if mode != "v":
        y = y * lax.rsqrt(_head_sumsq(y, e_ref[...]) + EPS)
        if mode == "q":
            y = y * (head_dim ** -0.5)
    o_ref[...] = y.astype(o_ref.dtype)


def _dn_conv(p, conv_w, section, mode, stream, width, head_dim):
    m = p.shape[0]
    tc = _tile(width, 512, 128)
    tm = _tile(min(m, max(stream.conv_len, 256)), 256, stream.conv_len)
    ones_bd = jnp.asarray(np.kron(np.eye(tc // head_dim), np.ones((head_dim, head_dim))), BF16)
    nsec = width // tc
    return pl.pallas_call(
        functools.partial(_dn_conv_kernel, conv_len=stream.conv_len, mode=mode, head_dim=head_dim),
        grid=(m // tm, nsec),
        in_specs=[
            pl.BlockSpec((tm, tc), lambda i, j: (i, section * nsec + j)),
            pl.BlockSpec((CONV_W, tc), lambda i, j: (0, section * nsec + j)),
            pl.BlockSpec((tc, tc), lambda i, j: (0, 0)),
        ],
        out_specs=pl.BlockSpec((tm, tc), lambda i, j: (i, j)),
        out_shape=jax.ShapeDtypeStruct((m, width), BF16),
        compiler_params=_cparams("parallel", "parallel"),
        name="dn_conv_" + mode,
    )(p, conv_w, ones_bd)


def _dn_gates_kernel(ab_ref, al_ref, dt_ref, tl_ref, tu_ref, g_ref, beta_ref, *, nh):
    ab = ab_ref[...]
    a = ab[:, :2 * nh] + dt_ref[...]
    softplus = jnp.maximum(a, 0.0) + jnp.log1p(jnp.exp(-jnp.abs(a)))
    la = -jnp.exp(al_ref[...]) * softplus
    beta_ref[...] = _sigmoid(ab[:, 2 * nh:])
    parts = _split3(la)
    g_pre = sum(_dot(tl_ref[...], p) for p in parts)
    g_suf = sum(_dot(tu_ref[...], p) for p in parts)
    col = lax.broadcasted_iota(jnp.int32, la.shape, 1)
    g_ref[...] = jnp.where(col < nh, g_pre, g_suf)


def _dn_gates(ab, a_log, dt_bias, nh):
    m = ab.shape[0]
    tm = _tile(m, 256, CHUNK)
    r = np.arange(tm)
    same = (r[:, None] // CHUNK) == (r[None, :] // CHUNK)
    tri_l = jnp.asarray(same & (r[None, :] <= r[:, None]), BF16)
    tri_u = jnp.asarray(same & (r[None, :] >= r[:, None]), BF16)
    return pl.pallas_call(
        functools.partial(_dn_gates_kernel, nh=nh),
        grid=(m // tm,),
        in_specs=[
            pl.BlockSpec((tm, 4 * nh), lambda i: (i, 0)),
            pl.BlockSpec((1, 2 * nh), lambda i: (0, 0)),
            pl.BlockSpec((1, 2 * nh), lambda i: (0, 0)),
            pl.BlockSpec((tm, tm), lambda i: (0, 0)),
            pl.BlockSpec((tm, tm), lambda i: (0, 0)),
        ],
        out_specs=[pl.BlockSpec((tm, 2 * nh), lambda i: (i, 0))] * 2,
        out_shape=[jax.ShapeDtypeStruct((m, 2 * nh), F32)] * 2,
        compiler_params=_cparams("parallel"),
        name="dn_gates",
    )(ab, a_log.reshape(1, 2 * nh), dt_bias.reshape(1, 2 * nh), tri_l, tri_u)


def _unit_tri_inverse(lm):
    n = lm.shape[0]
    ri = lax.broadcasted_iota(jnp.int32, lm.shape, 0)
    ci = lax.broadcasted_iota(jnp.int32, lm.shape, 1)
    eye = (ri == ci).astype(F32)
    t = eye - jnp.where((ri >> 1) == (ci >> 1), lm, 0.0)
    for lvl in range(1, _log2(n)):
        off = jnp.where(((ri >> (lvl + 1)) == (ci >> (lvl + 1))) & ((ri >> lvl) != (ci >> lvl)), lm, 0.0)
        tb = t.astype(BF16)
        t = t - _dot(_dot(tb, off.astype(BF16)).astype(BF16), tb)
    return t


def _dn_scan_kernel(q_ref, k_ref, v_ref, gc_ref, gr_ref, b_ref, s0_ref, o_ref, s_ref, *, backward, nh, hd, hg):
    c = pl.program_id(2)
    head0 = pl.program_id(1) * hg

    @pl.when(c == 0)
    def _():
        s_ref[...] = s0_ref[...]

    n = CHUNK
    ri = lax.broadcasted_iota(jnp.int32, (n, n), 0)
    ci = lax.broadcasted_iota(jnp.int32, (n, n), 1)
    causal = (ci >= ri) if backward else (ci <= ri)
    strict = (ci > ri) if backward else (ci < ri)
    last = 0 if backward else n - 1
    dcol = nh if backward else 0
    gc_all = gc_ref[...]
    gr_all = gr_ref[0]
    b_all = b_ref[...]
    lane = lax.broadcasted_iota(jnp.int32, gc_all.shape, 1)
    for hh in range(hg):
        sel = lane == (dcol + head0 + hh)
        gcol = jnp.sum(jnp.where(sel, gc_all, 0.0), axis=1, keepdims=True)
        bcol = jnp.sum(jnp.where(sel, b_all, 0.0), axis=1, keepdims=True)
        rsel = lax.broadcasted_iota(jnp.int32, gr_all.shape, 0) == (dcol + head0 + hh)
        grow = jnp.sum(jnp.where(rsel, gr_all, 0.0), axis=0, keepdims=True)
        glast = gcol[last:last + 1, :]
        sl = slice(hh * hd, (hh + 1) * hd)
        q, k, v = q_ref[:, sl], k_ref[:, sl], v_ref[:, sl]
        kf, qf, vf = k.astype(F32), q.astype(F32), v.astype(F32)
        decay = jnp.exp(jnp.where(causal, gcol - grow, -jnp.inf))
        kk = _dot_nt(k, k)
        qk = _dot_nt(q, k)
        lm = jnp.where(strict, bcol * kk * decay, 0.0)
        intra = qk * decay
        t = _unit_tri_inverse(lm)
        eg = jnp.exp(gcol)
        rhs = jnp.concatenate([vf * bcol, kf * (bcol * eg)], axis=1).astype(BF16)
        uw = _dot(t.astype(BF16), rhs)
        u, w = uw[:, :hd], uw[:, hd:]
        s = s_ref[0, hh]
        ws = _dot(jnp.concatenate([w, qf * eg], axis=0).astype(BF16), s.astype(BF16))
        v_new = u - ws[:n]
        vb = v_new.astype(BF16)
        o_ref[:, sl] = ws[n:] + _dot(intra.astype(BF16), vb)
        kg = (kf * jnp.exp(glast - gcol)).astype(BF16)
        s_ref[0, hh] = s * jnp.exp(glast) + _dot_tn(kg, vb)


def _dn_scan(q, k, v, g, g_rows, beta, s0, backward, batch, nh, hd):
    m = q.shape[0]
    t_len = m // batch
    nch = t_len // CHUNK
    hg = min(SCAN_HEADS, nh)

    def tok(b, h, c):
        return (b * nch + (nch - 1 - c if backward else c), h)

    def tok0(b, h, c):
        return (b * nch + (nch - 1 - c if backward else c), 0)

    def tok3(b, h, c):
        return (b * nch + (nch - 1 - c if backward else c), 0, 0)

    st = lambda b, h, c: (b, h, 0, 0)
    return pl.pallas_call(
        functools.partial(_dn_scan_kernel, backward=backward, nh=nh, hd=hd, hg=hg),
        grid=(batch, nh // hg, nch),
        in_specs=[
            pl.BlockSpec((CHUNK, hg * hd), tok),
            pl.BlockSpec((CHUNK, hg * hd), tok),
            pl.BlockSpec((CHUNK, hg * hd), tok),
            pl.BlockSpec((CHUNK, 2 * nh), tok0),
            pl.BlockSpec((1, 2 * nh, CHUNK), tok3),
            pl.BlockSpec((CHUNK, 2 * nh), tok0),
            pl.BlockSpec((1, hg, hd, hd), st),
        ],
        out_specs=[pl.BlockSpec((CHUNK, hg * hd), tok), pl.BlockSpec((1, hg, hd, hd), st)],
        out_shape=[jax.ShapeDtypeStruct((m, nh * hd), F32), jax.ShapeDtypeStruct((batch, nh, hd, hd), F32)],
        compiler_params=_cparams("parallel", "parallel", "arbitrary"),
        name="dn_scan_bwd" if backward else "dn_scan_fwd",
    )(q, k, v, g, g_rows, beta, s0)


def _dn_out_kernel(of_ref, ob_ref, gate_ref, nw_ref, e_ref, y_ref, *, hd):
    o = of_ref[...] + ob_ref[...]
    ms = _head_sumsq(o, e_ref[...]) * (1.0 / hd)
    y = o * lax.rsqrt(ms + EPS) * nw_ref[...]
    y_ref[...] = (y * _silu(gate_ref[...].astype(F32))).astype(y_ref.dtype)


def _dn_out(o_f, o_b, p, norm_w, width, hd):
    m = o_f.shape[0]
    tc = _tile(width, 512, 128)
    tm = _tile(m, 512)
    nsec = width // tc
    ones_bd = jnp.asarray(np.kron(np.eye(tc // hd), np.ones((hd, hd))), BF16)
    nw = jnp.tile(norm_w.astype(F32), tc // hd).reshape(1, tc)
    return pl.pallas_call(
        functools.partial(_dn_out_kernel, hd=hd),
        grid=(m // tm, nsec),
        in_specs=[
            pl.BlockSpec((tm, tc), lambda i, j: (i, j)),
            pl.BlockSpec((tm, tc), lambda i, j: (i, j)),
            pl.BlockSpec((tm, tc), lambda i, j: (i, 3 * nsec + j)),
            pl.BlockSpec((1, tc), lambda i, j: (0, 0)),
            pl.BlockSpec((tc, tc), lambda i, j: (0, 0)),
        ],
        out_specs=pl.BlockSpec((tm, tc), lambda i, j: (i, j)),
        out_shape=jax.ShapeDtypeStruct((m, width), BF16),
        compiler_params=_cparams("parallel", "parallel"),
        name="dn_out",
    )(o_f, o_b, p, nw, ones_bd)


def _fold_kernel(w_ref, t_ref, o_ref):
    w_hi, w_lo = _split2(w_ref[...])
    t_hi, t_lo = _split2(t_ref[0])
    o_ref[...] = (_dot(w_hi, t_hi) + _dot(w_lo, t_hi) + _dot(w_hi, t_lo)).astype(o_ref.dtype)


def _fold_channel_dft(w_in, groups):
    d, width = w_in.shape
    c = width // groups
    idx = np.arange(c)
    ang = 2.0 * np.pi * ((idx[:, None] * idx[None, :]) % c) / c
    table = jnp.asarray(np.stack([np.cos(ang), -np.sin(ang)]) / math.sqrt(c), F32)
    return pl.pallas_call(
        _fold_kernel,
        grid=(2, groups),
        in_specs=[pl.BlockSpec((d, c), lambda s, g: (0, g)), pl.BlockSpec((1, c, c), lambda s, g: (s, 0, 0))],
        out_specs=pl.BlockSpec((d, c), lambda s, g: (0, s * groups + g)),
        out_shape=jax.ShapeDtypeStruct((d, 2 * width), BF16),
        compiler_params=_cparams("parallel", "parallel"),
        name="fold_channel_dft",
    )(w_in, table)


def _dft1_kernel(zr_ref, zi_ref, f_ref, yr_ref, yi_ref):
    n1 = zr_ref.shape[1]
    z = jnp.concatenate([zr_ref[0], zi_ref[0]], axis=0)
    y = _dot(f_ref[...], z)
    yr_ref[0] = y[:n1].astype(yr_ref.dtype)
    yi_ref[0] = y[n1:].astype(yi_ref.dtype)


def _dft2_kernel(yr_ref, yi_ref, g_ref, o_ref):
    z = jnp.concatenate([yr_ref[0], yi_ref[0]], axis=0)
    o_ref[0] = _dot(g_ref[0], z).astype(o_ref.dtype)


def _position_dft_real(u, batch, width):
    t_len = u.shape[0] // batch
    n1 = 1 << (int(math.log2(t_len)) // 2)
    n2 = t_len // n1
    a1 = 2.0 * np.pi * ((np.arange(n1)[:, None] * np.arange(n1)[None, :]) % n1) / n1
    c1, s1 = np.cos(a1), np.sin(a1)
    f1m = jnp.asarray(np.block([[c1, s1], [-s1, c1]]) / math.sqrt(n1), BF16)
    f1i, f2i, t2i = np.arange(n1)[:, None, None], np.arange(n2)[None, :, None], np.arange(n2)[None, None, :]
    theta = 2.0 * np.pi * (((t2i * f1i) % t_len) / t_len + ((t2i * f2i) % n2) / n2)
    g2m = jnp.asarray(np.concatenate([np.cos(theta), np.sin(theta)], axis=2) / math.sqrt(n2), BF16)

    tn = _tile(width, 2048, 128)
    nj = width // tn
    u3 = u.reshape(batch, n1, n2 * 2 * width)
    yr, yi = pl.pallas_call(
        _dft1_kernel,
        grid=(batch, n2, nj),
        in_specs=[
            pl.BlockSpec((1, n1, tn), lambda b, t, j: (b, 0, t * 2 * nj + j)),
            pl.BlockSpec((1, n1, tn), lambda b, t, j: (b, 0, t * 2 * nj + nj + j)),
            pl.BlockSpec((2 * n1, 2 * n1), lambda b, t, j: (0, 0)),
        ],
        out_specs=[pl.BlockSpec((1, n1, tn), lambda b, t, j: (b, 0, t * nj + j))] * 2,
        out_shape=[jax.ShapeDtypeStruct((batch, n1, n2 * width), BF16)] * 2,
        compiler_params=_cparams("parallel", "parallel", "parallel"),
        name="dft_stage1",
    )(u3, u3, f1m)
    yr = yr.reshape(batch * n1, n2, width)
    yi = yi.reshape(batch * n1, n2, width)
    out = pl.pallas_call(
        _dft2_kernel,
        grid=(batch, n1),
        in_specs=[
            pl.BlockSpec((1, n2, width), lambda b, f: (b * n1 + f, 0, 0)),
            pl.BlockSpec((1, n2, width), lambda b, f: (b * n1 + f, 0, 0)),
            pl.BlockSpec((1, n2, 2 * n2), lambda b, f: (f, 0, 0)),
        ],
        out_specs=pl.BlockSpec((1, n2, width), lambda b, f: (b, 0, f)),
        out_shape=jax.ShapeDtypeStruct((batch, n2, n1 * width), BF16),
        compiler_params=_cparams("parallel", "parallel"),
        name="dft_stage2",
    )(yr, yi, g2m)
    return out.reshape(batch * t_len, width)


def _ffn_prep_kernel(x_ref, nw_ref, sh_ref, sc_ref, rw_ref, rb_ref, h_ref, comb_ref, *, n_exp):
    h = _modulated(x_ref[...], nw_ref[...], sh_ref[0], sc_ref[0])
    h_ref[...] = h.astype(h_ref.dtype)
    h_hi, h_lo = _split2(h)
    r_hi, r_lo = _split2(rw_ref[...])
    logits = _dot_nt(r_hi, h_hi) + _dot_nt(r_lo, h_hi) + _dot_nt(r_hi, h_lo)
    s = _sigmoid(logits)
    sel = s + rb_ref[...]
    srow = [s[e:e + 1, :] for e in range(n_exp)]
    row = [sel[e:e + 1, :] for e in range(n_exp)]
    epg = n_exp // N_GROUPS

    def beats(a, ia, b, ib):
        return (a >= b) if ia < ib else (a > b)

    rank, gscore = {}, []
    for g in range(N_GROUPS):
        members = range(g * epg, (g + 1) * epg)
        for i in members:
            rank[i] = sum(beats(row[j], j, row[i], i).astype(F32) for j in members if j != i)
        gscore.append(sum(jnp.where(rank[i] < GROUP_SCORE_K, row[i], 0.0) for i in members))
    picked = []
    for g in range(N_GROUPS):
        grank = sum(beats(gscore[j], j, gscore[g], g).astype(F32) for j in range(N_GROUPS) if j != g)
        for i in range(g * epg, (g + 1) * epg):
            picked.append(jnp.where((grank < 1.0) & (rank[i] < TOP_K), srow[i], 0.0))
    denom = sum(picked)
    comb_ref[...] = jnp.concatenate([p / denom for p in picked], axis=0)


def _ffn_prep(x, nw, mod, stream, layer, router_wt, router_b):
    m, d = x.shape
    n_exp = router_wt.shape[0]
    tm = _tile(min(m, stream.rows_per_mod), 512, 128)
    return pl.pallas_call(
        functools.partial(_ffn_prep_kernel, n_exp=n_exp),
        grid=(m // tm,),
        in_specs=[
            pl.BlockSpec((tm, d), lambda i: (i, 0)),
            pl.BlockSpec((1, d), lambda i: (0, 0)),
            mod.spec(stream, layer, 3, tm),
            mod.spec(stream, layer, 4, tm),
            pl.BlockSpec((n_exp, d), lambda i: (0, 0)),
            pl.BlockSpec((n_exp, 1), lambda i: (0, 0)),
        ],
        out_specs=[pl.BlockSpec((tm, d), lambda i: (i, 0)), pl.BlockSpec((n_exp, tm), lambda i: (0, i))],
        out_shape=[jax.ShapeDtypeStruct((m, d), BF16), jax.ShapeDtypeStruct((n_exp, m), F32)],
        compiler_params=_cparams("parallel"),
        name="ffn_prep",
    )(x, nw.reshape(1, d), mod.table, mod.table, router_wt, router_b.reshape(n_exp, 1))


def _moe_dense_kernel(h_ref, comb_ref, wg_ref, wu_ref, wd_ref, x_ref, g_ref, o_ref, acc_ref):
    e = pl.program_id(1)

    @pl.when(e == 0)
    def _():
        acc_ref[...] = jnp.zeros_like(acc_ref)

    h = h_ref[...]
    a = _dot(h, wg_ref[0])
    act = _silu(a) * _dot(h, wu_ref[0])
    comb = comb_ref[...]
    lane = lax.broadcasted_iota(jnp.int32, comb.shape, 1)
    ccol = jnp.sum(jnp.where(lane == e, comb, 0.0), axis=1, keepdims=True)
    acc_ref[...] += _dot((act * ccol).astype(BF16), wd_ref[0])

    @pl.when(e == pl.num_programs(1) - 1)
    def _():
        o_ref[...] = x_ref[...] + g_ref[0] * acc_ref[...]


def _moe_dense(h, comb, wg, wu, wd, x, mod, stream, layer):
    m, d = h.shape
    n_exp, _, f = wg.shape
    tm = _tile(min(m, stream.rows_per_mod), 512)
    return pl.pallas_call(
        _moe_dense_kernel,
        grid=(m // tm, n_exp),
        in_specs=[
            pl.BlockSpec((tm, d), lambda i, e: (i, 0)),
            pl.BlockSpec((tm, n_exp), lambda i, e: (i, 0)),
            pl.BlockSpec((1, d, f), lambda i, e: (e, 0, 0)),
            pl.BlockSpec((1, d, f), lambda i, e: (e, 0, 0)),
            pl.BlockSpec((1, f, d), lambda i, e: (e, 0, 0)),
            pl.BlockSpec((tm, d), lambda i, e: (i, 0)),
            mod.spec(stream, layer, 5, tm),
        ],
        out_specs=pl.BlockSpec((tm, d), lambda i, e: (i, 0)),
        out_shape=jax.ShapeDtypeStruct((m, d), F32),
        scratch_shapes=[pltpu.VMEM((tm, d), F32)],
        compiler_params=_cparams("parallel", "arbitrary"),
        name="moe_dense",
    )(h, comb, wg, wu, wd, x, mod.table)


def _deltanet_streams(h, w_main, w_ab, conv_w, a_log, dt_bias, stream, nh, hd):
    width = nh * hd
    p = _matmul(h, w_main, BF16)
    ab = _matmul(h, w_ab, F32, tn_pref=4 * nh)
    q = _dn_conv(p, conv_w, 0, "q", stream, width, hd)
    k = _dn_conv(p, conv_w, 1, "k", stream, width, hd)
    v = _dn_conv(p, conv_w, 2, "v", stream, width, hd)
    g, beta = _dn_gates(ab, a_log, dt_bias, nh)
    nchunks = g.shape[0] // CHUNK
    g_rows = g.reshape(nchunks, CHUNK, 2 * nh).transpose(0, 2, 1)
    return p, q, k, v, g, g_rows, beta


def kernel(x, c, ctx, c_ctx, ada_w, ada_b, norm_mix_w, norm_ffn_w, dn_w_in, dn_conv_w, dn_A_log, dn_dt_bias, dn_norm_w, dn_w_out, fn_w_in, fn_w_out, router_w, router_bias, moe_w_gate, moe_w_up, moe_w_down, final_norm_w):
    batch, seq, d = x.shape
    ctx_len = ctx.shape[1]
    depth = ada_w.shape[0]
    nh = DN_HEADS
    dn_width = dn_w_out.shape[1]
    hd = dn_width // nh
    fn_width = fn_w_in.shape[2]

    lat = _Stream(rows_per_mod=seq, mod_base=0, seq_len=seq, conv_len=GRID_W)
    cst = _Stream(rows_per_mod=batch * ctx_len, mod_base=batch, seq_len=ctx_len, conv_len=ctx_len)

    mod_rows = -(-(batch + 1) // 8) * 8
    cond = jnp.zeros((mod_rows, d), F32).at[:batch].set(c).at[batch].set(c_ctx)
    table = _ada_table(cond, ada_w, ada_b)
    mod = _Mod(table.reshape(depth * mod_rows * N_MOD, 1, d), mod_rows, d)

    router_wt = router_w.T
    xs = x.reshape(batch * seq, d)
    zs = ctx.reshape(batch * ctx_len, d)

    def channel_mixer(s, stream, layer):
        h, comb_t = _ffn_prep(s, norm_ffn_w[layer], mod, stream, layer, router_wt, router_bias)
        return _moe_dense(h, comb_t.T, moe_w_gate[layer].astype(BF16), moe_w_up[layer].astype(BF16),
                          moe_w_down[layer].astype(BF16), s, mod, stream, layer)

    for i in range(depth):
        use_dn = i % N_MIXERS == 0
        j = i // N_MIXERS
        ctx_out = i < depth - 1
        ctx_live = use_dn or ctx_out

        h_lat = _modulate(xs, norm_mix_w[i], mod, lat, i, 0, 1)
        h_ctx = _modulate(zs, norm_mix_w[i], mod, cst, i, 0, 1) if ctx_live else None
        if use_dn:
            w_in = dn_w_in[j].astype(BF16)
            w_main, w_ab = w_in[:, :4 * dn_width], w_in[:, 4 * dn_width:]
            w_out = dn_w_out[j].astype(BF16)
            a_log, dt_bias = dn_A_log[j].reshape(-1), dn_dt_bias[j].reshape(-1)
            cs = _deltanet_streams(h_ctx, w_main, w_ab, dn_conv_w[j], a_log, dt_bias, cst, nh, hd)
            ls = _deltanet_streams(h_lat, w_main, w_ab, dn_conv_w[j], a_log, dt_bias, lat, nh, hd)
            o_ctx, o_lat = [], []
            for backward in (False, True):
                s0 = jnp.zeros((batch, nh, hd, hd), F32)
                oc, s_ctx = _dn_scan(cs[1], cs[2], cs[3], cs[4], cs[5], cs[6], s0, backward, batch, nh, hd)
                ol, _ = _dn_scan(ls[1], ls[2], ls[3], ls[4], ls[5], ls[6], s_ctx, backward, batch, nh, hd)
                o_ctx.append(oc)
                o_lat.append(ol)
            y_lat = _dn_out(o_lat[0], o_lat[1], ls[0], dn_norm_w[j], dn_width, hd)
            xs = _matmul_gated_residual(y_lat, w_out, xs, mod, lat, i, 2)
            if ctx_out:
                y_ctx = _dn_out(o_ctx[0], o_ctx[1], cs[0], dn_norm_w[j], dn_width, hd)
                z_mix = _matmul_gated_residual(y_ctx, w_out, zs, mod, cst, i, 2)
        else:
            w_fold = _fold_channel_dft(fn_w_in[j], FN_GROUPS)
            w_out = fn_w_out[j].astype(BF16)
            mixed = _position_dft_real(_matmul(h_lat, w_fold, BF16), batch, fn_width)
            xs = _matmul_gated_residual(mixed, w_out, xs, mod, lat, i, 2)
            if ctx_out:
                mixed_c = _position_dft_real(_matmul(h_ctx, w_fold, BF16), batch, fn_width)
                z_mix = _matmul_gated_residual(mixed_c, w_out, zs, mod, cst, i, 2)

        xs = channel_mixer(xs, lat, i)
        if ctx_out:
            zs = channel_mixer(z_mix, cst, i)

    return _final_norm(xs, final_norm_w).reshape(batch, seq, d)
```

```python
import functools
import math

import numpy as np
import jax
import jax.numpy as jnp
from jax import lax
from jax.experimental import pallas as pl
from jax.experimental.pallas import tpu as pltpu

F32 = jnp.float32
BF16 = jnp.bfloat16

GRID_W = 64
CHUNK = 64
CONV_W = 5
DN_HEADS = 16
FN_GROUPS = 4
N_GROUPS = 4
GROUP_SCORE_K = 2
TOP_K = 2
N_MOD = 6
N_MIXERS = 2
EPS = 1e-6

VMEM_LIMIT = 56 * 1024 * 1024


def _cparams(*sem):
    return pltpu.CompilerParams(dimension_semantics=sem, vmem_limit_bytes=VMEM_LIMIT)


def _dot(a, b):
    return jnp.dot(a, b, preferred_element_type=F32)


def _dot_nt(a, b):
    return lax.dot_general(a, b, (((1,), (1,)), ((), ())), preferred_element_type=F32)


def _dot_tn(a, b):
    return lax.dot_general(a, b, (((0,), (0,)), ((), ())), preferred_element_type=F32)


def _split2(x):
    hi = x.astype(BF16)
    lo = (x - hi.astype(F32)).astype(BF16)
    return hi, lo


def _split3(x):
    hi = x.astype(BF16)
    r = x - hi.astype(F32)
    mid = r.astype(BF16)
    lo = (r - mid.astype(F32)).astype(BF16)
    return hi, mid, lo


def _silu(x):
    return x / (1.0 + jnp.exp(-x))


def _sigmoid(x):
    return 1.0 / (1.0 + jnp.exp(-x))


def _tile(n, pref, mult=8):
    if n <= pref:
        return n
    t = (pref // mult) * mult
    while t >= mult:
        if n % t == 0:
            return t
        t -= mult
    return n


class _Stream:
    def __init__(self, rows_per_mod, mod_base, seq_len, conv_len):
        self.rows_per_mod = rows_per_mod
        self.mod_base = mod_base
        self.seq_len = seq_len
        self.conv_len = conv_len


def _log2(n):
    assert n & (n - 1) == 0, n
    return n.bit_length() - 1


def _ada_kernel(c_ref, w_ref, b_ref, o_ref):
    s_hi, s_lo = _split2(_silu(c_ref[...]))
    w_hi, w_lo = _split2(w_ref[0])
    acc = _dot(s_hi, w_hi) + _dot(s_lo, w_hi) + _dot(s_hi, w_lo)
    o_ref[0] = acc + b_ref[0]


def _ada_table(cond, ada_w, ada_b):
    depth, d, n = ada_w.shape
    r = cond.shape[0]
    tn = _tile(n, 512, 128)
    return pl.pallas_call(
        _ada_kernel,
        grid=(depth, n // tn),
        in_specs=[
            pl.BlockSpec((r, d), lambda l, j: (0, 0)),
            pl.BlockSpec((1, d, tn), lambda l, j: (l, 0, j)),
            pl.BlockSpec((1, 1, tn), lambda l, j: (l, 0, j)),
        ],
        out_specs=pl.BlockSpec((1, r, tn), lambda l, j: (l, 0, j)),
        out_shape=jax.ShapeDtypeStruct((depth, r, n), F32),
        compiler_params=_cparams("parallel", "parallel"),
        name="ada_table",
    )(cond, ada_w, ada_b.reshape(depth, 1, n))


def _modulated(x, nw, shift, scale):
    ms = jnp.mean(x * x, axis=-1, keepdims=True)
    return (x * lax.rsqrt(ms + EPS) * nw) * (1.0 + scale) + shift


def _modulate_kernel(x_ref, nw_ref, sh_ref, sc_ref, o_ref):
    o_ref[...] = _modulated(x_ref[...], nw_ref[...], sh_ref[0], sc_ref[0]).astype(o_ref.dtype)


class _Mod:
    def __init__(self, table3, mod_rows, d):
        self.table = table3
        self.mod_rows = mod_rows
        self.d = d

    def spec(self, stream, layer, k, tm, width=None, col=None):
        width = self.d if width is None else width
        rows, base, mr = stream.rows_per_mod, stream.mod_base, self.mod_rows

        def index_map(*g):
            row = base + (g[0] * tm) // rows
            return ((layer * mr + row) * N_MOD + k, 0, 0 if col is None else col(*g))
        return pl.BlockSpec((1, 1, width), index_map)


def _modulate(x, nw, mod, stream, layer, k_shift, k_scale):
    m, d = x.shape
    tm = _tile(min(m, stream.rows_per_mod), 512)
    return pl.pallas_call(
        _modulate_kernel,
        grid=(m // tm,),
        in_specs=[
            pl.BlockSpec((tm, d), lambda i: (i, 0)),
            pl.BlockSpec((1, d), lambda i: (0, 0)),
            mod.spec(stream, layer, k_shift, tm),
            mod.spec(stream, layer, k_scale, tm),
        ],
        out_specs=pl.BlockSpec((tm, d), lambda i: (i, 0)),
        out_shape=jax.ShapeDtypeStruct((m, d), BF16),
        compiler_params=_cparams("parallel"),
        name="modulate",
    )(x, nw.reshape(1, d), mod.table, mod.table)


def _final_norm_kernel(x_ref, nw_ref, o_ref):
    x = x_ref[...]
    ms = jnp.mean(x * x, axis=-1, keepdims=True)
    o_ref[...] = x * lax.rsqrt(ms + EPS) * nw_ref[...]


def _final_norm(x, nw):
    m, d = x.shape
    tm = _tile(m, 512)
    return pl.pallas_call(
        _final_norm_kernel,
        grid=(m // tm,),
        in_specs=[pl.BlockSpec((tm, d), lambda i: (i, 0)), pl.BlockSpec((1, d), lambda i: (0, 0))],
        out_specs=pl.BlockSpec((tm, d), lambda i: (i, 0)),
        out_shape=jax.ShapeDtypeStruct((m, d), F32),
        compiler_params=_cparams("parallel"),
        name="final_norm",
    )(x, nw.reshape(1, d))


def _matmul_kernel(a_ref, w_ref, o_ref):
    o_ref[...] = _dot(a_ref[...], w_ref[...]).astype(o_ref.dtype)


def _matmul(a, w, out_dtype, tm_pref=1024, tn_pref=512, col0=0, n=None):
    m, k = a.shape
    n = w.shape[1] if n is None else n
    tm, tn = _tile(m, tm_pref), _tile(n, tn_pref, 128)
    assert col0 % tn == 0
    cb = col0 // tn
    return pl.pallas_call(
        _matmul_kernel,
        grid=(m // tm, n // tn),
        in_specs=[pl.BlockSpec((tm, k), lambda i, j: (i, 0)), pl.BlockSpec((k, tn), lambda i, j: (0, cb + j))],
        out_specs=pl.BlockSpec((tm, tn), lambda i, j: (i, j)),
        out_shape=jax.ShapeDtypeStruct((m, n), out_dtype),
        compiler_params=_cparams("parallel", "parallel"),
        name="matmul",
    )(a, w)


def _matmul_res_kernel(a_ref, w_ref, x_ref, g_ref, o_ref):
    o_ref[...] = x_ref[...] + g_ref[0] * _dot(a_ref[...], w_ref[...])


def _matmul_gated_residual(a, w, x, mod, stream, layer, k_gate, tm_pref=1024, tn_pref=512):
    m, k = a.shape
    n = w.shape[1]
    tm = _tile(min(m, stream.rows_per_mod), tm_pref)
    tn = _tile(n, tn_pref, 128)
    return pl.pallas_call(
        _matmul_res_kernel,
        grid=(m // tm, n // tn),
        in_specs=[
            pl.BlockSpec((tm, k), lambda i, j: (i, 0)),
            pl.BlockSpec((k, tn), lambda i, j: (0, j)),
            pl.BlockSpec((tm, tn), lambda i, j: (i, j)),
            mod.spec(stream, layer, k_gate, tm, width=tn, col=lambda i, j: j),
        ],
        out_specs=pl.BlockSpec((tm, tn), lambda i, j: (i, j)),
        out_shape=jax.ShapeDtypeStruct((m, n), F32),
        compiler_params=_cparams("parallel", "parallel"),
        name="matmul_gated_residual",
    )(a, w, x, mod.table)


def _head_sumsq(y, ones_bd):
    sq_hi, sq_lo = _split2(y * y)
    return _dot(sq_hi, ones_bd) + _dot(sq_lo, ones_bd)


def _dn_conv_kernel(u_ref, cw_ref, e_ref, o_ref, *, conv_len, mode, head_dim):
    u = u_ref[...].astype(F32)
    tm = u.shape[0]
    pos = lax.broadcasted_iota(jnp.int32, u.shape, 0) & (conv_len - 1)
    pad = CONV_W // 2
    acc = u * cw_ref[pad:pad + 1, :]
    for j in range(CONV_W):
        d = j - pad
        if d == 0:
            continue
        shifted = pltpu.roll(u, (-d) % tm, 0)
        ok = (pos + d >= 0) & (pos + d < conv_len)
        acc = acc + jnp.where(ok, shifted, 0.0) * cw_ref[j:j + 1, :]
    y = _silu(acc)
    if mode != "v":
        y = y * lax.rsqrt(_head_sumsq(y, e_ref[...]) + EPS)
        if mode == "q":
            y = y * (head_dim ** -0.5)
    o_ref[...] = y.astype(o_ref.dtype)


def _dn_conv(p, conv_w, section, mode, stream, width, head_dim):
    m = p.shape[0]
    tc = _tile(width, 512, 128)
    tm = _tile(min(m, max(stream.conv_len, 256)), 256, stream.conv_len)
    ones_bd = jnp.asarray(np.kron(np.eye(tc // head_dim), np.ones((head_dim, head_dim))), BF16)
    nsec = width // tc
    return pl.pallas_call(
        functools.partial(_dn_conv_kernel, conv_len=stream.conv_len, mode=mode, head_dim=head_dim),
        grid=(m // tm, nsec),
        in_specs=[
            pl.BlockSpec((tm, tc), lambda i, j: (i, section * nsec + j)),
            pl.BlockSpec((CONV_W, tc), lambda i, j: (0, section * nsec + j)),
            pl.BlockSpec((tc, tc), lambda i, j: (0, 0)),
        ],
        out_specs=pl.BlockSpec((tm, tc), lambda i, j: (i, j)),
        out_shape=jax.ShapeDtypeStruct((m, width), BF16),
        compiler_params=_cparams("parallel", "parallel"),
        name="dn_conv_" + mode,
    )(p, conv_w, ones_bd)


def _dn_gates_kernel(ab_ref, al_ref, dt_ref, tl_ref, tu_ref, g_ref, beta_ref, *, nh):
    ab = ab_ref[...]
    a = ab[:, :2 * nh] + dt_ref[...]
    softplus = jnp.maximum(a, 0.0) + jnp.log1p(jnp.exp(-jnp.abs(a)))
    la = -jnp.exp(al_ref[...]) * softplus
    beta_ref[...] = _sigmoid(ab[:, 2 * nh:])
    parts = _split3(la)
    g_pre = sum(_dot(tl_ref[...], p) for p in parts)
    g_suf = sum(_dot(tu_ref[...], p) for p in parts)
    col = lax.broadcasted_iota(jnp.int32, la.shape, 1)
    g_ref[...] = jnp.where(col < nh, g_pre, g_suf)


def _dn_gates(ab, a_log, dt_bias, nh):
    m = ab.shape[0]
    tm = _tile(m, 256, CHUNK)
    r = np.arange(tm)
    same = (r[:, None] // CHUNK) == (r[None, :] // CHUNK)
    tri_l = jnp.asarray(same & (r[None, :] <= r[:, None]), BF16)
    tri_u = jnp.asarray(same & (r[None, :] >= r[:, None]), BF16)
    return pl.pallas_call(
        functools.partial(_dn_gates_kernel, nh=nh),
        grid=(m // tm,),
        in_specs=[
            pl.BlockSpec((tm, 4 * nh), lambda i: (i, 0)),
            pl.BlockSpec((1, 2 * nh), lambda i: (0, 0)),
            pl.BlockSpec((1, 2 * nh), lambda i: (0, 0)),
            pl.BlockSpec((tm, tm), lambda i: (0, 0)),
            pl.BlockSpec((tm, tm), lambda i: (0, 0)),
        ],
        out_specs=[pl.BlockSpec((tm, 2 * nh), lambda i: (i, 0))] * 2,
        out_shape=[jax.ShapeDtypeStruct((m, 2 * nh), F32)] * 2,
        compiler_params=_cparams("parallel"),
        name="dn_gates",
    )(ab, a_log.reshape(1, 2 * nh), dt_bias.reshape(1, 2 * nh), tri_l, tri_u)


def _unit_tri_inverses(lms):
    shape = lms[0].shape
    ri = lax.broadcasted_iota(jnp.int32, shape, 0)
    ci = lax.broadcasted_iota(jnp.int32, shape, 1)
    eye = (ri == ci).astype(F32)
    pair = (ri >> 1) == (ci >> 1)
    ts = [eye - jnp.where(pair, lm, 0.0) for lm in lms]
    for lvl in range(1, _log2(shape[0])):
        cross = ((ri >> (lvl + 1)) == (ci >> (lvl + 1))) & ((ri >> lvl) != (ci >> lvl))
        offs = [jnp.where(cross, lm, 0.0).astype(BF16) for lm in lms]
        tbs = [t.astype(BF16) for t in ts]
        ps = [_dot(tb, off).astype(BF16) for tb, off in zip(tbs, offs)]
        ts = [t - _dot(p, tb) for t, p, tb in zip(ts, ps, tbs)]
    return ts


def _dn_scan_kernel(q_ref, k_ref, v_ref, gc_ref, gr_ref, b_ref, s0_ref, o_ref, s_ref, *, backward, nh, hd):
    @pl.when(pl.program_id(1) == 0)
    def _():
        s_ref[...] = s0_ref[...]

    n = CHUNK
    heads = range(nh)
    ri = lax.broadcasted_iota(jnp.int32, (n, n), 0)
    ci = lax.broadcasted_iota(jnp.int32, (n, n), 1)
    causal = (ci >= ri) if backward else (ci <= ri)
    strict = (ci > ri) if backward else (ci < ri)
    last = 0 if backward else n - 1
    dcol = nh if backward else 0
    gc_all = gc_ref[...]
    gr_all = gr_ref[0]
    b_all = b_ref[...]
    gcol = [gc_all[:, dcol + h:dcol + h + 1] for h in heads]
    grow = [gr_all[dcol + h:dcol + h + 1, :] for h in heads]
    bcol = [b_all[:, dcol + h:dcol + h + 1] for h in heads]
    glast = [g[last:last + 1, :] for g in gcol]
    q = [q_ref[:, h * hd:(h + 1) * hd] for h in heads]
    k = [k_ref[:, h * hd:(h + 1) * hd] for h in heads]
    v = [v_ref[:, h * hd:(h + 1) * hd] for h in heads]
    kk = [_dot_nt(k[h], k[h]) for h in heads]
    qk = [_dot_nt(q[h], k[h]) for h in heads]
    decay = [jnp.exp(jnp.where(causal, gcol[h] - grow[h], -jnp.inf)) for h in heads]
    lm = [jnp.where(strict, bcol[h] * kk[h] * decay[h], 0.0) for h in heads]
    intra = [(qk[h] * decay[h]).astype(BF16) for h in heads]
    t = _unit_tri_inverses(lm)
    eg = [jnp.exp(g) for g in gcol]
    rhs = [jnp.concatenate([v[h].astype(F32) * bcol[h], k[h].astype(F32) * (bcol[h] * eg[h])],
                           axis=1).astype(BF16) for h in heads]
    uw = [_dot(t[h].astype(BF16), rhs[h]) for h in heads]
    s = [s_ref[0, h] for h in heads]
    wq = [jnp.concatenate([uw[h][:, hd:], q[h].astype(F32) * eg[h]], axis=0).astype(BF16) for h in heads]
    ws = [_dot(wq[h], s[h].astype(BF16)) for h in heads]
    vb = [(uw[h][:, :hd] - ws[h][:n]).astype(BF16) for h in heads]
    kg = [(k[h].astype(F32) * jnp.exp(glast[h] - gcol[h])).astype(BF16) for h in heads]
    for h in heads:
        o_ref[:, h * hd:(h + 1) * hd] = ws[h][n:] + _dot(intra[h], vb[h])
    for h in heads:
        s_ref[0, h] = s[h] * jnp.exp(glast[h]) + _dot_tn(kg[h], vb[h])


def _dn_scan(q, k, v, g, g_rows, beta, s0, backward, batch, nh, hd):
    m = q.shape[0]
    nch = m // batch // CHUNK

    def tok(b, c):
        return (b * nch + (nch - 1 - c if backward else c), 0)

    def tok3(b, c):
        return (b * nch + (nch - 1 - c if backward else c), 0, 0)

    st = lambda b, c: (b, 0, 0, 0)
    return pl.pallas_call(
        functools.partial(_dn_scan_kernel, backward=backward, nh=nh, hd=hd),
        grid=(batch, nch),
        in_specs=[
            pl.BlockSpec((CHUNK, nh * hd), tok),
            pl.BlockSpec((CHUNK, nh * hd), tok),
            pl.BlockSpec((CHUNK, nh * hd), tok),
            pl.BlockSpec((CHUNK, 2 * nh), tok),
            pl.BlockSpec((1, 2 * nh, CHUNK), tok3),
            pl.BlockSpec((CHUNK, 2 * nh), tok),
            pl.BlockSpec((1, nh, hd, hd), st),
        ],
        out_specs=[pl.BlockSpec((CHUNK, nh * hd), tok), pl.BlockSpec((1, nh, hd, hd), st)],
        out_shape=[jax.ShapeDtypeStruct((m, nh * hd), F32), jax.ShapeDtypeStruct((batch, nh, hd, hd), F32)],
        compiler_params=_cparams("parallel", "arbitrary"),
        name="dn_scan_bwd" if backward else "dn_scan_fwd",
    )(q, k, v, g, g_rows, beta, s0)


def _dn_out_kernel(of_ref, ob_ref, gate_ref, nw_ref, e_ref, y_ref, *, hd):
    o = of_ref[...] + ob_ref[...]
    ms = _head_sumsq(o, e_ref[...]) * (1.0 / hd)
    y = o * lax.rsqrt(ms + EPS) * nw_ref[...]
    y_ref[...] = (y * _silu(gate_ref[...].astype(F32))).astype(y_ref.dtype)


def _dn_out(o_f, o_b, p, norm_w, width, hd):
    m = o_f.shape[0]
    tc = _tile(width, 512, 128)
    tm = _tile(m, 512)
    nsec = width // tc
    ones_bd = jnp.asarray(np.kron(np.eye(tc // hd), np.ones((hd, hd))), BF16)
    nw = jnp.tile(norm_w.astype(F32), tc // hd).reshape(1, tc)
    return pl.pallas_call(
        functools.partial(_dn_out_kernel, hd=hd),
        grid=(m // tm, nsec),
        in_specs=[
            pl.BlockSpec((tm, tc), lambda i, j: (i, j)),
            pl.BlockSpec((tm, tc), lambda i, j: (i, j)),
            pl.BlockSpec((tm, tc), lambda i, j: (i, 3 * nsec + j)),
            pl.BlockSpec((1, tc), lambda i, j: (0, 0)),
            pl.BlockSpec((tc, tc), lambda i, j: (0, 0)),
        ],
        out_specs=pl.BlockSpec((tm, tc), lambda i, j: (i, j)),
        out_shape=jax.ShapeDtypeStruct((m, width), BF16),
        compiler_params=_cparams("parallel", "parallel"),
        name="dn_out",
    )(o_f, o_b, p, nw, ones_bd)


def _fold_kernel(w_ref, t_ref, o_ref):
    w_hi, w_lo = _split2(w_ref[...])
    t_hi, t_lo = _split2(t_ref[0])
    o_ref[...] = (_dot(w_hi, t_hi) + _dot(w_lo, t_hi) + _dot(w_hi, t_lo)).astype(o_ref.dtype)


def _fold_channel_dft(w_in, groups):
    d, width = w_in.shape
    c = width // groups
    idx = np.arange(c)
    ang = 2.0 * np.pi * ((idx[:, None] * idx[None, :]) % c) / c
    table = jnp.asarray(np.stack([np.cos(ang), -np.sin(ang)]) / math.sqrt(c), F32)
    return pl.pallas_call(
        _fold_kernel,
        grid=(2, groups),
        in_specs=[pl.BlockSpec((d, c), lambda s, g: (0, g)), pl.BlockSpec((1, c, c), lambda s, g: (s, 0, 0))],
        out_specs=pl.BlockSpec((d, c), lambda s, g: (0, s * groups + g)),
        out_shape=jax.ShapeDtypeStruct((d, 2 * width), BF16),
        compiler_params=_cparams("parallel", "parallel"),
        name="fold_channel_dft",
    )(w_in, table)


def _dft1_kernel(zr_ref, zi_ref, f_ref, yr_ref, yi_ref):
    n1 = zr_ref.shape[1]
    z = jnp.concatenate([zr_ref[0], zi_ref[0]], axis=0)
    y = _dot(f_ref[...], z)
    yr_ref[0] = y[:n1].astype(yr_ref.dtype)
    yi_ref[0] = y[n1:].astype(yi_ref.dtype)


def _dft2_kernel(yr_ref, yi_ref, g_ref, o_ref):
    z = jnp.concatenate([yr_ref[0], yi_ref[0]], axis=0)
    o_ref[0] = _dot(g_ref[0], z).astype(o_ref.dtype)


def _position_dft_real(u, batch, width):
    t_len = u.shape[0] // batch
    n1 = 1 << (int(math.log2(t_len)) // 2)
    n2 = t_len // n1
    a1 = 2.0 * np.pi * ((np.arange(n1)[:, None] * np.arange(n1)[None, :]) % n1) / n1
    c1, s1 = np.cos(a1), np.sin(a1)
    f1m = jnp.asarray(np.block([[c1, s1], [-s1, c1]]) / math.sqrt(n1), BF16)
    f1i, f2i, t2i = np.arange(n1)[:, None, None], np.arange(n2)[None, :, None], np.arange(n2)[None, None, :]
    theta = 2.0 * np.pi * (((t2i * f1i) % t_len) / t_len + ((t2i * f2i) % n2) / n2)
    g2m = jnp.asarray(np.concatenate([np.cos(theta), np.sin(theta)], axis=2) / math.sqrt(n2), BF16)

    tn = _tile(width, 2048, 128)
    nj = width // tn
    u3 = u.reshape(batch, n1, n2 * 2 * width)
    yr, yi = pl.pallas_call(
        _dft1_kernel,
        grid=(batch, n2, nj),
        in_specs=[
            pl.BlockSpec((1, n1, tn), lambda b, t, j: (b, 0, t * 2 * nj + j)),
            pl.BlockSpec((1, n1, tn), lambda b, t, j: (b, 0, t * 2 * nj + nj + j)),
            pl.BlockSpec((2 * n1, 2 * n1), lambda b, t, j: (0, 0)),
        ],
        out_specs=[pl.BlockSpec((1, n1, tn), lambda b, t, j: (b, 0, t * nj + j))] * 2,
        out_shape=[jax.ShapeDtypeStruct((batch, n1, n2 * width), BF16)] * 2,
        compiler_params=_cparams("parallel", "parallel", "parallel"),
        name="dft_stage1",
    )(u3, u3, f1m)
    yr = yr.reshape(batch * n1, n2, width)
    yi = yi.reshape(batch * n1, n2, width)
    out = pl.pallas_call(
        _dft2_kernel,
        grid=(batch, n1),
        in_specs=[
            pl.BlockSpec((1, n2, width), lambda b, f: (b * n1 + f, 0, 0)),
            pl.BlockSpec((1, n2, width), lambda b, f: (b * n1 + f, 0, 0)),
            pl.BlockSpec((1, n2, 2 * n2), lambda b, f: (f, 0, 0)),
        ],
        out_specs=pl.BlockSpec((1, n2, width), lambda b, f: (b, 0, f)),
        out_shape=jax.ShapeDtypeStruct((batch, n2, n1 * width), BF16),
        compiler_params=_cparams("parallel", "parallel"),
        name="dft_stage2",
    )(yr, yi, g2m)
    return out.reshape(batch * t_len, width)


def _ffn_prep_kernel(x_ref, nw_ref, sh_ref, sc_ref, rw_ref, rb_ref, h_ref, comb_ref, *, n_exp):
    h = _modulated(x_ref[...], nw_ref[...], sh_ref[0], sc_ref[0])
    h_ref[...] = h.astype(h_ref.dtype)
    h_hi, h_lo = _split2(h)
    r_hi, r_lo = _split2(rw_ref[...])
    logits = _dot_nt(r_hi, h_hi) + _dot_nt(r_lo, h_hi) + _dot_nt(r_hi, h_lo)
    s = _sigmoid(logits)
    sel = s + rb_ref[...]
    srow = [s[e:e + 1, :] for e in range(n_exp)]
    row = [sel[e:e + 1, :] for e in range(n_exp)]
    epg = n_exp // N_GROUPS

    def beats(a, ia, b, ib):
        return (a >= b) if ia < ib else (a > b)

    rank, gscore = {}, []
    for g in range(N_GROUPS):
        members = range(g * epg, (g + 1) * epg)
        for i in members:
            rank[i] = sum(beats(row[j], j, row[i], i).astype(F32) for j in members if j != i)
        gscore.append(sum(jnp.where(rank[i] < GROUP_SCORE_K, row[i], 0.0) for i in members))
    picked = []
    for g in range(N_GROUPS):
        grank = sum(beats(gscore[j], j, gscore[g], g).astype(F32) for j in range(N_GROUPS) if j != g)
        for i in range(g * epg, (g + 1) * epg):
            picked.append(jnp.where((grank < 1.0) & (rank[i] < TOP_K), srow[i], 0.0))
    denom = sum(picked)
    comb_ref[...] = jnp.concatenate([p / denom for p in picked], axis=0)


def _ffn_prep(x, nw, mod, stream, layer, router_wt, router_b):
    m, d = x.shape
    n_exp = router_wt.shape[0]
    tm = _tile(min(m, stream.rows_per_mod), 512, 128)
    return pl.pallas_call(
        functools.partial(_ffn_prep_kernel, n_exp=n_exp),
        grid=(m // tm,),
        in_specs=[
            pl.BlockSpec((tm, d), lambda i: (i, 0)),
            pl.BlockSpec((1, d), lambda i: (0, 0)),
            mod.spec(stream, layer, 3, tm),
            mod.spec(stream, layer, 4, tm),
            pl.BlockSpec((n_exp, d), lambda i: (0, 0)),
            pl.BlockSpec((n_exp, 1), lambda i: (0, 0)),
        ],
        out_specs=[pl.BlockSpec((tm, d), lambda i: (i, 0)), pl.BlockSpec((n_exp, tm), lambda i: (0, i))],
        out_shape=[jax.ShapeDtypeStruct((m, d), BF16), jax.ShapeDtypeStruct((n_exp, m), F32)],
        compiler_params=_cparams("parallel"),
        name="ffn_prep",
    )(x, nw.reshape(1, d), mod.table, mod.table, router_wt, router_b.reshape(n_exp, 1))


def _moe_dense_kernel(h_ref, comb_ref, wg_ref, wu_ref, wd_ref, x_ref, g_ref, o_ref, acc_ref):
    e = pl.program_id(1)

    @pl.when(e == 0)
    def _():
        acc_ref[...] = jnp.zeros_like(acc_ref)

    h = h_ref[...]
    a = _dot(h, wg_ref[0])
    act = _silu(a) * _dot(h, wu_ref[0])
    comb = comb_ref[...]
    lane = lax.broadcasted_iota(jnp.int32, comb.shape, 1)
    ccol = jnp.sum(jnp.where(lane == e, comb, 0.0), axis=1, keepdims=True)
    acc_ref[...] += _dot((act * ccol).astype(BF16), wd_ref[0])

    @pl.when(e == pl.num_programs(1) - 1)
    def _():
        o_ref[...] = x_ref[...] + g_ref[0] * acc_ref[...]


def _moe_dense(h, comb, wg, wu, wd, x, mod, stream, layer):
    m, d = h.shape
    n_exp, _, f = wg.shape
    tm = _tile(min(m, stream.rows_per_mod), 512)
    return pl.pallas_call(
        _moe_dense_kernel,
        grid=(m // tm, n_exp),
        in_specs=[
            pl.BlockSpec((tm, d), lambda i, e: (i, 0)),
            pl.BlockSpec((tm, n_exp), lambda i, e: (i, 0)),
            pl.BlockSpec((1, d, f), lambda i, e: (e, 0, 0)),
            pl.BlockSpec((1, d, f), lambda i, e: (e, 0, 0)),
            pl.BlockSpec((1, f, d), lambda i, e: (e, 0, 0)),
            pl.BlockSpec((tm, d), lambda i, e: (i, 0)),
            mod.spec(stream, layer, 5, tm),
        ],
        out_specs=pl.BlockSpec((tm, d), lambda i, e: (i, 0)),
        out_shape=jax.ShapeDtypeStruct((m, d), F32),
        scratch_shapes=[pltpu.VMEM((tm, d), F32)],
        compiler_params=_cparams("parallel", "arbitrary"),
        name="moe_dense",
    )(h, comb, wg, wu, wd, x, mod.table)


def _deltanet_streams(h, w_in, conv_w, a_log, dt_bias, stream, nh, hd):
    width = nh * hd
    p = _matmul(h, w_in, BF16, n=4 * width)
    ab = _matmul(h, w_in[:, 4 * width:], F32, tn_pref=4 * nh)
    q = _dn_conv(p, conv_w, 0, "q", stream, width, hd)
    k = _dn_conv(p, conv_w, 1, "k", stream, width, hd)
    v = _dn_conv(p, conv_w, 2, "v", stream, width, hd)
    g, beta = _dn_gates(ab, a_log, dt_bias, nh)
    nchunks = g.shape[0] // CHUNK
    g_rows = g.reshape(nchunks, CHUNK, 2 * nh).transpose(0, 2, 1)
    return p, q, k, v, g, g_rows, beta


def kernel(x, c, ctx, c_ctx, ada_w, ada_b, norm_mix_w, norm_ffn_w, dn_w_in, dn_conv_w, dn_A_log, dn_dt_bias, dn_norm_w, dn_w_out, fn_w_in, fn_w_out, router_w, router_bias, moe_w_gate, moe_w_up, moe_w_down, final_norm_w):
    batch, seq, d = x.shape
    ctx_len = ctx.shape[1]
    depth = ada_w.shape[0]
    nh = DN_HEADS
    dn_width = dn_w_out.shape[1]
    hd = dn_width // nh
    fn_width = fn_w_in.shape[2]

    lat = _Stream(rows_per_mod=seq, mod_base=0, seq_len=seq, conv_len=GRID_W)
    cst = _Stream(rows_per_mod=batch * ctx_len, mod_base=batch, seq_len=ctx_len, conv_len=ctx_len)

    mod_rows = -(-(batch + 1) // 8) * 8
    cond = jnp.zeros((mod_rows, d), F32).at[:batch].set(c).at[batch].set(c_ctx)
    table = _ada_table(cond, ada_w, ada_b)
    mod = _Mod(table.reshape(depth * mod_rows * N_MOD, 1, d), mod_rows, d)

    router_wt = router_w.T
    xs = x.reshape(batch * seq, d)
    zs = ctx.reshape(batch * ctx_len, d)

    def channel_mixer(s, stream, layer):
        h, comb_t = _ffn_prep(s, norm_ffn_w[layer], mod, stream, layer, router_wt, router_bias)
        return _moe_dense(h, comb_t.T, moe_w_gate[layer].astype(BF16), moe_w_up[layer].astype(BF16),
                          moe_w_down[layer].astype(BF16), s, mod, stream, layer)

    for i in range(depth):
        use_dn = i % N_MIXERS == 0
        j = i // N_MIXERS
        ctx_out = i < depth - 1
        ctx_live = use_dn or ctx_out

        h_lat = _modulate(xs, norm_mix_w[i], mod, lat, i, 0, 1)
        h_ctx = _modulate(zs, norm_mix_w[i], mod, cst, i, 0, 1) if ctx_live else None
        if use_dn:
            w_in = dn_w_in[j].astype(BF16)
            w_out = dn_w_out[j].astype(BF16)
            a_log, dt_bias = dn_A_log[j].reshape(-1), dn_dt_bias[j].reshape(-1)
            cs = _deltanet_streams(h_ctx, w_in, dn_conv_w[j], a_log, dt_bias, cst, nh, hd)
            ls = _deltanet_streams(h_lat, w_in, dn_conv_w[j], a_log, dt_bias, lat, nh, hd)
            o_ctx, o_lat = [], []
            for backward in (False, True):
                s0 = jnp.zeros((batch, nh, hd, hd), F32)
                oc, s_ctx = _dn_scan(cs[1], cs[2], cs[3], cs[4], cs[5], cs[6], s0, backward, batch, nh, hd)
                ol, _ = _dn_scan(ls[1], ls[2], ls[3], ls[4], ls[5], ls[6], s_ctx, backward, batch, nh, hd)
                o_ctx.append(oc)
                o_lat.append(ol)
            y_lat = _dn_out(o_lat[0], o_lat[1], ls[0], dn_norm_w[j], dn_width, hd)
            xs = _matmul_gated_residual(y_lat, w_out, xs, mod, lat, i, 2)
            if ctx_out:
                y_ctx = _dn_out(o_ctx[0], o_ctx[1], cs[0], dn_norm_w[j], dn_width, hd)
                z_mix = _matmul_gated_residual(y_ctx, w_out, zs, mod, cst, i, 2)
        else:
            w_fold = _fold_channel_dft(fn_w_in[j], FN_GROUPS)
            w_out = fn_w_out[j].astype(BF16)
            mixed = _position_dft_real(_matmul(h_lat, w_fold, BF16), batch, fn_width)
            xs = _matmul_gated_residual(mixed, w_out, xs, mod, lat, i, 2)
            if ctx_out:
                mixed_c = _position_dft_real(_matmul(h_ctx, w_fold, BF16), batch, fn_width)
                z_mix = _matmul_gated_residual(mixed_c, w_out, zs, mod, cst, i, 2)

        xs = channel_mixer(xs, lat, i)
        if ctx_out:
            zs = channel_mixer(z_mix, cst, i)

    return _final_norm(xs, final_norm_w).reshape(batch, seq, d)
```

```python
import functools
import math

import numpy as np
import jax
import jax.numpy as jnp
from jax import lax
from jax.experimental import pallas as pl
from jax.experimental.pallas import tpu as pltpu

F32 = jnp.float32
BF16 = jnp.bfloat16

GRID_W = 64
CHUNK = 64
CONV_W = 5
DN_HEADS = 16
FN_GROUPS = 4
N_GROUPS = 4
GROUP_SCORE_K = 2
TOP_K = 2
N_MOD = 6
N_MIXERS = 2
EPS = 1e-6

VMEM_LIMIT = 56 * 1024 * 1024


def _cparams(*sem):
    return pltpu.CompilerParams(dimension_semantics=sem, vmem_limit_bytes=VMEM_LIMIT)


def _dot(a, b):
    return jnp.dot(a, b, preferred_element_type=F32)


def _dot_nt(a, b):
    return lax.dot_general(a, b, (((1,), (1,)), ((), ())), preferred_element_type=F32)


def _dot_tn(a, b):
    return lax.dot_general(a, b, (((0,), (0,)), ((), ())), preferred_element_type=F32)


def _split2(x):
    hi = x.astype(BF16)
    lo = (x - hi.astype(F32)).astype(BF16)
    return hi, lo


def _split3(x):
    hi = x.astype(BF16)
    r = x - hi.astype(F32)
    mid = r.astype(BF16)
    lo = (r - mid.astype(F32)).astype(BF16)
    return hi, mid, lo


def _silu(x):
    return x / (1.0 + jnp.exp(-x))


def _sigmoid(x):
    return 1.0 / (1.0 + jnp.exp(-x))


def _tile(n, pref, mult=8):
    if n <= pref:
        return n
    t = (pref // mult) * mult
    while t >= mult:
        if n % t == 0:
            return t
        t -= mult
    return n


class _Stream:
    def __init__(self, rows_per_mod, mod_base, seq_len, conv_len):
        self.rows_per_mod = rows_per_mod
        self.mod_base = mod_base
        self.seq_len = seq_len
        self.conv_len = conv_len


def _log2(n):
    assert n & (n - 1) == 0, n
    return n.bit_length() - 1


def _ada_kernel(c_ref, w_ref, b_ref, o_ref):
    s_hi, s_lo = _split2(_silu(c_ref[...]))
    w_hi, w_lo = _split2(w_ref[0])
    acc = _dot(s_hi, w_hi) + _dot(s_lo, w_hi) + _dot(s_hi, w_lo)
    o_ref[0] = acc + b_ref[0]


def _ada_table(cond, ada_w, ada_b):
    depth, d, n = ada_w.shape
    r = cond.shape[0]
    tn = _tile(n, 512, 128)
    return pl.pallas_call(
        _ada_kernel,
        grid=(depth, n // tn),
        in_specs=[
            pl.BlockSpec((r, d), lambda l, j: (0, 0)),
            pl.BlockSpec((1, d, tn), lambda l, j: (l, 0, j)),
            pl.BlockSpec((1, 1, tn), lambda l, j: (l, 0, j)),
        ],
        out_specs=pl.BlockSpec((1, r, tn), lambda l, j: (l, 0, j)),
        out_shape=jax.ShapeDtypeStruct((depth, r, n), F32),
        compiler_params=_cparams("parallel", "parallel"),
        name="ada_table",
    )(cond, ada_w, ada_b.reshape(depth, 1, n))


def _modulated(x, nw, shift, scale):
    ms = jnp.mean(x * x, axis=-1, keepdims=True)
    return (x * lax.rsqrt(ms + EPS) * nw) * (1.0 + scale) + shift


def _modulate_kernel(x_ref, nw_ref, sh_ref, sc_ref, o_ref):
    o_ref[...] = _modulated(x_ref[...], nw_ref[...], sh_ref[0], sc_ref[0]).astype(o_ref.dtype)


class _Mod:
    def __init__(self, table3, mod_rows, d):
        self.table = table3
        self.mod_rows = mod_rows
        self.d = d

    def spec(self, stream, layer, k, tm, width=None, col=None):
        width = self.d if width is None else width
        rows, base, mr = stream.rows_per_mod, stream.mod_base, self.mod_rows

        def index_map(*g):
            row = base + (g[0] * tm) // rows
            return ((layer * mr + row) * N_MOD + k, 0, 0 if col is None else col(*g))
        return pl.BlockSpec((1, 1, width), index_map)


def _modulate(x, nw, mod, stream, layer, k_shift, k_scale):
    m, d = x.shape
    tm = _tile(min(m, stream.rows_per_mod), 512)
    return pl.pallas_call(
        _modulate_kernel,
        grid=(m // tm,),
        in_specs=[
            pl.BlockSpec((tm, d), lambda i: (i, 0)),
            pl.BlockSpec((1, d), lambda i: (0, 0)),
            mod.spec(stream, layer, k_shift, tm),
            mod.spec(stream, layer, k_scale, tm),
        ],
        out_specs=pl.BlockSpec((tm, d), lambda i: (i, 0)),
        out_shape=jax.ShapeDtypeStruct((m, d), BF16),
        compiler_params=_cparams("parallel"),
        name="modulate",
    )(x, nw.reshape(1, d), mod.table, mod.table)


def _final_norm_kernel(x_ref, nw_ref, o_ref):
    x = x_ref[...]
    ms = jnp.mean(x * x, axis=-1, keepdims=True)
    o_ref[...] = x * lax.rsqrt(ms + EPS) * nw_ref[...]


def _final_norm(x, nw):
    m, d = x.shape
    tm = _tile(m, 512)
    return pl.pallas_call(
        _final_norm_kernel,
        grid=(m // tm,),
        in_specs=[pl.BlockSpec((tm, d), lambda i: (i, 0)), pl.BlockSpec((1, d), lambda i: (0, 0))],
        out_specs=pl.BlockSpec((tm, d), lambda i: (i, 0)),
        out_shape=jax.ShapeDtypeStruct((m, d), F32),
        compiler_params=_cparams("parallel"),
        name="final_norm",
    )(x, nw.reshape(1, d))


def _matmul_kernel(a_ref, w_ref, o_ref):
    o_ref[...] = _dot(a_ref[...], w_ref[...]).astype(o_ref.dtype)


def _matmul(a, w, out_dtype, tm_pref=1024, tn_pref=512, col0=0, n=None):
    m, k = a.shape
    n = w.shape[1] if n is None else n
    tm, tn = _tile(m, tm_pref), _tile(n, tn_pref, 128)
    assert col0 % tn == 0
    cb = col0 // tn
    return pl.pallas_call(
        _matmul_kernel,
        grid=(m // tm, n // tn),
        in_specs=[pl.BlockSpec((tm, k), lambda i, j: (i, 0)), pl.BlockSpec((k, tn), lambda i, j: (0, cb + j))],
        out_specs=pl.BlockSpec((tm, tn), lambda i, j: (i, j)),
        out_shape=jax.ShapeDtypeStruct((m, n), out_dtype),
        compiler_params=_cparams("parallel", "parallel"),
        name="matmul",
    )(a, w)


def _matmul_res_kernel(a_ref, w_ref, x_ref, g_ref, o_ref):
    o_ref[...] = x_ref[...] + g_ref[0] * _dot(a_ref[...], w_ref[...])


def _matmul_gated_residual(a, w, x, mod, stream, layer, k_gate, tm_pref=1024, tn_pref=512):
    m, k = a.shape
    n = w.shape[1]
    tm = _tile(min(m, stream.rows_per_mod), tm_pref)
    tn = _tile(n, tn_pref, 128)
    return pl.pallas_call(
        _matmul_res_kernel,
        grid=(m // tm, n // tn),
        in_specs=[
            pl.BlockSpec((tm, k), lambda i, j: (i, 0)),
            pl.BlockSpec((k, tn), lambda i, j: (0, j)),
            pl.BlockSpec((tm, tn), lambda i, j: (i, j)),
            mod.spec(stream, layer, k_gate, tm, width=tn, col=lambda i, j: j),
        ],
        out_specs=pl.BlockSpec((tm, tn), lambda i, j: (i, j)),
        out_shape=jax.ShapeDtypeStruct((m, n), F32),
        compiler_params=_cparams("parallel", "parallel"),
        name="matmul_gated_residual",
    )(a, w, x, mod.table)


def _head_sumsq(y, ones_bd):
    sq_hi, sq_lo = _split2(y * y)
    return _dot(sq_hi, ones_bd) + _dot(sq_lo, ones_bd)


def _dn_conv_kernel(u_ref, cw_ref, e_ref, o_ref, *, conv_len, mode, head_dim):
    u = u_ref[...].astype(F32)
    tm = u.shape[0]
    pos = lax.broadcasted_iota(jnp.int32, u.shape, 0) & (conv_len - 1)
    pad = CONV_W // 2
    acc = u * cw_ref[pad:pad + 1, :]
    for j in range(CONV_W):
        d = j - pad
        if d == 0:
            continue
        shifted = pltpu.roll(u, (-d) % tm, 0)
        ok = (pos + d >= 0) & (pos + d < conv_len)
        acc = acc + jnp.where(ok, shifted, 0.0) * cw_ref[j:j + 1, :]
    y = _silu(acc)
    if mode != "v":
        y = y * lax.rsqrt(_head_sumsq(y, e_ref[...]) + EPS)
        if mode == "q":
            y = y * (head_dim ** -0.5)
    o_ref[...] = y.astype(o_ref.dtype)


def _dn_conv(p, conv_w, section, mode, stream, width, head_dim):
    m = p.shape[0]
    tc = _tile(width, 512, 128)
    tm = _tile(min(m, max(stream.conv_len, 256)), 256, stream.conv_len)
    ones_bd = jnp.asarray(np.kron(np.eye(tc // head_dim), np.ones((head_dim, head_dim))), BF16)
    nsec = width // tc
    return pl.pallas_call(
        functools.partial(_dn_conv_kernel, conv_len=stream.conv_len, mode=mode, head_dim=head_dim),
        grid=(m // tm, nsec),
        in_specs=[
            pl.BlockSpec((tm, tc), lambda i, j: (i, section * nsec + j)),
            pl.BlockSpec((CONV_W, tc), lambda i, j: (0, section * nsec + j)),
            pl.BlockSpec((tc, tc), lambda i, j: (0, 0)),
        ],
        out_specs=pl.BlockSpec((tm, tc), lambda i, j: (i, j)),
        out_shape=jax.ShapeDtypeStruct((m, width), BF16),
        compiler_params=_cparams("parallel", "parallel"),
        name="dn_conv_" + mode,
    )(p, conv_w, ones_bd)


def _dn_gates_kernel(ab_ref, al_ref, dt_ref, tl_ref, tu_ref, g_ref, beta_ref, *, nh):
    ab = ab_ref[...]
    a = ab[:, :2 * nh] + dt_ref[...]
    softplus = jnp.maximum(a, 0.0) + jnp.log1p(jnp.exp(-jnp.abs(a)))
    la = -jnp.exp(al_ref[...]) * softplus
    beta_ref[...] = _sigmoid(ab[:, 2 * nh:])
    parts = _split3(la)
    g_pre = sum(_dot(tl_ref[...], p) for p in parts)
    g_suf = sum(_dot(tu_ref[...], p) for p in parts)
    col = lax.broadcasted_iota(jnp.int32, la.shape, 1)
    g_ref[...] = jnp.where(col < nh, g_pre, g_suf)


def _dn_gates(ab, a_log, dt_bias, nh):
    m = ab.shape[0]
    tm = _tile(m, 256, CHUNK)
    r = np.arange(tm)
    same = (r[:, None] // CHUNK) == (r[None, :] // CHUNK)
    tri_l = jnp.asarray(same & (r[None, :] <= r[:, None]), BF16)
    tri_u = jnp.asarray(same & (r[None, :] >= r[:, None]), BF16)
    return pl.pallas_call(
        functools.partial(_dn_gates_kernel, nh=nh),
        grid=(m // tm,),
        in_specs=[
            pl.BlockSpec((tm, 4 * nh), lambda i: (i, 0)),
            pl.BlockSpec((1, 2 * nh), lambda i: (0, 0)),
            pl.BlockSpec((1, 2 * nh), lambda i: (0, 0)),
            pl.BlockSpec((tm, tm), lambda i: (0, 0)),
            pl.BlockSpec((tm, tm), lambda i: (0, 0)),
        ],
        out_specs=[pl.BlockSpec((tm, 2 * nh), lambda i: (i, 0))] * 2,
        out_shape=[jax.ShapeDtypeStruct((m, 2 * nh), F32)] * 2,
        compiler_params=_cparams("parallel"),
        name="dn_gates",
    )(ab, a_log.reshape(1, 2 * nh), dt_bias.reshape(1, 2 * nh), tri_l, tri_u)


def _unit_tri_inverses(lms):
    shape = lms[0].shape
    ri = lax.broadcasted_iota(jnp.int32, shape, 0)
    ci = lax.broadcasted_iota(jnp.int32, shape, 1)
    eye = (ri == ci).astype(F32)
    pair = (ri >> 1) == (ci >> 1)
    ts = [eye - jnp.where(pair, lm, 0.0) for lm in lms]
    for lvl in range(1, _log2(shape[0])):
        cross = ((ri >> (lvl + 1)) == (ci >> (lvl + 1))) & ((ri >> lvl) != (ci >> lvl))
        offs = [jnp.where(cross, lm, 0.0).astype(BF16) for lm in lms]
        tbs = [t.astype(BF16) for t in ts]
        ps = [_dot(tb, off).astype(BF16) for tb, off in zip(tbs, offs)]
        ts = [t - _dot(p, tb) for t, p, tb in zip(ts, ps, tbs)]
    return ts


def _dn_scan_kernel(q_ref, k_ref, v_ref, gc_ref, gr_ref, b_ref, s0_ref, o_ref, s_ref, *, backward, nh, hd):
    @pl.when(pl.program_id(1) == 0)
    def _():
        s_ref[...] = s0_ref[...]

    n = CHUNK
    heads = range(nh)
    ri = lax.broadcasted_iota(jnp.int32, (n, n), 0)
    ci = lax.broadcasted_iota(jnp.int32, (n, n), 1)
    causal = (ci >= ri) if backward else (ci <= ri)
    strict = (ci > ri) if backward else (ci < ri)
    last = 0 if backward else n - 1
    dcol = nh if backward else 0
    gc_all = gc_ref[...]
    gr_all = gr_ref[0]
    b_all = b_ref[...]
    gcol = [gc_all[:, dcol + h:dcol + h + 1] for h in heads]
    grow = [gr_all[dcol + h:dcol + h + 1, :] for h in heads]
    bcol = [b_all[:, dcol + h:dcol + h + 1] for h in heads]
    glast = [g[last:last + 1, :] for g in gcol]
    q = [q_ref[:, h * hd:(h + 1) * hd] for h in heads]
    k = [k_ref[:, h * hd:(h + 1) * hd] for h in heads]
    v = [v_ref[:, h * hd:(h + 1) * hd] for h in heads]
    kk = [_dot_nt(k[h], k[h]) for h in heads]
    qk = [_dot_nt(q[h], k[h]) for h in heads]
    decay = [jnp.exp(jnp.where(causal, gcol[h] - grow[h], -jnp.inf)) for h in heads]
    lm = [jnp.where(strict, bcol[h] * kk[h] * decay[h], 0.0) for h in heads]
    intra = [(qk[h] * decay[h]).astype(BF16) for h in heads]
    t = _unit_tri_inverses(lm)
    eg = [jnp.exp(g) for g in gcol]
    rhs = [jnp.concatenate([v[h].astype(F32) * bcol[h], k[h].astype(F32) * (bcol[h] * eg[h])],
                           axis=1).astype(BF16) for h in heads]
    uw = [_dot(t[h].astype(BF16), rhs[h]) for h in heads]
    s = [s_ref[0, h] for h in heads]
    wq = [jnp.concatenate([uw[h][:, hd:], q[h].astype(F32) * eg[h]], axis=0).astype(BF16) for h in heads]
    ws = [_dot(wq[h], s[h].astype(BF16)) for h in heads]
    vb = [(uw[h][:, :hd] - ws[h][:n]).astype(BF16) for h in heads]
    kg = [(k[h].astype(F32) * jnp.exp(glast[h] - gcol[h])).astype(BF16) for h in heads]
    for h in heads:
        o_ref[:, h * hd:(h + 1) * hd] = ws[h][n:] + _dot(intra[h], vb[h])
    for h in heads:
        s_ref[0, h] = s[h] * jnp.exp(glast[h]) + _dot_tn(kg[h], vb[h])


def _dn_scan(q, k, v, g, g_rows, beta, s0, backward, batch, nh, hd):
    m = q.shape[0]
    nch = m // batch // CHUNK

    def tok(b, c):
        return (b * nch + (nch - 1 - c if backward else c), 0)

    def tok3(b, c):
        return (b * nch + (nch - 1 - c if backward else c), 0, 0)

    st = lambda b, c: (b, 0, 0, 0)
    return pl.pallas_call(
        functools.partial(_dn_scan_kernel, backward=backward, nh=nh, hd=hd),
        grid=(batch, nch),
        in_specs=[
            pl.BlockSpec((CHUNK, nh * hd), tok),
            pl.BlockSpec((CHUNK, nh * hd), tok),
            pl.BlockSpec((CHUNK, nh * hd), tok),
            pl.BlockSpec((CHUNK, 2 * nh), tok),
            pl.BlockSpec((1, 2 * nh, CHUNK), tok3),
            pl.BlockSpec((CHUNK, 2 * nh), tok),
            pl.BlockSpec((1, nh, hd, hd), st),
        ],
        out_specs=[pl.BlockSpec((CHUNK, nh * hd), tok), pl.BlockSpec((1, nh, hd, hd), st)],
        out_shape=[jax.ShapeDtypeStruct((m, nh * hd), F32), jax.ShapeDtypeStruct((batch, nh, hd, hd), F32)],
        compiler_params=_cparams("parallel", "arbitrary"),
        name="dn_scan_bwd" if backward else "dn_scan_fwd",
    )(q, k, v, g, g_rows, beta, s0)


def _dn_out_kernel(of_ref, ob_ref, gate_ref, nw_ref, e_ref, y_ref, *, hd):
    o = of_ref[...] + ob_ref[...]
    ms = _head_sumsq(o, e_ref[...]) * (1.0 / hd)
    y = o * lax.rsqrt(ms + EPS) * nw_ref[...]
    y_ref[...] = (y * _silu(gate_ref[...].astype(F32))).astype(y_ref.dtype)


def _dn_out(o_f, o_b, p, norm_w, width, hd):
    m = o_f.shape[0]
    tc = _tile(width, 512, 128)
    tm = _tile(m, 512)
    nsec = width // tc
    ones_bd = jnp.asarray(np.kron(np.eye(tc // hd), np.ones((hd, hd))), BF16)
    nw = jnp.tile(norm_w.astype(F32), tc // hd).reshape(1, tc)
    return pl.pallas_call(
        functools.partial(_dn_out_kernel, hd=hd),
        grid=(m // tm, nsec),
        in_specs=[
            pl.BlockSpec((tm, tc), lambda i, j: (i, j)),
            pl.BlockSpec((tm, tc), lambda i, j: (i, j)),
            pl.BlockSpec((tm, tc), lambda i, j: (i, 3 * nsec + j)),
            pl.BlockSpec((1, tc), lambda i, j: (0, 0)),
            pl.BlockSpec((tc, tc), lambda i, j: (0, 0)),
        ],
        out_specs=pl.BlockSpec((tm, tc), lambda i, j: (i, j)),
        out_shape=jax.ShapeDtypeStruct((m, width), BF16),
        compiler_params=_cparams("parallel", "parallel"),
        name="dn_out",
    )(o_f, o_b, p, nw, ones_bd)


def _fold_kernel(w_ref, t_ref, o_ref):
    w_hi, w_lo = _split2(w_ref[...])
    t_hi, t_lo = _split2(t_ref[0])
    o_ref[...] = (_dot(w_hi, t_hi) + _dot(w_lo, t_hi) + _dot(w_hi, t_lo)).astype(o_ref.dtype)


def _fold_channel_dft(w_in, groups):
    d, width = w_in.shape
    c = width // groups
    idx = np.arange(c)
    ang = 2.0 * np.pi * ((idx[:, None] * idx[None, :]) % c) / c
    table = jnp.asarray(np.stack([np.cos(ang), -np.sin(ang)]) / math.sqrt(c), F32)
    return pl.pallas_call(
        _fold_kernel,
        grid=(2, groups),
        in_specs=[pl.BlockSpec((d, c), lambda s, g: (0, g)), pl.BlockSpec((1, c, c), lambda s, g: (s, 0, 0))],
        out_specs=pl.BlockSpec((d, c), lambda s, g: (0, s * groups + g)),
        out_shape=jax.ShapeDtypeStruct((d, 2 * width), BF16),
        compiler_params=_cparams("parallel", "parallel"),
        name="fold_channel_dft",
    )(w_in, table)


def _dft1_kernel(zr_ref, zi_ref, f_ref, yr_ref, yi_ref):
    n1 = zr_ref.shape[1]
    z = jnp.concatenate([zr_ref[0], zi_ref[0]], axis=0)
    y = _dot(f_ref[...], z)
    yr_ref[0] = y[:n1].astype(yr_ref.dtype)
    yi_ref[0] = y[n1:].astype(yi_ref.dtype)


def _dft2_kernel(yr_ref, yi_ref, g_ref, o_ref):
    z = jnp.concatenate([yr_ref[0], yi_ref[0]], axis=0)
    o_ref[0] = _dot(g_ref[0], z).astype(o_ref.dtype)


def _position_dft_real(u, batch, width):
    t_len = u.shape[0] // batch
    n1 = 1 << (int(math.log2(t_len)) // 2)
    n2 = t_len // n1
    a1 = 2.0 * np.pi * ((np.arange(n1)[:, None] * np.arange(n1)[None, :]) % n1) / n1
    c1, s1 = np.cos(a1), np.sin(a1)
    f1m = jnp.asarray(np.block([[c1, s1], [-s1, c1]]) / math.sqrt(n1), BF16)
    f1i, f2i, t2i = np.arange(n1)[:, None, None], np.arange(n2)[None, :, None], np.arange(n2)[None, None, :]
    theta = 2.0 * np.pi * (((t2i * f1i) % t_len) / t_len + ((t2i * f2i) % n2) / n2)
    g2m = jnp.asarray(np.concatenate([np.cos(theta), np.sin(theta)], axis=2) / math.sqrt(n2), BF16)

    tn = _tile(width, 2048, 128)
    nj = width // tn
    u3 = u.reshape(batch, n1, n2 * 2 * width)
    yr, yi = pl.pallas_call(
        _dft1_kernel,
        grid=(batch, n2, nj),
        in_specs=[
            pl.BlockSpec((1, n1, tn), lambda b, t, j: (b, 0, t * 2 * nj + j)),
            pl.BlockSpec((1, n1, tn), lambda b, t, j: (b, 0, t * 2 * nj + nj + j)),
            pl.BlockSpec((2 * n1, 2 * n1), lambda b, t, j: (0, 0)),
        ],
        out_specs=[pl.BlockSpec((1, n1, tn), lambda b, t, j: (b, 0, t * nj + j))] * 2,
        out_shape=[jax.ShapeDtypeStruct((batch, n1, n2 * width), BF16)] * 2,
        compiler_params=_cparams("parallel", "parallel", "parallel"),
        name="dft_stage1",
    )(u3, u3, f1m)
    yr = yr.reshape(batch * n1, n2, width)
    yi = yi.reshape(batch * n1, n2, width)
    out = pl.pallas_call(
        _dft2_kernel,
        grid=(batch, n1),
        in_specs=[
            pl.BlockSpec((1, n2, width), lambda b, f: (b * n1 + f, 0, 0)),
            pl.BlockSpec((1, n2, width), lambda b, f: (b * n1 + f, 0, 0)),
            pl.BlockSpec((1, n2, 2 * n2), lambda b, f: (f, 0, 0)),
        ],
        out_specs=pl.BlockSpec((1, n2, width), lambda b, f: (b, 0, f)),
        out_shape=jax.ShapeDtypeStruct((batch, n2, n1 * width), BF16),
        compiler_params=_cparams("parallel", "parallel"),
        name="dft_stage2",
    )(yr, yi, g2m)
    return out.reshape(batch * t_len, width)


def _ffn_prep_kernel(x_ref, nw_ref, sh_ref, sc_ref, rw_ref, rb_ref, h_ref, comb_ref, *, n_exp):
    h = _modulated(x_ref[...], nw_ref[...], sh_ref[0], sc_ref[0])
    h_ref[...] = h.astype(h_ref.dtype)
    h_hi, h_lo = _split2(h)
    r_hi, r_lo = _split2(rw_ref[...])
    logits = _dot_nt(r_hi, h_hi) + _dot_nt(r_lo, h_hi) + _dot_nt(r_hi, h_lo)
    s = _sigmoid(logits)
    sel = s + rb_ref[...]
    srow = [s[e:e + 1, :] for e in range(n_exp)]
    row = [sel[e:e + 1, :] for e in range(n_exp)]
    epg = n_exp // N_GROUPS

    def beats(a, ia, b, ib):
        return (a >= b) if ia < ib else (a > b)

    rank, gscore = {}, []
    for g in range(N_GROUPS):
        members = range(g * epg, (g + 1) * epg)
        for i in members:
            rank[i] = sum(beats(row[j], j, row[i], i).astype(F32) for j in members if j != i)
        gscore.append(sum(jnp.where(rank[i] < GROUP_SCORE_K, row[i], 0.0) for i in members))
    picked = []
    for g in range(N_GROUPS):
        grank = sum(beats(gscore[j], j, gscore[g], g).astype(F32) for j in range(N_GROUPS) if j != g)
        for i in range(g * epg, (g + 1) * epg):
            picked.append(jnp.where((grank < 1.0) & (rank[i] < TOP_K), srow[i], 0.0))
    denom = sum(picked)
    comb_ref[...] = jnp.concatenate([p / denom for p in picked], axis=0)


def _ffn_prep(x, nw, mod, stream, layer, router_wt, router_b):
    m, d = x.shape
    n_exp = router_wt.shape[0]
    tm = _tile(min(m, stream.rows_per_mod), 512, 128)
    return pl.pallas_call(
        functools.partial(_ffn_prep_kernel, n_exp=n_exp),
        grid=(m // tm,),
        in_specs=[
            pl.BlockSpec((tm, d), lambda i: (i, 0)),
            pl.BlockSpec((1, d), lambda i: (0, 0)),
            mod.spec(stream, layer, 3, tm),
            mod.spec(stream, layer, 4, tm),
            pl.BlockSpec((n_exp, d), lambda i: (0, 0)),
            pl.BlockSpec((n_exp, 1), lambda i: (0, 0)),
        ],
        out_specs=[pl.BlockSpec((tm, d), lambda i: (i, 0)), pl.BlockSpec((n_exp, tm), lambda i: (0, i))],
        out_shape=[jax.ShapeDtypeStruct((m, d), F32), jax.ShapeDtypeStruct((n_exp, m), F32)],
        compiler_params=_cparams("parallel"),
        name="ffn_prep",
    )(x, nw.reshape(1, d), mod.table, mod.table, router_wt, router_b.reshape(n_exp, 1))


def _route(comb_t, tg):
    assert TOP_K == 2
    n_exp, n = comb_t.shape
    i32 = jnp.int32
    mask = comb_t > 0.0
    cnt = jnp.sum(mask.astype(i32), axis=1)
    padded = ((cnt + tg - 1) // tg) * tg
    ends = jnp.cumsum(padded)
    dest = (ends - padded)[:, None] + jnp.cumsum(mask.astype(i32), axis=1) - 1
    eidx = jnp.arange(n_exp, dtype=i32)[:, None]
    first = jnp.min(jnp.where(mask, eidx, n_exp), axis=0)
    final = jnp.max(jnp.where(mask, eidx, -1), axis=0)
    is_a = mask & (eidx == first)
    is_b = mask & (eidx == final) & (final != first)
    pick = lambda sel, val: jnp.sum(jnp.where(sel, val, 0), axis=0)
    d_a, d_b = pick(is_a, dest), pick(is_b, dest)
    w_ab = jnp.stack([pick(is_a, comb_t), pick(is_b, comb_t)], axis=1)
    p_rows = TOP_K * n + n_exp * tg
    tok = jnp.arange(n, dtype=i32)
    src = jnp.zeros((p_rows,), i32)
    src = src.at[jnp.where(jnp.any(is_a, axis=0), d_a, p_rows)].set(tok, mode="drop")
    src = src.at[jnp.where(jnp.any(is_b, axis=0), d_b, p_rows)].set(tok, mode="drop")
    tile_start = jnp.arange(p_rows // tg, dtype=i32) * tg
    tile_exp = jnp.minimum(jnp.searchsorted(ends, tile_start, side="right"), n_exp - 1).astype(i32)
    tile_ok = (tile_start < ends[-1]).astype(i32)
    return src, tile_exp, tile_ok, d_a.astype(i32), d_b.astype(i32), w_ab


def _row_copies(idx_ref, base, n_rows, src_hbm, dst_buf, sem, wait):
    def body(r, carry):
        cp = pltpu.make_async_copy(src_hbm.at[pl.ds(idx_ref[base + r], 1)], dst_buf.at[pl.ds(r, 1)], sem)
        if wait:
            cp.wait()
        else:
            cp.start()
        return carry
    lax.fori_loop(0, n_rows, body, 0, unroll=8)


def _moe_expert_kernel(texp_ref, tval_ref, src_ref, h_hbm, wg_ref, wu_ref, wd_ref, y_ref, xbuf, sem, *, tg):
    i = pl.program_id(0)

    def fetch(tile, slot, wait):
        @pl.when(tval_ref[tile] != 0)
        def _():
            _row_copies(src_ref, tile * tg, tg, h_hbm, xbuf.at[slot], sem.at[slot], wait)

    @pl.when(i == 0)
    def _():
        fetch(0, 0, False)

    @pl.when(i + 1 < pl.num_programs(0))
    def _():
        fetch(i + 1, (i + 1) % 2, False)

    fetch(i, i % 2, True)

    @pl.when(tval_ref[i] != 0)
    def _():
        x = xbuf[i % 2].astype(BF16)
        act = (_silu(_dot(x, wg_ref[0])) * _dot(x, wu_ref[0])).astype(BF16)
        y_ref[...] = _dot(act, wd_ref[0])

    @pl.when(tval_ref[i] == 0)
    def _():
        y_ref[...] = jnp.zeros_like(y_ref)


def _moe_experts(h, src, tile_exp, tile_ok, wg, wu, wd, tg):
    n, d = h.shape
    _, _, f = wg.shape
    p_rows = src.shape[0]
    wmap = lambda i, te, ok, s: (te[i], 0, 0)
    return pl.pallas_call(
        functools.partial(_moe_expert_kernel, tg=tg),
        grid_spec=pltpu.PrefetchScalarGridSpec(
            num_scalar_prefetch=3,
            grid=(p_rows // tg,),
            in_specs=[
                pl.BlockSpec(memory_space=pl.ANY),
                pl.BlockSpec((1, d, f), wmap),
                pl.BlockSpec((1, d, f), wmap),
                pl.BlockSpec((1, f, d), wmap),
            ],
            out_specs=pl.BlockSpec((tg, d), lambda i, te, ok, s: (i, 0)),
            scratch_shapes=[pltpu.VMEM((2, tg, d), F32), pltpu.SemaphoreType.DMA((2,))],
        ),
        out_shape=jax.ShapeDtypeStruct((p_rows, d), F32),
        compiler_params=_cparams("arbitrary"),
        name="moe_experts",
    )(tile_exp, tile_ok, src, h, wg, wu, wd)


def _moe_combine_kernel(da_ref, db_ref, y_hbm, x_ref, w_ref, g_ref, o_ref, ya, yb, sem, *, tm):
    i = pl.program_id(0)

    def fetch(tile, slot, wait):
        _row_copies(da_ref, tile * tm, tm, y_hbm, ya.at[slot], sem.at[0, slot], wait)
        _row_copies(db_ref, tile * tm, tm, y_hbm, yb.at[slot], sem.at[1, slot], wait)

    @pl.when(i == 0)
    def _():
        fetch(0, 0, False)

    @pl.when(i + 1 < pl.num_programs(0))
    def _():
        fetch(i + 1, (i + 1) % 2, False)

    fetch(i, i % 2, True)
    w = w_ref[...]
    mix = w[:, 0:1] * ya[i % 2] + w[:, 1:2] * yb[i % 2]
    o_ref[...] = x_ref[...] + g_ref[0] * mix


def _moe_combine(y, d_a, d_b, w_ab, x, mod, stream, layer):
    m, d = x.shape
    tm = _tile(min(m, stream.rows_per_mod), 256)
    return pl.pallas_call(
        functools.partial(_moe_combine_kernel, tm=tm),
        grid_spec=pltpu.PrefetchScalarGridSpec(
            num_scalar_prefetch=2,
            grid=(m // tm,),
            in_specs=[
                pl.BlockSpec(memory_space=pl.ANY),
                pl.BlockSpec((tm, d), lambda i, a, b: (i, 0)),
                pl.BlockSpec((tm, 2), lambda i, a, b: (i, 0)),
                mod.spec(stream, layer, 5, tm),
            ],
            out_specs=pl.BlockSpec((tm, d), lambda i, a, b: (i, 0)),
            scratch_shapes=[pltpu.VMEM((2, tm, d), F32), pltpu.VMEM((2, tm, d), F32),
                            pltpu.SemaphoreType.DMA((2, 2))],
        ),
        out_shape=jax.ShapeDtypeStruct((m, d), F32),
        compiler_params=_cparams("arbitrary"),
        name="moe_combine",
    )(d_a, d_b, y, x, w_ab, mod.table)


def _moe(h, comb_t, wg, wu, wd, x, mod, stream, layer):
    tg = 256 if h.shape[0] * TOP_K >= 256 * wg.shape[0] else 64
    src, tile_exp, tile_ok, d_a, d_b, w_ab = _route(comb_t, tg)
    y = _moe_experts(h, src, tile_exp, tile_ok, wg, wu, wd, tg)
    return _moe_combine(y, d_a, d_b, w_ab, x, mod, stream, layer)


def _deltanet_streams(h, w_in, conv_w, a_log, dt_bias, stream, nh, hd):
    width = nh * hd
    p = _matmul(h, w_in, BF16, n=4 * width)
    ab = _matmul(h, w_in[:, 4 * width:], F32, tn_pref=4 * nh)
    q = _dn_conv(p, conv_w, 0, "q", stream, width, hd)
    k = _dn_conv(p, conv_w, 1, "k", stream, width, hd)
    v = _dn_conv(p, conv_w, 2, "v", stream, width, hd)
    g, beta = _dn_gates(ab, a_log, dt_bias, nh)
    nchunks = g.shape[0] // CHUNK
    g_rows = g.reshape(nchunks, CHUNK, 2 * nh).transpose(0, 2, 1)
    return p, q, k, v, g, g_rows, beta


def kernel(x, c, ctx, c_ctx, ada_w, ada_b, norm_mix_w, norm_ffn_w, dn_w_in, dn_conv_w, dn_A_log, dn_dt_bias, dn_norm_w, dn_w_out, fn_w_in, fn_w_out, router_w, router_bias, moe_w_gate, moe_w_up, moe_w_down, final_norm_w):
    batch, seq, d = x.shape
    ctx_len = ctx.shape[1]
    depth = ada_w.shape[0]
    nh = DN_HEADS
    dn_width = dn_w_out.shape[1]
    hd = dn_width // nh
    fn_width = fn_w_in.shape[2]

    lat = _Stream(rows_per_mod=seq, mod_base=0, seq_len=seq, conv_len=GRID_W)
    cst = _Stream(rows_per_mod=batch * ctx_len, mod_base=batch, seq_len=ctx_len, conv_len=ctx_len)

    mod_rows = -(-(batch + 1) // 8) * 8
    cond = jnp.zeros((mod_rows, d), F32).at[:batch].set(c).at[batch].set(c_ctx)
    table = _ada_table(cond, ada_w, ada_b)
    mod = _Mod(table.reshape(depth * mod_rows * N_MOD, 1, d), mod_rows, d)

    router_wt = router_w.T
    xs = x.reshape(batch * seq, d)
    zs = ctx.reshape(batch * ctx_len, d)

    def channel_mixer(s, stream, layer):
        h, comb_t = _ffn_prep(s, norm_ffn_w[layer], mod, stream, layer, router_wt, router_bias)
        return _moe(h, comb_t, moe_w_gate[layer].astype(BF16), moe_w_up[layer].astype(BF16),
                    moe_w_down[layer].astype(BF16), s, mod, stream, layer)

    for i in range(depth):
        use_dn = i % N_MIXERS == 0
        j = i // N_MIXERS
        ctx_out = i < depth - 1
        ctx_live = use_dn or ctx_out

        h_lat = _modulate(xs, norm_mix_w[i], mod, lat, i, 0, 1)
        h_ctx = _modulate(zs, norm_mix_w[i], mod, cst, i, 0, 1) if ctx_live else None
        if use_dn:
            w_in = dn_w_in[j].astype(BF16)
            w_out = dn_w_out[j].astype(BF16)
            a_log, dt_bias = dn_A_log[j].reshape(-1), dn_dt_bias[j].reshape(-1)
            cs = _deltanet_streams(h_ctx, w_in, dn_conv_w[j], a_log, dt_bias, cst, nh, hd)
            ls = _deltanet_streams(h_lat, w_in, dn_conv_w[j], a_log, dt_bias, lat, nh, hd)
            o_ctx, o_lat = [], []
            for backward in (False, True):
                s0 = jnp.zeros((batch, nh, hd, hd), F32)
                oc, s_ctx = _dn_scan(cs[1], cs[2], cs[3], cs[4], cs[5], cs[6], s0, backward, batch, nh, hd)
                ol, _ = _dn_scan(ls[1], ls[2], ls[3], ls[4], ls[5], ls[6], s_ctx, backward, batch, nh, hd)
                o_ctx.append(oc)
                o_lat.append(ol)
            y_lat = _dn_out(o_lat[0], o_lat[1], ls[0], dn_norm_w[j], dn_width, hd)
            xs = _matmul_gated_residual(y_lat, w_out, xs, mod, lat, i, 2)
            if ctx_out:
                y_ctx = _dn_out(o_ctx[0], o_ctx[1], cs[0], dn_norm_w[j], dn_width, hd)
                z_mix = _matmul_gated_residual(y_ctx, w_out, zs, mod, cst, i, 2)
        else:
            w_fold = _fold_channel_dft(fn_w_in[j], FN_GROUPS)
            w_out = fn_w_out[j].astype(BF16)
            mixed = _position_dft_real(_matmul(h_lat, w_fold, BF16), batch, fn_width)
            xs = _matmul_gated_residual(mixed, w_out, xs, mod, lat, i, 2)
            if ctx_out:
                mixed_c = _position_dft_real(_matmul(h_ctx, w_fold, BF16), batch, fn_width)
                z_mix = _matmul_gated_residual(mixed_c, w_out, zs, mod, cst, i, 2)

        xs = channel_mixer(xs, lat, i)
        if ctx_out:
            zs = channel_mixer(z_mix, cst, i)

    return _final_norm(xs, final_norm_w).reshape(batch, seq, d)
```

```python
import functools
import math

import numpy as np
import jax
import jax.numpy as jnp
from jax import lax
from jax.experimental import pallas as pl
from jax.experimental.pallas import tpu as pltpu

F32 = jnp.float32
BF16 = jnp.bfloat16

GRID_W = 64
CHUNK = 64
CONV_W = 5
DN_HEADS = 16
FN_GROUPS = 4
N_GROUPS = 4
GROUP_SCORE_K = 2
TOP_K = 2
N_MOD = 6
N_MIXERS = 2
EPS = 1e-6

VMEM_LIMIT = 56 * 1024 * 1024
LANES = 128


def _cparams(*sem):
    return pltpu.CompilerParams(dimension_semantics=sem, vmem_limit_bytes=VMEM_LIMIT)


def _dot(a, b):
    return jnp.dot(a, b, preferred_element_type=F32)


def _dot_nt(a, b):
    return lax.dot_general(a, b, (((1,), (1,)), ((), ())), preferred_element_type=F32)


def _dot_tn(a, b):
    return lax.dot_general(a, b, (((0,), (0,)), ((), ())), preferred_element_type=F32)


def _split2(x):
    hi = x.astype(BF16)
    lo = (x - hi.astype(F32)).astype(BF16)
    return hi, lo


def _split3(x):
    hi = x.astype(BF16)
    r = x - hi.astype(F32)
    mid = r.astype(BF16)
    lo = (r - mid.astype(F32)).astype(BF16)
    return hi, mid, lo


def _silu(x):
    return x / (1.0 + jnp.exp(-x))


def _sigmoid(x):
    return 1.0 / (1.0 + jnp.exp(-x))


def _tile(n, pref, mult=8):
    if n <= pref:
        return n
    t = (pref // mult) * mult
    while t >= mult:
        if n % t == 0:
            return t
        t -= mult
    return n


class _Stream:
    def __init__(self, rows_per_mod, mod_base, seq_len, conv_len):
        self.rows_per_mod = rows_per_mod
        self.mod_base = mod_base
        self.seq_len = seq_len
        self.conv_len = conv_len


def _log2(n):
    assert n & (n - 1) == 0, n
    return n.bit_length() - 1


def _ada_kernel(c_ref, w_ref, b_ref, o_ref):
    s_hi, s_lo = _split2(_silu(c_ref[...]))
    w_hi, w_lo = _split2(w_ref[0])
    acc = _dot(s_hi, w_hi) + _dot(s_lo, w_hi) + _dot(s_hi, w_lo)
    o_ref[0] = acc + b_ref[0]


def _ada_table(cond, ada_w, ada_b):
    depth, d, n = ada_w.shape
    r = cond.shape[0]
    tn = _tile(n, 512, 128)
    return pl.pallas_call(
        _ada_kernel,
        grid=(depth, n // tn),
        in_specs=[
            pl.BlockSpec((r, d), lambda l, j: (0, 0)),
            pl.BlockSpec((1, d, tn), lambda l, j: (l, 0, j)),
            pl.BlockSpec((1, 1, tn), lambda l, j: (l, 0, j)),
        ],
        out_specs=pl.BlockSpec((1, r, tn), lambda l, j: (l, 0, j)),
        out_shape=jax.ShapeDtypeStruct((depth, r, n), F32),
        compiler_params=_cparams("parallel", "parallel"),
        name="ada_table",
    )(cond, ada_w, ada_b.reshape(depth, 1, n))


def _modulated(x, nw, shift, scale):
    ms = jnp.mean(x * x, axis=-1, keepdims=True)
    return (x * lax.rsqrt(ms + EPS) * nw) * (1.0 + scale) + shift


def _modulate_kernel(x_ref, nw_ref, sh_ref, sc_ref, o_ref):
    o_ref[...] = _modulated(x_ref[...], nw_ref[...], sh_ref[0], sc_ref[0]).astype(o_ref.dtype)


class _Mod:
    def __init__(self, table3, mod_rows, d):
        self.table = table3
        self.mod_rows = mod_rows
        self.d = d

    def spec(self, stream, layer, k, tm, width=None, col=None):
        width = self.d if width is None else width
        rows, base, mr = stream.rows_per_mod, stream.mod_base, self.mod_rows

        def index_map(*g):
            row = base + (g[0] * tm) // rows
            return ((layer * mr + row) * N_MOD + k, 0, 0 if col is None else col(*g))
        return pl.BlockSpec((1, 1, width), index_map)


def _modulate(x, nw, mod, stream, layer, k_shift, k_scale):
    m, d = x.shape
    tm = _tile(min(m, stream.rows_per_mod), 512)
    return pl.pallas_call(
        _modulate_kernel,
        grid=(m // tm,),
        in_specs=[
            pl.BlockSpec((tm, d), lambda i: (i, 0)),
            pl.BlockSpec((1, d), lambda i: (0, 0)),
            mod.spec(stream, layer, k_shift, tm),
            mod.spec(stream, layer, k_scale, tm),
        ],
        out_specs=pl.BlockSpec((tm, d), lambda i: (i, 0)),
        out_shape=jax.ShapeDtypeStruct((m, d), BF16),
        compiler_params=_cparams("parallel"),
        name="modulate",
    )(x, nw.reshape(1, d), mod.table, mod.table)


def _final_norm_kernel(x_ref, nw_ref, o_ref):
    x = x_ref[...]
    ms = jnp.mean(x * x, axis=-1, keepdims=True)
    o_ref[...] = x * lax.rsqrt(ms + EPS) * nw_ref[...]


def _final_norm(x, nw):
    m, d = x.shape
    tm = _tile(m, 512)
    return pl.pallas_call(
        _final_norm_kernel,
        grid=(m // tm,),
        in_specs=[pl.BlockSpec((tm, d), lambda i: (i, 0)), pl.BlockSpec((1, d), lambda i: (0, 0))],
        out_specs=pl.BlockSpec((tm, d), lambda i: (i, 0)),
        out_shape=jax.ShapeDtypeStruct((m, d), F32),
        compiler_params=_cparams("parallel"),
        name="final_norm",
    )(x, nw.reshape(1, d))


def _matmul_kernel(a_ref, w_ref, o_ref):
    o_ref[...] = _dot(a_ref[...], w_ref[...]).astype(o_ref.dtype)


def _matmul(a, w, out_dtype, tm_pref=1024, tn_pref=512, col0=0, n=None):
    m, k = a.shape
    n = w.shape[1] if n is None else n
    tm, tn = _tile(m, tm_pref), _tile(n, tn_pref, 128)
    assert col0 % tn == 0
    cb = col0 // tn
    return pl.pallas_call(
        _matmul_kernel,
        grid=(m // tm, n // tn),
        in_specs=[pl.BlockSpec((tm, k), lambda i, j: (i, 0)), pl.BlockSpec((k, tn), lambda i, j: (0, cb + j))],
        out_specs=pl.BlockSpec((tm, tn), lambda i, j: (i, j)),
        out_shape=jax.ShapeDtypeStruct((m, n), out_dtype),
        compiler_params=_cparams("parallel", "parallel"),
        name="matmul",
    )(a, w)


def _matmul_res_kernel(a_ref, w_ref, x_ref, g_ref, o_ref):
    o_ref[...] = x_ref[...] + g_ref[0] * _dot(a_ref[...], w_ref[...])


def _matmul_gated_residual(a, w, x, mod, stream, layer, k_gate, tm_pref=1024, tn_pref=512):
    m, k = a.shape
    n = w.shape[1]
    tm = _tile(min(m, stream.rows_per_mod), tm_pref)
    tn = _tile(n, tn_pref, 128)
    return pl.pallas_call(
        _matmul_res_kernel,
        grid=(m // tm, n // tn),
        in_specs=[
            pl.BlockSpec((tm, k), lambda i, j: (i, 0)),
            pl.BlockSpec((k, tn), lambda i, j: (0, j)),
            pl.BlockSpec((tm, tn), lambda i, j: (i, j)),
            mod.spec(stream, layer, k_gate, tm, width=tn, col=lambda i, j: j),
        ],
        out_specs=pl.BlockSpec((tm, tn), lambda i, j: (i, j)),
        out_shape=jax.ShapeDtypeStruct((m, n), F32),
        compiler_params=_cparams("parallel", "parallel"),
        name="matmul_gated_residual",
    )(a, w, x, mod.table)


def _head_sumsq(y, ones_bd):
    sq_hi, sq_lo = _split2(y * y)
    return _dot(sq_hi, ones_bd) + _dot(sq_lo, ones_bd)


def _dn_conv_kernel(u_ref, cw_ref, e_ref, o_ref, *, conv_len, mode, head_dim):
    u = u_ref[...].astype(F32)
    tm = u.shape[0]
    pos = lax.broadcasted_iota(jnp.int32, u.shape, 0) & (conv_len - 1)
    pad = CONV_W // 2
    acc = u * cw_ref[pad:pad + 1, :]
    for j in range(CONV_W):
        d = j - pad
        if d == 0:
            continue
        shifted = pltpu.roll(u, (-d) % tm, 0)
        ok = (pos + d >= 0) & (pos + d < conv_len)
        acc = acc + jnp.where(ok, shifted, 0.0) * cw_ref[j:j + 1, :]
    y = _silu(acc)
    if mode != "v":
        y = y * lax.rsqrt(_head_sumsq(y, e_ref[...]) + EPS)
        if mode == "q":
            y = y * (head_dim ** -0.5)
    o_ref[...] = y.astype(o_ref.dtype)


def _dn_conv(p, conv_w, section, mode, stream, width, head_dim):
    m = p.shape[0]
    tc = _tile(width, 512, 128)
    tm = _tile(min(m, max(stream.conv_len, 256)), 256, stream.conv_len)
    ones_bd = jnp.asarray(np.kron(np.eye(tc // head_dim), np.ones((head_dim, head_dim))), BF16)
    nsec = width // tc
    return pl.pallas_call(
        functools.partial(_dn_conv_kernel, conv_len=stream.conv_len, mode=mode, head_dim=head_dim),
        grid=(m // tm, nsec),
        in_specs=[
            pl.BlockSpec((tm, tc), lambda i, j: (i, section * nsec + j)),
            pl.BlockSpec((CONV_W, tc), lambda i, j: (0, section * nsec + j)),
            pl.BlockSpec((tc, tc), lambda i, j: (0, 0)),
        ],
        out_specs=pl.BlockSpec((tm, tc), lambda i, j: (i, j)),
        out_shape=jax.ShapeDtypeStruct((m, width), BF16),
        compiler_params=_cparams("parallel", "parallel"),
        name="dn_conv_" + mode,
    )(p, conv_w, ones_bd)


def _dn_gates_kernel(ab_ref, al_ref, dt_ref, tl_ref, tu_ref, g_ref, beta_ref, *, nh):
    ab = ab_ref[:, :4 * nh]
    a = ab[:, :2 * nh] + dt_ref[...]
    softplus = jnp.maximum(a, 0.0) + jnp.log1p(jnp.exp(-jnp.abs(a)))
    la = -jnp.exp(al_ref[...]) * softplus
    beta_ref[...] = _sigmoid(ab[:, 2 * nh:])
    parts = _split3(la)
    g_pre = sum(_dot(tl_ref[...], p) for p in parts)
    g_suf = sum(_dot(tu_ref[...], p) for p in parts)
    col = lax.broadcasted_iota(jnp.int32, la.shape, 1)
    g_ref[...] = jnp.where(col < nh, g_pre, g_suf)


def _dn_gates(ab, a_log, dt_bias, nh):
    m = ab.shape[0]
    tm = _tile(m, 256, CHUNK)
    r = np.arange(tm)
    same = (r[:, None] // CHUNK) == (r[None, :] // CHUNK)
    tri_l = jnp.asarray(same & (r[None, :] <= r[:, None]), BF16)
    tri_u = jnp.asarray(same & (r[None, :] >= r[:, None]), BF16)
    return pl.pallas_call(
        functools.partial(_dn_gates_kernel, nh=nh),
        grid=(m // tm,),
        in_specs=[
            pl.BlockSpec((tm, ab.shape[1]), lambda i: (i, 0)),
            pl.BlockSpec((1, 2 * nh), lambda i: (0, 0)),
            pl.BlockSpec((1, 2 * nh), lambda i: (0, 0)),
            pl.BlockSpec((tm, tm), lambda i: (0, 0)),
            pl.BlockSpec((tm, tm), lambda i: (0, 0)),
        ],
        out_specs=[pl.BlockSpec((tm, 2 * nh), lambda i: (i, 0))] * 2,
        out_shape=[jax.ShapeDtypeStruct((m, 2 * nh), F32)] * 2,
        compiler_params=_cparams("parallel"),
        name="dn_gates",
    )(ab, a_log.reshape(1, 2 * nh), dt_bias.reshape(1, 2 * nh), tri_l, tri_u)


def _unit_tri_inverses(lms):
    shape = lms[0].shape
    ri = lax.broadcasted_iota(jnp.int32, shape, 0)
    ci = lax.broadcasted_iota(jnp.int32, shape, 1)
    eye = (ri == ci).astype(F32)
    pair = (ri >> 1) == (ci >> 1)
    ts = [eye - jnp.where(pair, lm, 0.0) for lm in lms]
    for lvl in range(1, _log2(shape[0])):
        cross = ((ri >> (lvl + 1)) == (ci >> (lvl + 1))) & ((ri >> lvl) != (ci >> lvl))
        offs = [jnp.where(cross, lm, 0.0).astype(BF16) for lm in lms]
        tbs = [t.astype(BF16) for t in ts]
        ps = [_dot(tb, off).astype(BF16) for tb, off in zip(tbs, offs)]
        ts = [t - _dot(p, tb) for t, p, tb in zip(ts, ps, tbs)]
    return ts


def _dn_scan_kernel(q_ref, k_ref, v_ref, gc_ref, gr_ref, b_ref, s0_ref, o_ref, s_ref, *, backward, nh, hd):
    @pl.when(pl.program_id(1) == 0)
    def _():
        s_ref[...] = s0_ref[...]

    n = CHUNK
    heads = range(nh)
    ri = lax.broadcasted_iota(jnp.int32, (n, n), 0)
    ci = lax.broadcasted_iota(jnp.int32, (n, n), 1)
    causal = (ci >= ri) if backward else (ci <= ri)
    strict = (ci > ri) if backward else (ci < ri)
    last = 0 if backward else n - 1
    dcol = nh if backward else 0
    gc_all = gc_ref[...]
    gr_all = gr_ref[0]
    b_all = b_ref[...]
    gcol = [gc_all[:, dcol + h:dcol + h + 1] for h in heads]
    grow = [gr_all[dcol + h:dcol + h + 1, :] for h in heads]
    bcol = [b_all[:, dcol + h:dcol + h + 1] for h in heads]
    glast = [g[last:last + 1, :] for g in gcol]
    q = [q_ref[:, h * hd:(h + 1) * hd] for h in heads]
    k = [k_ref[:, h * hd:(h + 1) * hd] for h in heads]
    v = [v_ref[:, h * hd:(h + 1) * hd] for h in heads]
    kk = [_dot_nt(k[h], k[h]) for h in heads]
    qk = [_dot_nt(q[h], k[h]) for h in heads]
    decay = [jnp.exp(jnp.where(causal, gcol[h] - grow[h], -jnp.inf)) for h in heads]
    lm = [jnp.where(strict, bcol[h] * kk[h] * decay[h], 0.0) for h in heads]
    intra = [(qk[h] * decay[h]).astype(BF16) for h in heads]
    t = _unit_tri_inverses(lm)
    eg = [jnp.exp(g) for g in gcol]
    rhs = [jnp.concatenate([v[h].astype(F32) * bcol[h], k[h].astype(F32) * (bcol[h] * eg[h])],
                           axis=1).astype(BF16) for h in heads]
    uw = [_dot(t[h].astype(BF16), rhs[h]) for h in heads]
    s = [s_ref[0, h] for h in heads]
    wq = [jnp.concatenate([uw[h][:, hd:], q[h].astype(F32) * eg[h]], axis=0).astype(BF16) for h in heads]
    ws = [_dot(wq[h], s[h].astype(BF16)) for h in heads]
    vb = [(uw[h][:, :hd] - ws[h][:n]).astype(BF16) for h in heads]
    kg = [(k[h].astype(F32) * jnp.exp(glast[h] - gcol[h])).astype(BF16) for h in heads]
    for h in heads:
        o_ref[:, h * hd:(h + 1) * hd] = ws[h][n:] + _dot(intra[h], vb[h])
    for h in heads:
        s_ref[0, h] = s[h] * jnp.exp(glast[h]) + _dot_tn(kg[h], vb[h])


def _dn_scan(q, k, v, g, g_rows, beta, s0, backward, batch, nh, hd):
    m = q.shape[0]
    nch = m // batch // CHUNK

    def tok(b, c):
        return (b * nch + (nch - 1 - c if backward else c), 0)

    def tok3(b, c):
        return (b * nch + (nch - 1 - c if backward else c), 0, 0)

    st = lambda b, c: (b, 0, 0, 0)
    return pl.pallas_call(
        functools.partial(_dn_scan_kernel, backward=backward, nh=nh, hd=hd),
        grid=(batch, nch),
        in_specs=[
            pl.BlockSpec((CHUNK, nh * hd), tok),
            pl.BlockSpec((CHUNK, nh * hd), tok),
            pl.BlockSpec((CHUNK, nh * hd), tok),
            pl.BlockSpec((CHUNK, 2 * nh), tok),
            pl.BlockSpec((1, 2 * nh, CHUNK), tok3),
            pl.BlockSpec((CHUNK, 2 * nh), tok),
            pl.BlockSpec((1, nh, hd, hd), st),
        ],
        out_specs=[pl.BlockSpec((CHUNK, nh * hd), tok), pl.BlockSpec((1, nh, hd, hd), st)],
        out_shape=[jax.ShapeDtypeStruct((m, nh * hd), F32), jax.ShapeDtypeStruct((batch, nh, hd, hd), F32)],
        compiler_params=_cparams("parallel", "arbitrary"),
        name="dn_scan_bwd" if backward else "dn_scan_fwd",
    )(q, k, v, g, g_rows, beta, s0)


def _dn_out_kernel(of_ref, ob_ref, gate_ref, nw_ref, e_ref, y_ref, *, hd):
    o = of_ref[...] + ob_ref[...]
    ms = _head_sumsq(o, e_ref[...]) * (1.0 / hd)
    y = o * lax.rsqrt(ms + EPS) * nw_ref[...]
    y_ref[...] = (y * _silu(gate_ref[...].astype(F32))).astype(y_ref.dtype)


def _dn_out(o_f, o_b, p, norm_w, width, hd):
    m = o_f.shape[0]
    tc = _tile(width, 512, 128)
    tm = _tile(m, 512)
    nsec = width // tc
    ones_bd = jnp.asarray(np.kron(np.eye(tc // hd), np.ones((hd, hd))), BF16)
    nw = jnp.tile(norm_w.astype(F32), tc // hd).reshape(1, tc)
    return pl.pallas_call(
        functools.partial(_dn_out_kernel, hd=hd),
        grid=(m // tm, nsec),
        in_specs=[
            pl.BlockSpec((tm, tc), lambda i, j: (i, j)),
            pl.BlockSpec((tm, tc), lambda i, j: (i, j)),
            pl.BlockSpec((tm, tc), lambda i, j: (i, 3 * nsec + j)),
            pl.BlockSpec((1, tc), lambda i, j: (0, 0)),
            pl.BlockSpec((tc, tc), lambda i, j: (0, 0)),
        ],
        out_specs=pl.BlockSpec((tm, tc), lambda i, j: (i, j)),
        out_shape=jax.ShapeDtypeStruct((m, width), BF16),
        compiler_params=_cparams("parallel", "parallel"),
        name="dn_out",
    )(o_f, o_b, p, nw, ones_bd)


def _fold_kernel(w_ref, t_ref, o_ref):
    w_hi, w_lo = _split2(w_ref[...])
    t_hi, t_lo = _split2(t_ref[0])
    o_ref[...] = (_dot(w_hi, t_hi) + _dot(w_lo, t_hi) + _dot(w_hi, t_lo)).astype(o_ref.dtype)


def _fold_channel_dft(w_in, groups):
    d, width = w_in.shape
    c = width // groups
    idx = np.arange(c)
    ang = 2.0 * np.pi * ((idx[:, None] * idx[None, :]) % c) / c
    table = jnp.asarray(np.stack([np.cos(ang), -np.sin(ang)]) / math.sqrt(c), F32)
    return pl.pallas_call(
        _fold_kernel,
        grid=(2, groups),
        in_specs=[pl.BlockSpec((d, c), lambda s, g: (0, g)), pl.BlockSpec((1, c, c), lambda s, g: (s, 0, 0))],
        out_specs=pl.BlockSpec((d, c), lambda s, g: (0, s * groups + g)),
        out_shape=jax.ShapeDtypeStruct((d, 2 * width), BF16),
        compiler_params=_cparams("parallel", "parallel"),
        name="fold_channel_dft",
    )(w_in, table)


def _dft_kernel(zr_ref, zi_ref, f_ref, g_ref, o_ref, zf, yf, of, *, n1, n2):
    slabs = range(of.shape[0])
    lanes = lambda l: slice(l * LANES, (l + 1) * LANES)
    for l in slabs:
        zf[0, l] = zr_ref[:, lanes(l)].astype(F32)
        zf[1, l] = zi_ref[:, lanes(l)].astype(F32)

    def stage1(t2, carry):
        z = jnp.concatenate(
            [jnp.concatenate([zf[part, l, pl.ds(t2, n1, stride=n2), :] for l in slabs], axis=1) for part in (0, 1)],
            axis=0)
        y = _dot(f_ref[...], z.astype(BF16))
        for l in slabs:
            yf[l, pl.ds(pl.multiple_of(t2 * 2 * n1, 2 * n1), 2 * n1), :] = y[:, lanes(l)]
        return carry

    lax.fori_loop(0, n2, stage1, 0, unroll=8)

    def stage2(f1, carry):
        z = jnp.concatenate(
            [jnp.concatenate([yf[l, pl.ds(off + f1, n2, stride=2 * n1), :] for l in slabs], axis=1)
             for off in (0, n1)], axis=0)
        r = _dot(g_ref[f1], z.astype(BF16))
        for l in slabs:
            of[l, pl.ds(f1, n2, stride=n1), :] = r[:, lanes(l)]
        return carry

    lax.fori_loop(0, n1, stage2, 0, unroll=8)
    for l in slabs:
        o_ref[:, lanes(l)] = of[l].astype(o_ref.dtype)


def _position_dft_real(u, batch, width):
    t_len = u.shape[0] // batch
    n1 = 1 << (int(math.log2(t_len)) // 2)
    n2 = t_len // n1
    a1 = 2.0 * np.pi * ((np.arange(n1)[:, None] * np.arange(n1)[None, :]) % n1) / n1
    c1, s1 = np.cos(a1), np.sin(a1)
    f1m = jnp.asarray(np.block([[c1, s1], [-s1, c1]]) / math.sqrt(n1), BF16)
    f1i, f2i, t2i = np.arange(n1)[:, None, None], np.arange(n2)[None, :, None], np.arange(n2)[None, None, :]
    theta = 2.0 * np.pi * (((t2i * f1i) % t_len) / t_len + ((t2i * f2i) % n2) / n2)
    g2m = jnp.asarray(np.concatenate([np.cos(theta), np.sin(theta)], axis=2) / math.sqrt(n2), BF16)

    tc = _tile(width, 256, 128)
    nj = width // tc
    return pl.pallas_call(
        functools.partial(_dft_kernel, n1=n1, n2=n2),
        grid=(batch, nj),
        in_specs=[
            pl.BlockSpec((t_len, tc), lambda b, j: (b, j)),
            pl.BlockSpec((t_len, tc), lambda b, j: (b, nj + j)),
            pl.BlockSpec((2 * n1, 2 * n1), lambda b, j: (0, 0)),
            pl.BlockSpec((n1, n2, 2 * n2), lambda b, j: (0, 0, 0)),
        ],
        out_specs=pl.BlockSpec((t_len, tc), lambda b, j: (b, j)),
        out_shape=jax.ShapeDtypeStruct((batch * t_len, width), BF16),
        scratch_shapes=[pltpu.VMEM((2, tc // LANES, t_len, LANES), F32),
                        pltpu.VMEM((tc // LANES, n2 * 2 * n1, LANES), F32),
                        pltpu.VMEM((tc // LANES, t_len, LANES), F32)],
        compiler_params=_cparams("parallel", "parallel"),
        name="position_dft",
    )(u, u, f1m, g2m)


def _ffn_prep_kernel(x_ref, nw_ref, sh_ref, sc_ref, rw_ref, rb_ref, h_ref, comb_ref, *, n_exp):
    h = _modulated(x_ref[...], nw_ref[...], sh_ref[0], sc_ref[0])
    h_ref[...] = h.astype(h_ref.dtype)
    h_hi, h_lo = _split2(h)
    r_hi, r_lo = _split2(rw_ref[...])
    logits = _dot_nt(r_hi, h_hi) + _dot_nt(r_lo, h_hi) + _dot_nt(r_hi, h_lo)
    s = _sigmoid(logits)
    sel = s + rb_ref[...]
    srow = [s[e:e + 1, :] for e in range(n_exp)]
    row = [sel[e:e + 1, :] for e in range(n_exp)]
    epg = n_exp // N_GROUPS

    def beats(a, ia, b, ib):
        return (a >= b) if ia < ib else (a > b)

    rank, gscore = {}, []
    for g in range(N_GROUPS):
        members = range(g * epg, (g + 1) * epg)
        for i in members:
            rank[i] = sum(beats(row[j], j, row[i], i).astype(F32) for j in members if j != i)
        gscore.append(sum(jnp.where(rank[i] < GROUP_SCORE_K, row[i], 0.0) for i in members))
    picked = []
    for g in range(N_GROUPS):
        grank = sum(beats(gscore[j], j, gscore[g], g).astype(F32) for j in range(N_GROUPS) if j != g)
        for i in range(g * epg, (g + 1) * epg):
            picked.append(jnp.where((grank < 1.0) & (rank[i] < TOP_K), srow[i], 0.0))
    denom = sum(picked)
    comb_ref[...] = jnp.concatenate([p / denom for p in picked], axis=0)


def _ffn_prep(x, nw, mod, stream, layer, router_wt, router_b):
    m, d = x.shape
    n_exp = router_wt.shape[0]
    tm = _tile(min(m, stream.rows_per_mod), 512, 128)
    return pl.pallas_call(
        functools.partial(_ffn_prep_kernel, n_exp=n_exp),
        grid=(m // tm,),
        in_specs=[
            pl.BlockSpec((tm, d), lambda i: (i, 0)),
            pl.BlockSpec((1, d), lambda i: (0, 0)),
            mod.spec(stream, layer, 3, tm),
            mod.spec(stream, layer, 4, tm),
            pl.BlockSpec((n_exp, d), lambda i: (0, 0)),
            pl.BlockSpec((n_exp, 1), lambda i: (0, 0)),
        ],
        out_specs=[pl.BlockSpec((tm, d), lambda i: (i, 0)), pl.BlockSpec((n_exp, tm), lambda i: (0, i))],
        out_shape=[jax.ShapeDtypeStruct((m, d), F32), jax.ShapeDtypeStruct((n_exp, m), F32)],
        compiler_params=_cparams("parallel"),
        name="ffn_prep",
    )(x, nw.reshape(1, d), mod.table, mod.table, router_wt, router_b.reshape(n_exp, 1))


def _route(comb_t, tg):
    assert TOP_K == 2
    n_exp, n = comb_t.shape
    i32 = jnp.int32
    mask = comb_t > 0.0
    cnt = jnp.sum(mask.astype(i32), axis=1)
    padded = ((cnt + tg - 1) // tg) * tg
    ends = jnp.cumsum(padded)
    dest = (ends - padded)[:, None] + jnp.cumsum(mask.astype(i32), axis=1) - 1
    eidx = jnp.arange(n_exp, dtype=i32)[:, None]
    first = jnp.min(jnp.where(mask, eidx, n_exp), axis=0)
    final = jnp.max(jnp.where(mask, eidx, -1), axis=0)
    is_a = mask & (eidx == first)
    is_b = mask & (eidx == final) & (final != first)
    pick = lambda sel, val: jnp.sum(jnp.where(sel, val, 0), axis=0)
    d_a, d_b = pick(is_a, dest), pick(is_b, dest)
    w_ab = jnp.stack([pick(is_a, comb_t), pick(is_b, comb_t)], axis=1)
    p_rows = TOP_K * n + n_exp * tg
    tok = jnp.arange(n, dtype=i32)
    src = jnp.zeros((p_rows,), i32)
    src = src.at[jnp.where(jnp.any(is_a, axis=0), d_a, p_rows)].set(tok, mode="drop")
    src = src.at[jnp.where(jnp.any(is_b, axis=0), d_b, p_rows)].set(tok, mode="drop")
    tile_start = jnp.arange(p_rows // tg, dtype=i32) * tg
    tile_exp = jnp.minimum(jnp.sum((tile_start[:, None] >= ends[None, :]).astype(i32), axis=1), n_exp - 1)
    tile_ok = (tile_start < ends[-1]).astype(i32)
    return src, tile_exp, tile_ok, d_a.astype(i32), d_b.astype(i32), w_ab


def _row_copies(idx_ref, base, n_rows, src_hbm, dst_buf, sem, wait):
    def body(r, carry):
        cp = pltpu.make_async_copy(src_hbm.at[pl.ds(idx_ref[base + r], 1)], dst_buf.at[pl.ds(r, 1)], sem)
        if wait:
            cp.wait()
        else:
            cp.start()
        return carry
    lax.fori_loop(0, n_rows, body, 0, unroll=8)


def _moe_expert_kernel(texp_ref, tval_ref, src_ref, h_hbm, wg_ref, wu_ref, wd_ref, y_ref, xbuf, sem, *, tg):
    i = pl.program_id(0)

    def fetch(tile, slot, wait):
        @pl.when(tval_ref[tile] != 0)
        def _():
            _row_copies(src_ref, tile * tg, tg, h_hbm, xbuf.at[slot], sem.at[slot], wait)

    @pl.when(i == 0)
    def _():
        fetch(0, 0, False)

    @pl.when(i + 1 < pl.num_programs(0))
    def _():
        fetch(i + 1, (i + 1) % 2, False)

    fetch(i, i % 2, True)

    @pl.when(tval_ref[i] != 0)
    def _():
        x = xbuf[i % 2].astype(BF16)
        act = (_silu(_dot(x, wg_ref[0])) * _dot(x, wu_ref[0])).astype(BF16)
        y_ref[...] = _dot(act, wd_ref[0])

    @pl.when(tval_ref[i] == 0)
    def _():
        y_ref[...] = jnp.zeros_like(y_ref)


def _moe_experts(h, src, tile_exp, tile_ok, wg, wu, wd, tg):
    n, d = h.shape
    _, _, f = wg.shape
    p_rows = src.shape[0]
    wmap = lambda i, te, ok, s: (te[i], 0, 0)
    return pl.pallas_call(
        functools.partial(_moe_expert_kernel, tg=tg),
        grid_spec=pltpu.PrefetchScalarGridSpec(
            num_scalar_prefetch=3,
            grid=(p_rows // tg,),
            in_specs=[
                pl.BlockSpec(memory_space=pl.ANY),
                pl.BlockSpec((1, d, f), wmap),
                pl.BlockSpec((1, d, f), wmap),
                pl.BlockSpec((1, f, d), wmap),
            ],
            out_specs=pl.BlockSpec((tg, d), lambda i, te, ok, s: (i, 0)),
            scratch_shapes=[pltpu.VMEM((2, tg, d), F32), pltpu.SemaphoreType.DMA((2,))],
        ),
        out_shape=jax.ShapeDtypeStruct((p_rows, d), F32),
        compiler_params=_cparams("arbitrary"),
        name="moe_experts",
    )(tile_exp, tile_ok, src, h, wg, wu, wd)


def _moe_combine_kernel(da_ref, db_ref, y_hbm, x_ref, w_ref, g_ref, o_ref, ya, yb, sem, *, tm):
    i = pl.program_id(0)

    def fetch(tile, slot, wait):
        _row_copies(da_ref, tile * tm, tm, y_hbm, ya.at[slot], sem.at[0, slot], wait)
        _row_copies(db_ref, tile * tm, tm, y_hbm, yb.at[slot], sem.at[1, slot], wait)

    @pl.when(i == 0)
    def _():
        fetch(0, 0, False)

    @pl.when(i + 1 < pl.num_programs(0))
    def _():
        fetch(i + 1, (i + 1) % 2, False)

    fetch(i, i % 2, True)
    w = w_ref[...]
    mix = w[:, 0:1] * ya[i % 2] + w[:, 1:2] * yb[i % 2]
    o_ref[...] = x_ref[...] + g_ref[0] * mix


def _moe_combine(y, d_a, d_b, w_ab, x, mod, stream, layer):
    m, d = x.shape
    tm = _tile(min(m, stream.rows_per_mod), 256)
    return pl.pallas_call(
        functools.partial(_moe_combine_kernel, tm=tm),
        grid_spec=pltpu.PrefetchScalarGridSpec(
            num_scalar_prefetch=2,
            grid=(m // tm,),
            in_specs=[
                pl.BlockSpec(memory_space=pl.ANY),
                pl.BlockSpec((tm, d), lambda i, a, b: (i, 0)),
                pl.BlockSpec((tm, 2), lambda i, a, b: (i, 0)),
                mod.spec(stream, layer, 5, tm),
            ],
            out_specs=pl.BlockSpec((tm, d), lambda i, a, b: (i, 0)),
            scratch_shapes=[pltpu.VMEM((2, tm, d), F32), pltpu.VMEM((2, tm, d), F32),
                            pltpu.SemaphoreType.DMA((2, 2))],
        ),
        out_shape=jax.ShapeDtypeStruct((m, d), F32),
        compiler_params=_cparams("arbitrary"),
        name="moe_combine",
    )(d_a, d_b, y, x, w_ab, mod.table)


def _moe(h, comb_t, wg, wu, wd, x, mod, stream, layer):
    tg = 256 if h.shape[0] * TOP_K >= 256 * wg.shape[0] else 64
    src, tile_exp, tile_ok, d_a, d_b, w_ab = _route(comb_t, tg)
    y = _moe_experts(h, src, tile_exp, tile_ok, wg, wu, wd, tg)
    return _moe_combine(y, d_a, d_b, w_ab, x, mod, stream, layer)


def _deltanet_streams(h, w_in, conv_w, a_log, dt_bias, stream, nh, hd):
    width = nh * hd
    p = _matmul(h, w_in, BF16, n=4 * width)
    ab = _matmul(h, w_in, F32, tn_pref=128, col0=4 * width, n=128)
    q = _dn_conv(p, conv_w, 0, "q", stream, width, hd)
    k = _dn_conv(p, conv_w, 1, "k", stream, width, hd)
    v = _dn_conv(p, conv_w, 2, "v", stream, width, hd)
    g, beta = _dn_gates(ab, a_log, dt_bias, nh)
    nchunks = g.shape[0] // CHUNK
    g_rows = g.reshape(nchunks, CHUNK, 2 * nh).transpose(0, 2, 1)
    return p, q, k, v, g, g_rows, beta


def kernel(x, c, ctx, c_ctx, ada_w, ada_b, norm_mix_w, norm_ffn_w, dn_w_in, dn_conv_w, dn_A_log, dn_dt_bias, dn_norm_w, dn_w_out, fn_w_in, fn_w_out, router_w, router_bias, moe_w_gate, moe_w_up, moe_w_down, final_norm_w):
    batch, seq, d = x.shape
    ctx_len = ctx.shape[1]
    depth = ada_w.shape[0]
    nh = DN_HEADS
    dn_width = dn_w_out.shape[1]
    hd = dn_width // nh
    fn_width = fn_w_in.shape[2]

    lat = _Stream(rows_per_mod=seq, mod_base=0, seq_len=seq, conv_len=GRID_W)
    cst = _Stream(rows_per_mod=batch * ctx_len, mod_base=batch, seq_len=ctx_len, conv_len=ctx_len)

    mod_rows = -(-(batch + 1) // 8) * 8
    cond = jnp.zeros((mod_rows, d), F32).at[:batch].set(c).at[batch].set(c_ctx)
    table = _ada_table(cond, ada_w, ada_b)
    mod = _Mod(table.reshape(depth * mod_rows * N_MOD, 1, d), mod_rows, d)

    router_wt = router_w.T
    xs = x.reshape(batch * seq, d)
    zs = ctx.reshape(batch * ctx_len, d)

    def channel_mixer(s, stream, layer):
        h, comb_t = _ffn_prep(s, norm_ffn_w[layer], mod, stream, layer, router_wt, router_bias)
        return _moe(h, comb_t, moe_w_gate[layer].astype(BF16), moe_w_up[layer].astype(BF16),
                    moe_w_down[layer].astype(BF16), s, mod, stream, layer)

    for i in range(depth):
        use_dn = i % N_MIXERS == 0
        j = i // N_MIXERS
        ctx_out = i < depth - 1
        ctx_live = use_dn or ctx_out

        h_lat = _modulate(xs, norm_mix_w[i], mod, lat, i, 0, 1)
        h_ctx = _modulate(zs, norm_mix_w[i], mod, cst, i, 0, 1) if ctx_live else None
        if use_dn:
            w_in = dn_w_in[j].astype(BF16)
            w_out = dn_w_out[j].astype(BF16)
            a_log, dt_bias = dn_A_log[j].reshape(-1), dn_dt_bias[j].reshape(-1)
            cs = _deltanet_streams(h_ctx, w_in, dn_conv_w[j], a_log, dt_bias, cst, nh, hd)
            ls = _deltanet_streams(h_lat, w_in, dn_conv_w[j], a_log, dt_bias, lat, nh, hd)
            o_ctx, o_lat = [], []
            for backward in (False, True):
                s0 = jnp.zeros((batch, nh, hd, hd), F32)
                oc, s_ctx = _dn_scan(cs[1], cs[2], cs[3], cs[4], cs[5], cs[6], s0, backward, batch, nh, hd)
                ol, _ = _dn_scan(ls[1], ls[2], ls[3], ls[4], ls[5], ls[6], s_ctx, backward, batch, nh, hd)
                o_ctx.append(oc)
                o_lat.append(ol)
            y_lat = _dn_out(o_lat[0], o_lat[1], ls[0], dn_norm_w[j], dn_width, hd)
            xs = _matmul_gated_residual(y_lat, w_out, xs, mod, lat, i, 2)
            if ctx_out:
                y_ctx = _dn_out(o_ctx[0], o_ctx[1], cs[0], dn_norm_w[j], dn_width, hd)
                z_mix = _matmul_gated_residual(y_ctx, w_out, zs, mod, cst, i, 2)
        else:
            w_fold = _fold_channel_dft(fn_w_in[j], FN_GROUPS)
            w_out = fn_w_out[j].astype(BF16)
            mixed = _position_dft_real(_matmul(h_lat, w_fold, BF16), batch, fn_width)
            xs = _matmul_gated_residual(mixed, w_out, xs, mod, lat, i, 2)
            if ctx_out:
                mixed_c = _position_dft_real(_matmul(h_ctx, w_fold, BF16), batch, fn_width)
                z_mix = _matmul_gated_residual(mixed_c, w_out, zs, mod, cst, i, 2)

        xs = channel_mixer(xs, lat, i)
        if ctx_out:
            zs = channel_mixer(z_mix, cst, i)

    return _final_norm(xs, final_norm_w).reshape(batch, seq, d)
```

```python
import functools
import math

import numpy as np
import jax
import jax.numpy as jnp
from jax import lax
from jax.experimental import pallas as pl
from jax.experimental.pallas import tpu as pltpu

F32 = jnp.float32
BF16 = jnp.bfloat16
I32 = jnp.int32

GRID_W = 64
CHUNK = 64
CONV_W = 5
DN_HEADS = 16
FN_GROUPS = 4
N_GROUPS = 4
GROUP_SCORE_K = 2
TOP_K = 2
N_MOD = 6
N_MIXERS = 2
EPS = 1e-6

VMEM_LIMIT = 56 * 1024 * 1024
LANES = 128


def _cparams(*sem):
    return pltpu.CompilerParams(dimension_semantics=sem, vmem_limit_bytes=VMEM_LIMIT)


def _dot(a, b):
    return jnp.dot(a, b, preferred_element_type=F32)


def _dot_nt(a, b):
    return lax.dot_general(a, b, (((1,), (1,)), ((), ())), preferred_element_type=F32)


def _dot_tn(a, b):
    return lax.dot_general(a, b, (((0,), (0,)), ((), ())), preferred_element_type=F32)


def _split2(x):
    hi = x.astype(BF16)
    lo = (x - hi.astype(F32)).astype(BF16)
    return hi, lo


def _split3(x):
    hi = x.astype(BF16)
    r = x - hi.astype(F32)
    mid = r.astype(BF16)
    lo = (r - mid.astype(F32)).astype(BF16)
    return hi, mid, lo


def _silu(x):
    return x / (1.0 + jnp.exp(-x))


def _sigmoid(x):
    return 1.0 / (1.0 + jnp.exp(-x))


def _tile(n, pref, mult=8):
    if n <= pref:
        return n
    t = (pref // mult) * mult
    while t >= mult:
        if n % t == 0:
            return t
        t -= mult
    return n


def _log2(n):
    assert n & (n - 1) == 0, n
    return n.bit_length() - 1


def _ada_kernel(c_ref, w_ref, b_ref, o_ref):
    s_hi, s_lo = _split2(_silu(c_ref[...]))
    w_hi, w_lo = _split2(w_ref[0])
    acc = _dot(s_hi, w_hi) + _dot(s_lo, w_hi) + _dot(s_hi, w_lo)
    o_ref[0] = acc + b_ref[0]


def _ada_table(cond, ada_w, ada_b):
    depth, d, n = ada_w.shape
    r = cond.shape[0]
    tn = _tile(n, 512, 128)
    return pl.pallas_call(
        _ada_kernel,
        grid=(depth, n // tn),
        in_specs=[
            pl.BlockSpec((r, d), lambda l, j: (0, 0)),
            pl.BlockSpec((1, d, tn), lambda l, j: (l, 0, j)),
            pl.BlockSpec((1, 1, tn), lambda l, j: (l, 0, j)),
        ],
        out_specs=pl.BlockSpec((1, r, tn), lambda l, j: (l, 0, j)),
        out_shape=jax.ShapeDtypeStruct((depth, r, n), F32),
        compiler_params=_cparams("parallel", "parallel"),
        name="ada_table",
    )(cond, ada_w, ada_b.reshape(depth, 1, n))


class _Mod:
    def __init__(self, table3, mod_rows, d, rows_per_mod):
        self.table = table3
        self.mod_rows = mod_rows
        self.d = d
        self.rows_per_mod = rows_per_mod

    def spec(self, layer, k, tm, row_axis=0, width=None, col=None):
        assert self.rows_per_mod % tm == 0
        width = self.d if width is None else width
        rows, mr = self.rows_per_mod, self.mod_rows

        def index_map(*g):
            row = (g[row_axis] * tm) // rows
            return ((layer * mr + row) * N_MOD + k, 0, 0 if col is None else col(*g))
        return pl.BlockSpec((1, 1, width), index_map)


def _modulated(x, nw, shift, scale):
    ms = jnp.mean(x * x, axis=-1, keepdims=True)
    return (x * lax.rsqrt(ms + EPS) * nw) * (1.0 + scale) + shift


def _modulate_kernel(x_ref, nw_ref, sh_ref, sc_ref, o_ref):
    o_ref[...] = _modulated(x_ref[...], nw_ref[...], sh_ref[0], sc_ref[0]).astype(o_ref.dtype)


def _modulate(x, m, nw, mod, layer, k_shift, k_scale):
    d = x.shape[1]
    tm = _tile(math.gcd(m, mod.rows_per_mod), 512)
    return pl.pallas_call(
        _modulate_kernel,
        grid=(m // tm,),
        in_specs=[
            pl.BlockSpec((tm, d), lambda i: (i, 0)),
            pl.BlockSpec((1, d), lambda i: (0, 0)),
            mod.spec(layer, k_shift, tm),
            mod.spec(layer, k_scale, tm),
        ],
        out_specs=pl.BlockSpec((tm, d), lambda i: (i, 0)),
        out_shape=jax.ShapeDtypeStruct((m, d), BF16),
        compiler_params=_cparams("parallel"),
        name="modulate",
    )(x, nw.reshape(1, d), mod.table, mod.table)


def _final_norm_kernel(x_ref, nw_ref, o_ref):
    x = x_ref[...]
    ms = jnp.mean(x * x, axis=-1, keepdims=True)
    o_ref[...] = x * lax.rsqrt(ms + EPS) * nw_ref[...]


def _final_norm(x, m, nw):
    d = x.shape[1]
    tm = _tile(m, 512)
    return pl.pallas_call(
        _final_norm_kernel,
        grid=(m // tm,),
        in_specs=[pl.BlockSpec((tm, d), lambda i: (i, 0)), pl.BlockSpec((1, d), lambda i: (0, 0))],
        out_specs=pl.BlockSpec((tm, d), lambda i: (i, 0)),
        out_shape=jax.ShapeDtypeStruct((m, d), F32),
        compiler_params=_cparams("parallel"),
        name="final_norm",
    )(x, nw.reshape(1, d))


def _matmul_kernel(a_ref, w_ref, o_ref, wb_ref, *, valid_cols):
    @pl.when(pl.program_id(1) == 0)
    def _():
        w = w_ref[0]
        if valid_cols < w.shape[1]:
            w = jnp.where(lax.broadcasted_iota(I32, w.shape, 1) < valid_cols, w, 0.0)
        wb_ref[...] = w.astype(BF16)

    o_ref[...] = _dot(a_ref[...], wb_ref[...]).astype(o_ref.dtype)


def _matmul(a, w, layer, out_dtype, tn_pref, col0=0, n=None):
    m, k = a.shape
    n = w.shape[2] if n is None else n
    tm, tn = _tile(m, 1024), _tile(n, tn_pref, LANES)
    assert col0 % tn == 0
    cb = col0 // tn
    valid_cols = tn if col0 + n <= w.shape[2] else w.shape[2] - col0
    assert valid_cols == tn or n == tn
    return pl.pallas_call(
        functools.partial(_matmul_kernel, valid_cols=valid_cols),
        grid=(n // tn, m // tm),
        in_specs=[pl.BlockSpec((tm, k), lambda j, i: (i, 0)),
                  pl.BlockSpec((1, k, tn), lambda j, i: (layer, 0, cb + j))],
        out_specs=pl.BlockSpec((tm, tn), lambda j, i: (i, j)),
        out_shape=jax.ShapeDtypeStruct((m, n), out_dtype),
        scratch_shapes=[pltpu.VMEM((k, tn), BF16)],
        compiler_params=_cparams("parallel", "arbitrary"),
        name="matmul",
    )(a, w)


def _matmul_res_kernel(a_ref, w_ref, x_ref, g_ref, o_ref, wb_ref):
    @pl.when(pl.program_id(1) == 0)
    def _():
        wb_ref[...] = w_ref[0].astype(BF16)

    o_ref[...] = x_ref[...] + g_ref[0] * _dot(a_ref[...], wb_ref[...])


def _matmul_gated_residual(a, w, layer_w, x, mod, layer, k_gate, tn_pref=512):
    m, k = a.shape
    n = w.shape[2]
    tm = _tile(math.gcd(m, mod.rows_per_mod), 1024)
    tn = _tile(n, tn_pref, LANES)
    return pl.pallas_call(
        _matmul_res_kernel,
        grid=(n // tn, m // tm),
        in_specs=[
            pl.BlockSpec((tm, k), lambda j, i: (i, 0)),
            pl.BlockSpec((1, k, tn), lambda j, i: (layer_w, 0, j)),
            pl.BlockSpec((tm, tn), lambda j, i: (i, j)),
            mod.spec(layer, k_gate, tm, row_axis=1, width=tn, col=lambda j, i: j),
        ],
        out_specs=pl.BlockSpec((tm, tn), lambda j, i: (i, j)),
        out_shape=jax.ShapeDtypeStruct((m, n), F32),
        scratch_shapes=[pltpu.VMEM((k, tn), BF16)],
        compiler_params=_cparams("parallel", "arbitrary"),
        name="matmul_gated_residual",
    )(a, w, x, mod.table)


def _head_sumsq(y, ones_bd):
    sq_hi, sq_lo = _split2(y * y)
    return _dot(sq_hi, ones_bd) + _dot(sq_lo, ones_bd)


def _dn_conv_kernel(u_ref, cw_ref, e_ref, o_ref, coef_ref, *, lat_tiles, lat_len, ctx_len, mode, head_dim):
    i = pl.program_id(1)
    tm = u_ref.shape[0]
    pad = CONV_W // 2
    cw = cw_ref[0]

    @pl.when((i == 0) | (i == lat_tiles))
    def _():
        conv_len = jnp.where(i < lat_tiles, lat_len, ctx_len)
        pos = lax.broadcasted_iota(I32, (tm, 1), 0) & (conv_len - 1)
        for j in range(CONV_W):
            d = j - pad
            inside = ((pos + d >= 0) & (pos + d < conv_len)).astype(F32)
            coef_ref[j] = inside * cw[j:j + 1, :]

    u = u_ref[...].astype(F32)
    acc = u * cw[pad:pad + 1, :]
    for j in range(CONV_W):
        d = j - pad
        if d != 0:
            acc = acc + pltpu.roll(u, (-d) % tm, 0) * coef_ref[j]
    y = _silu(acc)
    if mode != "v":
        y = y * lax.rsqrt(_head_sumsq(y, e_ref[...]) + EPS)
        if mode == "q":
            y = y * (head_dim ** -0.5)
    o_ref[...] = y.astype(o_ref.dtype)


def _dn_conv(p, conv_w, layer, section, mode, m_lat, ctx_len, width, head_dim):
    m = p.shape[0]
    tc = _tile(width, 512, LANES)
    tm = max(GRID_W, ctx_len, 256)
    assert m_lat % tm == 0 and m % tm == 0 and tm % GRID_W == 0 and tm % ctx_len == 0
    ones_bd = jnp.asarray(np.kron(np.eye(tc // head_dim), np.ones((head_dim, head_dim))), BF16)
    nsec = width // tc
    return pl.pallas_call(
        functools.partial(_dn_conv_kernel, lat_tiles=m_lat // tm, lat_len=GRID_W, ctx_len=ctx_len, mode=mode,
                          head_dim=head_dim),
        grid=(nsec, m // tm),
        in_specs=[
            pl.BlockSpec((tm, tc), lambda j, i: (i, section * nsec + j)),
            pl.BlockSpec((1, CONV_W, tc), lambda j, i: (layer, 0, section * nsec + j)),
            pl.BlockSpec((tc, tc), lambda j, i: (0, 0)),
        ],
        out_specs=pl.BlockSpec((tm, tc), lambda j, i: (i, j)),
        out_shape=jax.ShapeDtypeStruct((m, width), BF16),
        scratch_shapes=[pltpu.VMEM((CONV_W, tm, tc), F32)],
        compiler_params=_cparams("parallel", "arbitrary"),
        name="dn_conv_" + mode,
    )(p, conv_w, ones_bd)


def _dn_gates_kernel(ab_ref, al_ref, dt_ref, tl_ref, tu_ref, g_ref, beta_ref, *, nh):
    ab = ab_ref[:, :4 * nh]
    a = ab[:, :2 * nh] + dt_ref[...]
    softplus = jnp.maximum(a, 0.0) + jnp.log1p(jnp.exp(-jnp.abs(a)))
    la = -jnp.exp(al_ref[...]) * softplus
    beta_ref[...] = _sigmoid(ab[:, 2 * nh:])
    parts = _split3(la)
    g_pre = sum(_dot(tl_ref[...], p) for p in parts)
    g_suf = sum(_dot(tu_ref[...], p) for p in parts)
    col = lax.broadcasted_iota(I32, la.shape, 1)
    g_ref[...] = jnp.where(col < nh, g_pre, g_suf)


def _dn_gates(ab, a_log, dt_bias, nh):
    m = ab.shape[0]
    tm = _tile(m, 256, CHUNK)
    r = np.arange(tm)
    same = (r[:, None] // CHUNK) == (r[None, :] // CHUNK)
    tri_l = jnp.asarray(same & (r[None, :] <= r[:, None]), BF16)
    tri_u = jnp.asarray(same & (r[None, :] >= r[:, None]), BF16)
    return pl.pallas_call(
        functools.partial(_dn_gates_kernel, nh=nh),
        grid=(m // tm,),
        in_specs=[
            pl.BlockSpec((tm, ab.shape[1]), lambda i: (i, 0)),
            pl.BlockSpec((1, 2 * nh), lambda i: (0, 0)),
            pl.BlockSpec((1, 2 * nh), lambda i: (0, 0)),
            pl.BlockSpec((tm, tm), lambda i: (0, 0)),
            pl.BlockSpec((tm, tm), lambda i: (0, 0)),
        ],
        out_specs=[pl.BlockSpec((tm, 2 * nh), lambda i: (i, 0))] * 2,
        out_shape=[jax.ShapeDtypeStruct((m, 2 * nh), F32)] * 2,
        compiler_params=_cparams("parallel"),
        name="dn_gates",
    )(ab, a_log.reshape(1, 2 * nh), dt_bias.reshape(1, 2 * nh), tri_l, tri_u)


def _unit_tri_inverses(lms):
    shape = lms[0].shape
    ri = lax.broadcasted_iota(I32, shape, 0)
    ci = lax.broadcasted_iota(I32, shape, 1)
    eye = (ri == ci).astype(F32)
    pair = (ri >> 1) == (ci >> 1)
    ts = [eye - jnp.where(pair, lm, 0.0) for lm in lms]
    for lvl in range(1, _log2(shape[0])):
        cross = ((ri >> (lvl + 1)) == (ci >> (lvl + 1))) & ((ri >> lvl) != (ci >> lvl))
        offs = [jnp.where(cross, lm, 0.0).astype(BF16) for lm in lms]
        tbs = [t.astype(BF16) for t in ts]
        ps = [_dot(tb, off).astype(BF16) for tb, off in zip(tbs, offs)]
        ts = [t - _dot(p, tb) for t, p, tb in zip(ts, ps, tbs)]
    return ts


def _dn_scan_kernel(q_ref, k_ref, v_ref, gc_ref, gr_ref, b_ref, o_ref, s_ref, *, backward, nh, hd):
    @pl.when(pl.program_id(1) == 0)
    def _():
        s_ref[...] = jnp.zeros_like(s_ref)

    n = CHUNK
    heads = range(nh)
    ri = lax.broadcasted_iota(I32, (n, n), 0)
    ci = lax.broadcasted_iota(I32, (n, n), 1)
    causal = (ci >= ri) if backward else (ci <= ri)
    strict = (ci > ri) if backward else (ci < ri)
    last = 0 if backward else n - 1
    dcol = nh if backward else 0
    gc_all = gc_ref[...]
    gr_all = gr_ref[0]
    b_all = b_ref[...]
    gcol = [gc_all[:, dcol + h:dcol + h + 1] for h in heads]
    grow = [gr_all[dcol + h:dcol + h + 1, :] for h in heads]
    bcol = [b_all[:, dcol + h:dcol + h + 1] for h in heads]
    glast = [g[last:last + 1, :] for g in gcol]
    q = [q_ref[:, h * hd:(h + 1) * hd] for h in heads]
    k = [k_ref[:, h * hd:(h + 1) * hd] for h in heads]
    v = [v_ref[:, h * hd:(h + 1) * hd] for h in heads]
    kq_k = [_dot_nt(jnp.concatenate([k[h], q[h]], axis=0), k[h]) for h in heads]
    decay = [jnp.exp(jnp.where(causal, gcol[h] - grow[h], -jnp.inf)) for h in heads]
    lm = [jnp.where(strict, bcol[h] * kq_k[h][:n] * decay[h], 0.0) for h in heads]
    intra = [(kq_k[h][n:] * decay[h]).astype(BF16) for h in heads]
    t = _unit_tri_inverses(lm)
    eg = [jnp.exp(g) for g in gcol]
    rhs = [jnp.concatenate([v[h].astype(F32) * bcol[h], k[h].astype(F32) * (bcol[h] * eg[h])],
                           axis=1).astype(BF16) for h in heads]
    uw = [_dot(t[h].astype(BF16), rhs[h]) for h in heads]
    s = [s_ref[h] for h in heads]
    wq = [jnp.concatenate([uw[h][:, hd:], q[h].astype(F32) * eg[h]], axis=0).astype(BF16) for h in heads]
    ws = [_dot(wq[h], s[h].astype(BF16)) for h in heads]
    vb = [(uw[h][:, :hd] - ws[h][:n]).astype(BF16) for h in heads]
    kg = [(k[h].astype(F32) * jnp.exp(glast[h] - gcol[h])).astype(BF16) for h in heads]
    for h in heads:
        o_ref[:, h * hd:(h + 1) * hd] = ws[h][n:] + _dot(intra[h], vb[h])
    for h in heads:
        s_ref[h] = s[h] * jnp.exp(glast[h]) + _dot_tn(kg[h], vb[h])


def _dn_scan(q, k, v, g, g_rows, beta, backward, batch, m_lat, nh, hd):
    m = q.shape[0]
    nl = m_lat // batch // CHUNK
    nc = (m - m_lat) // batch // CHUNK
    ctx0 = m_lat // CHUNK

    def blk(b, c):
        ctx_blk = ctx0 + b * nc + (nc - 1 - c if backward else c)
        lat_blk = b * nl + (nl - 1 - (c - nc) if backward else c - nc)
        return jnp.where(c < nc, ctx_blk, lat_blk)

    tok = lambda b, c: (blk(b, c), 0)
    tok3 = lambda b, c: (blk(b, c), 0, 0)
    return pl.pallas_call(
        functools.partial(_dn_scan_kernel, backward=backward, nh=nh, hd=hd),
        grid=(batch, nc + nl),
        in_specs=[
            pl.BlockSpec((CHUNK, nh * hd), tok),
            pl.BlockSpec((CHUNK, nh * hd), tok),
            pl.BlockSpec((CHUNK, nh * hd), tok),
            pl.BlockSpec((CHUNK, 2 * nh), tok),
            pl.BlockSpec((1, 2 * nh, CHUNK), tok3),
            pl.BlockSpec((CHUNK, 2 * nh), tok),
        ],
        out_specs=pl.BlockSpec((CHUNK, nh * hd), tok),
        out_shape=jax.ShapeDtypeStruct((m, nh * hd), F32),
        scratch_shapes=[pltpu.VMEM((nh, hd, hd), F32)],
        compiler_params=_cparams("parallel", "arbitrary"),
        name="dn_scan_bwd" if backward else "dn_scan_fwd",
    )(q, k, v, g, g_rows, beta)


def _dn_out_kernel(of_ref, ob_ref, gate_ref, nw_ref, e_ref, y_ref, *, hd):
    o = of_ref[...] + ob_ref[...]
    ms = _head_sumsq(o, e_ref[...]) * (1.0 / hd)
    y = o * lax.rsqrt(ms + EPS) * nw_ref[...]
    y_ref[...] = (y * _silu(gate_ref[...].astype(F32))).astype(y_ref.dtype)


def _dn_out(o_f, o_b, p, norm_w, width, hd):
    m = o_f.shape[0]
    tc = _tile(width, 512, LANES)
    tm = _tile(m, 512)
    nsec = width // tc
    ones_bd = jnp.asarray(np.kron(np.eye(tc // hd), np.ones((hd, hd))), BF16)
    nw = jnp.tile(norm_w.astype(F32), tc // hd).reshape(1, tc)
    return pl.pallas_call(
        functools.partial(_dn_out_kernel, hd=hd),
        grid=(m // tm, nsec),
        in_specs=[
            pl.BlockSpec((tm, tc), lambda i, j: (i, j)),
            pl.BlockSpec((tm, tc), lambda i, j: (i, j)),
            pl.BlockSpec((tm, tc), lambda i, j: (i, 3 * nsec + j)),
            pl.BlockSpec((1, tc), lambda i, j: (0, 0)),
            pl.BlockSpec((tc, tc), lambda i, j: (0, 0)),
        ],
        out_specs=pl.BlockSpec((tm, tc), lambda i, j: (i, j)),
        out_shape=jax.ShapeDtypeStruct((m, width), BF16),
        compiler_params=_cparams("parallel", "parallel"),
        name="dn_out",
    )(o_f, o_b, p, nw, ones_bd)


def _fold_kernel(w_ref, t_ref, o_ref):
    w_hi, w_lo = _split2(w_ref[0])
    t_hi, t_lo = _split2(t_ref[0])
    o_ref[0] = (_dot(w_hi, t_hi) + _dot(w_lo, t_hi) + _dot(w_hi, t_lo)).astype(o_ref.dtype)


def _fold_channel_dft(w_in, layer, groups):
    _, d, width = w_in.shape
    c = width // groups
    idx = np.arange(c)
    ang = 2.0 * np.pi * ((idx[:, None] * idx[None, :]) % c) / c
    table = jnp.asarray(np.stack([np.cos(ang), -np.sin(ang)]) / math.sqrt(c), F32)
    return pl.pallas_call(
        _fold_kernel,
        grid=(2, groups),
        in_specs=[pl.BlockSpec((1, d, c), lambda s, g: (layer, 0, g)),
                  pl.BlockSpec((1, c, c), lambda s, g: (s, 0, 0))],
        out_specs=pl.BlockSpec((1, d, c), lambda s, g: (0, 0, s * groups + g)),
        out_shape=jax.ShapeDtypeStruct((1, d, 2 * width), BF16),
        compiler_params=_cparams("parallel", "parallel"),
        name="fold_channel_dft",
    )(w_in, table)


def _dft_kernel(zr_ref, zi_ref, f_ref, g_ref, o_ref, zf, yf, of, *, n1, n2):
    slabs = range(of.shape[0])
    lanes = lambda l: slice(l * LANES, (l + 1) * LANES)
    for l in slabs:
        zf[0, l] = zr_ref[:, lanes(l)].astype(F32)
        zf[1, l] = zi_ref[:, lanes(l)].astype(F32)

    def stage1(t2, carry):
        z = jnp.concatenate(
            [jnp.concatenate([zf[part, l, pl.ds(t2, n1, stride=n2), :] for l in slabs], axis=1) for part in (0, 1)],
            axis=0)
        y = _dot(f_ref[...], z.astype(BF16))
        for l in slabs:
            yf[l, pl.ds(pl.multiple_of(t2 * 2 * n1, 2 * n1), 2 * n1), :] = y[:, lanes(l)]
        return carry

    lax.fori_loop(0, n2, stage1, 0, unroll=8)

    def stage2(f1, carry):
        z = jnp.concatenate(
            [jnp.concatenate([yf[l, pl.ds(off + f1, n2, stride=2 * n1), :] for l in slabs], axis=1)
             for off in (0, n1)], axis=0)
        r = _dot(g_ref[f1], z.astype(BF16))
        for l in slabs:
            of[l, pl.ds(f1, n2, stride=n1), :] = r[:, lanes(l)]
        return carry

    lax.fori_loop(0, n1, stage2, 0, unroll=8)
    for l in slabs:
        o_ref[:, lanes(l)] = of[l].astype(o_ref.dtype)


def _position_dft_real(u, row0, batch, t_len, width):
    n1 = 1 << (_log2(t_len) // 2)
    n2 = t_len // n1
    a1 = 2.0 * np.pi * ((np.arange(n1)[:, None] * np.arange(n1)[None, :]) % n1) / n1
    c1, s1 = np.cos(a1), np.sin(a1)
    f1m = jnp.asarray(np.block([[c1, s1], [-s1, c1]]) / math.sqrt(n1), BF16)
    f1i, f2i, t2i = np.arange(n1)[:, None, None], np.arange(n2)[None, :, None], np.arange(n2)[None, None, :]
    theta = 2.0 * np.pi * (((t2i * f1i) % t_len) / t_len + ((t2i * f2i) % n2) / n2)
    g2m = jnp.asarray(np.concatenate([np.cos(theta), np.sin(theta)], axis=2) / math.sqrt(n2), BF16)

    tc = _tile(width, 256, LANES)
    nj = width // tc
    assert row0 % t_len == 0
    b0 = row0 // t_len
    return pl.pallas_call(
        functools.partial(_dft_kernel, n1=n1, n2=n2),
        grid=(batch, nj),
        in_specs=[
            pl.BlockSpec((t_len, tc), lambda b, j: (b0 + b, j)),
            pl.BlockSpec((t_len, tc), lambda b, j: (b0 + b, nj + j)),
            pl.BlockSpec((2 * n1, 2 * n1), lambda b, j: (0, 0)),
            pl.BlockSpec((n1, n2, 2 * n2), lambda b, j: (0, 0, 0)),
        ],
        out_specs=pl.BlockSpec((t_len, tc), lambda b, j: (b, j)),
        out_shape=jax.ShapeDtypeStruct((batch * t_len, width), BF16),
        scratch_shapes=[pltpu.VMEM((2, tc // LANES, t_len, LANES), F32),
                        pltpu.VMEM((tc // LANES, n2 * 2 * n1, LANES), F32),
                        pltpu.VMEM((tc // LANES, t_len, LANES), F32)],
        compiler_params=_cparams("parallel", "parallel"),
        name="position_dft",
    )(u, u, f1m, g2m)


def _ffn_prep_kernel(x_ref, nw_ref, sh_ref, sc_ref, rw_ref, rb_ref, h_ref, comb_ref, *, n_exp):
    h = _modulated(x_ref[...], nw_ref[...], sh_ref[0], sc_ref[0])
    h_ref[...] = h.astype(h_ref.dtype)
    h_hi, h_lo = _split2(h)
    r_hi, r_lo = _split2(rw_ref[...])
    logits = _dot_nt(r_hi, h_hi) + _dot_nt(r_lo, h_hi) + _dot_nt(r_hi, h_lo)
    s = _sigmoid(logits)
    sel = s + rb_ref[...]
    srow = [s[e:e + 1, :] for e in range(n_exp)]
    row = [sel[e:e + 1, :] for e in range(n_exp)]
    epg = n_exp // N_GROUPS

    def beats(a, ia, b, ib):
        return (a >= b) if ia < ib else (a > b)

    rank, gscore = {}, []
    for g in range(N_GROUPS):
        members = range(g * epg, (g + 1) * epg)
        for i in members:
            rank[i] = sum(beats(row[j], j, row[i], i).astype(F32) for j in members if j != i)
        gscore.append(sum(jnp.where(rank[i] < GROUP_SCORE_K, row[i], 0.0) for i in members))
    picked = []
    for g in range(N_GROUPS):
        grank = sum(beats(gscore[j], j, gscore[g], g).astype(F32) for j in range(N_GROUPS) if j != g)
        for i in range(g * epg, (g + 1) * epg):
            picked.append(jnp.where((grank < 1.0) & (rank[i] < TOP_K), srow[i], 0.0))
    denom = sum(picked)
    comb_ref[...] = jnp.concatenate([p / denom for p in picked], axis=0)


def _ffn_prep(x, m, nw, mod, layer, router_wt, router_b):
    d = x.shape[1]
    n_exp = router_wt.shape[0]
    tm = _tile(math.gcd(m, mod.rows_per_mod), 512, LANES)
    return pl.pallas_call(
        functools.partial(_ffn_prep_kernel, n_exp=n_exp),
        grid=(m // tm,),
        in_specs=[
            pl.BlockSpec((tm, d), lambda i: (i, 0)),
            pl.BlockSpec((1, d), lambda i: (0, 0)),
            mod.spec(layer, 3, tm),
            mod.spec(layer, 4, tm),
            pl.BlockSpec((n_exp, d), lambda i: (0, 0)),
            pl.BlockSpec((n_exp, 1), lambda i: (0, 0)),
        ],
        out_specs=[pl.BlockSpec((tm, d), lambda i: (i, 0)), pl.BlockSpec((n_exp, tm), lambda i: (0, i))],
        out_shape=[jax.ShapeDtypeStruct((m, d), F32), jax.ShapeDtypeStruct((n_exp, m), F32)],
        compiler_params=_cparams("parallel"),
        name="ffn_prep",
    )(x, nw.reshape(1, d), mod.table, mod.table, router_wt, router_b.reshape(n_exp, 1))


def _route(comb_t, tg):
    assert TOP_K == 2
    n_exp, n = comb_t.shape
    mask = comb_t > 0.0
    cnt = jnp.sum(mask.astype(I32), axis=1)
    padded = ((cnt + tg - 1) // tg) * tg
    ends = jnp.cumsum(padded)
    dest = (ends - padded)[:, None] + jnp.cumsum(mask.astype(I32), axis=1) - 1
    eidx = jnp.arange(n_exp, dtype=I32)[:, None]
    first = jnp.min(jnp.where(mask, eidx, n_exp), axis=0)
    final = jnp.max(jnp.where(mask, eidx, -1), axis=0)
    is_a = mask & (eidx == first)
    is_b = mask & (eidx == final) & (final != first)
    pick = lambda sel, val: jnp.sum(jnp.where(sel, val, 0), axis=0)
    w_ab = jnp.stack([pick(is_a, comb_t), pick(is_b, comb_t)], axis=1)
    p_rows = TOP_K * n + n_exp * tg
    d_a = jnp.where(jnp.any(is_a, axis=0), pick(is_a, dest), p_rows).astype(I32)
    d_b = jnp.where(jnp.any(is_b, axis=0), pick(is_b, dest), p_rows).astype(I32)
    tile_start = jnp.arange(p_rows // tg, dtype=I32) * tg
    tile_exp = jnp.minimum(jnp.sum((tile_start[:, None] >= ends[None, :]).astype(I32), axis=1), n_exp - 1)
    tile_ok = (tile_start < ends[-1]).astype(I32)
    tile_new = jnp.concatenate([jnp.ones((1,), I32), (tile_exp[1:] != tile_exp[:-1]).astype(I32)])
    return tile_exp, tile_ok, tile_new, d_a, d_b, w_ab, p_rows


def _src_kernel(da_ref, db_ref, src_ref, *, n_tok):
    def zero(r, carry):
        src_ref[r] = 0
        return carry
    lax.fori_loop(0, src_ref.shape[0], zero, 0, unroll=8)

    def put(t, carry):
        src_ref[da_ref[t]] = t
        src_ref[db_ref[t]] = t
        return carry
    lax.fori_loop(0, n_tok, put, 0, unroll=8)


def _sorted_row_tokens(d_a, d_b, p_rows):
    n_tok = d_a.shape[0]
    smem = pl.BlockSpec(memory_space=pltpu.SMEM)
    return pl.pallas_call(
        functools.partial(_src_kernel, n_tok=n_tok),
        in_specs=[smem, smem],
        out_specs=smem,
        out_shape=jax.ShapeDtypeStruct((p_rows + 1,), I32),
        name="sorted_row_tokens",
    )(d_a, d_b)


def _row_copies(idx_ref, base, n_rows, src_hbm, dst_buf, sem, wait):
    def body(r, carry):
        cp = pltpu.make_async_copy(src_hbm.at[pl.ds(idx_ref[base + r], 1)], dst_buf.at[pl.ds(r, 1)], sem)
        if wait:
            cp.wait()
        else:
            cp.start()
        return carry
    lax.fori_loop(0, n_rows, body, 0, unroll=8)


def _moe_expert_kernel(texp_ref, tval_ref, tnew_ref, src_ref, h_hbm, wg_ref, wu_ref, wd_ref, y_ref,
                       xbuf, sem, wgb, wub, wdb, *, tg):
    i = pl.program_id(0)

    def fetch(tile, slot, wait):
        @pl.when(tval_ref[tile] != 0)
        def _():
            _row_copies(src_ref, tile * tg, tg, h_hbm, xbuf.at[slot], sem.at[slot], wait)

    @pl.when(i == 0)
    def _():
        fetch(0, 0, False)

    @pl.when(i + 1 < pl.num_programs(0))
    def _():
        fetch(i + 1, (i + 1) % 2, False)

    @pl.when((tval_ref[i] != 0) & (tnew_ref[i] != 0))
    def _():
        wgb[...] = wg_ref[0, 0].astype(BF16)
        wub[...] = wu_ref[0, 0].astype(BF16)
        wdb[...] = wd_ref[0, 0].astype(BF16)

    fetch(i, i % 2, True)

    @pl.when(tval_ref[i] != 0)
    def _():
        x = xbuf[i % 2].astype(BF16)
        act = (_silu(_dot(x, wgb[...])) * _dot(x, wub[...])).astype(BF16)
        y_ref[...] = _dot(act, wdb[...])

    @pl.when(tval_ref[i] == 0)
    def _():
        y_ref[...] = jnp.zeros_like(y_ref)


def _moe_experts(h, src, tile_exp, tile_ok, tile_new, wg, wu, wd, layer, tg, p_rows):
    n, d = h.shape
    f = wg.shape[3]
    wmap = lambda i, te, ok, new, s: (layer, te[i], 0, 0)
    return pl.pallas_call(
        functools.partial(_moe_expert_kernel, tg=tg),
        grid_spec=pltpu.PrefetchScalarGridSpec(
            num_scalar_prefetch=4,
            grid=(p_rows // tg,),
            in_specs=[
                pl.BlockSpec(memory_space=pl.ANY),
                pl.BlockSpec((1, 1, d, f), wmap),
                pl.BlockSpec((1, 1, d, f), wmap),
                pl.BlockSpec((1, 1, f, d), wmap),
            ],
            out_specs=pl.BlockSpec((tg, d), lambda i, te, ok, new, s: (i, 0)),
            scratch_shapes=[pltpu.VMEM((2, tg, d), F32), pltpu.SemaphoreType.DMA((2,)),
                            pltpu.VMEM((d, f), BF16), pltpu.VMEM((d, f), BF16), pltpu.VMEM((f, d), BF16)],
        ),
        out_shape=jax.ShapeDtypeStruct((p_rows, d), F32),
        compiler_params=_cparams("arbitrary"),
        name="moe_experts",
    )(tile_exp, tile_ok, tile_new, src, h, wg, wu, wd)


def _moe_combine_kernel(da_ref, db_ref, y_hbm, x_ref, w_ref, g_ref, o_ref, ya, yb, sem, *, tm):
    i = pl.program_id(0)

    def fetch(tile, slot, wait):
        _row_copies(da_ref, tile * tm, tm, y_hbm, ya.at[slot], sem.at[0, slot], wait)
        _row_copies(db_ref, tile * tm, tm, y_hbm, yb.at[slot], sem.at[1, slot], wait)

    @pl.when(i == 0)
    def _():
        fetch(0, 0, False)

    @pl.when(i + 1 < pl.num_programs(0))
    def _():
        fetch(i + 1, (i + 1) % 2, False)

    fetch(i, i % 2, True)
    w = w_ref[...]
    mix = w[:, 0:1] * ya[i % 2] + w[:, 1:2] * yb[i % 2]
    o_ref[...] = x_ref[...] + g_ref[0] * mix


def _moe_combine(y, row_a, row_b, w_ab, x, m, mod, layer):
    d = x.shape[1]
    tm = _tile(math.gcd(m, mod.rows_per_mod), 256)
    return pl.pallas_call(
        functools.partial(_moe_combine_kernel, tm=tm),
        grid_spec=pltpu.PrefetchScalarGridSpec(
            num_scalar_prefetch=2,
            grid=(m // tm,),
            in_specs=[
                pl.BlockSpec(memory_space=pl.ANY),
                pl.BlockSpec((tm, d), lambda i, a, b: (i, 0)),
                pl.BlockSpec((tm, 2), lambda i, a, b: (i, 0)),
                mod.spec(layer, 5, tm),
            ],
            out_specs=pl.BlockSpec((tm, d), lambda i, a, b: (i, 0)),
            scratch_shapes=[pltpu.VMEM((2, tm, d), F32), pltpu.VMEM((2, tm, d), F32),
                            pltpu.SemaphoreType.DMA((2, 2))],
        ),
        out_shape=jax.ShapeDtypeStruct((m, d), F32),
        compiler_params=_cparams("arbitrary"),
        name="moe_combine",
    )(row_a, row_b, y, x, w_ab, mod.table)


def _moe(h, comb_t, wg, wu, wd, layer_w, x, m, mod, layer):
    tg = 256 if m * TOP_K >= 256 * wg.shape[1] else 64
    tile_exp, tile_ok, tile_new, d_a, d_b, w_ab, p_rows = _route(comb_t, tg)
    src = _sorted_row_tokens(d_a, d_b, p_rows)
    y = _moe_experts(h, src, tile_exp, tile_ok, tile_new, wg, wu, wd, layer_w, tg, p_rows)
    none = lambda rows: jnp.where(rows == p_rows, 0, rows)
    return _moe_combine(y, none(d_a), none(d_b), w_ab, x, m, mod, layer)


def kernel(x, c, ctx, c_ctx, ada_w, ada_b, norm_mix_w, norm_ffn_w, dn_w_in, dn_conv_w, dn_A_log, dn_dt_bias, dn_norm_w, dn_w_out, fn_w_in, fn_w_out, router_w, router_bias, moe_w_gate, moe_w_up, moe_w_down, final_norm_w):
    batch, seq, d = x.shape
    ctx_len = ctx.shape[1]
    depth = ada_w.shape[0]
    nh = DN_HEADS
    dn_width = dn_w_out.shape[1]
    hd = dn_width // nh
    fn_width = fn_w_in.shape[2]
    m_lat, m_all = batch * seq, batch * (seq + ctx_len)
    assert batch * ctx_len <= seq and seq % ctx_len == 0

    mod_rows = -(-(batch + 1) // 8) * 8
    cond = jnp.zeros((mod_rows, d), F32).at[:batch].set(c).at[batch].set(c_ctx)
    table = _ada_table(cond, ada_w, ada_b)
    mod = _Mod(table.reshape(depth * mod_rows * N_MOD, 1, d), mod_rows, d, seq)

    router_wt = router_w.T
    xs = jnp.concatenate([x.reshape(m_lat, d), ctx.reshape(batch * ctx_len, d)], axis=0)

    for i in range(depth):
        use_dn = i % N_MIXERS == 0
        j = i // N_MIXERS
        ctx_out = i < depth - 1
        ctx_live = use_dn or ctx_out
        m = m_all if ctx_live else m_lat

        h = _modulate(xs, m, norm_mix_w[i], mod, i, 0, 1)
        if use_dn:
            p = _matmul(h, dn_w_in, j, BF16, 1024, n=4 * dn_width)
            ab = _matmul(h, dn_w_in, j, F32, LANES, col0=4 * dn_width, n=LANES)
            q = _dn_conv(p, dn_conv_w, j, 0, "q", m_lat, ctx_len, dn_width, hd)
            k = _dn_conv(p, dn_conv_w, j, 1, "k", m_lat, ctx_len, dn_width, hd)
            v = _dn_conv(p, dn_conv_w, j, 2, "v", m_lat, ctx_len, dn_width, hd)
            g, beta = _dn_gates(ab, dn_A_log[j].reshape(-1), dn_dt_bias[j].reshape(-1), nh)
            g_rows = g.reshape(m // CHUNK, CHUNK, 2 * nh).transpose(0, 2, 1)
            o_f = _dn_scan(q, k, v, g, g_rows, beta, False, batch, m_lat, nh, hd)
            o_b = _dn_scan(q, k, v, g, g_rows, beta, True, batch, m_lat, nh, hd)
            y = _dn_out(o_f, o_b, p, dn_norm_w[j], dn_width, hd)
            xs = _matmul_gated_residual(y, dn_w_out, j, xs, mod, i, 2)
        else:
            w_fold = _fold_channel_dft(fn_w_in, j, FN_GROUPS)
            u = _matmul(h, w_fold, 0, BF16, 1024)
            mixed = _position_dft_real(u, 0, batch, seq, fn_width)
            if ctx_out:
                mixed = jnp.concatenate([mixed, _position_dft_real(u, m_lat, batch, ctx_len, fn_width)], axis=0)
            xs = _matmul_gated_residual(mixed, fn_w_out, j, xs, mod, i, 2)

        hf, comb_t = _ffn_prep(xs, m, norm_ffn_w[i], mod, i, router_wt, router_bias)
        xs = _moe(hf, comb_t, moe_w_gate, moe_w_up, moe_w_down, i, xs, m, mod, i)

    return _final_norm(xs, m_lat, final_norm_w).reshape(batch, seq, d)
```

```python
import functools
import math

import numpy as np
import jax
import jax.numpy as jnp
from jax import lax
from jax.experimental import pallas as pl
from jax.experimental.pallas import tpu as pltpu

F32 = jnp.float32
BF16 = jnp.bfloat16
I32 = jnp.int32

GRID_W = 64
CHUNK = 64
CONV_W = 5
DN_HEADS = 16
FN_GROUPS = 4
N_GROUPS = 4
GROUP_SCORE_K = 2
TOP_K = 2
N_MOD = 6
N_MIXERS = 2
EPS = 1e-6

VMEM_LIMIT = 56 * 1024 * 1024
LANES = 128


def _cparams(*sem):
    return pltpu.CompilerParams(dimension_semantics=sem, vmem_limit_bytes=VMEM_LIMIT)


def _dot(a, b):
    return jnp.dot(a, b, preferred_element_type=F32)


def _dot_nt(a, b):
    return lax.dot_general(a, b, (((1,), (1,)), ((), ())), preferred_element_type=F32)


def _dot_tn(a, b):
    return lax.dot_general(a, b, (((0,), (0,)), ((), ())), preferred_element_type=F32)


def _split2(x):
    hi = x.astype(BF16)
    lo = (x - hi.astype(F32)).astype(BF16)
    return hi, lo


def _split3(x):
    hi = x.astype(BF16)
    r = x - hi.astype(F32)
    mid = r.astype(BF16)
    lo = (r - mid.astype(F32)).astype(BF16)
    return hi, mid, lo


def _silu(x):
    return x / (1.0 + jnp.exp(-x))


def _sigmoid(x):
    return 1.0 / (1.0 + jnp.exp(-x))


def _tile(n, pref, mult=8):
    if n <= pref:
        return n
    t = (pref // mult) * mult
    while t >= mult:
        if n % t == 0:
            return t
        t -= mult
    return n


def _log2(n):
    assert n & (n - 1) == 0, n
    return n.bit_length() - 1


def _ada_kernel(c_ref, w_ref, b_ref, o_ref):
    s_hi, s_lo = _split2(_silu(c_ref[...]))
    w_hi, w_lo = _split2(w_ref[0])
    acc = _dot(s_hi, w_hi) + _dot(s_lo, w_hi) + _dot(s_hi, w_lo)
    o_ref[0] = acc + b_ref[0]


def _ada_table(cond, ada_w, ada_b):
    depth, d, n = ada_w.shape
    r = cond.shape[0]
    tn = _tile(n, 512, 128)
    return pl.pallas_call(
        _ada_kernel,
        grid=(depth, n // tn),
        in_specs=[
            pl.BlockSpec((r, d), lambda l, j: (0, 0)),
            pl.BlockSpec((1, d, tn), lambda l, j: (l, 0, j)),
            pl.BlockSpec((1, 1, tn), lambda l, j: (l, 0, j)),
        ],
        out_specs=pl.BlockSpec((1, r, tn), lambda l, j: (l, 0, j)),
        out_shape=jax.ShapeDtypeStruct((depth, r, n), F32),
        compiler_params=_cparams("parallel", "parallel"),
        name="ada_table",
    )(cond, ada_w, ada_b.reshape(depth, 1, n))


class _Mod:
    def __init__(self, table3, mod_rows, d, rows_per_mod):
        self.table = table3
        self.mod_rows = mod_rows
        self.d = d
        self.rows_per_mod = rows_per_mod

    def spec(self, layer, k, tm, row_axis=0, width=None, col=None):
        assert self.rows_per_mod % tm == 0
        width = self.d if width is None else width
        rows, mr = self.rows_per_mod, self.mod_rows

        def index_map(*g):
            row = (g[row_axis] * tm) // rows
            return ((layer * mr + row) * N_MOD + k, 0, 0 if col is None else col(*g))
        return pl.BlockSpec((1, 1, width), index_map)


def _modulated(x, nw, shift, scale):
    ms = jnp.mean(x * x, axis=-1, keepdims=True)
    return (x * lax.rsqrt(ms + EPS) * nw) * (1.0 + scale) + shift


def _modulate_kernel(x_ref, nw_ref, sh_ref, sc_ref, o_ref):
    o_ref[...] = _modulated(x_ref[...], nw_ref[...], sh_ref[0], sc_ref[0]).astype(o_ref.dtype)


def _modulate(x, m, nw, mod, layer, k_shift, k_scale):
    d = x.shape[1]
    tm = _tile(math.gcd(m, mod.rows_per_mod), 512)
    return pl.pallas_call(
        _modulate_kernel,
        grid=(m // tm,),
        in_specs=[
            pl.BlockSpec((tm, d), lambda i: (i, 0)),
            pl.BlockSpec((1, d), lambda i: (0, 0)),
            mod.spec(layer, k_shift, tm),
            mod.spec(layer, k_scale, tm),
        ],
        out_specs=pl.BlockSpec((tm, d), lambda i: (i, 0)),
        out_shape=jax.ShapeDtypeStruct((m, d), BF16),
        compiler_params=_cparams("parallel"),
        name="modulate",
    )(x, nw.reshape(1, d), mod.table, mod.table)


def _final_norm_kernel(x_ref, nw_ref, o_ref):
    x = x_ref[...]
    ms = jnp.mean(x * x, axis=-1, keepdims=True)
    o_ref[...] = x * lax.rsqrt(ms + EPS) * nw_ref[...]


def _final_norm(x, m, nw):
    d = x.shape[1]
    tm = _tile(m, 512)
    return pl.pallas_call(
        _final_norm_kernel,
        grid=(m // tm,),
        in_specs=[pl.BlockSpec((tm, d), lambda i: (i, 0)), pl.BlockSpec((1, d), lambda i: (0, 0))],
        out_specs=pl.BlockSpec((tm, d), lambda i: (i, 0)),
        out_shape=jax.ShapeDtypeStruct((m, d), F32),
        compiler_params=_cparams("parallel"),
        name="final_norm",
    )(x, nw.reshape(1, d))


def _matmul_kernel(a_ref, w_ref, o_ref, wb_ref, *, valid_cols):
    @pl.when(pl.program_id(1) == 0)
    def _():
        w = w_ref[0]
        if valid_cols < w.shape[1]:
            w = jnp.where(lax.broadcasted_iota(I32, w.shape, 1) < valid_cols, w, 0.0)
        wb_ref[...] = w.astype(BF16)

    o_ref[...] = _dot(a_ref[...], wb_ref[...]).astype(o_ref.dtype)


def _matmul(a, w, layer, out_dtype, tn_pref, col0=0, n=None):
    m, k = a.shape
    n = w.shape[2] if n is None else n
    tm, tn = _tile(m, 1024), _tile(n, tn_pref, LANES)
    assert col0 % tn == 0
    cb = col0 // tn
    valid_cols = tn if col0 + n <= w.shape[2] else w.shape[2] - col0
    assert valid_cols == tn or n == tn
    return pl.pallas_call(
        functools.partial(_matmul_kernel, valid_cols=valid_cols),
        grid=(n // tn, m // tm),
        in_specs=[pl.BlockSpec((tm, k), lambda j, i: (i, 0)),
                  pl.BlockSpec((1, k, tn), lambda j, i: (layer, 0, cb + j))],
        out_specs=pl.BlockSpec((tm, tn), lambda j, i: (i, j)),
        out_shape=jax.ShapeDtypeStruct((m, n), out_dtype),
        scratch_shapes=[pltpu.VMEM((k, tn), BF16)],
        compiler_params=_cparams("parallel", "arbitrary"),
        name="matmul",
    )(a, w)


def _matmul_res_kernel(a_ref, w_ref, x_ref, g_ref, o_ref, wb_ref):
    @pl.when(pl.program_id(1) == 0)
    def _():
        wb_ref[...] = w_ref[0].astype(BF16)

    o_ref[...] = x_ref[...] + g_ref[0] * _dot(a_ref[...], wb_ref[...])


def _matmul_gated_residual(a, w, layer_w, x, mod, layer, k_gate, tn_pref=512):
    m, k = a.shape
    n = w.shape[2]
    tm = _tile(math.gcd(m, mod.rows_per_mod), 1024)
    tn = _tile(n, tn_pref, LANES)
    return pl.pallas_call(
        _matmul_res_kernel,
        grid=(n // tn, m // tm),
        in_specs=[
            pl.BlockSpec((tm, k), lambda j, i: (i, 0)),
            pl.BlockSpec((1, k, tn), lambda j, i: (layer_w, 0, j)),
            pl.BlockSpec((tm, tn), lambda j, i: (i, j)),
            mod.spec(layer, k_gate, tm, row_axis=1, width=tn, col=lambda j, i: j),
        ],
        out_specs=pl.BlockSpec((tm, tn), lambda j, i: (i, j)),
        out_shape=jax.ShapeDtypeStruct((m, n), F32),
        scratch_shapes=[pltpu.VMEM((k, tn), BF16)],
        compiler_params=_cparams("parallel", "arbitrary"),
        name="matmul_gated_residual",
    )(a, w, x, mod.table)


def _head_sumsq(y, ones_bd):
    sq_hi, sq_lo = _split2(y * y)
    return _dot(sq_hi, ones_bd) + _dot(sq_lo, ones_bd)


def _dn_conv_kernel(u_ref, cw_ref, e_ref, o_ref, coef_ref, *, lat_tiles, lat_len, ctx_len, mode, head_dim):
    i = pl.program_id(1)
    tm = u_ref.shape[0]
    pad = CONV_W // 2
    cw = cw_ref[0]

    @pl.when((i == 0) | (i == lat_tiles))
    def _():
        conv_len = jnp.where(i < lat_tiles, lat_len, ctx_len)
        pos = lax.broadcasted_iota(I32, (tm, 1), 0) & (conv_len - 1)
        for j in range(CONV_W):
            d = j - pad
            inside = ((pos + d >= 0) & (pos + d < conv_len)).astype(F32)
            coef_ref[j] = inside * cw[j:j + 1, :]

    u = u_ref[...].astype(F32)
    acc = u * cw[pad:pad + 1, :]
    for j in range(CONV_W):
        d = j - pad
        if d != 0:
            acc = acc + pltpu.roll(u, (-d) % tm, 0) * coef_ref[j]
    y = _silu(acc)
    if mode != "v":
        y = y * lax.rsqrt(_head_sumsq(y, e_ref[...]) + EPS)
        if mode == "q":
            y = y * (head_dim ** -0.5)
    o_ref[...] = y.astype(o_ref.dtype)


def _dn_conv(p, conv_w, layer, section, mode, m_lat, ctx_len, width, head_dim):
    m = p.shape[0]
    tc = _tile(width, 512, LANES)
    tm = max(GRID_W, ctx_len, 256)
    assert m_lat % tm == 0 and m % tm == 0 and tm % GRID_W == 0 and tm % ctx_len == 0
    ones_bd = jnp.asarray(np.kron(np.eye(tc // head_dim), np.ones((head_dim, head_dim))), BF16)
    nsec = width // tc
    return pl.pallas_call(
        functools.partial(_dn_conv_kernel, lat_tiles=m_lat // tm, lat_len=GRID_W, ctx_len=ctx_len, mode=mode,
                          head_dim=head_dim),
        grid=(nsec, m // tm),
        in_specs=[
            pl.BlockSpec((tm, tc), lambda j, i: (i, section * nsec + j)),
            pl.BlockSpec((1, CONV_W, tc), lambda j, i: (layer, 0, section * nsec + j)),
            pl.BlockSpec((tc, tc), lambda j, i: (0, 0)),
        ],
        out_specs=pl.BlockSpec((tm, tc), lambda j, i: (i, j)),
        out_shape=jax.ShapeDtypeStruct((m, width), BF16),
        scratch_shapes=[pltpu.VMEM((CONV_W, tm, tc), F32)],
        compiler_params=_cparams("parallel", "arbitrary"),
        name="dn_conv_" + mode,
    )(p, conv_w, ones_bd)


def _dn_gates_kernel(ab_ref, al_ref, dt_ref, tl_ref, tu_ref, g_ref, beta_ref, *, nh):
    ab = ab_ref[:, :4 * nh]
    a = ab[:, :2 * nh] + dt_ref[...]
    softplus = jnp.maximum(a, 0.0) + jnp.log1p(jnp.exp(-jnp.abs(a)))
    la = -jnp.exp(al_ref[...]) * softplus
    beta_ref[...] = _sigmoid(ab[:, 2 * nh:])
    parts = _split3(la)
    g_pre = sum(_dot(tl_ref[...], p) for p in parts)
    g_suf = sum(_dot(tu_ref[...], p) for p in parts)
    col = lax.broadcasted_iota(I32, la.shape, 1)
    g_ref[...] = jnp.where(col < nh, g_pre, g_suf)


def _dn_gates(ab, a_log, dt_bias, nh):
    m = ab.shape[0]
    tm = _tile(m, 256, CHUNK)
    r = np.arange(tm)
    same = (r[:, None] // CHUNK) == (r[None, :] // CHUNK)
    tri_l = jnp.asarray(same & (r[None, :] <= r[:, None]), BF16)
    tri_u = jnp.asarray(same & (r[None, :] >= r[:, None]), BF16)
    return pl.pallas_call(
        functools.partial(_dn_gates_kernel, nh=nh),
        grid=(m // tm,),
        in_specs=[
            pl.BlockSpec((tm, ab.shape[1]), lambda i: (i, 0)),
            pl.BlockSpec((1, 2 * nh), lambda i: (0, 0)),
            pl.BlockSpec((1, 2 * nh), lambda i: (0, 0)),
            pl.BlockSpec((tm, tm), lambda i: (0, 0)),
            pl.BlockSpec((tm, tm), lambda i: (0, 0)),
        ],
        out_specs=[pl.BlockSpec((tm, 2 * nh), lambda i: (i, 0))] * 2,
        out_shape=[jax.ShapeDtypeStruct((m, 2 * nh), F32)] * 2,
        compiler_params=_cparams("parallel"),
        name="dn_gates",
    )(ab, a_log.reshape(1, 2 * nh), dt_bias.reshape(1, 2 * nh), tri_l, tri_u)


def _unit_tri_inverses(lms):
    shape = lms[0].shape
    ri = lax.broadcasted_iota(I32, shape, 0)
    ci = lax.broadcasted_iota(I32, shape, 1)
    eye = (ri == ci).astype(F32)
    pair = (ri >> 1) == (ci >> 1)
    ts = [eye - jnp.where(pair, lm, 0.0) for lm in lms]
    for lvl in range(1, _log2(shape[0])):
        cross = ((ri >> (lvl + 1)) == (ci >> (lvl + 1))) & ((ri >> lvl) != (ci >> lvl))
        offs = [jnp.where(cross, lm, 0.0).astype(BF16) for lm in lms]
        tbs = [t.astype(BF16) for t in ts]
        ps = [_dot(tb, off).astype(BF16) for tb, off in zip(tbs, offs)]
        ts = [t - _dot(p, tb) for t, p, tb in zip(ts, ps, tbs)]
    return ts


def _dn_scan_kernel(q_ref, k_ref, v_ref, gc_ref, gr_ref, b_ref, o_ref, s_ref, *, backward, nh, hd):
    @pl.when(pl.program_id(1) == 0)
    def _():
        s_ref[...] = jnp.zeros_like(s_ref)

    n = CHUNK
    heads = range(nh)
    ri = lax.broadcasted_iota(I32, (n, n), 0)
    ci = lax.broadcasted_iota(I32, (n, n), 1)
    causal = (ci >= ri) if backward else (ci <= ri)
    strict = (ci > ri) if backward else (ci < ri)
    last = 0 if backward else n - 1
    dcol = nh if backward else 0
    gc_all = gc_ref[...]
    gr_all = gr_ref[0]
    b_all = b_ref[...]
    gcol = [gc_all[:, dcol + h:dcol + h + 1] for h in heads]
    grow = [gr_all[dcol + h:dcol + h + 1, :] for h in heads]
    bcol = [b_all[:, dcol + h:dcol + h + 1] for h in heads]
    glast = [g[last:last + 1, :] for g in gcol]
    q = [q_ref[:, h * hd:(h + 1) * hd] for h in heads]
    k = [k_ref[:, h * hd:(h + 1) * hd] for h in heads]
    v = [v_ref[:, h * hd:(h + 1) * hd] for h in heads]
    kq_k = [_dot_nt(jnp.concatenate([k[h], q[h]], axis=0), k[h]) for h in heads]
    decay = [jnp.exp(jnp.where(causal, gcol[h] - grow[h], -jnp.inf)) for h in heads]
    lm = [jnp.where(strict, bcol[h] * kq_k[h][:n] * decay[h], 0.0) for h in heads]
    intra = [(kq_k[h][n:] * decay[h]).astype(BF16) for h in heads]
    t = _unit_tri_inverses(lm)
    eg = [jnp.exp(g) for g in gcol]
    rhs = [jnp.concatenate([v[h].astype(F32) * bcol[h], k[h].astype(F32) * (bcol[h] * eg[h])],
                           axis=1).astype(BF16) for h in heads]
    uw = [_dot(t[h].astype(BF16), rhs[h]) for h in heads]
    s = [s_ref[h] for h in heads]
    wq = [jnp.concatenate([uw[h][:, hd:], q[h].astype(F32) * eg[h]], axis=0).astype(BF16) for h in heads]
    ws = [_dot(wq[h], s[h].astype(BF16)) for h in heads]
    vb = [(uw[h][:, :hd] - ws[h][:n]).astype(BF16) for h in heads]
    kg = [(k[h].astype(F32) * jnp.exp(glast[h] - gcol[h])).astype(BF16) for h in heads]
    for h in heads:
        o_ref[:, h * hd:(h + 1) * hd] = ws[h][n:] + _dot(intra[h], vb[h])
    for h in heads:
        s_ref[h] = s[h] * jnp.exp(glast[h]) + _dot_tn(kg[h], vb[h])


def _dn_scan(q, k, v, g, g_rows, beta, backward, batch, m_lat, nh, hd):
    m = q.shape[0]
    nl = m_lat // batch // CHUNK
    nc = (m - m_lat) // batch // CHUNK
    ctx0 = m_lat // CHUNK

    def blk(b, c):
        ctx_blk = ctx0 + b * nc + (nc - 1 - c if backward else c)
        lat_blk = b * nl + (nl - 1 - (c - nc) if backward else c - nc)
        return jnp.where(c < nc, ctx_blk, lat_blk)

    tok = lambda b, c: (blk(b, c), 0)
    tok3 = lambda b, c: (blk(b, c), 0, 0)
    return pl.pallas_call(
        functools.partial(_dn_scan_kernel, backward=backward, nh=nh, hd=hd),
        grid=(batch, nc + nl),
        in_specs=[
            pl.BlockSpec((CHUNK, nh * hd), tok),
            pl.BlockSpec((CHUNK, nh * hd), tok),
            pl.BlockSpec((CHUNK, nh * hd), tok),
            pl.BlockSpec((CHUNK, 2 * nh), tok),
            pl.BlockSpec((1, 2 * nh, CHUNK), tok3),
            pl.BlockSpec((CHUNK, 2 * nh), tok),
        ],
        out_specs=pl.BlockSpec((CHUNK, nh * hd), tok),
        out_shape=jax.ShapeDtypeStruct((m, nh * hd), F32),
        scratch_shapes=[pltpu.VMEM((nh, hd, hd), F32)],
        compiler_params=_cparams("parallel", "arbitrary"),
        name="dn_scan_bwd" if backward else "dn_scan_fwd",
    )(q, k, v, g, g_rows, beta)


def _dn_out_kernel(of_ref, ob_ref, gate_ref, nw_ref, e_ref, y_ref, *, hd):
    o = of_ref[...] + ob_ref[...]
    ms = _head_sumsq(o, e_ref[...]) * (1.0 / hd)
    y = o * lax.rsqrt(ms + EPS) * nw_ref[...]
    y_ref[...] = (y * _silu(gate_ref[...].astype(F32))).astype(y_ref.dtype)


def _dn_out(o_f, o_b, p, norm_w, width, hd):
    m = o_f.shape[0]
    tc = _tile(width, 512, LANES)
    tm = _tile(m, 512)
    nsec = width // tc
    ones_bd = jnp.asarray(np.kron(np.eye(tc // hd), np.ones((hd, hd))), BF16)
    nw = jnp.tile(norm_w.astype(F32), tc // hd).reshape(1, tc)
    return pl.pallas_call(
        functools.partial(_dn_out_kernel, hd=hd),
        grid=(m // tm, nsec),
        in_specs=[
            pl.BlockSpec((tm, tc), lambda i, j: (i, j)),
            pl.BlockSpec((tm, tc), lambda i, j: (i, j)),
            pl.BlockSpec((tm, tc), lambda i, j: (i, 3 * nsec + j)),
            pl.BlockSpec((1, tc), lambda i, j: (0, 0)),
            pl.BlockSpec((tc, tc), lambda i, j: (0, 0)),
        ],
        out_specs=pl.BlockSpec((tm, tc), lambda i, j: (i, j)),
        out_shape=jax.ShapeDtypeStruct((m, width), BF16),
        compiler_params=_cparams("parallel", "parallel"),
        name="dn_out",
    )(o_f, o_b, p, nw, ones_bd)


def _fold_kernel(w_ref, t_ref, o_ref):
    w_hi, w_lo = _split2(w_ref[0])
    t_hi, t_lo = _split2(t_ref[0])
    o_ref[0] = (_dot(w_hi, t_hi) + _dot(w_lo, t_hi) + _dot(w_hi, t_lo)).astype(o_ref.dtype)


def _fold_channel_dft(w_in, layer, groups):
    _, d, width = w_in.shape
    c = width // groups
    idx = np.arange(c)
    ang = 2.0 * np.pi * ((idx[:, None] * idx[None, :]) % c) / c
    table = jnp.asarray(np.stack([np.cos(ang), -np.sin(ang)]) / math.sqrt(c), F32)
    return pl.pallas_call(
        _fold_kernel,
        grid=(2, groups),
        in_specs=[pl.BlockSpec((1, d, c), lambda s, g: (layer, 0, g)),
                  pl.BlockSpec((1, c, c), lambda s, g: (s, 0, 0))],
        out_specs=pl.BlockSpec((1, d, c), lambda s, g: (0, 0, s * groups + g)),
        out_shape=jax.ShapeDtypeStruct((1, d, 2 * width), BF16),
        compiler_params=_cparams("parallel", "parallel"),
        name="fold_channel_dft",
    )(w_in, table)


def _dft_kernel(zr_ref, zi_ref, f_ref, g_ref, o_ref, zf, yf, of, *, n1, n2):
    slabs = range(of.shape[0])
    lanes = lambda l: slice(l * LANES, (l + 1) * LANES)
    for l in slabs:
        zf[0, l] = zr_ref[:, lanes(l)].astype(F32)
        zf[1, l] = zi_ref[:, lanes(l)].astype(F32)

    def stage1(t2, carry):
        z = jnp.concatenate(
            [jnp.concatenate([zf[part, l, pl.ds(t2, n1, stride=n2), :] for l in slabs], axis=1) for part in (0, 1)],
            axis=0)
        y = _dot(f_ref[...], z.astype(BF16))
        for l in slabs:
            yf[l, pl.ds(pl.multiple_of(t2 * 2 * n1, 2 * n1), 2 * n1), :] = y[:, lanes(l)]
        return carry

    lax.fori_loop(0, n2, stage1, 0, unroll=8)

    def stage2(f1, carry):
        z = jnp.concatenate(
            [jnp.concatenate([yf[l, pl.ds(off + f1, n2, stride=2 * n1), :] for l in slabs], axis=1)
             for off in (0, n1)], axis=0)
        r = _dot(g_ref[f1], z.astype(BF16))
        for l in slabs:
            of[l, pl.ds(f1, n2, stride=n1), :] = r[:, lanes(l)]
        return carry

    lax.fori_loop(0, n1, stage2, 0, unroll=8)
    for l in slabs:
        o_ref[:, lanes(l)] = of[l].astype(o_ref.dtype)


def _position_dft_real(u, row0, batch, t_len, width):
    n1 = 1 << (_log2(t_len) // 2)
    n2 = t_len // n1
    a1 = 2.0 * np.pi * ((np.arange(n1)[:, None] * np.arange(n1)[None, :]) % n1) / n1
    c1, s1 = np.cos(a1), np.sin(a1)
    f1m = jnp.asarray(np.block([[c1, s1], [-s1, c1]]) / math.sqrt(n1), BF16)
    f1i, f2i, t2i = np.arange(n1)[:, None, None], np.arange(n2)[None, :, None], np.arange(n2)[None, None, :]
    theta = 2.0 * np.pi * (((t2i * f1i) % t_len) / t_len + ((t2i * f2i) % n2) / n2)
    g2m = jnp.asarray(np.concatenate([np.cos(theta), np.sin(theta)], axis=2) / math.sqrt(n2), BF16)

    tc = _tile(width, 256, LANES)
    nj = width // tc
    assert row0 % t_len == 0
    b0 = row0 // t_len
    return pl.pallas_call(
        functools.partial(_dft_kernel, n1=n1, n2=n2),
        grid=(batch, nj),
        in_specs=[
            pl.BlockSpec((t_len, tc), lambda b, j: (b0 + b, j)),
            pl.BlockSpec((t_len, tc), lambda b, j: (b0 + b, nj + j)),
            pl.BlockSpec((2 * n1, 2 * n1), lambda b, j: (0, 0)),
            pl.BlockSpec((n1, n2, 2 * n2), lambda b, j: (0, 0, 0)),
        ],
        out_specs=pl.BlockSpec((t_len, tc), lambda b, j: (b, j)),
        out_shape=jax.ShapeDtypeStruct((batch * t_len, width), BF16),
        scratch_shapes=[pltpu.VMEM((2, tc // LANES, t_len, LANES), F32),
                        pltpu.VMEM((tc // LANES, n2 * 2 * n1, LANES), F32),
                        pltpu.VMEM((tc // LANES, t_len, LANES), F32)],
        compiler_params=_cparams("parallel", "parallel"),
        name="position_dft",
    )(u, u, f1m, g2m)


def _ffn_prep_kernel(x_ref, nw_ref, sh_ref, sc_ref, rw_ref, rb_ref, h_ref, comb_ref, *, n_exp):
    h = _modulated(x_ref[...], nw_ref[...], sh_ref[0], sc_ref[0])
    h_ref[...] = h.astype(h_ref.dtype)
    h_hi, h_lo = _split2(h)
    r_hi, r_lo = _split2(rw_ref[...])
    logits = _dot_nt(r_hi, h_hi) + _dot_nt(r_lo, h_hi) + _dot_nt(r_hi, h_lo)
    s = _sigmoid(logits)
    sel = s + rb_ref[...]
    srow = [s[e:e + 1, :] for e in range(n_exp)]
    row = [sel[e:e + 1, :] for e in range(n_exp)]
    epg = n_exp // N_GROUPS

    def beats(a, ia, b, ib):
        return (a >= b) if ia < ib else (a > b)

    rank, gscore = {}, []
    for g in range(N_GROUPS):
        members = range(g * epg, (g + 1) * epg)
        for i in members:
            rank[i] = sum(beats(row[j], j, row[i], i).astype(F32) for j in members if j != i)
        gscore.append(sum(jnp.where(rank[i] < GROUP_SCORE_K, row[i], 0.0) for i in members))
    picked = []
    for g in range(N_GROUPS):
        grank = sum(beats(gscore[j], j, gscore[g], g).astype(F32) for j in range(N_GROUPS) if j != g)
        for i in range(g * epg, (g + 1) * epg):
            picked.append(jnp.where((grank < 1.0) & (rank[i] < TOP_K), srow[i], 0.0))
    denom = sum(picked)
    comb_ref[...] = jnp.concatenate([p / denom for p in picked], axis=0)


def _ffn_prep(x, m, nw, mod, layer, router_wt, router_b):
    d = x.shape[1]
    n_exp = router_wt.shape[0]
    tm = _tile(math.gcd(m, mod.rows_per_mod), 512, LANES)
    return pl.pallas_call(
        functools.partial(_ffn_prep_kernel, n_exp=n_exp),
        grid=(m // tm,),
        in_specs=[
            pl.BlockSpec((tm, d), lambda i: (i, 0)),
            pl.BlockSpec((1, d), lambda i: (0, 0)),
            mod.spec(layer, 3, tm),
            mod.spec(layer, 4, tm),
            pl.BlockSpec((n_exp, d), lambda i: (0, 0)),
            pl.BlockSpec((n_exp, 1), lambda i: (0, 0)),
        ],
        out_specs=[pl.BlockSpec((tm, d), lambda i: (i, 0)), pl.BlockSpec((n_exp, tm), lambda i: (0, i))],
        out_shape=[jax.ShapeDtypeStruct((m, d), F32), jax.ShapeDtypeStruct((n_exp, m), F32)],
        compiler_params=_cparams("parallel"),
        name="ffn_prep",
    )(x, nw.reshape(1, d), mod.table, mod.table, router_wt, router_b.reshape(n_exp, 1))


def _route(comb_t, tg):
    assert TOP_K == 2
    n_exp, n = comb_t.shape
    mask = comb_t > 0.0
    cnt = jnp.sum(mask.astype(I32), axis=1)
    padded = ((cnt + tg - 1) // tg) * tg
    ends = jnp.cumsum(padded)
    dest = (ends - padded)[:, None] + jnp.cumsum(mask.astype(I32), axis=1) - 1
    eidx = jnp.arange(n_exp, dtype=I32)[:, None]
    first = jnp.min(jnp.where(mask, eidx, n_exp), axis=0)
    final = jnp.max(jnp.where(mask, eidx, -1), axis=0)
    is_a = mask & (eidx == first)
    is_b = mask & (eidx == final) & (final != first)
    pick = lambda sel, val: jnp.sum(jnp.where(sel, val, 0), axis=0)
    w_ab = jnp.stack([pick(is_a, comb_t), pick(is_b, comb_t)], axis=1)
    p_rows = TOP_K * n + n_exp * tg
    d_a = jnp.where(jnp.any(is_a, axis=0), pick(is_a, dest), p_rows).astype(I32)
    d_b = jnp.where(jnp.any(is_b, axis=0), pick(is_b, dest), p_rows).astype(I32)
    tile_start = jnp.arange(p_rows // tg, dtype=I32) * tg
    tile_exp = jnp.minimum(jnp.sum((tile_start[:, None] >= ends[None, :]).astype(I32), axis=1), n_exp - 1)
    tile_ok = (tile_start < ends[-1]).astype(I32)
    tile_new = jnp.concatenate([jnp.ones((1,), I32), (tile_exp[1:] != tile_exp[:-1]).astype(I32)])
    return tile_exp, tile_ok, tile_new, d_a, d_b, w_ab, p_rows


def _src_kernel(da_ref, db_ref, src_ref, *, n_tok):
    def zero(r, carry):
        src_ref[r] = 0
        return carry
    lax.fori_loop(0, src_ref.shape[0], zero, 0, unroll=8)

    def put(t, carry):
        src_ref[da_ref[t]] = t
        src_ref[db_ref[t]] = t
        return carry
    lax.fori_loop(0, n_tok, put, 0, unroll=8)


def _sorted_row_tokens(d_a, d_b, p_rows, spare):
    n_tok = d_a.shape[0]
    smem = pl.BlockSpec(memory_space=pltpu.SMEM)
    return pl.pallas_call(
        functools.partial(_src_kernel, n_tok=n_tok),
        in_specs=[smem, smem],
        out_specs=smem,
        out_shape=jax.ShapeDtypeStruct((p_rows + spare,), I32),
        name="sorted_row_tokens",
    )(d_a, d_b)


def _row_copies(idx_ref, base, n_rows, src_hbm, dst_buf, sem, wait):
    def body(r, carry):
        cp = pltpu.make_async_copy(src_hbm.at[pl.ds(idx_ref[base + r], 1)], dst_buf.at[pl.ds(r, 1)], sem)
        if wait:
            cp.wait()
        else:
            cp.start()
        return carry
    lax.fori_loop(0, n_rows, body, 0, unroll=8)


def _moe_expert_kernel(texp_ref, tval_ref, tnew_ref, src_ref, h_hbm, wg_ref, wu_ref, wd_ref, y_ref,
                       xbuf, sem, wgb, wub, wdb, *, tg):
    i = pl.program_id(0)
    n_tiles = pl.num_programs(0)
    bufs = (xbuf.at[0], xbuf.at[1])
    prev_valid = tval_ref[jnp.maximum(i - 1, 0)] != 0
    valid = tval_ref[i] != 0

    @pl.when(i == 0)
    def _():
        _row_copies(src_ref, 0, tg, h_hbm, bufs[0], sem.at[0], wait=False)

    @pl.when(valid & (tnew_ref[i] != 0))
    def _():
        wgb[...] = wg_ref[0, 0].astype(BF16)
        wub[...] = wu_ref[0, 0].astype(BF16)
        wdb[...] = wd_ref[0, 0].astype(BF16)

    for slot in (0, 1):
        @pl.when(((i == 0) | prev_valid) & (i % 2 == slot))
        def _():
            _row_copies(src_ref, i * tg, tg, h_hbm, bufs[slot], sem.at[slot], wait=True)

        @pl.when(valid & (i % 2 == slot))
        def _():
            for r in range(tg):
                pltpu.make_async_copy(h_hbm.at[pl.ds(src_ref[(i + 1) * tg + r], 1)],
                                      bufs[1 - slot].at[pl.ds(r, 1)], sem.at[1 - slot]).start()
            x = bufs[slot][...].astype(BF16)
            act = (_silu(_dot(x, wgb[...])) * _dot(x, wub[...])).astype(BF16)
            y_ref[...] = _dot(act, wdb[...])

        @pl.when(valid & (i == n_tiles - 1) & (i % 2 == slot))
        def _():
            _row_copies(src_ref, (i + 1) * tg, tg, h_hbm, bufs[1 - slot], sem.at[1 - slot], wait=True)

    @pl.when(jnp.logical_not(valid))
    def _():
        y_ref[...] = jnp.zeros_like(y_ref)


def _moe_experts(h, src, tile_exp, tile_ok, tile_new, wg, wu, wd, layer, tg, p_rows):
    n, d = h.shape
    f = wg.shape[3]
    wmap = lambda i, te, ok, new, s: (layer, te[i], 0, 0)
    return pl.pallas_call(
        functools.partial(_moe_expert_kernel, tg=tg),
        grid_spec=pltpu.PrefetchScalarGridSpec(
            num_scalar_prefetch=4,
            grid=(p_rows // tg,),
            in_specs=[
                pl.BlockSpec(memory_space=pl.ANY),
                pl.BlockSpec((1, 1, d, f), wmap),
                pl.BlockSpec((1, 1, d, f), wmap),
                pl.BlockSpec((1, 1, f, d), wmap),
            ],
            out_specs=pl.BlockSpec((tg, d), lambda i, te, ok, new, s: (i, 0)),
            scratch_shapes=[pltpu.VMEM((2, tg, d), F32), pltpu.SemaphoreType.DMA((2,)),
                            pltpu.VMEM((d, f), BF16), pltpu.VMEM((d, f), BF16), pltpu.VMEM((f, d), BF16)],
        ),
        out_shape=jax.ShapeDtypeStruct((p_rows, d), F32),
        compiler_params=_cparams("arbitrary"),
        name="moe_experts",
    )(tile_exp, tile_ok, tile_new, src, h, wg, wu, wd)


def _moe_combine_kernel(da_ref, db_ref, y_hbm, x_ref, w_ref, g_ref, o_ref, ya, yb, sem, *, tm):
    i = pl.program_id(0)

    def fetch(tile, slot, wait):
        _row_copies(da_ref, tile * tm, tm, y_hbm, ya.at[slot], sem.at[0, slot], wait)
        _row_copies(db_ref, tile * tm, tm, y_hbm, yb.at[slot], sem.at[1, slot], wait)

    @pl.when(i == 0)
    def _():
        fetch(0, 0, False)

    @pl.when(i + 1 < pl.num_programs(0))
    def _():
        fetch(i + 1, (i + 1) % 2, False)

    fetch(i, i % 2, True)
    w = w_ref[...]
    mix = w[:, 0:1] * ya[i % 2] + w[:, 1:2] * yb[i % 2]
    o_ref[...] = x_ref[...] + g_ref[0] * mix


def _moe_combine(y, row_a, row_b, w_ab, x, m, mod, layer):
    d = x.shape[1]
    tm = _tile(math.gcd(m, mod.rows_per_mod), 256)
    return pl.pallas_call(
        functools.partial(_moe_combine_kernel, tm=tm),
        grid_spec=pltpu.PrefetchScalarGridSpec(
            num_scalar_prefetch=2,
            grid=(m // tm,),
            in_specs=[
                pl.BlockSpec(memory_space=pl.ANY),
                pl.BlockSpec((tm, d), lambda i, a, b: (i, 0)),
                pl.BlockSpec((tm, 2), lambda i, a, b: (i, 0)),
                mod.spec(layer, 5, tm),
            ],
            out_specs=pl.BlockSpec((tm, d), lambda i, a, b: (i, 0)),
            scratch_shapes=[pltpu.VMEM((2, tm, d), F32), pltpu.VMEM((2, tm, d), F32),
                            pltpu.SemaphoreType.DMA((2, 2))],
        ),
        out_shape=jax.ShapeDtypeStruct((m, d), F32),
        compiler_params=_cparams("arbitrary"),
        name="moe_combine",
    )(row_a, row_b, y, x, w_ab, mod.table)


def _moe(h, comb_t, wg, wu, wd, layer_w, x, m, mod, layer):
    tg = 256 if m * TOP_K >= 256 * wg.shape[1] else 64
    tile_exp, tile_ok, tile_new, d_a, d_b, w_ab, p_rows = _route(comb_t, tg)
    src = _sorted_row_tokens(d_a, d_b, p_rows, tg)
    y = _moe_experts(h, src, tile_exp, tile_ok, tile_new, wg, wu, wd, layer_w, tg, p_rows)
    none = lambda rows: jnp.where(rows == p_rows, 0, rows)
    return _moe_combine(y, none(d_a), none(d_b), w_ab, x, m, mod, layer)


def kernel(x, c, ctx, c_ctx, ada_w, ada_b, norm_mix_w, norm_ffn_w, dn_w_in, dn_conv_w, dn_A_log, dn_dt_bias, dn_norm_w, dn_w_out, fn_w_in, fn_w_out, router_w, router_bias, moe_w_gate, moe_w_up, moe_w_down, final_norm_w):
    batch, seq, d = x.shape
    ctx_len = ctx.shape[1]
    depth = ada_w.shape[0]
    nh = DN_HEADS
    dn_width = dn_w_out.shape[1]
    hd = dn_width // nh
    fn_width = fn_w_in.shape[2]
    m_lat, m_all = batch * seq, batch * (seq + ctx_len)
    assert batch * ctx_len <= seq and seq % ctx_len == 0

    mod_rows = -(-(batch + 1) // 8) * 8
    cond = jnp.zeros((mod_rows, d), F32).at[:batch].set(c).at[batch].set(c_ctx)
    table = _ada_table(cond, ada_w, ada_b)
    mod = _Mod(table.reshape(depth * mod_rows * N_MOD, 1, d), mod_rows, d, seq)

    router_wt = router_w.T
    xs = jnp.concatenate([x.reshape(m_lat, d), ctx.reshape(batch * ctx_len, d)], axis=0)

    for i in range(depth):
        use_dn = i % N_MIXERS == 0
        j = i // N_MIXERS
        ctx_out = i < depth - 1
        ctx_live = use_dn or ctx_out
        m = m_all if ctx_live else m_lat

        h = _modulate(xs, m, norm_mix_w[i], mod, i, 0, 1)
        if use_dn:
            p = _matmul(h, dn_w_in, j, BF16, 1024, n=4 * dn_width)
            ab = _matmul(h, dn_w_in, j, F32, LANES, col0=4 * dn_width, n=LANES)
            q = _dn_conv(p, dn_conv_w, j, 0, "q", m_lat, ctx_len, dn_width, hd)
            k = _dn_conv(p, dn_conv_w, j, 1, "k", m_lat, ctx_len, dn_width, hd)
            v = _dn_conv(p, dn_conv_w, j, 2, "v", m_lat, ctx_len, dn_width, hd)
            g, beta = _dn_gates(ab, dn_A_log[j].reshape(-1), dn_dt_bias[j].reshape(-1), nh)
            g_rows = g.reshape(m // CHUNK, CHUNK, 2 * nh).transpose(0, 2, 1)
            o_f = _dn_scan(q, k, v, g, g_rows, beta, False, batch, m_lat, nh, hd)
            o_b = _dn_scan(q, k, v, g, g_rows, beta, True, batch, m_lat, nh, hd)
            y = _dn_out(o_f, o_b, p, dn_norm_w[j], dn_width, hd)
            xs = _matmul_gated_residual(y, dn_w_out, j, xs, mod, i, 2)
        else:
            w_fold = _fold_channel_dft(fn_w_in, j, FN_GROUPS)
            u = _matmul(h, w_fold, 0, BF16, 1024)
            mixed = _position_dft_real(u, 0, batch, seq, fn_width)
            if ctx_out:
                mixed = jnp.concatenate([mixed, _position_dft_real(u, m_lat, batch, ctx_len, fn_width)], axis=0)
            xs = _matmul_gated_residual(mixed, fn_w_out, j, xs, mod, i, 2)

        hf, comb_t = _ffn_prep(xs, m, norm_ffn_w[i], mod, i, router_wt, router_bias)
        xs = _moe(hf, comb_t, moe_w_gate, moe_w_up, moe_w_down, i, xs, m, mod, i)

    return _final_norm(xs, m_lat, final_norm_w).reshape(batch, seq, d)
```

```python
import functools
import math

import numpy as np
import jax
import jax.numpy as jnp
from jax import lax
from jax.experimental import pallas as pl
from jax.experimental.pallas import tpu as pltpu

F32 = jnp.float32
BF16 = jnp.bfloat16
I32 = jnp.int32

GRID_W = 64
CHUNK = 64
CONV_W = 5
DN_HEADS = 16
FN_GROUPS = 4
N_GROUPS = 4
GROUP_SCORE_K = 2
TOP_K = 2
N_MOD = 6
N_MIXERS = 2
EPS = 1e-6

VMEM_LIMIT = 56 * 1024 * 1024
LANES = 128


def _cparams(*sem):
    return pltpu.CompilerParams(dimension_semantics=sem, vmem_limit_bytes=VMEM_LIMIT)


def _dot(a, b):
    return jnp.dot(a, b, preferred_element_type=F32)


def _dot_nt(a, b):
    return lax.dot_general(a, b, (((1,), (1,)), ((), ())), preferred_element_type=F32)


def _dot_tn(a, b):
    return lax.dot_general(a, b, (((0,), (0,)), ((), ())), preferred_element_type=F32)


def _split2(x):
    hi = x.astype(BF16)
    lo = (x - hi.astype(F32)).astype(BF16)
    return hi, lo


def _split3(x):
    hi = x.astype(BF16)
    r = x - hi.astype(F32)
    mid = r.astype(BF16)
    lo = (r - mid.astype(F32)).astype(BF16)
    return hi, mid, lo


def _silu(x):
    return x / (1.0 + jnp.exp(-x))


def _sigmoid(x):
    return 1.0 / (1.0 + jnp.exp(-x))


def _tile(n, pref, mult=8):
    if n <= pref:
        return n
    t = (pref // mult) * mult
    while t >= mult:
        if n % t == 0:
            return t
        t -= mult
    return n


def _log2(n):
    assert n & (n - 1) == 0, n
    return n.bit_length() - 1


def _ada_kernel(c_ref, w_ref, b_ref, o_ref):
    s_hi, s_lo = _split2(_silu(c_ref[...]))
    w_hi, w_lo = _split2(w_ref[0])
    acc = _dot(s_hi, w_hi) + _dot(s_lo, w_hi) + _dot(s_hi, w_lo)
    o_ref[0] = acc + b_ref[0]


def _ada_table(cond, ada_w, ada_b):
    depth, d, n = ada_w.shape
    r = cond.shape[0]
    tn = _tile(n, 512, 128)
    return pl.pallas_call(
        _ada_kernel,
        grid=(depth, n // tn),
        in_specs=[
            pl.BlockSpec((r, d), lambda l, j: (0, 0)),
            pl.BlockSpec((1, d, tn), lambda l, j: (l, 0, j)),
            pl.BlockSpec((1, 1, tn), lambda l, j: (l, 0, j)),
        ],
        out_specs=pl.BlockSpec((1, r, tn), lambda l, j: (l, 0, j)),
        out_shape=jax.ShapeDtypeStruct((depth, r, n), F32),
        compiler_params=_cparams("parallel", "parallel"),
        name="ada_table",
    )(cond, ada_w, ada_b.reshape(depth, 1, n))


class _Mod:
    def __init__(self, table3, mod_rows, d, rows_per_mod):
        self.table = table3
        self.mod_rows = mod_rows
        self.d = d
        self.rows_per_mod = rows_per_mod

    def spec(self, layer, k, tm, row_axis=0, width=None, col=None):
        assert self.rows_per_mod % tm == 0
        width = self.d if width is None else width
        rows, mr = self.rows_per_mod, self.mod_rows

        def index_map(*g):
            row = (g[row_axis] * tm) // rows
            return ((layer * mr + row) * N_MOD + k, 0, 0 if col is None else col(*g))
        return pl.BlockSpec((1, 1, width), index_map)


def _modulated(x, nw, shift, scale):
    ms = jnp.mean(x * x, axis=-1, keepdims=True)
    return (x * lax.rsqrt(ms + EPS) * nw) * (1.0 + scale) + shift


def _modulate_kernel(x_ref, nw_ref, sh_ref, sc_ref, o_ref):
    o_ref[...] = _modulated(x_ref[...], nw_ref[...], sh_ref[0], sc_ref[0]).astype(o_ref.dtype)


def _modulate(x, m, nw, mod, layer, k_shift, k_scale):
    d = x.shape[1]
    tm = _tile(math.gcd(m, mod.rows_per_mod), 512)
    return pl.pallas_call(
        _modulate_kernel,
        grid=(m // tm,),
        in_specs=[
            pl.BlockSpec((tm, d), lambda i: (i, 0)),
            pl.BlockSpec((1, d), lambda i: (0, 0)),
            mod.spec(layer, k_shift, tm),
            mod.spec(layer, k_scale, tm),
        ],
        out_specs=pl.BlockSpec((tm, d), lambda i: (i, 0)),
        out_shape=jax.ShapeDtypeStruct((m, d), BF16),
        compiler_params=_cparams("parallel"),
        name="modulate",
    )(x, nw.reshape(1, d), mod.table, mod.table)


def _final_norm_kernel(x_ref, nw_ref, o_ref):
    x = x_ref[...]
    ms = jnp.mean(x * x, axis=-1, keepdims=True)
    o_ref[...] = x * lax.rsqrt(ms + EPS) * nw_ref[...]


def _final_norm(x, m, nw):
    d = x.shape[1]
    tm = _tile(m, 512)
    return pl.pallas_call(
        _final_norm_kernel,
        grid=(m // tm,),
        in_specs=[pl.BlockSpec((tm, d), lambda i: (i, 0)), pl.BlockSpec((1, d), lambda i: (0, 0))],
        out_specs=pl.BlockSpec((tm, d), lambda i: (i, 0)),
        out_shape=jax.ShapeDtypeStruct((m, d), F32),
        compiler_params=_cparams("parallel"),
        name="final_norm",
    )(x, nw.reshape(1, d))


def _matmul_kernel(a_ref, w_ref, o_ref, wb_ref, *, valid_cols):
    @pl.when(pl.program_id(1) == 0)
    def _():
        w = w_ref[0]
        if valid_cols < w.shape[1]:
            w = jnp.where(lax.broadcasted_iota(I32, w.shape, 1) < valid_cols, w, 0.0)
        wb_ref[...] = w.astype(BF16)

    o_ref[...] = _dot(a_ref[...], wb_ref[...]).astype(o_ref.dtype)


def _matmul(a, w, layer, out_dtype, tn_pref, col0=0, n=None):
    m, k = a.shape
    n = w.shape[2] if n is None else n
    tm, tn = _tile(m, 1024), _tile(n, tn_pref, LANES)
    assert col0 % tn == 0
    cb = col0 // tn
    valid_cols = tn if col0 + n <= w.shape[2] else w.shape[2] - col0
    assert valid_cols == tn or n == tn
    return pl.pallas_call(
        functools.partial(_matmul_kernel, valid_cols=valid_cols),
        grid=(n // tn, m // tm),
        in_specs=[pl.BlockSpec((tm, k), lambda j, i: (i, 0)),
                  pl.BlockSpec((1, k, tn), lambda j, i: (layer, 0, cb + j))],
        out_specs=pl.BlockSpec((tm, tn), lambda j, i: (i, j)),
        out_shape=jax.ShapeDtypeStruct((m, n), out_dtype),
        scratch_shapes=[pltpu.VMEM((k, tn), BF16)],
        compiler_params=_cparams("parallel", "arbitrary"),
        name="matmul",
    )(a, w)


def _matmul_res_kernel(a_ref, w_ref, x_ref, g_ref, o_ref, wb_ref):
    @pl.when(pl.program_id(1) == 0)
    def _():
        wb_ref[...] = w_ref[0].astype(BF16)

    o_ref[...] = x_ref[...] + g_ref[0] * _dot(a_ref[...], wb_ref[...])


def _matmul_gated_residual(a, w, layer_w, x, mod, layer, k_gate, tn_pref=512):
    m, k = a.shape
    n = w.shape[2]
    tm = _tile(math.gcd(m, mod.rows_per_mod), 1024)
    tn = _tile(n, tn_pref, LANES)
    return pl.pallas_call(
        _matmul_res_kernel,
        grid=(n // tn, m // tm),
        in_specs=[
            pl.BlockSpec((tm, k), lambda j, i: (i, 0)),
            pl.BlockSpec((1, k, tn), lambda j, i: (layer_w, 0, j)),
            pl.BlockSpec((tm, tn), lambda j, i: (i, j)),
            mod.spec(layer, k_gate, tm, row_axis=1, width=tn, col=lambda j, i: j),
        ],
        out_specs=pl.BlockSpec((tm, tn), lambda j, i: (i, j)),
        out_shape=jax.ShapeDtypeStruct((m, n), F32),
        scratch_shapes=[pltpu.VMEM((k, tn), BF16)],
        compiler_params=_cparams("parallel", "arbitrary"),
        name="matmul_gated_residual",
    )(a, w, x, mod.table)


def _head_sumsq(y, ones_bd):
    sq_hi, sq_lo = _split2(y * y)
    return _dot(sq_hi, ones_bd) + _dot(sq_lo, ones_bd)


def _dn_conv_kernel(u_ref, cw_ref, e_ref, o_ref, coef_ref, *, lat_tiles, lat_len, ctx_len, mode, head_dim):
    i = pl.program_id(1)
    tm = u_ref.shape[0]
    pad = CONV_W // 2
    cw = cw_ref[0]

    @pl.when((i == 0) | (i == lat_tiles))
    def _():
        conv_len = jnp.where(i < lat_tiles, lat_len, ctx_len)
        pos = lax.broadcasted_iota(I32, (tm, 1), 0) & (conv_len - 1)
        for j in range(CONV_W):
            d = j - pad
            inside = ((pos + d >= 0) & (pos + d < conv_len)).astype(F32)
            coef_ref[j] = inside * cw[j:j + 1, :]

    u = u_ref[...].astype(F32)
    acc = u * cw[pad:pad + 1, :]
    for j in range(CONV_W):
        d = j - pad
        if d != 0:
            acc = acc + pltpu.roll(u, (-d) % tm, 0) * coef_ref[j]
    y = _silu(acc)
    if mode != "v":
        y = y * lax.rsqrt(_head_sumsq(y, e_ref[...]) + EPS)
        if mode == "q":
            y = y * (head_dim ** -0.5)
    o_ref[...] = y.astype(o_ref.dtype)


def _dn_conv(p, conv_w, layer, section, mode, m_lat, ctx_len, width, head_dim):
    m = p.shape[0]
    tc = _tile(width, 512, LANES)
    tm = max(GRID_W, ctx_len, 256)
    assert m_lat % tm == 0 and m % tm == 0 and tm % GRID_W == 0 and tm % ctx_len == 0
    ones_bd = jnp.asarray(np.kron(np.eye(tc // head_dim), np.ones((head_dim, head_dim))), BF16)
    nsec = width // tc
    return pl.pallas_call(
        functools.partial(_dn_conv_kernel, lat_tiles=m_lat // tm, lat_len=GRID_W, ctx_len=ctx_len, mode=mode,
                          head_dim=head_dim),
        grid=(nsec, m // tm),
        in_specs=[
            pl.BlockSpec((tm, tc), lambda j, i: (i, section * nsec + j)),
            pl.BlockSpec((1, CONV_W, tc), lambda j, i: (layer, 0, section * nsec + j)),
            pl.BlockSpec((tc, tc), lambda j, i: (0, 0)),
        ],
        out_specs=pl.BlockSpec((tm, tc), lambda j, i: (i, j)),
        out_shape=jax.ShapeDtypeStruct((m, width), BF16),
        scratch_shapes=[pltpu.VMEM((CONV_W, tm, tc), F32)],
        compiler_params=_cparams("parallel", "arbitrary"),
        name="dn_conv_" + mode,
    )(p, conv_w, ones_bd)


def _dn_gates_kernel(ab_ref, al_ref, dt_ref, tl_ref, tu_ref, g_ref, beta_ref, *, nh):
    ab = ab_ref[:, :4 * nh]
    a = ab[:, :2 * nh] + dt_ref[...]
    softplus = jnp.maximum(a, 0.0) + jnp.log1p(jnp.exp(-jnp.abs(a)))
    la = -jnp.exp(al_ref[...]) * softplus
    beta_ref[...] = _sigmoid(ab[:, 2 * nh:])
    parts = _split3(la)
    g_pre = sum(_dot(tl_ref[...], p) for p in parts)
    g_suf = sum(_dot(tu_ref[...], p) for p in parts)
    col = lax.broadcasted_iota(I32, la.shape, 1)
    g_ref[...] = jnp.where(col < nh, g_pre, g_suf)


def _dn_gates(ab, a_log, dt_bias, nh):
    m = ab.shape[0]
    tm = _tile(m, 256, CHUNK)
    r = np.arange(tm)
    same = (r[:, None] // CHUNK) == (r[None, :] // CHUNK)
    tri_l = jnp.asarray(same & (r[None, :] <= r[:, None]), BF16)
    tri_u = jnp.asarray(same & (r[None, :] >= r[:, None]), BF16)
    return pl.pallas_call(
        functools.partial(_dn_gates_kernel, nh=nh),
        grid=(m // tm,),
        in_specs=[
            pl.BlockSpec((tm, ab.shape[1]), lambda i: (i, 0)),
            pl.BlockSpec((1, 2 * nh), lambda i: (0, 0)),
            pl.BlockSpec((1, 2 * nh), lambda i: (0, 0)),
            pl.BlockSpec((tm, tm), lambda i: (0, 0)),
            pl.BlockSpec((tm, tm), lambda i: (0, 0)),
        ],
        out_specs=[pl.BlockSpec((tm, 2 * nh), lambda i: (i, 0))] * 2,
        out_shape=[jax.ShapeDtypeStruct((m, 2 * nh), F32)] * 2,
        compiler_params=_cparams("parallel"),
        name="dn_gates",
    )(ab, a_log.reshape(1, 2 * nh), dt_bias.reshape(1, 2 * nh), tri_l, tri_u)


def _unit_tri_inverses(lms):
    shape = lms[0].shape
    ri = lax.broadcasted_iota(I32, shape, 0)
    ci = lax.broadcasted_iota(I32, shape, 1)
    eye = (ri == ci).astype(F32)
    pair = (ri >> 1) == (ci >> 1)
    ts = [eye - jnp.where(pair, lm, 0.0) for lm in lms]
    for lvl in range(1, _log2(shape[0])):
        cross = ((ri >> (lvl + 1)) == (ci >> (lvl + 1))) & ((ri >> lvl) != (ci >> lvl))
        offs = [jnp.where(cross, lm, 0.0).astype(BF16) for lm in lms]
        tbs = [t.astype(BF16) for t in ts]
        ps = [_dot(tb, off).astype(BF16) for tb, off in zip(tbs, offs)]
        ts = [t - _dot(p, tb) for t, p, tb in zip(ts, ps, tbs)]
    return ts


def _dn_scan_kernel(q_ref, k_ref, v_ref, gc_ref, gr_ref, b_ref, o_ref, s_ref, *, backward, nh, hd):
    @pl.when(pl.program_id(1) == 0)
    def _():
        s_ref[...] = jnp.zeros_like(s_ref)

    n = CHUNK
    heads = range(nh)
    ri = lax.broadcasted_iota(I32, (n, n), 0)
    ci = lax.broadcasted_iota(I32, (n, n), 1)
    causal = (ci >= ri) if backward else (ci <= ri)
    strict = (ci > ri) if backward else (ci < ri)
    last = 0 if backward else n - 1
    dcol = nh if backward else 0
    gc_all = gc_ref[...]
    gr_all = gr_ref[0]
    b_all = b_ref[...]
    gcol = [gc_all[:, dcol + h:dcol + h + 1] for h in heads]
    grow = [gr_all[dcol + h:dcol + h + 1, :] for h in heads]
    bcol = [b_all[:, dcol + h:dcol + h + 1] for h in heads]
    glast = [g[last:last + 1, :] for g in gcol]
    q = [q_ref[:, h * hd:(h + 1) * hd] for h in heads]
    k = [k_ref[:, h * hd:(h + 1) * hd] for h in heads]
    v = [v_ref[:, h * hd:(h + 1) * hd] for h in heads]
    kq_k = [_dot_nt(jnp.concatenate([k[h], q[h]], axis=0), k[h]) for h in heads]
    decay = [jnp.exp(jnp.where(causal, gcol[h] - grow[h], -jnp.inf)) for h in heads]
    lm = [jnp.where(strict, bcol[h] * kq_k[h][:n] * decay[h], 0.0) for h in heads]
    intra = [(kq_k[h][n:] * decay[h]).astype(BF16) for h in heads]
    eg = [jnp.exp(g) for g in gcol]
    rhs = [jnp.concatenate([v[h].astype(F32) * bcol[h], k[h].astype(F32) * (bcol[h] * eg[h])],
                           axis=1).astype(BF16) for h in heads]
    t = _unit_tri_inverses(lm)
    uw = [_dot(t[h].astype(BF16), rhs[h]) for h in heads]
    s = [s_ref[h] for h in heads]
    wq = [jnp.concatenate([uw[h][:, hd:], q[h].astype(F32) * eg[h]], axis=0).astype(BF16) for h in heads]
    ws = [_dot(wq[h], s[h].astype(BF16)) for h in heads]
    vb = [(uw[h][:, :hd] - ws[h][:n]).astype(BF16) for h in heads]
    kg = [(k[h].astype(F32) * jnp.exp(glast[h] - gcol[h])).astype(BF16) for h in heads]
    for h in heads:
        o_ref[:, h * hd:(h + 1) * hd] = ws[h][n:] + _dot(intra[h], vb[h])
    for h in heads:
        s_ref[h] = s[h] * jnp.exp(glast[h]) + _dot_tn(kg[h], vb[h])


def _dn_scan(q, k, v, g, g_rows, beta, backward, batch, m_lat, nh, hd):
    m = q.shape[0]
    nl = m_lat // batch // CHUNK
    nc = (m - m_lat) // batch // CHUNK
    ctx0 = m_lat // CHUNK

    def blk(b, c):
        ctx_blk = ctx0 + b * nc + (nc - 1 - c if backward else c)
        lat_blk = b * nl + (nl - 1 - (c - nc) if backward else c - nc)
        return jnp.where(c < nc, ctx_blk, lat_blk)

    tok = lambda b, c: (blk(b, c), 0)
    tok3 = lambda b, c: (blk(b, c), 0, 0)
    return pl.pallas_call(
        functools.partial(_dn_scan_kernel, backward=backward, nh=nh, hd=hd),
        grid=(batch, nc + nl),
        in_specs=[
            pl.BlockSpec((CHUNK, nh * hd), tok),
            pl.BlockSpec((CHUNK, nh * hd), tok),
            pl.BlockSpec((CHUNK, nh * hd), tok),
            pl.BlockSpec((CHUNK, 2 * nh), tok),
            pl.BlockSpec((1, 2 * nh, CHUNK), tok3),
            pl.BlockSpec((CHUNK, 2 * nh), tok),
        ],
        out_specs=pl.BlockSpec((CHUNK, nh * hd), tok),
        out_shape=jax.ShapeDtypeStruct((m, nh * hd), F32),
        scratch_shapes=[pltpu.VMEM((nh, hd, hd), F32)],
        compiler_params=_cparams("parallel", "arbitrary"),
        name="dn_scan_bwd" if backward else "dn_scan_fwd",
    )(q, k, v, g, g_rows, beta)


def _dn_out_kernel(of_ref, ob_ref, gate_ref, nw_ref, e_ref, y_ref, *, hd):
    o = of_ref[...] + ob_ref[...]
    ms = _head_sumsq(o, e_ref[...]) * (1.0 / hd)
    y = o * lax.rsqrt(ms + EPS) * nw_ref[...]
    y_ref[...] = (y * _silu(gate_ref[...].astype(F32))).astype(y_ref.dtype)


def _dn_out(o_f, o_b, p, norm_w, width, hd):
    m = o_f.shape[0]
    tc = _tile(width, 512, LANES)
    tm = _tile(m, 512)
    nsec = width // tc
    ones_bd = jnp.asarray(np.kron(np.eye(tc // hd), np.ones((hd, hd))), BF16)
    nw = jnp.tile(norm_w.astype(F32), tc // hd).reshape(1, tc)
    return pl.pallas_call(
        functools.partial(_dn_out_kernel, hd=hd),
        grid=(m // tm, nsec),
        in_specs=[
            pl.BlockSpec((tm, tc), lambda i, j: (i, j)),
            pl.BlockSpec((tm, tc), lambda i, j: (i, j)),
            pl.BlockSpec((tm, tc), lambda i, j: (i, 3 * nsec + j)),
            pl.BlockSpec((1, tc), lambda i, j: (0, 0)),
            pl.BlockSpec((tc, tc), lambda i, j: (0, 0)),
        ],
        out_specs=pl.BlockSpec((tm, tc), lambda i, j: (i, j)),
        out_shape=jax.ShapeDtypeStruct((m, width), BF16),
        compiler_params=_cparams("parallel", "parallel"),
        name="dn_out",
    )(o_f, o_b, p, nw, ones_bd)


def _fold_kernel(w_ref, t_ref, o_ref):
    w_hi, w_lo = _split2(w_ref[0])
    t_hi, t_lo = _split2(t_ref[0])
    o_ref[0] = (_dot(w_hi, t_hi) + _dot(w_lo, t_hi) + _dot(w_hi, t_lo)).astype(o_ref.dtype)


def _fold_channel_dft(w_in, layer, groups):
    _, d, width = w_in.shape
    c = width // groups
    idx = np.arange(c)
    ang = 2.0 * np.pi * ((idx[:, None] * idx[None, :]) % c) / c
    table = jnp.asarray(np.stack([np.cos(ang), -np.sin(ang)]) / math.sqrt(c), F32)
    return pl.pallas_call(
        _fold_kernel,
        grid=(2, groups),
        in_specs=[pl.BlockSpec((1, d, c), lambda s, g: (layer, 0, g)),
                  pl.BlockSpec((1, c, c), lambda s, g: (s, 0, 0))],
        out_specs=pl.BlockSpec((1, d, c), lambda s, g: (0, 0, s * groups + g)),
        out_shape=jax.ShapeDtypeStruct((1, d, 2 * width), BF16),
        compiler_params=_cparams("parallel", "parallel"),
        name="fold_channel_dft",
    )(w_in, table)


def _dft_kernel(zr_ref, zi_ref, f_ref, g_ref, o_ref, zf, yf, of, *, n1, n2):
    slabs = range(of.shape[0])
    lanes = lambda l: slice(l * LANES, (l + 1) * LANES)
    for l in slabs:
        zf[0, l] = zr_ref[:, lanes(l)].astype(F32)
        zf[1, l] = zi_ref[:, lanes(l)].astype(F32)

    def stage1(t2, carry):
        z = jnp.concatenate(
            [jnp.concatenate([zf[part, l, pl.ds(t2, n1, stride=n2), :] for l in slabs], axis=1) for part in (0, 1)],
            axis=0)
        y = _dot(f_ref[...], z.astype(BF16))
        for l in slabs:
            yf[l, pl.ds(pl.multiple_of(t2 * 2 * n1, 2 * n1), 2 * n1), :] = y[:, lanes(l)]
        return carry

    lax.fori_loop(0, n2, stage1, 0, unroll=8)

    def stage2(f1, carry):
        z = jnp.concatenate(
            [jnp.concatenate([yf[l, pl.ds(off + f1, n2, stride=2 * n1), :] for l in slabs], axis=1)
             for off in (0, n1)], axis=0)
        r = _dot(g_ref[f1], z.astype(BF16))
        for l in slabs:
            of[l, pl.ds(f1, n2, stride=n1), :] = r[:, lanes(l)]
        return carry

    lax.fori_loop(0, n1, stage2, 0, unroll=8)
    for l in slabs:
        o_ref[:, lanes(l)] = of[l].astype(o_ref.dtype)


def _position_dft_real(u, row0, batch, t_len, width):
    n1 = 1 << (_log2(t_len) // 2)
    n2 = t_len // n1
    a1 = 2.0 * np.pi * ((np.arange(n1)[:, None] * np.arange(n1)[None, :]) % n1) / n1
    c1, s1 = np.cos(a1), np.sin(a1)
    f1m = jnp.asarray(np.block([[c1, s1], [-s1, c1]]) / math.sqrt(n1), BF16)
    f1i, f2i, t2i = np.arange(n1)[:, None, None], np.arange(n2)[None, :, None], np.arange(n2)[None, None, :]
    theta = 2.0 * np.pi * (((t2i * f1i) % t_len) / t_len + ((t2i * f2i) % n2) / n2)
    g2m = jnp.asarray(np.concatenate([np.cos(theta), np.sin(theta)], axis=2) / math.sqrt(n2), BF16)

    tc = _tile(width, 256, LANES)
    nj = width // tc
    assert row0 % t_len == 0
    b0 = row0 // t_len
    return pl.pallas_call(
        functools.partial(_dft_kernel, n1=n1, n2=n2),
        grid=(batch, nj),
        in_specs=[
            pl.BlockSpec((t_len, tc), lambda b, j: (b0 + b, j)),
            pl.BlockSpec((t_len, tc), lambda b, j: (b0 + b, nj + j)),
            pl.BlockSpec((2 * n1, 2 * n1), lambda b, j: (0, 0)),
            pl.BlockSpec((n1, n2, 2 * n2), lambda b, j: (0, 0, 0)),
        ],
        out_specs=pl.BlockSpec((t_len, tc), lambda b, j: (b, j)),
        out_shape=jax.ShapeDtypeStruct((batch * t_len, width), BF16),
        scratch_shapes=[pltpu.VMEM((2, tc // LANES, t_len, LANES), F32),
                        pltpu.VMEM((tc // LANES, n2 * 2 * n1, LANES), F32),
                        pltpu.VMEM((tc // LANES, t_len, LANES), F32)],
        compiler_params=_cparams("parallel", "parallel"),
        name="position_dft",
    )(u, u, f1m, g2m)


def _ffn_prep_kernel(x_ref, nw_ref, sh_ref, sc_ref, rw_ref, rb_ref, h_ref, comb_ref, *, n_exp):
    h = _modulated(x_ref[...], nw_ref[...], sh_ref[0], sc_ref[0])
    h_ref[...] = h.astype(h_ref.dtype)
    h_hi, h_lo = _split2(h)
    r_hi, r_lo = _split2(rw_ref[...])
    logits = _dot_nt(r_hi, h_hi) + _dot_nt(r_lo, h_hi) + _dot_nt(r_hi, h_lo)
    s = _sigmoid(logits)
    sel = s + rb_ref[...]
    srow = [s[e:e + 1, :] for e in range(n_exp)]
    row = [sel[e:e + 1, :] for e in range(n_exp)]
    epg = n_exp // N_GROUPS

    def beats(a, ia, b, ib):
        return (a >= b) if ia < ib else (a > b)

    rank, gscore = {}, []
    for g in range(N_GROUPS):
        members = range(g * epg, (g + 1) * epg)
        for i in members:
            rank[i] = sum(beats(row[j], j, row[i], i).astype(F32) for j in members if j != i)
        gscore.append(sum(jnp.where(rank[i] < GROUP_SCORE_K, row[i], 0.0) for i in members))
    picked = []
    for g in range(N_GROUPS):
        grank = sum(beats(gscore[j], j, gscore[g], g).astype(F32) for j in range(N_GROUPS) if j != g)
        for i in range(g * epg, (g + 1) * epg):
            picked.append(jnp.where((grank < 1.0) & (rank[i] < TOP_K), srow[i], 0.0))
    denom = sum(picked)
    comb_ref[...] = jnp.concatenate([p / denom for p in picked], axis=0)


def _ffn_prep(x, m, nw, mod, layer, router_wt, router_b):
    d = x.shape[1]
    n_exp = router_wt.shape[0]
    tm = _tile(math.gcd(m, mod.rows_per_mod), 512, LANES)
    return pl.pallas_call(
        functools.partial(_ffn_prep_kernel, n_exp=n_exp),
        grid=(m // tm,),
        in_specs=[
            pl.BlockSpec((tm, d), lambda i: (i, 0)),
            pl.BlockSpec((1, d), lambda i: (0, 0)),
            mod.spec(layer, 3, tm),
            mod.spec(layer, 4, tm),
            pl.BlockSpec((n_exp, d), lambda i: (0, 0)),
            pl.BlockSpec((n_exp, 1), lambda i: (0, 0)),
        ],
        out_specs=[pl.BlockSpec((tm, d), lambda i: (i, 0)), pl.BlockSpec((n_exp, tm), lambda i: (0, i))],
        out_shape=[jax.ShapeDtypeStruct((m, d), F32), jax.ShapeDtypeStruct((n_exp, m), F32)],
        compiler_params=_cparams("parallel"),
        name="ffn_prep",
    )(x, nw.reshape(1, d), mod.table, mod.table, router_wt, router_b.reshape(n_exp, 1))


def _route(comb_t, tg):
    assert TOP_K == 2
    n_exp, n = comb_t.shape
    mask = comb_t > 0.0
    cnt = jnp.sum(mask.astype(I32), axis=1)
    padded = ((cnt + tg - 1) // tg) * tg
    ends = jnp.cumsum(padded)
    dest = (ends - padded)[:, None] + jnp.cumsum(mask.astype(I32), axis=1) - 1
    eidx = jnp.arange(n_exp, dtype=I32)[:, None]
    first = jnp.min(jnp.where(mask, eidx, n_exp), axis=0)
    final = jnp.max(jnp.where(mask, eidx, -1), axis=0)
    is_a = mask & (eidx == first)
    is_b = mask & (eidx == final) & (final != first)
    pick = lambda sel, val: jnp.sum(jnp.where(sel, val, 0), axis=0)
    w_ab = jnp.stack([pick(is_a, comb_t), pick(is_b, comb_t)], axis=1)
    p_rows = TOP_K * n + n_exp * tg
    d_a = jnp.where(jnp.any(is_a, axis=0), pick(is_a, dest), p_rows).astype(I32)
    d_b = jnp.where(jnp.any(is_b, axis=0), pick(is_b, dest), p_rows).astype(I32)
    tile_start = jnp.arange(p_rows // tg, dtype=I32) * tg
    tile_exp = jnp.minimum(jnp.sum((tile_start[:, None] >= ends[None, :]).astype(I32), axis=1), n_exp - 1)
    tile_ok = (tile_start < ends[-1]).astype(I32)
    tile_new = jnp.concatenate([jnp.ones((1,), I32), (tile_exp[1:] != tile_exp[:-1]).astype(I32)])
    return tile_exp, tile_ok, tile_new, d_a, d_b, w_ab, p_rows


def _src_kernel(da_ref, db_ref, src_ref, *, n_tok):
    def zero(r, carry):
        src_ref[r] = 0
        return carry
    lax.fori_loop(0, src_ref.shape[0], zero, 0, unroll=8)

    def put(t, carry):
        src_ref[da_ref[t]] = t
        src_ref[db_ref[t]] = t
        return carry
    lax.fori_loop(0, n_tok, put, 0, unroll=8)


def _sorted_row_tokens(d_a, d_b, p_rows, spare):
    n_tok = d_a.shape[0]
    smem = pl.BlockSpec(memory_space=pltpu.SMEM)
    return pl.pallas_call(
        functools.partial(_src_kernel, n_tok=n_tok),
        in_specs=[smem, smem],
        out_specs=smem,
        out_shape=jax.ShapeDtypeStruct((p_rows + spare,), I32),
        name="sorted_row_tokens",
    )(d_a, d_b)


def _row_copies(idx_ref, base, n_rows, src_hbm, dst_buf, sem, wait):
    def body(r, carry):
        cp = pltpu.make_async_copy(src_hbm.at[pl.ds(idx_ref[base + r], 1)], dst_buf.at[pl.ds(r, 1)], sem)
        if wait:
            cp.wait()
        else:
            cp.start()
        return carry
    lax.fori_loop(0, n_rows, body, 0, unroll=8)


def _moe_expert_kernel(texp_ref, tval_ref, tnew_ref, src_ref, h_hbm, wg_ref, wu_ref, wd_ref, y_ref,
                       xbuf, sem, wgb, wub, wdb, *, tg):
    i = pl.program_id(0)
    n_tiles = pl.num_programs(0)
    bufs = (xbuf.at[0], xbuf.at[1])
    prev_valid = tval_ref[jnp.maximum(i - 1, 0)] != 0
    valid = tval_ref[i] != 0

    @pl.when(i == 0)
    def _():
        _row_copies(src_ref, 0, tg, h_hbm, bufs[0], sem.at[0], wait=False)

    @pl.when(valid & (tnew_ref[i] != 0))
    def _():
        wgb[...] = wg_ref[0, 0].astype(BF16)
        wub[...] = wu_ref[0, 0].astype(BF16)
        wdb[...] = wd_ref[0, 0].astype(BF16)

    for slot in (0, 1):
        @pl.when(((i == 0) | prev_valid) & (i % 2 == slot))
        def _():
            _row_copies(src_ref, i * tg, tg, h_hbm, bufs[slot], sem.at[slot], wait=True)

        @pl.when(valid & (i % 2 == slot))
        def _():
            def request(rows):
                for r in rows:
                    pltpu.make_async_copy(h_hbm.at[pl.ds(src_ref[(i + 1) * tg + r], 1)],
                                          bufs[1 - slot].at[pl.ds(r, 1)], sem.at[1 - slot]).start()

            quarter = tg // 4
            x = bufs[slot][...].astype(BF16)
            request(range(0, quarter))
            a = _dot(x, wgb[...])
            request(range(quarter, 2 * quarter))
            b = _dot(x, wub[...])
            request(range(2 * quarter, 3 * quarter))
            act = (_silu(a) * b).astype(BF16)
            request(range(3 * quarter, tg))
            y_ref[...] = _dot(act, wdb[...])

        @pl.when(valid & (i == n_tiles - 1) & (i % 2 == slot))
        def _():
            _row_copies(src_ref, (i + 1) * tg, tg, h_hbm, bufs[1 - slot], sem.at[1 - slot], wait=True)

    @pl.when(jnp.logical_not(valid))
    def _():
        y_ref[...] = jnp.zeros_like(y_ref)


def _moe_experts(h, src, tile_exp, tile_ok, tile_new, wg, wu, wd, layer, tg, p_rows):
    n, d = h.shape
    f = wg.shape[3]
    wmap = lambda i, te, ok, new, s: (layer, te[i], 0, 0)
    return pl.pallas_call(
        functools.partial(_moe_expert_kernel, tg=tg),
        grid_spec=pltpu.PrefetchScalarGridSpec(
            num_scalar_prefetch=4,
            grid=(p_rows // tg,),
            in_specs=[
                pl.BlockSpec(memory_space=pl.ANY),
                pl.BlockSpec((1, 1, d, f), wmap),
                pl.BlockSpec((1, 1, d, f), wmap),
                pl.BlockSpec((1, 1, f, d), wmap),
            ],
            out_specs=pl.BlockSpec((tg, d), lambda i, te, ok, new, s: (i, 0)),
            scratch_shapes=[pltpu.VMEM((2, tg, d), F32), pltpu.SemaphoreType.DMA((2,)),
                            pltpu.VMEM((d, f), BF16), pltpu.VMEM((d, f), BF16), pltpu.VMEM((f, d), BF16)],
        ),
        out_shape=jax.ShapeDtypeStruct((p_rows, d), F32),
        compiler_params=_cparams("arbitrary"),
        name="moe_experts",
    )(tile_exp, tile_ok, tile_new, src, h, wg, wu, wd)


def _moe_combine_kernel(da_ref, db_ref, y_hbm, x_ref, w_ref, g_ref, o_ref, ya, yb, sem, *, tm):
    i = pl.program_id(0)

    def fetch(tile, slot, wait):
        _row_copies(da_ref, tile * tm, tm, y_hbm, ya.at[slot], sem.at[0, slot], wait)
        _row_copies(db_ref, tile * tm, tm, y_hbm, yb.at[slot], sem.at[1, slot], wait)

    @pl.when(i == 0)
    def _():
        fetch(0, 0, False)

    @pl.when(i + 1 < pl.num_programs(0))
    def _():
        fetch(i + 1, (i + 1) % 2, False)

    fetch(i, i % 2, True)
    w = w_ref[...]
    mix = w[:, 0:1] * ya[i % 2] + w[:, 1:2] * yb[i % 2]
    o_ref[...] = x_ref[...] + g_ref[0] * mix


def _moe_combine(y, row_a, row_b, w_ab, x, m, mod, layer):
    d = x.shape[1]
    tm = _tile(math.gcd(m, mod.rows_per_mod), 256)
    return pl.pallas_call(
        functools.partial(_moe_combine_kernel, tm=tm),
        grid_spec=pltpu.PrefetchScalarGridSpec(
            num_scalar_prefetch=2,
            grid=(m // tm,),
            in_specs=[
                pl.BlockSpec(memory_space=pl.ANY),
                pl.BlockSpec((tm, d), lambda i, a, b: (i, 0)),
                pl.BlockSpec((tm, 2), lambda i, a, b: (i, 0)),
                mod.spec(layer, 5, tm),
            ],
            out_specs=pl.BlockSpec((tm, d), lambda i, a, b: (i, 0)),
            scratch_shapes=[pltpu.VMEM((2, tm, d), F32), pltpu.VMEM((2, tm, d), F32),
                            pltpu.SemaphoreType.DMA((2, 2))],
        ),
        out_shape=jax.ShapeDtypeStruct((m, d), F32),
        compiler_params=_cparams("arbitrary"),
        name="moe_combine",
    )(row_a, row_b, y, x, w_ab, mod.table)


def _moe(h, comb_t, wg, wu, wd, layer_w, x, m, mod, layer):
    tg = 256 if m * TOP_K >= 256 * wg.shape[1] else 64
    tile_exp, tile_ok, tile_new, d_a, d_b, w_ab, p_rows = _route(comb_t, tg)
    src = _sorted_row_tokens(d_a, d_b, p_rows, tg)
    y = _moe_experts(h, src, tile_exp, tile_ok, tile_new, wg, wu, wd, layer_w, tg, p_rows)
    none = lambda rows: jnp.where(rows == p_rows, 0, rows)
    return _moe_combine(y, none(d_a), none(d_b), w_ab, x, m, mod, layer)


def kernel(x, c, ctx, c_ctx, ada_w, ada_b, norm_mix_w, norm_ffn_w, dn_w_in, dn_conv_w, dn_A_log, dn_dt_bias, dn_norm_w, dn_w_out, fn_w_in, fn_w_out, router_w, router_bias, moe_w_gate, moe_w_up, moe_w_down, final_norm_w):
    batch, seq, d = x.shape
    ctx_len = ctx.shape[1]
    depth = ada_w.shape[0]
    nh = DN_HEADS
    dn_width = dn_w_out.shape[1]
    hd = dn_width // nh
    fn_width = fn_w_in.shape[2]
    m_lat, m_all = batch * seq, batch * (seq + ctx_len)
    assert batch * ctx_len <= seq and seq % ctx_len == 0

    mod_rows = -(-(batch + 1) // 8) * 8
    cond = jnp.zeros((mod_rows, d), F32).at[:batch].set(c).at[batch].set(c_ctx)
    table = _ada_table(cond, ada_w, ada_b)
    mod = _Mod(table.reshape(depth * mod_rows * N_MOD, 1, d), mod_rows, d, seq)

    router_wt = router_w.T
    xs = jnp.concatenate([x.reshape(m_lat, d), ctx.reshape(batch * ctx_len, d)], axis=0)

    for i in range(depth):
        use_dn = i % N_MIXERS == 0
        j = i // N_MIXERS
        ctx_out = i < depth - 1
        ctx_live = use_dn or ctx_out
        m = m_all if ctx_live else m_lat

        h = _modulate(xs, m, norm_mix_w[i], mod, i, 0, 1)
        if use_dn:
            p = _matmul(h, dn_w_in, j, BF16, 1024, n=4 * dn_width)
            ab = _matmul(h, dn_w_in, j, F32, LANES, col0=4 * dn_width, n=LANES)
            q = _dn_conv(p, dn_conv_w, j, 0, "q", m_lat, ctx_len, dn_width, hd)
            k = _dn_conv(p, dn_conv_w, j, 1, "k", m_lat, ctx_len, dn_width, hd)
            v = _dn_conv(p, dn_conv_w, j, 2, "v", m_lat, ctx_len, dn_width, hd)
            g, beta = _dn_gates(ab, dn_A_log[j].reshape(-1), dn_dt_bias[j].reshape(-1), nh)
            g_rows = g.reshape(m // CHUNK, CHUNK, 2 * nh).transpose(0, 2, 1)
            o_f = _dn_scan(q, k, v, g, g_rows, beta, False, batch, m_lat, nh, hd)
            o_b = _dn_scan(q, k, v, g, g_rows, beta, True, batch, m_lat, nh, hd)
            y = _dn_out(o_f, o_b, p, dn_norm_w[j], dn_width, hd)
            xs = _matmul_gated_residual(y, dn_w_out, j, xs, mod, i, 2)
        else:
            w_fold = _fold_channel_dft(fn_w_in, j, FN_GROUPS)
            u = _matmul(h, w_fold, 0, BF16, 1024)
            mixed = _position_dft_real(u, 0, batch, seq, fn_width)
            if ctx_out:
                mixed = jnp.concatenate([mixed, _position_dft_real(u, m_lat, batch, ctx_len, fn_width)], axis=0)
            xs = _matmul_gated_residual(mixed, fn_w_out, j, xs, mod, i, 2)

        hf, comb_t = _ffn_prep(xs, m, norm_ffn_w[i], mod, i, router_wt, router_bias)
        xs = _moe(hf, comb_t, moe_w_gate, moe_w_up, moe_w_down, i, xs, m, mod, i)

    return _final_norm(xs, m_lat, final_norm_w).reshape(batch, seq, d)
```

```python
import functools
import math

import numpy as np
import jax
import jax.numpy as jnp
from jax import lax
from jax.experimental import pallas as pl
from jax.experimental.pallas import tpu as pltpu

F32 = jnp.float32
BF16 = jnp.bfloat16
I32 = jnp.int32

GRID_W = 64
CHUNK = 64
CONV_W = 5
DN_HEADS = 16
FN_GROUPS = 4
N_GROUPS = 4
GROUP_SCORE_K = 2
TOP_K = 2
N_MOD = 6
N_MIXERS = 2
EPS = 1e-6

VMEM_LIMIT = 56 * 1024 * 1024
LANES = 128


def _cparams(*sem):
    return pltpu.CompilerParams(dimension_semantics=sem, vmem_limit_bytes=VMEM_LIMIT)


def _dot(a, b):
    return jnp.dot(a, b, preferred_element_type=F32)


def _dot_nt(a, b):
    return lax.dot_general(a, b, (((1,), (1,)), ((), ())), preferred_element_type=F32)


def _dot_tn(a, b):
    return lax.dot_general(a, b, (((0,), (0,)), ((), ())), preferred_element_type=F32)


def _split2(x):
    hi = x.astype(BF16)
    lo = (x - hi.astype(F32)).astype(BF16)
    return hi, lo


def _split3(x):
    hi = x.astype(BF16)
    r = x - hi.astype(F32)
    mid = r.astype(BF16)
    lo = (r - mid.astype(F32)).astype(BF16)
    return hi, mid, lo


def _silu(x):
    return x / (1.0 + jnp.exp(-x))


def _sigmoid(x):
    return 1.0 / (1.0 + jnp.exp(-x))


def _tile(n, pref, mult=8):
    if n <= pref:
        return n
    t = (pref // mult) * mult
    while t >= mult:
        if n % t == 0:
            return t
        t -= mult
    return n


def _log2(n):
    assert n & (n - 1) == 0, n
    return n.bit_length() - 1


def _ada_kernel(c_ref, w_ref, b_ref, o_ref):
    s_hi, s_lo = _split2(_silu(c_ref[...]))
    w_hi, w_lo = _split2(w_ref[0])
    acc = _dot(s_hi, w_hi) + _dot(s_lo, w_hi) + _dot(s_hi, w_lo)
    o_ref[0] = acc + b_ref[0]


def _ada_table(cond, ada_w, ada_b):
    depth, d, n = ada_w.shape
    r = cond.shape[0]
    tn = _tile(n, 512, 128)
    return pl.pallas_call(
        _ada_kernel,
        grid=(depth, n // tn),
        in_specs=[
            pl.BlockSpec((r, d), lambda l, j: (0, 0)),
            pl.BlockSpec((1, d, tn), lambda l, j: (l, 0, j)),
            pl.BlockSpec((1, 1, tn), lambda l, j: (l, 0, j)),
        ],
        out_specs=pl.BlockSpec((1, r, tn), lambda l, j: (l, 0, j)),
        out_shape=jax.ShapeDtypeStruct((depth, r, n), F32),
        compiler_params=_cparams("parallel", "parallel"),
        name="ada_table",
    )(cond, ada_w, ada_b.reshape(depth, 1, n))


class _Mod:
    def __init__(self, table3, mod_rows, d, rows_per_mod):
        self.table = table3
        self.mod_rows = mod_rows
        self.d = d
        self.rows_per_mod = rows_per_mod

    def spec(self, layer, k, tm, row_axis=0, width=None, col=None):
        assert self.rows_per_mod % tm == 0
        width = self.d if width is None else width
        rows, mr = self.rows_per_mod, self.mod_rows

        def index_map(*g):
            row = (g[row_axis] * tm) // rows
            return ((layer * mr + row) * N_MOD + k, 0, 0 if col is None else col(*g))
        return pl.BlockSpec((1, 1, width), index_map)


def _modulated(x, nw, shift, scale):
    ms = jnp.mean(x * x, axis=-1, keepdims=True)
    return (x * lax.rsqrt(ms + EPS) * nw) * (1.0 + scale) + shift


def _modulate_kernel(x_ref, nw_ref, sh_ref, sc_ref, o_ref):
    o_ref[...] = _modulated(x_ref[...], nw_ref[...], sh_ref[0], sc_ref[0]).astype(o_ref.dtype)


def _modulate(x, m, nw, mod, layer, k_shift, k_scale):
    d = x.shape[1]
    tm = _tile(math.gcd(m, mod.rows_per_mod), 512)
    return pl.pallas_call(
        _modulate_kernel,
        grid=(m // tm,),
        in_specs=[
            pl.BlockSpec((tm, d), lambda i: (i, 0)),
            pl.BlockSpec((1, d), lambda i: (0, 0)),
            mod.spec(layer, k_shift, tm),
            mod.spec(layer, k_scale, tm),
        ],
        out_specs=pl.BlockSpec((tm, d), lambda i: (i, 0)),
        out_shape=jax.ShapeDtypeStruct((m, d), BF16),
        compiler_params=_cparams("parallel"),
        name="modulate",
    )(x, nw.reshape(1, d), mod.table, mod.table)


def _final_norm_kernel(x_ref, nw_ref, o_ref):
    x = x_ref[...]
    ms = jnp.mean(x * x, axis=-1, keepdims=True)
    o_ref[...] = x * lax.rsqrt(ms + EPS) * nw_ref[...]


def _final_norm(x, m, nw):
    d = x.shape[1]
    tm = _tile(m, 512)
    return pl.pallas_call(
        _final_norm_kernel,
        grid=(m // tm,),
        in_specs=[pl.BlockSpec((tm, d), lambda i: (i, 0)), pl.BlockSpec((1, d), lambda i: (0, 0))],
        out_specs=pl.BlockSpec((tm, d), lambda i: (i, 0)),
        out_shape=jax.ShapeDtypeStruct((m, d), F32),
        compiler_params=_cparams("parallel"),
        name="final_norm",
    )(x, nw.reshape(1, d))


def _matmul_kernel(a_ref, w_ref, o_ref, wb_ref, *, valid_cols):
    @pl.when(pl.program_id(1) == 0)
    def _():
        w = w_ref[0]
        if valid_cols < w.shape[1]:
            w = jnp.where(lax.broadcasted_iota(I32, w.shape, 1) < valid_cols, w, 0.0)
        wb_ref[...] = w.astype(BF16)

    o_ref[...] = _dot(a_ref[...], wb_ref[...]).astype(o_ref.dtype)


def _matmul(a, w, layer, out_dtype, tn_pref, col0=0, n=None):
    m, k = a.shape
    n = w.shape[2] if n is None else n
    tm, tn = _tile(m, 1024), _tile(n, tn_pref, LANES)
    assert col0 % tn == 0
    cb = col0 // tn
    valid_cols = tn if col0 + n <= w.shape[2] else w.shape[2] - col0
    assert valid_cols == tn or n == tn
    return pl.pallas_call(
        functools.partial(_matmul_kernel, valid_cols=valid_cols),
        grid=(n // tn, m // tm),
        in_specs=[pl.BlockSpec((tm, k), lambda j, i: (i, 0)),
                  pl.BlockSpec((1, k, tn), lambda j, i: (layer, 0, cb + j))],
        out_specs=pl.BlockSpec((tm, tn), lambda j, i: (i, j)),
        out_shape=jax.ShapeDtypeStruct((m, n), out_dtype),
        scratch_shapes=[pltpu.VMEM((k, tn), BF16)],
        compiler_params=_cparams("parallel", "arbitrary"),
        name="matmul",
    )(a, w)


def _matmul_res_kernel(a_ref, w_ref, x_ref, g_ref, o_ref, wb_ref):
    @pl.when(pl.program_id(1) == 0)
    def _():
        wb_ref[...] = w_ref[0].astype(BF16)

    o_ref[...] = x_ref[...] + g_ref[0] * _dot(a_ref[...], wb_ref[...])


def _matmul_gated_residual(a, w, layer_w, x, mod, layer, k_gate, tn_pref=512):
    m, k = a.shape
    n = w.shape[2]
    tm = _tile(math.gcd(m, mod.rows_per_mod), 1024)
    tn = _tile(n, tn_pref, LANES)
    return pl.pallas_call(
        _matmul_res_kernel,
        grid=(n // tn, m // tm),
        in_specs=[
            pl.BlockSpec((tm, k), lambda j, i: (i, 0)),
            pl.BlockSpec((1, k, tn), lambda j, i: (layer_w, 0, j)),
            pl.BlockSpec((tm, tn), lambda j, i: (i, j)),
            mod.spec(layer, k_gate, tm, row_axis=1, width=tn, col=lambda j, i: j),
        ],
        out_specs=pl.BlockSpec((tm, tn), lambda j, i: (i, j)),
        out_shape=jax.ShapeDtypeStruct((m, n), F32),
        scratch_shapes=[pltpu.VMEM((k, tn), BF16)],
        compiler_params=_cparams("parallel", "arbitrary"),
        name="matmul_gated_residual",
    )(a, w, x, mod.table)


def _head_sumsq(y, ones_bd):
    sq_hi, sq_lo = _split2(y * y)
    return _dot(sq_hi, ones_bd) + _dot(sq_lo, ones_bd)


def _dn_conv_kernel(u_ref, cw_ref, e_ref, o_ref, coef_ref, *, lat_tiles, lat_len, ctx_len, mode, head_dim):
    i = pl.program_id(1)
    tm = u_ref.shape[0]
    pad = CONV_W // 2
    cw = cw_ref[0]

    @pl.when((i == 0) | (i == lat_tiles))
    def _():
        conv_len = jnp.where(i < lat_tiles, lat_len, ctx_len)
        pos = lax.broadcasted_iota(I32, (tm, 1), 0) & (conv_len - 1)
        for j in range(CONV_W):
            d = j - pad
            inside = ((pos + d >= 0) & (pos + d < conv_len)).astype(F32)
            coef_ref[j] = inside * cw[j:j + 1, :]

    u = u_ref[...].astype(F32)
    acc = u * cw[pad:pad + 1, :]
    for j in range(CONV_W):
        d = j - pad
        if d != 0:
            acc = acc + pltpu.roll(u, (-d) % tm, 0) * coef_ref[j]
    y = _silu(acc)
    if mode != "v":
        y = y * lax.rsqrt(_head_sumsq(y, e_ref[...]) + EPS)
        if mode == "q":
            y = y * (head_dim ** -0.5)
    o_ref[...] = y.astype(o_ref.dtype)


def _dn_conv(p, conv_w, layer, section, mode, m_lat, ctx_len, width, head_dim):
    m = p.shape[0]
    tc = _tile(width, 512, LANES)
    tm = max(GRID_W, ctx_len, 256)
    assert m_lat % tm == 0 and m % tm == 0 and tm % GRID_W == 0 and tm % ctx_len == 0
    ones_bd = jnp.asarray(np.kron(np.eye(tc // head_dim), np.ones((head_dim, head_dim))), BF16)
    nsec = width // tc
    return pl.pallas_call(
        functools.partial(_dn_conv_kernel, lat_tiles=m_lat // tm, lat_len=GRID_W, ctx_len=ctx_len, mode=mode,
                          head_dim=head_dim),
        grid=(nsec, m // tm),
        in_specs=[
            pl.BlockSpec((tm, tc), lambda j, i: (i, section * nsec + j)),
            pl.BlockSpec((1, CONV_W, tc), lambda j, i: (layer, 0, section * nsec + j)),
            pl.BlockSpec((tc, tc), lambda j, i: (0, 0)),
        ],
        out_specs=pl.BlockSpec((tm, tc), lambda j, i: (i, j)),
        out_shape=jax.ShapeDtypeStruct((m, width), BF16),
        scratch_shapes=[pltpu.VMEM((CONV_W, tm, tc), F32)],
        compiler_params=_cparams("parallel", "arbitrary"),
        name="dn_conv_" + mode,
    )(p, conv_w, ones_bd)


def _dn_gates_kernel(ab_ref, al_ref, dt_ref, tl_ref, tu_ref, g_ref, beta_ref, *, nh):
    ab = ab_ref[:, :4 * nh]
    a = ab[:, :2 * nh] + dt_ref[...]
    softplus = jnp.maximum(a, 0.0) + jnp.log1p(jnp.exp(-jnp.abs(a)))
    la = -jnp.exp(al_ref[...]) * softplus
    beta_ref[...] = _sigmoid(ab[:, 2 * nh:])
    parts = _split3(la)
    g_pre = sum(_dot(tl_ref[...], p) for p in parts)
    g_suf = sum(_dot(tu_ref[...], p) for p in parts)
    col = lax.broadcasted_iota(I32, la.shape, 1)
    g_ref[...] = jnp.where(col < nh, g_pre, g_suf)


def _dn_gates(ab, a_log, dt_bias, nh):
    m = ab.shape[0]
    tm = _tile(m, 256, CHUNK)
    r = np.arange(tm)
    same = (r[:, None] // CHUNK) == (r[None, :] // CHUNK)
    tri_l = jnp.asarray(same & (r[None, :] <= r[:, None]), BF16)
    tri_u = jnp.asarray(same & (r[None, :] >= r[:, None]), BF16)
    return pl.pallas_call(
        functools.partial(_dn_gates_kernel, nh=nh),
        grid=(m // tm,),
        in_specs=[
            pl.BlockSpec((tm, ab.shape[1]), lambda i: (i, 0)),
            pl.BlockSpec((1, 2 * nh), lambda i: (0, 0)),
            pl.BlockSpec((1, 2 * nh), lambda i: (0, 0)),
            pl.BlockSpec((tm, tm), lambda i: (0, 0)),
            pl.BlockSpec((tm, tm), lambda i: (0, 0)),
        ],
        out_specs=[pl.BlockSpec((tm, 2 * nh), lambda i: (i, 0))] * 2,
        out_shape=[jax.ShapeDtypeStruct((m, 2 * nh), F32)] * 2,
        compiler_params=_cparams("parallel"),
        name="dn_gates",
    )(ab, a_log.reshape(1, 2 * nh), dt_bias.reshape(1, 2 * nh), tri_l, tri_u)


def _unit_tri_inverses(lms):
    shape = lms[0].shape
    ri = lax.broadcasted_iota(I32, shape, 0)
    ci = lax.broadcasted_iota(I32, shape, 1)
    eye = (ri == ci).astype(F32)
    pair = (ri >> 1) == (ci >> 1)
    ts = [eye - jnp.where(pair, lm, 0.0) for lm in lms]
    for lvl in range(1, _log2(shape[0])):
        cross = ((ri >> (lvl + 1)) == (ci >> (lvl + 1))) & ((ri >> lvl) != (ci >> lvl))
        offs = [jnp.where(cross, lm, 0.0).astype(BF16) for lm in lms]
        tbs = [t.astype(BF16) for t in ts]
        ps = [_dot(tb, off).astype(BF16) for tb, off in zip(tbs, offs)]
        ts = [t - _dot(p, tb) for t, p, tb in zip(ts, ps, tbs)]
    return ts


def _dn_scan_kernel(q_ref, k_ref, v_ref, gc_ref, gr_ref, b_ref, o_ref, s_ref, *, backward, nh, hd):
    @pl.when(pl.program_id(1) == 0)
    def _():
        s_ref[...] = jnp.zeros_like(s_ref)

    n = CHUNK
    heads = range(nh)
    ri = lax.broadcasted_iota(I32, (n, n), 0)
    ci = lax.broadcasted_iota(I32, (n, n), 1)
    causal = (ci >= ri) if backward else (ci <= ri)
    strict = (ci > ri) if backward else (ci < ri)
    last = 0 if backward else n - 1
    dcol = nh if backward else 0
    gc_all = gc_ref[...]
    gr_all = gr_ref[0]
    b_all = b_ref[...]
    gcol = [gc_all[:, dcol + h:dcol + h + 1] for h in heads]
    grow = [gr_all[dcol + h:dcol + h + 1, :] for h in heads]
    bcol = [b_all[:, dcol + h:dcol + h + 1] for h in heads]
    glast = [g[last:last + 1, :] for g in gcol]
    q = [q_ref[:, h * hd:(h + 1) * hd] for h in heads]
    k = [k_ref[:, h * hd:(h + 1) * hd] for h in heads]
    v = [v_ref[:, h * hd:(h + 1) * hd] for h in heads]
    kq_k = [_dot_nt(jnp.concatenate([k[h], q[h]], axis=0), k[h]) for h in heads]
    decay = [jnp.exp(jnp.where(causal, gcol[h] - grow[h], -jnp.inf)) for h in heads]
    lm = [jnp.where(strict, bcol[h] * kq_k[h][:n] * decay[h], 0.0) for h in heads]
    intra = [(kq_k[h][n:] * decay[h]).astype(BF16) for h in heads]
    eg = [jnp.exp(g) for g in gcol]
    rhs = [jnp.concatenate([v[h].astype(F32) * bcol[h], k[h].astype(F32) * (bcol[h] * eg[h])],
                           axis=1).astype(BF16) for h in heads]
    t = _unit_tri_inverses(lm)
    uw = [_dot(t[h].astype(BF16), rhs[h]) for h in heads]
    s = [s_ref[h] for h in heads]
    wq = [jnp.concatenate([uw[h][:, hd:], q[h].astype(F32) * eg[h]], axis=0).astype(BF16) for h in heads]
    ws = [_dot(wq[h], s[h].astype(BF16)) for h in heads]
    vb = [(uw[h][:, :hd] - ws[h][:n]).astype(BF16) for h in heads]
    kg = [(k[h].astype(F32) * jnp.exp(glast[h] - gcol[h])).astype(BF16) for h in heads]
    for h in heads:
        o_ref[:, h * hd:(h + 1) * hd] = ws[h][n:] + _dot(intra[h], vb[h])
    for h in heads:
        s_ref[h] = s[h] * jnp.exp(glast[h]) + _dot_tn(kg[h], vb[h])


def _dn_scan(q, k, v, g, g_rows, beta, backward, batch, m_lat, nh, hd):
    m = q.shape[0]
    nl = m_lat // batch // CHUNK
    nc = (m - m_lat) // batch // CHUNK
    ctx0 = m_lat // CHUNK

    def blk(b, c):
        ctx_blk = ctx0 + b * nc + (nc - 1 - c if backward else c)
        lat_blk = b * nl + (nl - 1 - (c - nc) if backward else c - nc)
        return jnp.where(c < nc, ctx_blk, lat_blk)

    tok = lambda b, c: (blk(b, c), 0)
    tok3 = lambda b, c: (blk(b, c), 0, 0)
    return pl.pallas_call(
        functools.partial(_dn_scan_kernel, backward=backward, nh=nh, hd=hd),
        grid=(batch, nc + nl),
        in_specs=[
            pl.BlockSpec((CHUNK, nh * hd), tok),
            pl.BlockSpec((CHUNK, nh * hd), tok),
            pl.BlockSpec((CHUNK, nh * hd), tok),
            pl.BlockSpec((CHUNK, 2 * nh), tok),
            pl.BlockSpec((1, 2 * nh, CHUNK), tok3),
            pl.BlockSpec((CHUNK, 2 * nh), tok),
        ],
        out_specs=pl.BlockSpec((CHUNK, nh * hd), tok),
        out_shape=jax.ShapeDtypeStruct((m, nh * hd), F32),
        scratch_shapes=[pltpu.VMEM((nh, hd, hd), F32)],
        compiler_params=_cparams("parallel", "arbitrary"),
        name="dn_scan_bwd" if backward else "dn_scan_fwd",
    )(q, k, v, g, g_rows, beta)


def _dn_out_kernel(of_ref, ob_ref, gate_ref, nw_ref, e_ref, y_ref, *, hd):
    o = of_ref[...] + ob_ref[...]
    ms = _head_sumsq(o, e_ref[...]) * (1.0 / hd)
    y = o * lax.rsqrt(ms + EPS) * nw_ref[...]
    y_ref[...] = (y * _silu(gate_ref[...].astype(F32))).astype(y_ref.dtype)


def _dn_out(o_f, o_b, p, norm_w, width, hd):
    m = o_f.shape[0]
    tc = _tile(width, 512, LANES)
    tm = _tile(m, 512)
    nsec = width // tc
    ones_bd = jnp.asarray(np.kron(np.eye(tc // hd), np.ones((hd, hd))), BF16)
    nw = jnp.tile(norm_w.astype(F32), tc // hd).reshape(1, tc)
    return pl.pallas_call(
        functools.partial(_dn_out_kernel, hd=hd),
        grid=(m // tm, nsec),
        in_specs=[
            pl.BlockSpec((tm, tc), lambda i, j: (i, j)),
            pl.BlockSpec((tm, tc), lambda i, j: (i, j)),
            pl.BlockSpec((tm, tc), lambda i, j: (i, 3 * nsec + j)),
            pl.BlockSpec((1, tc), lambda i, j: (0, 0)),
            pl.BlockSpec((tc, tc), lambda i, j: (0, 0)),
        ],
        out_specs=pl.BlockSpec((tm, tc), lambda i, j: (i, j)),
        out_shape=jax.ShapeDtypeStruct((m, width), BF16),
        compiler_params=_cparams("parallel", "parallel"),
        name="dn_out",
    )(o_f, o_b, p, nw, ones_bd)


def _fold_kernel(w_ref, t_ref, o_ref):
    w_hi, w_lo = _split2(w_ref[0])
    t_hi, t_lo = _split2(t_ref[0])
    o_ref[0] = (_dot(w_hi, t_hi) + _dot(w_lo, t_hi) + _dot(w_hi, t_lo)).astype(o_ref.dtype)


def _fold_channel_dft(w_in, layer, groups):
    _, d, width = w_in.shape
    c = width // groups
    idx = np.arange(c)
    ang = 2.0 * np.pi * ((idx[:, None] * idx[None, :]) % c) / c
    table = jnp.asarray(np.stack([np.cos(ang), -np.sin(ang)]) / math.sqrt(c), F32)
    return pl.pallas_call(
        _fold_kernel,
        grid=(2, groups),
        in_specs=[pl.BlockSpec((1, d, c), lambda s, g: (layer, 0, g)),
                  pl.BlockSpec((1, c, c), lambda s, g: (s, 0, 0))],
        out_specs=pl.BlockSpec((1, d, c), lambda s, g: (0, 0, s * groups + g)),
        out_shape=jax.ShapeDtypeStruct((1, d, 2 * width), BF16),
        compiler_params=_cparams("parallel", "parallel"),
        name="fold_channel_dft",
    )(w_in, table)


def _dft_kernel(zr_ref, zi_ref, f_ref, g_ref, o_ref, zf, yf, of, *, n1, n2):
    slabs = range(of.shape[0])
    lanes = lambda l: slice(l * LANES, (l + 1) * LANES)
    for l in slabs:
        zf[0, l] = zr_ref[:, lanes(l)].astype(F32)
        zf[1, l] = zi_ref[:, lanes(l)].astype(F32)

    def stage1(t2, carry):
        z = jnp.concatenate(
            [jnp.concatenate([zf[part, l, pl.ds(t2, n1, stride=n2), :] for l in slabs], axis=1) for part in (0, 1)],
            axis=0)
        y = _dot(f_ref[...], z.astype(BF16))
        for l in slabs:
            yf[l, pl.ds(pl.multiple_of(t2 * 2 * n1, 2 * n1), 2 * n1), :] = y[:, lanes(l)]
        return carry

    lax.fori_loop(0, n2, stage1, 0, unroll=8)

    def stage2(f1, carry):
        z = jnp.concatenate(
            [jnp.concatenate([yf[l, pl.ds(off + f1, n2, stride=2 * n1), :] for l in slabs], axis=1)
             for off in (0, n1)], axis=0)
        r = _dot(g_ref[f1], z.astype(BF16))
        for l in slabs:
            of[l, pl.ds(f1, n2, stride=n1), :] = r[:, lanes(l)]
        return carry

    lax.fori_loop(0, n1, stage2, 0, unroll=8)
    for l in slabs:
        o_ref[:, lanes(l)] = of[l].astype(o_ref.dtype)


def _position_dft_real(u, row0, batch, t_len, width):
    n1 = 1 << (_log2(t_len) // 2)
    n2 = t_len // n1
    a1 = 2.0 * np.pi * ((np.arange(n1)[:, None] * np.arange(n1)[None, :]) % n1) / n1
    c1, s1 = np.cos(a1), np.sin(a1)
    f1m = jnp.asarray(np.block([[c1, s1], [-s1, c1]]) / math.sqrt(n1), BF16)
    f1i, f2i, t2i = np.arange(n1)[:, None, None], np.arange(n2)[None, :, None], np.arange(n2)[None, None, :]
    theta = 2.0 * np.pi * (((t2i * f1i) % t_len) / t_len + ((t2i * f2i) % n2) / n2)
    g2m = jnp.asarray(np.concatenate([np.cos(theta), np.sin(theta)], axis=2) / math.sqrt(n2), BF16)

    tc = _tile(width, 256, LANES)
    nj = width // tc
    assert row0 % t_len == 0
    b0 = row0 // t_len
    return pl.pallas_call(
        functools.partial(_dft_kernel, n1=n1, n2=n2),
        grid=(batch, nj),
        in_specs=[
            pl.BlockSpec((t_len, tc), lambda b, j: (b0 + b, j)),
            pl.BlockSpec((t_len, tc), lambda b, j: (b0 + b, nj + j)),
            pl.BlockSpec((2 * n1, 2 * n1), lambda b, j: (0, 0)),
            pl.BlockSpec((n1, n2, 2 * n2), lambda b, j: (0, 0, 0)),
        ],
        out_specs=pl.BlockSpec((t_len, tc), lambda b, j: (b, j)),
        out_shape=jax.ShapeDtypeStruct((batch * t_len, width), BF16),
        scratch_shapes=[pltpu.VMEM((2, tc // LANES, t_len, LANES), F32),
                        pltpu.VMEM((tc // LANES, n2 * 2 * n1, LANES), F32),
                        pltpu.VMEM((tc // LANES, t_len, LANES), F32)],
        compiler_params=_cparams("parallel", "parallel"),
        name="position_dft",
    )(u, u, f1m, g2m)


def _ffn_prep_kernel(x_ref, nw_ref, sh_ref, sc_ref, rw_ref, rb_ref, h_ref, comb_ref, *, n_exp):
    h = _modulated(x_ref[...], nw_ref[...], sh_ref[0], sc_ref[0])
    h_ref[...] = h.astype(h_ref.dtype)
    h_hi, h_lo = _split2(h)
    r_hi, r_lo = _split2(rw_ref[...])
    logits = _dot_nt(r_hi, h_hi) + _dot_nt(r_lo, h_hi) + _dot_nt(r_hi, h_lo)
    s = _sigmoid(logits)
    sel = s + rb_ref[...]
    srow = [s[e:e + 1, :] for e in range(n_exp)]
    row = [sel[e:e + 1, :] for e in range(n_exp)]
    epg = n_exp // N_GROUPS

    def beats(a, ia, b, ib):
        return (a >= b) if ia < ib else (a > b)

    rank, gscore = {}, []
    for g in range(N_GROUPS):
        members = range(g * epg, (g + 1) * epg)
        for i in members:
            rank[i] = sum(beats(row[j], j, row[i], i).astype(F32) for j in members if j != i)
        gscore.append(sum(jnp.where(rank[i] < GROUP_SCORE_K, row[i], 0.0) for i in members))
    picked = []
    for g in range(N_GROUPS):
        grank = sum(beats(gscore[j], j, gscore[g], g).astype(F32) for j in range(N_GROUPS) if j != g)
        for i in range(g * epg, (g + 1) * epg):
            picked.append(jnp.where((grank < 1.0) & (rank[i] < TOP_K), srow[i], 0.0))
    denom = sum(picked)
    comb_ref[...] = jnp.concatenate([p / denom for p in picked], axis=0)


def _ffn_prep(x, m, nw, mod, layer, router_wt, router_b):
    d = x.shape[1]
    n_exp = router_wt.shape[0]
    tm = _tile(math.gcd(m, mod.rows_per_mod), 512, LANES)
    return pl.pallas_call(
        functools.partial(_ffn_prep_kernel, n_exp=n_exp),
        grid=(m // tm,),
        in_specs=[
            pl.BlockSpec((tm, d), lambda i: (i, 0)),
            pl.BlockSpec((1, d), lambda i: (0, 0)),
            mod.spec(layer, 3, tm),
            mod.spec(layer, 4, tm),
            pl.BlockSpec((n_exp, d), lambda i: (0, 0)),
            pl.BlockSpec((n_exp, 1), lambda i: (0, 0)),
        ],
        out_specs=[pl.BlockSpec((tm, d), lambda i: (i, 0)), pl.BlockSpec((n_exp, tm), lambda i: (0, i))],
        out_shape=[jax.ShapeDtypeStruct((m, d), F32), jax.ShapeDtypeStruct((n_exp, m), F32)],
        compiler_params=_cparams("parallel"),
        name="ffn_prep",
    )(x, nw.reshape(1, d), mod.table, mod.table, router_wt, router_b.reshape(n_exp, 1))


def _route(comb_t, tg):
    assert TOP_K == 2
    n_exp, n = comb_t.shape
    mask = comb_t > 0.0
    cnt = jnp.sum(mask.astype(I32), axis=1)
    padded = ((cnt + tg - 1) // tg) * tg
    ends = jnp.cumsum(padded)
    dest = (ends - padded)[:, None] + jnp.cumsum(mask.astype(I32), axis=1) - 1
    eidx = jnp.arange(n_exp, dtype=I32)[:, None]
    first = jnp.min(jnp.where(mask, eidx, n_exp), axis=0)
    final = jnp.max(jnp.where(mask, eidx, -1), axis=0)
    is_a = mask & (eidx == first)
    is_b = mask & (eidx == final) & (final != first)
    pick = lambda sel, val: jnp.sum(jnp.where(sel, val, 0), axis=0)
    w_ab = jnp.stack([pick(is_a, comb_t), pick(is_b, comb_t)], axis=1)
    p_rows = TOP_K * n + n_exp * tg
    d_a = jnp.where(jnp.any(is_a, axis=0), pick(is_a, dest), p_rows).astype(I32)
    d_b = jnp.where(jnp.any(is_b, axis=0), pick(is_b, dest), p_rows).astype(I32)
    tile_start = jnp.arange(p_rows // tg, dtype=I32) * tg
    tile_exp = jnp.minimum(jnp.sum((tile_start[:, None] >= ends[None, :]).astype(I32), axis=1), n_exp - 1)
    tile_ok = (tile_start < ends[-1]).astype(I32)
    tile_new = jnp.concatenate([jnp.ones((1,), I32), (tile_exp[1:] != tile_exp[:-1]).astype(I32)])
    return tile_exp, tile_ok, tile_new, d_a, d_b, w_ab, p_rows


def _src_kernel(da_ref, db_ref, src_ref, *, n_tok):
    def zero(r, carry):
        src_ref[r] = 0
        return carry
    lax.fori_loop(0, src_ref.shape[0], zero, 0, unroll=8)

    def put(t, carry):
        src_ref[da_ref[t]] = t
        src_ref[db_ref[t]] = t
        return carry
    lax.fori_loop(0, n_tok, put, 0, unroll=8)


def _sorted_row_tokens(d_a, d_b, p_rows, spare):
    n_tok = d_a.shape[0]
    smem = pl.BlockSpec(memory_space=pltpu.SMEM)
    return pl.pallas_call(
        functools.partial(_src_kernel, n_tok=n_tok),
        in_specs=[smem, smem],
        out_specs=smem,
        out_shape=jax.ShapeDtypeStruct((p_rows + spare,), I32),
        name="sorted_row_tokens",
    )(d_a, d_b)


def _row_copies(idx_ref, base, n_rows, src_hbm, dst_buf, sem, wait):
    def body(r, carry):
        cp = pltpu.make_async_copy(src_hbm.at[pl.ds(idx_ref[base + r], 1)], dst_buf.at[pl.ds(r, 1)], sem)
        if wait:
            cp.wait()
        else:
            cp.start()
        return carry
    lax.fori_loop(0, n_rows, body, 0, unroll=8)


def _moe_expert_kernel(texp_ref, tval_ref, tnew_ref, src_ref, h_hbm, wg_ref, wu_ref, wd_ref, y_ref,
                       xbuf, sem, wgb, wub, wdb, *, tg):
    i = pl.program_id(0)
    n_tiles = pl.num_programs(0)
    bufs = (xbuf.at[0], xbuf.at[1])
    prev_valid = tval_ref[jnp.maximum(i - 1, 0)] != 0
    valid = tval_ref[i] != 0

    @pl.when(i == 0)
    def _():
        _row_copies(src_ref, 0, tg, h_hbm, bufs[0], sem.at[0], wait=False)

    @pl.when(valid & (tnew_ref[i] != 0))
    def _():
        wgb[...] = wg_ref[0, 0].astype(BF16)
        wub[...] = wu_ref[0, 0].astype(BF16)
        wdb[...] = wd_ref[0, 0].astype(BF16)

    for slot in (0, 1):
        @pl.when(((i == 0) | prev_valid) & (i % 2 == slot))
        def _():
            _row_copies(src_ref, i * tg, tg, h_hbm, bufs[slot], sem.at[slot], wait=True)

        @pl.when(valid & (i % 2 == slot))
        def _():
            def request(rows):
                for r in rows:
                    pltpu.make_async_copy(h_hbm.at[pl.ds(src_ref[(i + 1) * tg + r], 1)],
                                          bufs[1 - slot].at[pl.ds(r, 1)], sem.at[1 - slot]).start(priority=r % 2)

            third = tg // 3
            x = bufs[slot][...].astype(BF16)
            request(range(0, third))
            a = _dot(x, wgb[...])
            request(range(third, 2 * third))
            b = _dot(x, wub[...])
            request(range(2 * third, tg))
            act = (_silu(a) * b).astype(BF16)
            y_ref[...] = _dot(act, wdb[...])

        @pl.when(valid & (i == n_tiles - 1) & (i % 2 == slot))
        def _():
            _row_copies(src_ref, (i + 1) * tg, tg, h_hbm, bufs[1 - slot], sem.at[1 - slot], wait=True)

    @pl.when(jnp.logical_not(valid))
    def _():
        y_ref[...] = jnp.zeros_like(y_ref)


def _moe_experts(h, src, tile_exp, tile_ok, tile_new, wg, wu, wd, layer, tg, p_rows):
    n, d = h.shape
    f = wg.shape[3]
    wmap = lambda i, te, ok, new, s: (layer, te[i], 0, 0)
    return pl.pallas_call(
        functools.partial(_moe_expert_kernel, tg=tg),
        grid_spec=pltpu.PrefetchScalarGridSpec(
            num_scalar_prefetch=4,
            grid=(p_rows // tg,),
            in_specs=[
                pl.BlockSpec(memory_space=pl.ANY),
                pl.BlockSpec((1, 1, d, f), wmap),
                pl.BlockSpec((1, 1, d, f), wmap),
                pl.BlockSpec((1, 1, f, d), wmap),
            ],
            out_specs=pl.BlockSpec((tg, d), lambda i, te, ok, new, s: (i, 0)),
            scratch_shapes=[pltpu.VMEM((2, tg, d), F32), pltpu.SemaphoreType.DMA((2,)),
                            pltpu.VMEM((d, f), BF16), pltpu.VMEM((d, f), BF16), pltpu.VMEM((f, d), BF16)],
        ),
        out_shape=jax.ShapeDtypeStruct((p_rows, d), F32),
        compiler_params=_cparams("arbitrary"),
        name="moe_experts",
    )(tile_exp, tile_ok, tile_new, src, h, wg, wu, wd)


def _moe_combine_kernel(da_ref, db_ref, y_hbm, x_ref, w_ref, g_ref, o_ref, ya, yb, sem, *, tm):
    i = pl.program_id(0)

    def fetch(tile, slot, wait):
        _row_copies(da_ref, tile * tm, tm, y_hbm, ya.at[slot], sem.at[0, slot], wait)
        _row_copies(db_ref, tile * tm, tm, y_hbm, yb.at[slot], sem.at[1, slot], wait)

    @pl.when(i == 0)
    def _():
        fetch(0, 0, False)

    @pl.when(i + 1 < pl.num_programs(0))
    def _():
        slot = (i + 1) % 2
        for r in range(tm):
            for k, (idx_ref, buf) in enumerate(((da_ref, ya), (db_ref, yb))):
                pltpu.make_async_copy(y_hbm.at[pl.ds(idx_ref[(i + 1) * tm + r], 1)],
                                      buf.at[slot, pl.ds(r, 1)], sem.at[k, slot]).start(priority=k)

    fetch(i, i % 2, True)
    w = w_ref[...]
    mix = w[:, 0:1] * ya[i % 2] + w[:, 1:2] * yb[i % 2]
    o_ref[...] = x_ref[...] + g_ref[0] * mix


def _moe_combine(y, row_a, row_b, w_ab, x, m, mod, layer):
    d = x.shape[1]
    tm = _tile(math.gcd(m, mod.rows_per_mod), 256)
    return pl.pallas_call(
        functools.partial(_moe_combine_kernel, tm=tm),
        grid_spec=pltpu.PrefetchScalarGridSpec(
            num_scalar_prefetch=2,
            grid=(m // tm,),
            in_specs=[
                pl.BlockSpec(memory_space=pl.ANY),
                pl.BlockSpec((tm, d), lambda i, a, b: (i, 0)),
                pl.BlockSpec((tm, 2), lambda i, a, b: (i, 0)),
                mod.spec(layer, 5, tm),
            ],
            out_specs=pl.BlockSpec((tm, d), lambda i, a, b: (i, 0)),
            scratch_shapes=[pltpu.VMEM((2, tm, d), F32), pltpu.VMEM((2, tm, d), F32),
                            pltpu.SemaphoreType.DMA((2, 2))],
        ),
        out_shape=jax.ShapeDtypeStruct((m, d), F32),
        compiler_params=_cparams("arbitrary"),
        name="moe_combine",
    )(row_a, row_b, y, x, w_ab, mod.table)


def _moe(h, comb_t, wg, wu, wd, layer_w, x, m, mod, layer):
    tg = 256 if m * TOP_K >= 256 * wg.shape[1] else 64
    tile_exp, tile_ok, tile_new, d_a, d_b, w_ab, p_rows = _route(comb_t, tg)
    src = _sorted_row_tokens(d_a, d_b, p_rows, tg)
    y = _moe_experts(h, src, tile_exp, tile_ok, tile_new, wg, wu, wd, layer_w, tg, p_rows)
    none = lambda rows: jnp.where(rows == p_rows, 0, rows)
    return _moe_combine(y, none(d_a), none(d_b), w_ab, x, m, mod, layer)


def kernel(x, c, ctx, c_ctx, ada_w, ada_b, norm_mix_w, norm_ffn_w, dn_w_in, dn_conv_w, dn_A_log, dn_dt_bias, dn_norm_w, dn_w_out, fn_w_in, fn_w_out, router_w, router_bias, moe_w_gate, moe_w_up, moe_w_down, final_norm_w):
    batch, seq, d = x.shape
    ctx_len = ctx.shape[1]
    depth = ada_w.shape[0]
    nh = DN_HEADS
    dn_width = dn_w_out.shape[1]
    hd = dn_width // nh
    fn_width = fn_w_in.shape[2]
    m_lat, m_all = batch * seq, batch * (seq + ctx_len)
    assert batch * ctx_len <= seq and seq % ctx_len == 0

    mod_rows = -(-(batch + 1) // 8) * 8
    cond = jnp.zeros((mod_rows, d), F32).at[:batch].set(c).at[batch].set(c_ctx)
    table = _ada_table(cond, ada_w, ada_b)
    mod = _Mod(table.reshape(depth * mod_rows * N_MOD, 1, d), mod_rows, d, seq)

    router_wt = router_w.T
    xs = jnp.concatenate([x.reshape(m_lat, d), ctx.reshape(batch * ctx_len, d)], axis=0)

    for i in range(depth):
        use_dn = i % N_MIXERS == 0
        j = i // N_MIXERS
        ctx_out = i < depth - 1
        ctx_live = use_dn or ctx_out
        m = m_all if ctx_live else m_lat

        h = _modulate(xs, m, norm_mix_w[i], mod, i, 0, 1)
        if use_dn:
            p = _matmul(h, dn_w_in, j, BF16, 1024, n=4 * dn_width)
            ab = _matmul(h, dn_w_in, j, F32, LANES, col0=4 * dn_width, n=LANES)
            q = _dn_conv(p, dn_conv_w, j, 0, "q", m_lat, ctx_len, dn_width, hd)
            k = _dn_conv(p, dn_conv_w, j, 1, "k", m_lat, ctx_len, dn_width, hd)
            v = _dn_conv(p, dn_conv_w, j, 2, "v", m_lat, ctx_len, dn_width, hd)
            g, beta = _dn_gates(ab, dn_A_log[j].reshape(-1), dn_dt_bias[j].reshape(-1), nh)
            g_rows = g.reshape(m // CHUNK, CHUNK, 2 * nh).transpose(0, 2, 1)
            o_f = _dn_scan(q, k, v, g, g_rows, beta, False, batch, m_lat, nh, hd)
            o_b = _dn_scan(q, k, v, g, g_rows, beta, True, batch, m_lat, nh, hd)
            y = _dn_out(o_f, o_b, p, dn_norm_w[j], dn_width, hd)
            xs = _matmul_gated_residual(y, dn_w_out, j, xs, mod, i, 2)
        else:
            w_fold = _fold_channel_dft(fn_w_in, j, FN_GROUPS)
            u = _matmul(h, w_fold, 0, BF16, 1024)
            mixed = _position_dft_real(u, 0, batch, seq, fn_width)
            if ctx_out:
                mixed = jnp.concatenate([mixed, _position_dft_real(u, m_lat, batch, ctx_len, fn_width)], axis=0)
            xs = _matmul_gated_residual(mixed, fn_w_out, j, xs, mod, i, 2)

        hf, comb_t = _ffn_prep(xs, m, norm_ffn_w[i], mod, i, router_wt, router_bias)
        xs = _moe(hf, comb_t, moe_w_gate, moe_w_up, moe_w_down, i, xs, m, mod, i)

    return _final_norm(xs, m_lat, final_norm_w).reshape(batch, seq, d)
```

```python
import functools
import math

import numpy as np
import jax
import jax.numpy as jnp
from jax import lax
from jax.experimental import pallas as pl
from jax.experimental.pallas import tpu as pltpu

F32 = jnp.float32
BF16 = jnp.bfloat16
I32 = jnp.int32

GRID_W = 64
CHUNK = 64
CONV_W = 5
DN_HEADS = 16
FN_GROUPS = 4
N_GROUPS = 4
GROUP_SCORE_K = 2
TOP_K = 2
N_MOD = 6
N_MIXERS = 2
EPS = 1e-6

VMEM_LIMIT = 56 * 1024 * 1024
LANES = 128
MOE_GATHER_AHEAD = 2


def _cparams(*sem):
    return pltpu.CompilerParams(dimension_semantics=sem, vmem_limit_bytes=VMEM_LIMIT)


def _dot(a, b):
    return jnp.dot(a, b, preferred_element_type=F32)


def _dot_nt(a, b):
    return lax.dot_general(a, b, (((1,), (1,)), ((), ())), preferred_element_type=F32)


def _dot_tn(a, b):
    return lax.dot_general(a, b, (((0,), (0,)), ((), ())), preferred_element_type=F32)


def _split2(x):
    hi = x.astype(BF16)
    lo = (x - hi.astype(F32)).astype(BF16)
    return hi, lo


def _split3(x):
    hi = x.astype(BF16)
    r = x - hi.astype(F32)
    mid = r.astype(BF16)
    lo = (r - mid.astype(F32)).astype(BF16)
    return hi, mid, lo


def _silu(x):
    return x / (1.0 + jnp.exp(-x))


def _sigmoid(x):
    return 1.0 / (1.0 + jnp.exp(-x))


def _tile(n, pref, mult=8):
    if n <= pref:
        return n
    t = (pref // mult) * mult
    while t >= mult:
        if n % t == 0:
            return t
        t -= mult
    return n


def _log2(n):
    assert n & (n - 1) == 0, n
    return n.bit_length() - 1


def _ada_kernel(c_ref, w_ref, b_ref, o_ref):
    s_hi, s_lo = _split2(_silu(c_ref[...]))
    w_hi, w_lo = _split2(w_ref[0])
    acc = _dot(s_hi, w_hi) + _dot(s_lo, w_hi) + _dot(s_hi, w_lo)
    o_ref[0] = acc + b_ref[0]


def _ada_table(cond, ada_w, ada_b):
    depth, d, n = ada_w.shape
    r = cond.shape[0]
    tn = _tile(n, 512, 128)
    return pl.pallas_call(
        _ada_kernel,
        grid=(depth, n // tn),
        in_specs=[
            pl.BlockSpec((r, d), lambda l, j: (0, 0)),
            pl.BlockSpec((1, d, tn), lambda l, j: (l, 0, j)),
            pl.BlockSpec((1, 1, tn), lambda l, j: (l, 0, j)),
        ],
        out_specs=pl.BlockSpec((1, r, tn), lambda l, j: (l, 0, j)),
        out_shape=jax.ShapeDtypeStruct((depth, r, n), F32),
        compiler_params=_cparams("parallel", "parallel"),
        name="ada_table",
    )(cond, ada_w, ada_b.reshape(depth, 1, n))


class _Mod:
    def __init__(self, table3, mod_rows, d, rows_per_mod):
        self.table = table3
        self.mod_rows = mod_rows
        self.d = d
        self.rows_per_mod = rows_per_mod

    def spec(self, layer, k, tm, row_axis=0, width=None, col=None):
        assert self.rows_per_mod % tm == 0
        width = self.d if width is None else width
        rows, mr = self.rows_per_mod, self.mod_rows

        def index_map(*g):
            row = (g[row_axis] * tm) // rows
            return ((layer * mr + row) * N_MOD + k, 0, 0 if col is None else col(*g))
        return pl.BlockSpec((1, 1, width), index_map)


def _modulated(x, nw, shift, scale):
    ms = jnp.mean(x * x, axis=-1, keepdims=True)
    return (x * lax.rsqrt(ms + EPS) * nw) * (1.0 + scale) + shift


def _modulate_kernel(x_ref, nw_ref, sh_ref, sc_ref, o_ref):
    o_ref[...] = _modulated(x_ref[...], nw_ref[...], sh_ref[0], sc_ref[0]).astype(o_ref.dtype)


def _modulate(x, m, nw, mod, layer, k_shift, k_scale):
    d = x.shape[1]
    tm = _tile(math.gcd(m, mod.rows_per_mod), 512)
    return pl.pallas_call(
        _modulate_kernel,
        grid=(m // tm,),
        in_specs=[
            pl.BlockSpec((tm, d), lambda i: (i, 0)),
            pl.BlockSpec((1, d), lambda i: (0, 0)),
            mod.spec(layer, k_shift, tm),
            mod.spec(layer, k_scale, tm),
        ],
        out_specs=pl.BlockSpec((tm, d), lambda i: (i, 0)),
        out_shape=jax.ShapeDtypeStruct((m, d), BF16),
        compiler_params=_cparams("parallel"),
        name="modulate",
    )(x, nw.reshape(1, d), mod.table, mod.table)


def _final_norm_kernel(x_ref, nw_ref, o_ref):
    x = x_ref[...]
    ms = jnp.mean(x * x, axis=-1, keepdims=True)
    o_ref[...] = x * lax.rsqrt(ms + EPS) * nw_ref[...]


def _final_norm(x, m, nw):
    d = x.shape[1]
    tm = _tile(m, 512)
    return pl.pallas_call(
        _final_norm_kernel,
        grid=(m // tm,),
        in_specs=[pl.BlockSpec((tm, d), lambda i: (i, 0)), pl.BlockSpec((1, d), lambda i: (0, 0))],
        out_specs=pl.BlockSpec((tm, d), lambda i: (i, 0)),
        out_shape=jax.ShapeDtypeStruct((m, d), F32),
        compiler_params=_cparams("parallel"),
        name="final_norm",
    )(x, nw.reshape(1, d))


def _matmul_kernel(a_ref, w_ref, o_ref, wb_ref, *, valid_cols):
    @pl.when(pl.program_id(1) == 0)
    def _():
        w = w_ref[0]
        if valid_cols < w.shape[1]:
            w = jnp.where(lax.broadcasted_iota(I32, w.shape, 1) < valid_cols, w, 0.0)
        wb_ref[...] = w.astype(BF16)

    o_ref[...] = _dot(a_ref[...], wb_ref[...]).astype(o_ref.dtype)


def _matmul(a, w, layer, out_dtype, tn_pref, col0=0, n=None):
    m, k = a.shape
    n = w.shape[2] if n is None else n
    tm, tn = _tile(m, 1024), _tile(n, tn_pref, LANES)
    assert col0 % tn == 0
    cb = col0 // tn
    valid_cols = tn if col0 + n <= w.shape[2] else w.shape[2] - col0
    assert valid_cols == tn or n == tn
    return pl.pallas_call(
        functools.partial(_matmul_kernel, valid_cols=valid_cols),
        grid=(n // tn, m // tm),
        in_specs=[pl.BlockSpec((tm, k), lambda j, i: (i, 0)),
                  pl.BlockSpec((1, k, tn), lambda j, i: (layer, 0, cb + j))],
        out_specs=pl.BlockSpec((tm, tn), lambda j, i: (i, j)),
        out_shape=jax.ShapeDtypeStruct((m, n), out_dtype),
        scratch_shapes=[pltpu.VMEM((k, tn), BF16)],
        compiler_params=_cparams("parallel", "arbitrary"),
        name="matmul",
    )(a, w)


def _matmul_res_kernel(a_ref, w_ref, x_ref, g_ref, o_ref, wb_ref):
    @pl.when(pl.program_id(1) == 0)
    def _():
        wb_ref[...] = w_ref[0].astype(BF16)

    o_ref[...] = x_ref[...] + g_ref[0] * _dot(a_ref[...], wb_ref[...])


def _matmul_gated_residual(a, w, layer_w, x, mod, layer, k_gate, tn_pref=512):
    m, k = a.shape
    n = w.shape[2]
    tm = _tile(math.gcd(m, mod.rows_per_mod), 1024)
    tn = _tile(n, tn_pref, LANES)
    return pl.pallas_call(
        _matmul_res_kernel,
        grid=(n // tn, m // tm),
        in_specs=[
            pl.BlockSpec((tm, k), lambda j, i: (i, 0)),
            pl.BlockSpec((1, k, tn), lambda j, i: (layer_w, 0, j)),
            pl.BlockSpec((tm, tn), lambda j, i: (i, j)),
            mod.spec(layer, k_gate, tm, row_axis=1, width=tn, col=lambda j, i: j),
        ],
        out_specs=pl.BlockSpec((tm, tn), lambda j, i: (i, j)),
        out_shape=jax.ShapeDtypeStruct((m, n), F32),
        scratch_shapes=[pltpu.VMEM((k, tn), BF16)],
        compiler_params=_cparams("parallel", "arbitrary"),
        name="matmul_gated_residual",
    )(a, w, x, mod.table)


def _head_sumsq(y, ones_bd):
    sq_hi, sq_lo = _split2(y * y)
    return _dot(sq_hi, ones_bd) + _dot(sq_lo, ones_bd)


def _dn_conv_kernel(u_ref, cw_ref, e_ref, o_ref, coef_ref, *, lat_tiles, lat_len, ctx_len, mode, head_dim):
    i = pl.program_id(1)
    tm = u_ref.shape[0]
    pad = CONV_W // 2
    cw = cw_ref[0]

    @pl.when((i == 0) | (i == lat_tiles))
    def _():
        conv_len = jnp.where(i < lat_tiles, lat_len, ctx_len)
        pos = lax.broadcasted_iota(I32, (tm, 1), 0) & (conv_len - 1)
        for j in range(CONV_W):
            d = j - pad
            inside = ((pos + d >= 0) & (pos + d < conv_len)).astype(F32)
            coef_ref[j] = inside * cw[j:j + 1, :]

    u = u_ref[...].astype(F32)
    acc = u * cw[pad:pad + 1, :]
    for j in range(CONV_W):
        d = j - pad
        if d != 0:
            acc = acc + pltpu.roll(u, (-d) % tm, 0) * coef_ref[j]
    y = _silu(acc)
    if mode != "v":
        y = y * lax.rsqrt(_head_sumsq(y, e_ref[...]) + EPS)
        if mode == "q":
            y = y * (head_dim ** -0.5)
    o_ref[...] = y.astype(o_ref.dtype)


def _dn_conv(p, conv_w, layer, section, mode, m_lat, ctx_len, width, head_dim):
    m = p.shape[0]
    tc = _tile(width, 512, LANES)
    tm = max(GRID_W, ctx_len, 256)
    assert m_lat % tm == 0 and m % tm == 0 and tm % GRID_W == 0 and tm % ctx_len == 0
    ones_bd = jnp.asarray(np.kron(np.eye(tc // head_dim), np.ones((head_dim, head_dim))), BF16)
    nsec = width // tc
    return pl.pallas_call(
        functools.partial(_dn_conv_kernel, lat_tiles=m_lat // tm, lat_len=GRID_W, ctx_len=ctx_len, mode=mode,
                          head_dim=head_dim),
        grid=(nsec, m // tm),
        in_specs=[
            pl.BlockSpec((tm, tc), lambda j, i: (i, section * nsec + j)),
            pl.BlockSpec((1, CONV_W, tc), lambda j, i: (layer, 0, section * nsec + j)),
            pl.BlockSpec((tc, tc), lambda j, i: (0, 0)),
        ],
        out_specs=pl.BlockSpec((tm, tc), lambda j, i: (i, j)),
        out_shape=jax.ShapeDtypeStruct((m, width), BF16),
        scratch_shapes=[pltpu.VMEM((CONV_W, tm, tc), F32)],
        compiler_params=_cparams("parallel", "arbitrary"),
        name="dn_conv_" + mode,
    )(p, conv_w, ones_bd)


def _dn_gates_kernel(ab_ref, al_ref, dt_ref, tl_ref, tu_ref, g_ref, beta_ref, *, nh):
    ab = ab_ref[:, :4 * nh]
    a = ab[:, :2 * nh] + dt_ref[...]
    softplus = jnp.maximum(a, 0.0) + jnp.log1p(jnp.exp(-jnp.abs(a)))
    la = -jnp.exp(al_ref[...]) * softplus
    beta_ref[...] = _sigmoid(ab[:, 2 * nh:])
    parts = _split3(la)
    g_pre = sum(_dot(tl_ref[...], p) for p in parts)
    g_suf = sum(_dot(tu_ref[...], p) for p in parts)
    col = lax.broadcasted_iota(I32, la.shape, 1)
    g_ref[...] = jnp.where(col < nh, g_pre, g_suf)


def _dn_gates(ab, a_log, dt_bias, nh):
    m = ab.shape[0]
    tm = _tile(m, 256, CHUNK)
    r = np.arange(tm)
    same = (r[:, None] // CHUNK) == (r[None, :] // CHUNK)
    tri_l = jnp.asarray(same & (r[None, :] <= r[:, None]), BF16)
    tri_u = jnp.asarray(same & (r[None, :] >= r[:, None]), BF16)
    return pl.pallas_call(
        functools.partial(_dn_gates_kernel, nh=nh),
        grid=(m // tm,),
        in_specs=[
            pl.BlockSpec((tm, ab.shape[1]), lambda i: (i, 0)),
            pl.BlockSpec((1, 2 * nh), lambda i: (0, 0)),
            pl.BlockSpec((1, 2 * nh), lambda i: (0, 0)),
            pl.BlockSpec((tm, tm), lambda i: (0, 0)),
            pl.BlockSpec((tm, tm), lambda i: (0, 0)),
        ],
        out_specs=[pl.BlockSpec((tm, 2 * nh), lambda i: (i, 0))] * 2,
        out_shape=[jax.ShapeDtypeStruct((m, 2 * nh), F32)] * 2,
        compiler_params=_cparams("parallel"),
        name="dn_gates",
    )(ab, a_log.reshape(1, 2 * nh), dt_bias.reshape(1, 2 * nh), tri_l, tri_u)


def _unit_tri_inverses(lms):
    shape = lms[0].shape
    ri = lax.broadcasted_iota(I32, shape, 0)
    ci = lax.broadcasted_iota(I32, shape, 1)
    eye = (ri == ci).astype(F32)
    pair = (ri >> 1) == (ci >> 1)
    ts = [eye - jnp.where(pair, lm, 0.0) for lm in lms]
    for lvl in range(1, _log2(shape[0])):
        cross = ((ri >> (lvl + 1)) == (ci >> (lvl + 1))) & ((ri >> lvl) != (ci >> lvl))
        offs = [jnp.where(cross, lm, 0.0).astype(BF16) for lm in lms]
        tbs = [t.astype(BF16) for t in ts]
        ps = [_dot(tb, off).astype(BF16) for tb, off in zip(tbs, offs)]
        ts = [t - _dot(p, tb) for t, p, tb in zip(ts, ps, tbs)]
    return ts


def _dn_scan_kernel(q_ref, k_ref, v_ref, gc_ref, gr_ref, b_ref, o_ref, s_ref, *, backward, nh, hd):
    @pl.when(pl.program_id(1) == 0)
    def _():
        s_ref[...] = jnp.zeros_like(s_ref)

    n = CHUNK
    heads = range(nh)
    ri = lax.broadcasted_iota(I32, (n, n), 0)
    ci = lax.broadcasted_iota(I32, (n, n), 1)
    causal = (ci >= ri) if backward else (ci <= ri)
    strict = (ci > ri) if backward else (ci < ri)
    last = 0 if backward else n - 1
    dcol = nh if backward else 0
    gc_all = gc_ref[...]
    gr_all = gr_ref[0]
    b_all = b_ref[...]
    gcol = [gc_all[:, dcol + h:dcol + h + 1] for h in heads]
    grow = [gr_all[dcol + h:dcol + h + 1, :] for h in heads]
    bcol = [b_all[:, dcol + h:dcol + h + 1] for h in heads]
    glast = [g[last:last + 1, :] for g in gcol]
    q = [q_ref[:, h * hd:(h + 1) * hd] for h in heads]
    k = [k_ref[:, h * hd:(h + 1) * hd] for h in heads]
    v = [v_ref[:, h * hd:(h + 1) * hd] for h in heads]
    kq_k = [_dot_nt(jnp.concatenate([k[h], q[h]], axis=0), k[h]) for h in heads]
    decay = [jnp.exp(jnp.where(causal, gcol[h] - grow[h], -jnp.inf)) for h in heads]
    lm = [jnp.where(strict, bcol[h] * kq_k[h][:n] * decay[h], 0.0) for h in heads]
    intra = [(kq_k[h][n:] * decay[h]).astype(BF16) for h in heads]
    eg = [jnp.exp(g) for g in gcol]
    rhs = [jnp.concatenate([v[h].astype(F32) * bcol[h], k[h].astype(F32) * (bcol[h] * eg[h])],
                           axis=1).astype(BF16) for h in heads]
    t = _unit_tri_inverses(lm)
    uw = [_dot(t[h].astype(BF16), rhs[h]) for h in heads]
    s = [s_ref[h] for h in heads]
    wq = [jnp.concatenate([uw[h][:, hd:], q[h].astype(F32) * eg[h]], axis=0).astype(BF16) for h in heads]
    ws = [_dot(wq[h], s[h].astype(BF16)) for h in heads]
    vb = [(uw[h][:, :hd] - ws[h][:n]).astype(BF16) for h in heads]
    kg = [(k[h].astype(F32) * jnp.exp(glast[h] - gcol[h])).astype(BF16) for h in heads]
    for h in heads:
        o_ref[:, h * hd:(h + 1) * hd] = (ws[h][n:] + _dot(intra[h], vb[h])).astype(o_ref.dtype)
    for h in heads:
        s_ref[h] = s[h] * jnp.exp(glast[h]) + _dot_tn(kg[h], vb[h])


def _dn_scan(q, k, v, g, g_rows, beta, backward, batch, m_lat, nh, hd):
    m = q.shape[0]
    nl = m_lat // batch // CHUNK
    nc = (m - m_lat) // batch // CHUNK
    ctx0 = m_lat // CHUNK

    def blk(b, c):
        ctx_blk = ctx0 + b * nc + (nc - 1 - c if backward else c)
        lat_blk = b * nl + (nl - 1 - (c - nc) if backward else c - nc)
        return jnp.where(c < nc, ctx_blk, lat_blk)

    tok = lambda b, c: (blk(b, c), 0)
    tok3 = lambda b, c: (blk(b, c), 0, 0)
    return pl.pallas_call(
        functools.partial(_dn_scan_kernel, backward=backward, nh=nh, hd=hd),
        grid=(batch, nc + nl),
        in_specs=[
            pl.BlockSpec((CHUNK, nh * hd), tok),
            pl.BlockSpec((CHUNK, nh * hd), tok),
            pl.BlockSpec((CHUNK, nh * hd), tok),
            pl.BlockSpec((CHUNK, 2 * nh), tok),
            pl.BlockSpec((1, 2 * nh, CHUNK), tok3),
            pl.BlockSpec((CHUNK, 2 * nh), tok),
        ],
        out_specs=pl.BlockSpec((CHUNK, nh * hd), tok),
        out_shape=jax.ShapeDtypeStruct((m, nh * hd), BF16),
        scratch_shapes=[pltpu.VMEM((nh, hd, hd), F32)],
        compiler_params=_cparams("parallel", "arbitrary"),
        name="dn_scan_bwd" if backward else "dn_scan_fwd",
    )(q, k, v, g, g_rows, beta)


def _dn_out_kernel(of_ref, ob_ref, gate_ref, nw_ref, e_ref, y_ref, *, hd):
    o = of_ref[...].astype(F32) + ob_ref[...].astype(F32)
    ms = _head_sumsq(o, e_ref[...]) * (1.0 / hd)
    y = o * lax.rsqrt(ms + EPS) * nw_ref[...]
    y_ref[...] = (y * _silu(gate_ref[...].astype(F32))).astype(y_ref.dtype)


def _dn_out(o_f, o_b, p, norm_w, width, hd):
    m = o_f.shape[0]
    tc = _tile(width, 512, LANES)
    tm = _tile(m, 512)
    nsec = width // tc
    ones_bd = jnp.asarray(np.kron(np.eye(tc // hd), np.ones((hd, hd))), BF16)
    nw = jnp.tile(norm_w.astype(F32), tc // hd).reshape(1, tc)
    return pl.pallas_call(
        functools.partial(_dn_out_kernel, hd=hd),
        grid=(m // tm, nsec),
        in_specs=[
            pl.BlockSpec((tm, tc), lambda i, j: (i, j)),
            pl.BlockSpec((tm, tc), lambda i, j: (i, j)),
            pl.BlockSpec((tm, tc), lambda i, j: (i, 3 * nsec + j)),
            pl.BlockSpec((1, tc), lambda i, j: (0, 0)),
            pl.BlockSpec((tc, tc), lambda i, j: (0, 0)),
        ],
        out_specs=pl.BlockSpec((tm, tc), lambda i, j: (i, j)),
        out_shape=jax.ShapeDtypeStruct((m, width), BF16),
        compiler_params=_cparams("parallel", "parallel"),
        name="dn_out",
    )(o_f, o_b, p, nw, ones_bd)


def _fold_kernel(w_ref, t_ref, o_ref):
    w_hi, w_lo = _split2(w_ref[0])
    t_hi, t_lo = _split2(t_ref[0])
    o_ref[0] = (_dot(w_hi, t_hi) + _dot(w_lo, t_hi) + _dot(w_hi, t_lo)).astype(o_ref.dtype)


def _fold_channel_dft(w_in, layer, groups):
    _, d, width = w_in.shape
    c = width // groups
    idx = np.arange(c)
    ang = 2.0 * np.pi * ((idx[:, None] * idx[None, :]) % c) / c
    table = jnp.asarray(np.stack([np.cos(ang), -np.sin(ang)]) / math.sqrt(c), F32)
    return pl.pallas_call(
        _fold_kernel,
        grid=(2, groups),
        in_specs=[pl.BlockSpec((1, d, c), lambda s, g: (layer, 0, g)),
                  pl.BlockSpec((1, c, c), lambda s, g: (s, 0, 0))],
        out_specs=pl.BlockSpec((1, d, c), lambda s, g: (0, 0, s * groups + g)),
        out_shape=jax.ShapeDtypeStruct((1, d, 2 * width), BF16),
        compiler_params=_cparams("parallel", "parallel"),
        name="fold_channel_dft",
    )(w_in, table)


def _dft_kernel(zr_ref, zi_ref, f_ref, g_ref, o_ref, zf, yf, of, *, n1, n2):
    slabs = range(of.shape[0])
    lanes = lambda l: slice(l * LANES, (l + 1) * LANES)
    for l in slabs:
        zf[0, l] = zr_ref[:, lanes(l)].astype(F32)
        zf[1, l] = zi_ref[:, lanes(l)].astype(F32)

    def stage1(t2, carry):
        z = jnp.concatenate(
            [jnp.concatenate([zf[part, l, pl.ds(t2, n1, stride=n2), :] for l in slabs], axis=1) for part in (0, 1)],
            axis=0)
        y = _dot(f_ref[...], z.astype(BF16))
        for l in slabs:
            yf[l, pl.ds(pl.multiple_of(t2 * 2 * n1, 2 * n1), 2 * n1), :] = y[:, lanes(l)]
        return carry

    lax.fori_loop(0, n2, stage1, 0, unroll=8)

    def stage2(f1, carry):
        z = jnp.concatenate(
            [jnp.concatenate([yf[l, pl.ds(off + f1, n2, stride=2 * n1), :] for l in slabs], axis=1)
             for off in (0, n1)], axis=0)
        r = _dot(g_ref[f1], z.astype(BF16))
        for l in slabs:
            of[l, pl.ds(f1, n2, stride=n1), :] = r[:, lanes(l)]
        return carry

    lax.fori_loop(0, n1, stage2, 0, unroll=8)
    for l in slabs:
        o_ref[:, lanes(l)] = of[l].astype(o_ref.dtype)


def _position_dft_real(u, row0, batch, t_len, width):
    n1 = 1 << (_log2(t_len) // 2)
    n2 = t_len // n1
    a1 = 2.0 * np.pi * ((np.arange(n1)[:, None] * np.arange(n1)[None, :]) % n1) / n1
    c1, s1 = np.cos(a1), np.sin(a1)
    f1m = jnp.asarray(np.block([[c1, s1], [-s1, c1]]) / math.sqrt(n1), BF16)
    f1i, f2i, t2i = np.arange(n1)[:, None, None], np.arange(n2)[None, :, None], np.arange(n2)[None, None, :]
    theta = 2.0 * np.pi * (((t2i * f1i) % t_len) / t_len + ((t2i * f2i) % n2) / n2)
    g2m = jnp.asarray(np.concatenate([np.cos(theta), np.sin(theta)], axis=2) / math.sqrt(n2), BF16)

    tc = _tile(width, 256, LANES)
    nj = width // tc
    assert row0 % t_len == 0
    b0 = row0 // t_len
    return pl.pallas_call(
        functools.partial(_dft_kernel, n1=n1, n2=n2),
        grid=(batch, nj),
        in_specs=[
            pl.BlockSpec((t_len, tc), lambda b, j: (b0 + b, j)),
            pl.BlockSpec((t_len, tc), lambda b, j: (b0 + b, nj + j)),
            pl.BlockSpec((2 * n1, 2 * n1), lambda b, j: (0, 0)),
            pl.BlockSpec((n1, n2, 2 * n2), lambda b, j: (0, 0, 0)),
        ],
        out_specs=pl.BlockSpec((t_len, tc), lambda b, j: (b, j)),
        out_shape=jax.ShapeDtypeStruct((batch * t_len, width), BF16),
        scratch_shapes=[pltpu.VMEM((2, tc // LANES, t_len, LANES), F32),
                        pltpu.VMEM((tc // LANES, n2 * 2 * n1, LANES), F32),
                        pltpu.VMEM((tc // LANES, t_len, LANES), F32)],
        compiler_params=_cparams("parallel", "parallel"),
        name="position_dft",
    )(u, u, f1m, g2m)


def _ffn_prep_kernel(x_ref, nw_ref, sh_ref, sc_ref, rw_ref, rb_ref, h_ref, comb_ref, *, n_exp):
    h = _modulated(x_ref[...], nw_ref[...], sh_ref[0], sc_ref[0])
    h_ref[...] = h.astype(h_ref.dtype)
    h_hi, h_lo = _split2(h)
    r_hi, r_lo = _split2(rw_ref[...])
    logits = _dot_nt(r_hi, h_hi) + _dot_nt(r_lo, h_hi) + _dot_nt(r_hi, h_lo)
    s = _sigmoid(logits)
    sel = s + rb_ref[...]
    srow = [s[e:e + 1, :] for e in range(n_exp)]
    row = [sel[e:e + 1, :] for e in range(n_exp)]
    epg = n_exp // N_GROUPS

    def beats(a, ia, b, ib):
        return (a >= b) if ia < ib else (a > b)

    rank, gscore = {}, []
    for g in range(N_GROUPS):
        members = range(g * epg, (g + 1) * epg)
        for i in members:
            rank[i] = sum(beats(row[j], j, row[i], i).astype(F32) for j in members if j != i)
        gscore.append(sum(jnp.where(rank[i] < GROUP_SCORE_K, row[i], 0.0) for i in members))
    picked = []
    for g in range(N_GROUPS):
        grank = sum(beats(gscore[j], j, gscore[g], g).astype(F32) for j in range(N_GROUPS) if j != g)
        for i in range(g * epg, (g + 1) * epg):
            picked.append(jnp.where((grank < 1.0) & (rank[i] < TOP_K), srow[i], 0.0))
    denom = sum(picked)
    comb_ref[...] = jnp.concatenate([p / denom for p in picked], axis=0)


def _ffn_prep(x, m, nw, mod, layer, router_wt, router_b):
    d = x.shape[1]
    n_exp = router_wt.shape[0]
    tm = _tile(math.gcd(m, mod.rows_per_mod), 512, LANES)
    return pl.pallas_call(
        functools.partial(_ffn_prep_kernel, n_exp=n_exp),
        grid=(m // tm,),
        in_specs=[
            pl.BlockSpec((tm, d), lambda i: (i, 0)),
            pl.BlockSpec((1, d), lambda i: (0, 0)),
            mod.spec(layer, 3, tm),
            mod.spec(layer, 4, tm),
            pl.BlockSpec((n_exp, d), lambda i: (0, 0)),
            pl.BlockSpec((n_exp, 1), lambda i: (0, 0)),
        ],
        out_specs=[pl.BlockSpec((tm, d), lambda i: (i, 0)), pl.BlockSpec((n_exp, tm), lambda i: (0, i))],
        out_shape=[jax.ShapeDtypeStruct((m, d), F32), jax.ShapeDtypeStruct((n_exp, m), F32)],
        compiler_params=_cparams("parallel"),
        name="ffn_prep",
    )(x, nw.reshape(1, d), mod.table, mod.table, router_wt, router_b.reshape(n_exp, 1))


def _route(comb_t, tg):
    assert TOP_K == 2
    n_exp, n = comb_t.shape
    mask = comb_t > 0.0
    cnt = jnp.sum(mask.astype(I32), axis=1)
    padded = ((cnt + tg - 1) // tg) * tg
    ends = jnp.cumsum(padded)
    dest = (ends - padded)[:, None] + jnp.cumsum(mask.astype(I32), axis=1) - 1
    eidx = jnp.arange(n_exp, dtype=I32)[:, None]
    first = jnp.min(jnp.where(mask, eidx, n_exp), axis=0)
    final = jnp.max(jnp.where(mask, eidx, -1), axis=0)
    is_a = mask & (eidx == first)
    is_b = mask & (eidx == final) & (final != first)
    pick = lambda sel, val: jnp.sum(jnp.where(sel, val, 0), axis=0)
    w_ab = jnp.stack([pick(is_a, comb_t), pick(is_b, comb_t)], axis=1)
    p_rows = TOP_K * n + n_exp * tg
    d_a = jnp.where(jnp.any(is_a, axis=0), pick(is_a, dest), p_rows).astype(I32)
    d_b = jnp.where(jnp.any(is_b, axis=0), pick(is_b, dest), p_rows).astype(I32)
    tile_start = jnp.arange(p_rows // tg, dtype=I32) * tg
    tile_exp = jnp.minimum(jnp.sum((tile_start[:, None] >= ends[None, :]).astype(I32), axis=1), n_exp - 1)
    tile_ok = (tile_start < ends[-1]).astype(I32)
    tile_new = jnp.concatenate([jnp.ones((1,), I32), (tile_exp[1:] != tile_exp[:-1]).astype(I32)])
    return tile_exp, tile_ok, tile_new, d_a, d_b, w_ab, p_rows


def _src_kernel(da_ref, db_ref, src_ref, *, n_tok):
    def zero(r, carry):
        src_ref[r] = 0
        return carry
    lax.fori_loop(0, src_ref.shape[0], zero, 0, unroll=8)

    def put(t, carry):
        src_ref[da_ref[t]] = t
        src_ref[db_ref[t]] = t
        return carry
    lax.fori_loop(0, n_tok, put, 0, unroll=8)


def _sorted_row_tokens(d_a, d_b, p_rows, spare):
    n_tok = d_a.shape[0]
    smem = pl.BlockSpec(memory_space=pltpu.SMEM)
    return pl.pallas_call(
        functools.partial(_src_kernel, n_tok=n_tok),
        in_specs=[smem, smem],
        out_specs=smem,
        out_shape=jax.ShapeDtypeStruct((p_rows + spare,), I32),
        name="sorted_row_tokens",
    )(d_a, d_b)


def _row_copies(idx_ref, base, n_rows, src_hbm, dst_buf, sem, wait):
    def body(r, carry):
        cp = pltpu.make_async_copy(src_hbm.at[pl.ds(idx_ref[base + r], 1)], dst_buf.at[pl.ds(r, 1)], sem)
        if wait:
            cp.wait()
        else:
            cp.start()
        return carry
    lax.fori_loop(0, n_rows, body, 0, unroll=8)


def _moe_expert_kernel(texp_ref, tval_ref, tnew_ref, src_ref, h_hbm, wg_ref, wu_ref, wd_ref, y_ref,
                       xbuf, sem, wgb, wub, wdb, *, tg, n_tiles):
    i = pl.program_id(0)
    nbuf = MOE_GATHER_AHEAD + 1
    bufs = [xbuf.at[s] for s in range(nbuf)]
    valid = tval_ref[i] != 0

    def requested(tile):
        return (tile < MOE_GATHER_AHEAD) | (tval_ref[jnp.maximum(tile - MOE_GATHER_AHEAD, 0)] != 0)

    @pl.when(i == 0)
    def _():
        for t in range(MOE_GATHER_AHEAD):
            _row_copies(src_ref, t * tg, tg, h_hbm, bufs[t], sem.at[t], wait=False)

    @pl.when(valid & (tnew_ref[i] != 0))
    def _():
        wgb[...] = wg_ref[0, 0].astype(BF16)
        wub[...] = wu_ref[0, 0].astype(BF16)
        wdb[...] = wd_ref[0, 0].astype(BF16)

    for slot in range(nbuf):
        @pl.when(requested(i) & (i % nbuf == slot))
        def _():
            _row_copies(src_ref, i * tg, tg, h_hbm, bufs[slot], sem.at[slot], wait=True)

        @pl.when(valid & (i % nbuf == slot))
        def _():
            ahead = (slot + MOE_GATHER_AHEAD) % nbuf

            def request(rows):
                for r in rows:
                    pltpu.make_async_copy(h_hbm.at[pl.ds(src_ref[(i + MOE_GATHER_AHEAD) * tg + r], 1)],
                                          bufs[ahead].at[pl.ds(r, 1)], sem.at[ahead]).start(priority=r % 2)

            third = tg // 3
            x = bufs[slot][...].astype(BF16)
            request(range(0, third))
            a = _dot(x, wgb[...])
            request(range(third, 2 * third))
            b = _dot(x, wub[...])
            request(range(2 * third, tg))
            act = (_silu(a) * b).astype(BF16)
            y_ref[...] = _dot(act, wdb[...])

    @pl.when(i == n_tiles - 1)
    def _():
        for t in range(n_tiles, n_tiles + MOE_GATHER_AHEAD):
            @pl.when(tval_ref[t - MOE_GATHER_AHEAD] != 0)
            def _():
                _row_copies(src_ref, t * tg, tg, h_hbm, bufs[t % nbuf], sem.at[t % nbuf], wait=True)

    @pl.when(jnp.logical_not(valid))
    def _():
        y_ref[...] = jnp.zeros_like(y_ref)


def _moe_experts(h, src, tile_exp, tile_ok, tile_new, wg, wu, wd, layer, tg, p_rows):
    n, d = h.shape
    f = wg.shape[3]
    wmap = lambda i, te, ok, new, s: (layer, te[i], 0, 0)
    return pl.pallas_call(
        functools.partial(_moe_expert_kernel, tg=tg, n_tiles=p_rows // tg),
        grid_spec=pltpu.PrefetchScalarGridSpec(
            num_scalar_prefetch=4,
            grid=(p_rows // tg,),
            in_specs=[
                pl.BlockSpec(memory_space=pl.ANY),
                pl.BlockSpec((1, 1, d, f), wmap),
                pl.BlockSpec((1, 1, d, f), wmap),
                pl.BlockSpec((1, 1, f, d), wmap),
            ],
            out_specs=pl.BlockSpec((tg, d), lambda i, te, ok, new, s: (i, 0)),
            scratch_shapes=[pltpu.VMEM((MOE_GATHER_AHEAD + 1, tg, d), F32),
                            pltpu.SemaphoreType.DMA((MOE_GATHER_AHEAD + 1,)),
                            pltpu.VMEM((d, f), BF16), pltpu.VMEM((d, f), BF16), pltpu.VMEM((f, d), BF16)],
        ),
        out_shape=jax.ShapeDtypeStruct((p_rows, d), F32),
        compiler_params=_cparams("arbitrary"),
        name="moe_experts",
    )(tile_exp, tile_ok, tile_new, src, h, wg, wu, wd)


def _moe_combine_kernel(da_ref, db_ref, y_hbm, x_ref, w_ref, g_ref, o_ref, ya, yb, sem, *, tm):
    i = pl.program_id(0)

    def fetch(tile, slot, wait):
        _row_copies(da_ref, tile * tm, tm, y_hbm, ya.at[slot], sem.at[0, slot], wait)
        _row_copies(db_ref, tile * tm, tm, y_hbm, yb.at[slot], sem.at[1, slot], wait)

    @pl.when(i == 0)
    def _():
        fetch(0, 0, False)

    @pl.when(i + 1 < pl.num_programs(0))
    def _():
        slot = (i + 1) % 2
        for r in range(tm):
            for k, (idx_ref, buf) in enumerate(((da_ref, ya), (db_ref, yb))):
                pltpu.make_async_copy(y_hbm.at[pl.ds(idx_ref[(i + 1) * tm + r], 1)],
                                      buf.at[slot, pl.ds(r, 1)], sem.at[k, slot]).start(priority=k)

    fetch(i, i % 2, True)
    w = w_ref[...]
    mix = w[:, 0:1] * ya[i % 2] + w[:, 1:2] * yb[i % 2]
    o_ref[...] = x_ref[...] + g_ref[0] * mix


def _moe_combine(y, row_a, row_b, w_ab, x, m, mod, layer):
    d = x.shape[1]
    tm = _tile(math.gcd(m, mod.rows_per_mod), 256)
    return pl.pallas_call(
        functools.partial(_moe_combine_kernel, tm=tm),
        grid_spec=pltpu.PrefetchScalarGridSpec(
            num_scalar_prefetch=2,
            grid=(m // tm,),
            in_specs=[
                pl.BlockSpec(memory_space=pl.ANY),
                pl.BlockSpec((tm, d), lambda i, a, b: (i, 0)),
                pl.BlockSpec((tm, 2), lambda i, a, b: (i, 0)),
                mod.spec(layer, 5, tm),
            ],
            out_specs=pl.BlockSpec((tm, d), lambda i, a, b: (i, 0)),
            scratch_shapes=[pltpu.VMEM((2, tm, d), F32), pltpu.VMEM((2, tm, d), F32),
                            pltpu.SemaphoreType.DMA((2, 2))],
        ),
        out_shape=jax.ShapeDtypeStruct((m, d), F32),
        compiler_params=_cparams("arbitrary"),
        name="moe_combine",
    )(row_a, row_b, y, x, w_ab, mod.table)


def _moe(h, comb_t, wg, wu, wd, layer_w, x, m, mod, layer):
    tg = 256 if m * TOP_K >= 256 * wg.shape[1] else 64
    tile_exp, tile_ok, tile_new, d_a, d_b, w_ab, p_rows = _route(comb_t, tg)
    src = _sorted_row_tokens(d_a, d_b, p_rows, MOE_GATHER_AHEAD * tg)
    y = _moe_experts(h, src, tile_exp, tile_ok, tile_new, wg, wu, wd, layer_w, tg, p_rows)
    none = lambda rows: jnp.where(rows == p_rows, 0, rows)
    return _moe_combine(y, none(d_a), none(d_b), w_ab, x, m, mod, layer)


def kernel(x, c, ctx, c_ctx, ada_w, ada_b, norm_mix_w, norm_ffn_w, dn_w_in, dn_conv_w, dn_A_log, dn_dt_bias, dn_norm_w, dn_w_out, fn_w_in, fn_w_out, router_w, router_bias, moe_w_gate, moe_w_up, moe_w_down, final_norm_w):
    batch, seq, d = x.shape
    ctx_len = ctx.shape[1]
    depth = ada_w.shape[0]
    nh = DN_HEADS
    dn_width = dn_w_out.shape[1]
    hd = dn_width // nh
    fn_width = fn_w_in.shape[2]
    m_lat, m_all = batch * seq, batch * (seq + ctx_len)
    assert batch * ctx_len <= seq and seq % ctx_len == 0

    mod_rows = -(-(batch + 1) // 8) * 8
    cond = jnp.zeros((mod_rows, d), F32).at[:batch].set(c).at[batch].set(c_ctx)
    table = _ada_table(cond, ada_w, ada_b)
    mod = _Mod(table.reshape(depth * mod_rows * N_MOD, 1, d), mod_rows, d, seq)

    router_wt = router_w.T
    xs = jnp.concatenate([x.reshape(m_lat, d), ctx.reshape(batch * ctx_len, d)], axis=0)

    for i in range(depth):
        use_dn = i % N_MIXERS == 0
        j = i // N_MIXERS
        ctx_out = i < depth - 1
        ctx_live = use_dn or ctx_out
        m = m_all if ctx_live else m_lat

        h = _modulate(xs, m, norm_mix_w[i], mod, i, 0, 1)
        if use_dn:
            p = _matmul(h, dn_w_in, j, BF16, 1024, n=4 * dn_width)
            ab = _matmul(h, dn_w_in, j, F32, LANES, col0=4 * dn_width, n=LANES)
            q = _dn_conv(p, dn_conv_w, j, 0, "q", m_lat, ctx_len, dn_width, hd)
            k = _dn_conv(p, dn_conv_w, j, 1, "k", m_lat, ctx_len, dn_width, hd)
            v = _dn_conv(p, dn_conv_w, j, 2, "v", m_lat, ctx_len, dn_width, hd)
            g, beta = _dn_gates(ab, dn_A_log[j].reshape(-1), dn_dt_bias[j].reshape(-1), nh)
            g_rows = g.reshape(m // CHUNK, CHUNK, 2 * nh).transpose(0, 2, 1)
            o_f = _dn_scan(q, k, v, g, g_rows, beta, False, batch, m_lat, nh, hd)
            o_b = _dn_scan(q, k, v, g, g_rows, beta, True, batch, m_lat, nh, hd)
            y = _dn_out(o_f, o_b, p, dn_norm_w[j], dn_width, hd)
            xs = _matmul_gated_residual(y, dn_w_out, j, xs, mod, i, 2)
        else:
            w_fold = _fold_channel_dft(fn_w_in, j, FN_GROUPS)
            u = _matmul(h, w_fold, 0, BF16, 1024)
            mixed = _position_dft_real(u, 0, batch, seq, fn_width)
            if ctx_out:
                mixed = jnp.concatenate([mixed, _position_dft_real(u, m_lat, batch, ctx_len, fn_width)], axis=0)
            xs = _matmul_gated_residual(mixed, fn_w_out, j, xs, mod, i, 2)

        hf, comb_t = _ffn_prep(xs, m, norm_ffn_w[i], mod, i, router_wt, router_bias)
        xs = _moe(hf, comb_t, moe_w_gate, moe_w_up, moe_w_down, i, xs, m, mod, i)

    return _final_norm(xs, m_lat, final_norm_w).reshape(batch, seq, d)
```

```python
import functools
import math

import numpy as np
import jax
import jax.numpy as jnp
from jax import lax
from jax.experimental import pallas as pl
from jax.experimental.pallas import tpu as pltpu

F32 = jnp.float32
BF16 = jnp.bfloat16
I32 = jnp.int32

GRID_W = 64
CHUNK = 64
CONV_W = 5
DN_HEADS = 16
FN_GROUPS = 4
N_GROUPS = 4
GROUP_SCORE_K = 2
TOP_K = 2
N_MOD = 6
N_MIXERS = 2
EPS = 1e-6

VMEM_LIMIT = 56 * 1024 * 1024
LANES = 128
MOE_GATHER_AHEAD = 3
SCAN_GROUP = 32


def _cparams(*sem):
    return pltpu.CompilerParams(dimension_semantics=sem, vmem_limit_bytes=VMEM_LIMIT)


def _dot(a, b):
    return jnp.dot(a, b, preferred_element_type=F32)


def _dot_nt(a, b):
    return lax.dot_general(a, b, (((1,), (1,)), ((), ())), preferred_element_type=F32)


def _dot_tn(a, b):
    return lax.dot_general(a, b, (((0,), (0,)), ((), ())), preferred_element_type=F32)


def _split2(x):
    hi = x.astype(BF16)
    lo = (x - hi.astype(F32)).astype(BF16)
    return hi, lo


def _split3(x):
    hi = x.astype(BF16)
    r = x - hi.astype(F32)
    mid = r.astype(BF16)
    lo = (r - mid.astype(F32)).astype(BF16)
    return hi, mid, lo


def _silu(x):
    return x / (1.0 + jnp.exp(-x))


def _sigmoid(x):
    return 1.0 / (1.0 + jnp.exp(-x))


def _tile(n, pref, mult=8):
    if n <= pref:
        return n
    t = (pref // mult) * mult
    while t >= mult:
        if n % t == 0:
            return t
        t -= mult
    return n


def _log2(n):
    assert n & (n - 1) == 0, n
    return n.bit_length() - 1


def _ada_kernel(c_ref, w_ref, b_ref, o_ref):
    s_hi, s_lo = _split2(_silu(c_ref[...]))
    w_hi, w_lo = _split2(w_ref[0])
    acc = _dot(s_hi, w_hi) + _dot(s_lo, w_hi) + _dot(s_hi, w_lo)
    o_ref[0] = acc + b_ref[0]


def _ada_table(cond, ada_w, ada_b):
    depth, d, n = ada_w.shape
    r = cond.shape[0]
    tn = _tile(n, 512, 128)
    return pl.pallas_call(
        _ada_kernel,
        grid=(depth, n // tn),
        in_specs=[
            pl.BlockSpec((r, d), lambda l, j: (0, 0)),
            pl.BlockSpec((1, d, tn), lambda l, j: (l, 0, j)),
            pl.BlockSpec((1, 1, tn), lambda l, j: (l, 0, j)),
        ],
        out_specs=pl.BlockSpec((1, r, tn), lambda l, j: (l, 0, j)),
        out_shape=jax.ShapeDtypeStruct((depth, r, n), F32),
        compiler_params=_cparams("parallel", "parallel"),
        name="ada_table",
    )(cond, ada_w, ada_b.reshape(depth, 1, n))


class _Mod:
    def __init__(self, table3, mod_rows, d, rows_per_mod):
        self.table = table3
        self.mod_rows = mod_rows
        self.d = d
        self.rows_per_mod = rows_per_mod

    def spec(self, layer, k, tm, row_axis=0, width=None, col=None):
        assert self.rows_per_mod % tm == 0
        width = self.d if width is None else width
        rows, mr = self.rows_per_mod, self.mod_rows

        def index_map(*g):
            row = (g[row_axis] * tm) // rows
            return ((layer * mr + row) * N_MOD + k, 0, 0 if col is None else col(*g))
        return pl.BlockSpec((1, 1, width), index_map)


def _modulated(x, nw, shift, scale):
    ms = jnp.mean(x * x, axis=-1, keepdims=True)
    return (x * lax.rsqrt(ms + EPS) * nw) * (1.0 + scale) + shift


def _modulate_kernel(x_ref, nw_ref, sh_ref, sc_ref, o_ref):
    o_ref[...] = _modulated(x_ref[...], nw_ref[...], sh_ref[0], sc_ref[0]).astype(o_ref.dtype)


def _modulate(x, m, nw, mod, layer, k_shift, k_scale):
    d = x.shape[1]
    tm = _tile(math.gcd(m, mod.rows_per_mod), 512)
    return pl.pallas_call(
        _modulate_kernel,
        grid=(m // tm,),
        in_specs=[
            pl.BlockSpec((tm, d), lambda i: (i, 0)),
            pl.BlockSpec((1, d), lambda i: (0, 0)),
            mod.spec(layer, k_shift, tm),
            mod.spec(layer, k_scale, tm),
        ],
        out_specs=pl.BlockSpec((tm, d), lambda i: (i, 0)),
        out_shape=jax.ShapeDtypeStruct((m, d), BF16),
        compiler_params=_cparams("parallel"),
        name="modulate",
    )(x, nw.reshape(1, d), mod.table, mod.table)


def _final_norm_kernel(x_ref, nw_ref, o_ref):
    x = x_ref[...]
    ms = jnp.mean(x * x, axis=-1, keepdims=True)
    o_ref[...] = x * lax.rsqrt(ms + EPS) * nw_ref[...]


def _final_norm(x, m, nw):
    d = x.shape[1]
    tm = _tile(m, 512)
    return pl.pallas_call(
        _final_norm_kernel,
        grid=(m // tm,),
        in_specs=[pl.BlockSpec((tm, d), lambda i: (i, 0)), pl.BlockSpec((1, d), lambda i: (0, 0))],
        out_specs=pl.BlockSpec((tm, d), lambda i: (i, 0)),
        out_shape=jax.ShapeDtypeStruct((m, d), F32),
        compiler_params=_cparams("parallel"),
        name="final_norm",
    )(x, nw.reshape(1, d))


def _matmul_kernel(a_ref, w_ref, o_ref, wb_ref, *, valid_cols):
    @pl.when(pl.program_id(1) == 0)
    def _():
        w = w_ref[0]
        if valid_cols < w.shape[1]:
            w = jnp.where(lax.broadcasted_iota(I32, w.shape, 1) < valid_cols, w, 0.0)
        wb_ref[...] = w.astype(BF16)

    o_ref[...] = _dot(a_ref[...], wb_ref[...]).astype(o_ref.dtype)


def _matmul(a, w, layer, out_dtype, tn_pref, col0=0, n=None):
    m, k = a.shape
    n = w.shape[2] if n is None else n
    tm, tn = _tile(m, 1024), _tile(n, tn_pref, LANES)
    assert col0 % tn == 0
    cb = col0 // tn
    valid_cols = tn if col0 + n <= w.shape[2] else w.shape[2] - col0
    assert valid_cols == tn or n == tn
    return pl.pallas_call(
        functools.partial(_matmul_kernel, valid_cols=valid_cols),
        grid=(n // tn, m // tm),
        in_specs=[pl.BlockSpec((tm, k), lambda j, i: (i, 0)),
                  pl.BlockSpec((1, k, tn), lambda j, i: (layer, 0, cb + j))],
        out_specs=pl.BlockSpec((tm, tn), lambda j, i: (i, j)),
        out_shape=jax.ShapeDtypeStruct((m, n), out_dtype),
        scratch_shapes=[pltpu.VMEM((k, tn), BF16)],
        compiler_params=_cparams("parallel", "arbitrary"),
        name="matmul",
    )(a, w)


def _matmul_res_kernel(a_ref, w_ref, x_ref, g_ref, o_ref, wb_ref):
    @pl.when(pl.program_id(1) == 0)
    def _():
        wb_ref[...] = w_ref[0].astype(BF16)

    o_ref[...] = x_ref[...] + g_ref[0] * _dot(a_ref[...], wb_ref[...])


def _matmul_gated_residual(a, w, layer_w, x, mod, layer, k_gate, tn_pref=512):
    m, k = a.shape
    n = w.shape[2]
    tm = _tile(math.gcd(m, mod.rows_per_mod), 1024)
    tn = _tile(n, tn_pref, LANES)
    return pl.pallas_call(
        _matmul_res_kernel,
        grid=(n // tn, m // tm),
        in_specs=[
            pl.BlockSpec((tm, k), lambda j, i: (i, 0)),
            pl.BlockSpec((1, k, tn), lambda j, i: (layer_w, 0, j)),
            pl.BlockSpec((tm, tn), lambda j, i: (i, j)),
            mod.spec(layer, k_gate, tm, row_axis=1, width=tn, col=lambda j, i: j),
        ],
        out_specs=pl.BlockSpec((tm, tn), lambda j, i: (i, j)),
        out_shape=jax.ShapeDtypeStruct((m, n), F32),
        scratch_shapes=[pltpu.VMEM((k, tn), BF16)],
        compiler_params=_cparams("parallel", "arbitrary"),
        name="matmul_gated_residual",
    )(a, w, x, mod.table)


def _head_sumsq(y, ones_bd):
    sq_hi, sq_lo = _split2(y * y)
    return _dot(sq_hi, ones_bd) + _dot(sq_lo, ones_bd)


def _dn_conv_kernel(u_ref, cw_ref, e_ref, o_ref, coef_ref, *, lat_tiles, lat_len, ctx_len, mode, head_dim):
    i = pl.program_id(1)
    tm = u_ref.shape[0]
    pad = CONV_W // 2
    cw = cw_ref[0]

    @pl.when((i == 0) | (i == lat_tiles))
    def _():
        conv_len = jnp.where(i < lat_tiles, lat_len, ctx_len)
        pos = lax.broadcasted_iota(I32, (tm, 1), 0) & (conv_len - 1)
        for j in range(CONV_W):
            d = j - pad
            inside = ((pos + d >= 0) & (pos + d < conv_len)).astype(F32)
            coef_ref[j] = inside * cw[j:j + 1, :]

    u = u_ref[...].astype(F32)
    acc = u * cw[pad:pad + 1, :]
    for j in range(CONV_W):
        d = j - pad
        if d != 0:
            acc = acc + pltpu.roll(u, (-d) % tm, 0) * coef_ref[j]
    y = _silu(acc)
    if mode != "v":
        y = y * lax.rsqrt(_head_sumsq(y, e_ref[...]) + EPS)
        if mode == "q":
            y = y * (head_dim ** -0.5)
    o_ref[...] = y.astype(o_ref.dtype)


def _dn_conv(p, conv_w, layer, section, mode, m_lat, ctx_len, width, head_dim):
    m = p.shape[0]
    tc = _tile(width, 512, LANES)
    tm = max(GRID_W, ctx_len, 256)
    assert m_lat % tm == 0 and m % tm == 0 and tm % GRID_W == 0 and tm % ctx_len == 0
    ones_bd = jnp.asarray(np.kron(np.eye(tc // head_dim), np.ones((head_dim, head_dim))), BF16)
    nsec = width // tc
    return pl.pallas_call(
        functools.partial(_dn_conv_kernel, lat_tiles=m_lat // tm, lat_len=GRID_W, ctx_len=ctx_len, mode=mode,
                          head_dim=head_dim),
        grid=(nsec, m // tm),
        in_specs=[
            pl.BlockSpec((tm, tc), lambda j, i: (i, section * nsec + j)),
            pl.BlockSpec((1, CONV_W, tc), lambda j, i: (layer, 0, section * nsec + j)),
            pl.BlockSpec((tc, tc), lambda j, i: (0, 0)),
        ],
        out_specs=pl.BlockSpec((tm, tc), lambda j, i: (i, j)),
        out_shape=jax.ShapeDtypeStruct((m, width), BF16),
        scratch_shapes=[pltpu.VMEM((CONV_W, tm, tc), F32)],
        compiler_params=_cparams("parallel", "arbitrary"),
        name="dn_conv_" + mode,
    )(p, conv_w, ones_bd)


def _dn_gates_kernel(ab_ref, al_ref, dt_ref, tl_ref, tu_ref, g_ref, beta_ref, *, nh):
    ab = ab_ref[:, :4 * nh]
    a = ab[:, :2 * nh] + dt_ref[...]
    softplus = jnp.maximum(a, 0.0) + jnp.log1p(jnp.exp(-jnp.abs(a)))
    la = -jnp.exp(al_ref[...]) * softplus
    beta_ref[...] = _sigmoid(ab[:, 2 * nh:])
    parts = _split3(la)
    g_pre = sum(_dot(tl_ref[...], p) for p in parts)
    g_suf = sum(_dot(tu_ref[...], p) for p in parts)
    col = lax.broadcasted_iota(I32, la.shape, 1)
    g_ref[...] = jnp.where(col < nh, g_pre, g_suf)


def _dn_gates(ab, a_log, dt_bias, nh):
    m = ab.shape[0]
    tm = _tile(m, 256, CHUNK)
    r = np.arange(tm)
    same = (r[:, None] // CHUNK) == (r[None, :] // CHUNK)
    tri_l = jnp.asarray(same & (r[None, :] <= r[:, None]), BF16)
    tri_u = jnp.asarray(same & (r[None, :] >= r[:, None]), BF16)
    return pl.pallas_call(
        functools.partial(_dn_gates_kernel, nh=nh),
        grid=(m // tm,),
        in_specs=[
            pl.BlockSpec((tm, ab.shape[1]), lambda i: (i, 0)),
            pl.BlockSpec((1, 2 * nh), lambda i: (0, 0)),
            pl.BlockSpec((1, 2 * nh), lambda i: (0, 0)),
            pl.BlockSpec((tm, tm), lambda i: (0, 0)),
            pl.BlockSpec((tm, tm), lambda i: (0, 0)),
        ],
        out_specs=[pl.BlockSpec((tm, 2 * nh), lambda i: (i, 0))] * 2,
        out_shape=[jax.ShapeDtypeStruct((m, 2 * nh), F32)] * 2,
        compiler_params=_cparams("parallel"),
        name="dn_gates",
    )(ab, a_log.reshape(1, 2 * nh), dt_bias.reshape(1, 2 * nh), tri_l, tri_u)


def _unit_tri_inverses(lms):
    shape = lms[0].shape
    ri = lax.broadcasted_iota(I32, shape, 0)
    ci = lax.broadcasted_iota(I32, shape, 1)
    eye = (ri == ci).astype(F32)
    pair = (ri >> 1) == (ci >> 1)
    ts = [eye - jnp.where(pair, lm, 0.0) for lm in lms]
    for lvl in range(1, _log2(shape[0])):
        cross = ((ri >> (lvl + 1)) == (ci >> (lvl + 1))) & ((ri >> lvl) != (ci >> lvl))
        offs = [jnp.where(cross, lm, 0.0).astype(BF16) for lm in lms]
        tbs = [t.astype(BF16) for t in ts]
        ps = [_dot(tb, off).astype(BF16) for tb, off in zip(tbs, offs)]
        ts = [t - _dot(p, tb) for t, p, tb in zip(ts, ps, tbs)]
    return ts


def _dn_scan_kernel(qf_ref, kf_ref, vf_ref, gcf_ref, grf_ref, bf_ref, qb_ref, kb_ref, vb_ref, gcb_ref, grb_ref,
                    bb_ref, of_ref, ob_ref, sf_ref, sb_ref, *, nh, hd):
    @pl.when(pl.program_id(1) == 0)
    def _():
        sf_ref[...] = jnp.zeros_like(sf_ref)
        sb_ref[...] = jnp.zeros_like(sb_ref)

    n = CHUNK
    ri = lax.broadcasted_iota(I32, (n, n), 0)
    ci = lax.broadcasted_iota(I32, (n, n), 1)
    q_refs, k_refs, v_refs = (qf_ref, qb_ref), (kf_ref, kb_ref), (vf_ref, vb_ref)
    o_refs, s_refs = (of_ref, ob_ref), (sf_ref, sb_ref)
    causal_d = (ci <= ri, ci >= ri)
    strict_d = (ci < ri, ci > ri)
    last_d = (n - 1, 0)
    gc_d = (gcf_ref[...], gcb_ref[...])
    gr_d = (grf_ref[0], grb_ref[0])
    b_d = (bf_ref[...], bb_ref[...])

    def group(items):
        idx = range(len(items))
        gcol = [gc_d[d][:, d * nh + h:d * nh + h + 1] for d, h in items]
        grow = [gr_d[d][d * nh + h:d * nh + h + 1, :] for d, h in items]
        bcol = [b_d[d][:, d * nh + h:d * nh + h + 1] for d, h in items]
        glast = [gcol[j][last_d[d]:last_d[d] + 1, :] for j, (d, h) in enumerate(items)]
        q = [q_refs[d][:, h * hd:(h + 1) * hd] for d, h in items]
        k = [k_refs[d][:, h * hd:(h + 1) * hd] for d, h in items]
        v = [v_refs[d][:, h * hd:(h + 1) * hd] for d, h in items]
        kq_k = [_dot_nt(jnp.concatenate([k[j], q[j]], axis=0), k[j]) for j in idx]
        decay = [jnp.exp(jnp.where(causal_d[d], gcol[j] - grow[j], -jnp.inf))
                 for j, (d, h) in enumerate(items)]
        lm = [jnp.where(strict_d[d], bcol[j] * kq_k[j][:n] * decay[j], 0.0) for j, (d, h) in enumerate(items)]
        intra = [(kq_k[j][n:] * decay[j]).astype(BF16) for j in idx]
        eg = [jnp.exp(g) for g in gcol]
        rhs = [jnp.concatenate([v[j].astype(F32) * bcol[j], k[j].astype(F32) * (bcol[j] * eg[j])],
                               axis=1).astype(BF16) for j in idx]
        t = _unit_tri_inverses(lm)
        uw = [_dot(t[j].astype(BF16), rhs[j]) for j in idx]
        s = [s_refs[d][h] for d, h in items]
        wq = [jnp.concatenate([uw[j][:, hd:], q[j].astype(F32) * eg[j]], axis=0).astype(BF16) for j in idx]
        ws = [_dot(wq[j], s[j].astype(BF16)) for j in idx]
        vnew = [(uw[j][:, :hd] - ws[j][:n]).astype(BF16) for j in idx]
        kg = [(k[j].astype(F32) * jnp.exp(glast[j] - gcol[j])).astype(BF16) for j in idx]
        for j, (d, h) in enumerate(items):
            o_refs[d][:, h * hd:(h + 1) * hd] = (ws[j][n:] + _dot(intra[j], vnew[j])).astype(o_refs[d].dtype)
        for j, (d, h) in enumerate(items):
            s_refs[d][h] = s[j] * jnp.exp(glast[j]) + _dot_tn(kg[j], vnew[j])

    items = [(d, h) for h in range(nh) for d in (0, 1)]
    for i0 in range(0, len(items), SCAN_GROUP):
        group(items[i0:i0 + SCAN_GROUP])


def _dn_scan(q, k, v, g, g_rows, beta, batch, m_lat, nh, hd):
    m = q.shape[0]
    nl = m_lat // batch // CHUNK
    nc = (m - m_lat) // batch // CHUNK
    ctx0 = m_lat // CHUNK

    def blk(backward):
        def index(b, c):
            ctx_blk = ctx0 + b * nc + (nc - 1 - c if backward else c)
            lat_blk = b * nl + (nl - 1 - (c - nc) if backward else c - nc)
            return jnp.where(c < nc, ctx_blk, lat_blk)
        return index

    def specs(backward):
        tok = lambda b, c: (blk(backward)(b, c), 0)
        tok3 = lambda b, c: (blk(backward)(b, c), 0, 0)
        wide = pl.BlockSpec((CHUNK, nh * hd), tok)
        gate = pl.BlockSpec((CHUNK, 2 * nh), tok)
        return [wide, wide, wide, gate, pl.BlockSpec((1, 2 * nh, CHUNK), tok3), gate], wide

    in_f, out_f = specs(False)
    in_b, out_b = specs(True)
    return pl.pallas_call(
        functools.partial(_dn_scan_kernel, nh=nh, hd=hd),
        grid=(batch, nc + nl),
        in_specs=in_f + in_b,
        out_specs=[out_f, out_b],
        out_shape=[jax.ShapeDtypeStruct((m, nh * hd), BF16)] * 2,
        scratch_shapes=[pltpu.VMEM((nh, hd, hd), F32)] * 2,
        compiler_params=_cparams("parallel", "arbitrary"),
        name="dn_scan",
    )(q, k, v, g, g_rows, beta, q, k, v, g, g_rows, beta)


def _dn_out_kernel(of_ref, ob_ref, gate_ref, nw_ref, e_ref, y_ref, *, hd):
    o = of_ref[...].astype(F32) + ob_ref[...].astype(F32)
    ms = _head_sumsq(o, e_ref[...]) * (1.0 / hd)
    y = o * lax.rsqrt(ms + EPS) * nw_ref[...]
    y_ref[...] = (y * _silu(gate_ref[...].astype(F32))).astype(y_ref.dtype)


def _dn_out(o_f, o_b, p, norm_w, width, hd):
    m = o_f.shape[0]
    tc = _tile(width, 512, LANES)
    tm = _tile(m, 512)
    nsec = width // tc
    ones_bd = jnp.asarray(np.kron(np.eye(tc // hd), np.ones((hd, hd))), BF16)
    nw = jnp.tile(norm_w.astype(F32), tc // hd).reshape(1, tc)
    return pl.pallas_call(
        functools.partial(_dn_out_kernel, hd=hd),
        grid=(m // tm, nsec),
        in_specs=[
            pl.BlockSpec((tm, tc), lambda i, j: (i, j)),
            pl.BlockSpec((tm, tc), lambda i, j: (i, j)),
            pl.BlockSpec((tm, tc), lambda i, j: (i, 3 * nsec + j)),
            pl.BlockSpec((1, tc), lambda i, j: (0, 0)),
            pl.BlockSpec((tc, tc), lambda i, j: (0, 0)),
        ],
        out_specs=pl.BlockSpec((tm, tc), lambda i, j: (i, j)),
        out_shape=jax.ShapeDtypeStruct((m, width), BF16),
        compiler_params=_cparams("parallel", "parallel"),
        name="dn_out",
    )(o_f, o_b, p, nw, ones_bd)


def _fold_kernel(w_ref, t_ref, o_ref):
    w_hi, w_lo = _split2(w_ref[0])
    t_hi, t_lo = _split2(t_ref[0])
    o_ref[0] = (_dot(w_hi, t_hi) + _dot(w_lo, t_hi) + _dot(w_hi, t_lo)).astype(o_ref.dtype)


def _fold_channel_dft(w_in, layer, groups):
    _, d, width = w_in.shape
    c = width // groups
    idx = np.arange(c)
    ang = 2.0 * np.pi * ((idx[:, None] * idx[None, :]) % c) / c
    table = jnp.asarray(np.stack([np.cos(ang), -np.sin(ang)]) / math.sqrt(c), F32)
    return pl.pallas_call(
        _fold_kernel,
        grid=(2, groups),
        in_specs=[pl.BlockSpec((1, d, c), lambda s, g: (layer, 0, g)),
                  pl.BlockSpec((1, c, c), lambda s, g: (s, 0, 0))],
        out_specs=pl.BlockSpec((1, d, c), lambda s, g: (0, 0, s * groups + g)),
        out_shape=jax.ShapeDtypeStruct((1, d, 2 * width), BF16),
        compiler_params=_cparams("parallel", "parallel"),
        name="fold_channel_dft",
    )(w_in, table)


def _dft_kernel(zr_ref, zi_ref, f_ref, g_ref, o_ref, zf, yf, of, *, n1, n2):
    slabs = range(of.shape[0])
    lanes = lambda l: slice(l * LANES, (l + 1) * LANES)
    for l in slabs:
        zf[0, l] = zr_ref[:, lanes(l)].astype(F32)
        zf[1, l] = zi_ref[:, lanes(l)].astype(F32)

    def stage1(t2, carry):
        z = jnp.concatenate(
            [jnp.concatenate([zf[part, l, pl.ds(t2, n1, stride=n2), :] for l in slabs], axis=1) for part in (0, 1)],
            axis=0)
        y = _dot(f_ref[...], z.astype(BF16))
        for l in slabs:
            yf[l, pl.ds(pl.multiple_of(t2 * 2 * n1, 2 * n1), 2 * n1), :] = y[:, lanes(l)]
        return carry

    lax.fori_loop(0, n2, stage1, 0, unroll=8)

    def stage2(f1, carry):
        z = jnp.concatenate(
            [jnp.concatenate([yf[l, pl.ds(off + f1, n2, stride=2 * n1), :] for l in slabs], axis=1)
             for off in (0, n1)], axis=0)
        r = _dot(g_ref[f1], z.astype(BF16))
        for l in slabs:
            of[l, pl.ds(f1, n2, stride=n1), :] = r[:, lanes(l)]
        return carry

    lax.fori_loop(0, n1, stage2, 0, unroll=8)
    for l in slabs:
        o_ref[:, lanes(l)] = of[l].astype(o_ref.dtype)


def _position_dft_real(u, row0, batch, t_len, width):
    n1 = 1 << (_log2(t_len) // 2)
    n2 = t_len // n1
    a1 = 2.0 * np.pi * ((np.arange(n1)[:, None] * np.arange(n1)[None, :]) % n1) / n1
    c1, s1 = np.cos(a1), np.sin(a1)
    f1m = jnp.asarray(np.block([[c1, s1], [-s1, c1]]) / math.sqrt(n1), BF16)
    f1i, f2i, t2i = np.arange(n1)[:, None, None], np.arange(n2)[None, :, None], np.arange(n2)[None, None, :]
    theta = 2.0 * np.pi * (((t2i * f1i) % t_len) / t_len + ((t2i * f2i) % n2) / n2)
    g2m = jnp.asarray(np.concatenate([np.cos(theta), np.sin(theta)], axis=2) / math.sqrt(n2), BF16)

    tc = _tile(width, 256, LANES)
    nj = width // tc
    assert row0 % t_len == 0
    b0 = row0 // t_len
    return pl.pallas_call(
        functools.partial(_dft_kernel, n1=n1, n2=n2),
        grid=(batch, nj),
        in_specs=[
            pl.BlockSpec((t_len, tc), lambda b, j: (b0 + b, j)),
            pl.BlockSpec((t_len, tc), lambda b, j: (b0 + b, nj + j)),
            pl.BlockSpec((2 * n1, 2 * n1), lambda b, j: (0, 0)),
            pl.BlockSpec((n1, n2, 2 * n2), lambda b, j: (0, 0, 0)),
        ],
        out_specs=pl.BlockSpec((t_len, tc), lambda b, j: (b, j)),
        out_shape=jax.ShapeDtypeStruct((batch * t_len, width), BF16),
        scratch_shapes=[pltpu.VMEM((2, tc // LANES, t_len, LANES), F32),
                        pltpu.VMEM((tc // LANES, n2 * 2 * n1, LANES), F32),
                        pltpu.VMEM((tc // LANES, t_len, LANES), F32)],
        compiler_params=_cparams("parallel", "parallel"),
        name="position_dft",
    )(u, u, f1m, g2m)


def _ffn_prep_kernel(x_ref, nw_ref, sh_ref, sc_ref, rw_ref, rb_ref, h_ref, comb_ref, *, n_exp):
    h = _modulated(x_ref[...], nw_ref[...], sh_ref[0], sc_ref[0])
    h_ref[...] = h.astype(h_ref.dtype)
    h_hi, h_lo = _split2(h)
    r_hi, r_lo = _split2(rw_ref[...])
    logits = _dot_nt(r_hi, h_hi) + _dot_nt(r_lo, h_hi) + _dot_nt(r_hi, h_lo)
    s = _sigmoid(logits)
    sel = s + rb_ref[...]
    srow = [s[e:e + 1, :] for e in range(n_exp)]
    row = [sel[e:e + 1, :] for e in range(n_exp)]
    epg = n_exp // N_GROUPS

    def beats(a, ia, b, ib):
        return (a >= b) if ia < ib else (a > b)

    rank, gscore = {}, []
    for g in range(N_GROUPS):
        members = range(g * epg, (g + 1) * epg)
        for i in members:
            rank[i] = sum(beats(row[j], j, row[i], i).astype(F32) for j in members if j != i)
        gscore.append(sum(jnp.where(rank[i] < GROUP_SCORE_K, row[i], 0.0) for i in members))
    picked = []
    for g in range(N_GROUPS):
        grank = sum(beats(gscore[j], j, gscore[g], g).astype(F32) for j in range(N_GROUPS) if j != g)
        for i in range(g * epg, (g + 1) * epg):
            picked.append(jnp.where((grank < 1.0) & (rank[i] < TOP_K), srow[i], 0.0))
    denom = sum(picked)
    comb_ref[...] = jnp.concatenate([p / denom for p in picked], axis=0)


def _ffn_prep(x, m, nw, mod, layer, router_wt, router_b):
    d = x.shape[1]
    n_exp = router_wt.shape[0]
    tm = _tile(math.gcd(m, mod.rows_per_mod), 512, LANES)
    return pl.pallas_call(
        functools.partial(_ffn_prep_kernel, n_exp=n_exp),
        grid=(m // tm,),
        in_specs=[
            pl.BlockSpec((tm, d), lambda i: (i, 0)),
            pl.BlockSpec((1, d), lambda i: (0, 0)),
            mod.spec(layer, 3, tm),
            mod.spec(layer, 4, tm),
            pl.BlockSpec((n_exp, d), lambda i: (0, 0)),
            pl.BlockSpec((n_exp, 1), lambda i: (0, 0)),
        ],
        out_specs=[pl.BlockSpec((tm, d), lambda i: (i, 0)), pl.BlockSpec((n_exp, tm), lambda i: (0, i))],
        out_shape=[jax.ShapeDtypeStruct((m, d), F32), jax.ShapeDtypeStruct((n_exp, m), F32)],
        compiler_params=_cparams("parallel"),
        name="ffn_prep",
    )(x, nw.reshape(1, d), mod.table, mod.table, router_wt, router_b.reshape(n_exp, 1))


def _route(comb_t, tg):
    assert TOP_K == 2
    n_exp, n = comb_t.shape
    mask = comb_t > 0.0
    cnt = jnp.sum(mask.astype(I32), axis=1)
    padded = ((cnt + tg - 1) // tg) * tg
    ends = jnp.cumsum(padded)
    dest = (ends - padded)[:, None] + jnp.cumsum(mask.astype(I32), axis=1) - 1
    eidx = jnp.arange(n_exp, dtype=I32)[:, None]
    first = jnp.min(jnp.where(mask, eidx, n_exp), axis=0)
    final = jnp.max(jnp.where(mask, eidx, -1), axis=0)
    is_a = mask & (eidx == first)
    is_b = mask & (eidx == final) & (final != first)
    pick = lambda sel, val: jnp.sum(jnp.where(sel, val, 0), axis=0)
    w_ab = jnp.stack([pick(is_a, comb_t), pick(is_b, comb_t)], axis=1)
    p_rows = TOP_K * n + n_exp * tg
    d_a = jnp.where(jnp.any(is_a, axis=0), pick(is_a, dest), p_rows).astype(I32)
    d_b = jnp.where(jnp.any(is_b, axis=0), pick(is_b, dest), p_rows).astype(I32)
    tile_start = jnp.arange(p_rows // tg, dtype=I32) * tg
    tile_exp = jnp.minimum(jnp.sum((tile_start[:, None] >= ends[None, :]).astype(I32), axis=1), n_exp - 1)
    tile_ok = (tile_start < ends[-1]).astype(I32)
    tile_new = jnp.concatenate([jnp.ones((1,), I32), (tile_exp[1:] != tile_exp[:-1]).astype(I32)])
    return tile_exp, tile_ok, tile_new, d_a, d_b, w_ab, p_rows


def _src_kernel(da_ref, db_ref, src_ref, *, n_tok):
    def zero(r, carry):
        src_ref[r] = 0
        return carry
    lax.fori_loop(0, src_ref.shape[0], zero, 0, unroll=8)

    def put(t, carry):
        src_ref[da_ref[t]] = t
        src_ref[db_ref[t]] = t
        return carry
    lax.fori_loop(0, n_tok, put, 0, unroll=8)


def _sorted_row_tokens(d_a, d_b, p_rows, spare):
    n_tok = d_a.shape[0]
    smem = pl.BlockSpec(memory_space=pltpu.SMEM)
    return pl.pallas_call(
        functools.partial(_src_kernel, n_tok=n_tok),
        in_specs=[smem, smem],
        out_specs=smem,
        out_shape=jax.ShapeDtypeStruct((p_rows + spare,), I32),
        name="sorted_row_tokens",
    )(d_a, d_b)


def _row_copies(idx_ref, base, n_rows, src_hbm, dst_buf, sem, wait):
    def body(r, carry):
        cp = pltpu.make_async_copy(src_hbm.at[pl.ds(idx_ref[base + r], 1)], dst_buf.at[pl.ds(r, 1)], sem)
        if wait:
            cp.wait()
        else:
            cp.start()
        return carry
    lax.fori_loop(0, n_rows, body, 0, unroll=8)


def _moe_expert_kernel(texp_ref, tval_ref, tnew_ref, src_ref, h_hbm, wg_ref, wu_ref, wd_ref, y_ref,
                       xbuf, sem, wgb, wub, wdb, *, tg, n_tiles):
    i = pl.program_id(0)
    nbuf = MOE_GATHER_AHEAD + 1
    bufs = [xbuf.at[s] for s in range(nbuf)]
    valid = tval_ref[i] != 0

    def requested(tile):
        return (tile < MOE_GATHER_AHEAD) | (tval_ref[jnp.maximum(tile - MOE_GATHER_AHEAD, 0)] != 0)

    @pl.when(i == 0)
    def _():
        for t in range(MOE_GATHER_AHEAD):
            _row_copies(src_ref, t * tg, tg, h_hbm, bufs[t], sem.at[t], wait=False)

    @pl.when(valid & (tnew_ref[i] != 0))
    def _():
        wgb[...] = wg_ref[0, 0].astype(BF16)
        wub[...] = wu_ref[0, 0].astype(BF16)
        wdb[...] = wd_ref[0, 0].astype(BF16)

    for slot in range(nbuf):
        @pl.when(requested(i) & (i % nbuf == slot))
        def _():
            _row_copies(src_ref, i * tg, tg, h_hbm, bufs[slot], sem.at[slot], wait=True)

        @pl.when(valid & (i % nbuf == slot))
        def _():
            ahead = (slot + MOE_GATHER_AHEAD) % nbuf

            def request(rows):
                for r in rows:
                    pltpu.make_async_copy(h_hbm.at[pl.ds(src_ref[(i + MOE_GATHER_AHEAD) * tg + r], 1)],
                                          bufs[ahead].at[pl.ds(r, 1)], sem.at[ahead]).start(priority=r % 2)

            third = tg // 3
            x = bufs[slot][...].astype(BF16)
            request(range(0, third))
            a = _dot(x, wgb[...])
            request(range(third, 2 * third))
            b = _dot(x, wub[...])
            request(range(2 * third, tg))
            act = (_silu(a) * b).astype(BF16)
            y_ref[...] = _dot(act, wdb[...])

    @pl.when(i == n_tiles - 1)
    def _():
        for t in range(n_tiles, n_tiles + MOE_GATHER_AHEAD):
            @pl.when(tval_ref[t - MOE_GATHER_AHEAD] != 0)
            def _():
                _row_copies(src_ref, t * tg, tg, h_hbm, bufs[t % nbuf], sem.at[t % nbuf], wait=True)

    @pl.when(jnp.logical_not(valid))
    def _():
        y_ref[...] = jnp.zeros_like(y_ref)


def _moe_experts(h, src, tile_exp, tile_ok, tile_new, wg, wu, wd, layer, tg, p_rows):
    n, d = h.shape
    f = wg.shape[3]
    wmap = lambda i, te, ok, new, s: (layer, te[i], 0, 0)
    return pl.pallas_call(
        functools.partial(_moe_expert_kernel, tg=tg, n_tiles=p_rows // tg),
        grid_spec=pltpu.PrefetchScalarGridSpec(
            num_scalar_prefetch=4,
            grid=(p_rows // tg,),
            in_specs=[
                pl.BlockSpec(memory_space=pl.ANY),
                pl.BlockSpec((1, 1, d, f), wmap),
                pl.BlockSpec((1, 1, d, f), wmap),
                pl.BlockSpec((1, 1, f, d), wmap),
            ],
            out_specs=pl.BlockSpec((tg, d), lambda i, te, ok, new, s: (i, 0)),
            scratch_shapes=[pltpu.VMEM((MOE_GATHER_AHEAD + 1, tg, d), F32),
                            pltpu.SemaphoreType.DMA((MOE_GATHER_AHEAD + 1,)),
                            pltpu.VMEM((d, f), BF16), pltpu.VMEM((d, f), BF16), pltpu.VMEM((f, d), BF16)],
        ),
        out_shape=jax.ShapeDtypeStruct((p_rows, d), F32),
        compiler_params=_cparams("arbitrary"),
        name="moe_experts",
    )(tile_exp, tile_ok, tile_new, src, h, wg, wu, wd)


def _moe_combine_kernel(da_ref, db_ref, y_hbm, x_ref, w_ref, g_ref, o_ref, ya, yb, sem, *, tm):
    i = pl.program_id(0)

    def fetch(tile, slot, wait):
        _row_copies(da_ref, tile * tm, tm, y_hbm, ya.at[slot], sem.at[0, slot], wait)
        _row_copies(db_ref, tile * tm, tm, y_hbm, yb.at[slot], sem.at[1, slot], wait)

    @pl.when(i == 0)
    def _():
        fetch(0, 0, False)

    @pl.when(i + 1 < pl.num_programs(0))
    def _():
        slot = (i + 1) % 2
        for r in range(tm):
            for k, (idx_ref, buf) in enumerate(((da_ref, ya), (db_ref, yb))):
                pltpu.make_async_copy(y_hbm.at[pl.ds(idx_ref[(i + 1) * tm + r], 1)],
                                      buf.at[slot, pl.ds(r, 1)], sem.at[k, slot]).start(priority=k)

    fetch(i, i % 2, True)
    w = w_ref[...]
    mix = w[:, 0:1] * ya[i % 2] + w[:, 1:2] * yb[i % 2]
    o_ref[...] = x_ref[...] + g_ref[0] * mix


def _moe_combine(y, row_a, row_b, w_ab, x, m, mod, layer):
    d = x.shape[1]
    tm = _tile(math.gcd(m, mod.rows_per_mod), 256)
    return pl.pallas_call(
        functools.partial(_moe_combine_kernel, tm=tm),
        grid_spec=pltpu.PrefetchScalarGridSpec(
            num_scalar_prefetch=2,
            grid=(m // tm,),
            in_specs=[
                pl.BlockSpec(memory_space=pl.ANY),
                pl.BlockSpec((tm, d), lambda i, a, b: (i, 0)),
                pl.BlockSpec((tm, 2), lambda i, a, b: (i, 0)),
                mod.spec(layer, 5, tm),
            ],
            out_specs=pl.BlockSpec((tm, d), lambda i, a, b: (i, 0)),
            scratch_shapes=[pltpu.VMEM((2, tm, d), F32), pltpu.VMEM((2, tm, d), F32),
                            pltpu.SemaphoreType.DMA((2, 2))],
        ),
        out_shape=jax.ShapeDtypeStruct((m, d), F32),
        compiler_params=_cparams("arbitrary"),
        name="moe_combine",
    )(row_a, row_b, y, x, w_ab, mod.table)


def _moe(h, comb_t, wg, wu, wd, layer_w, x, m, mod, layer):
    tg = 256 if m * TOP_K >= 256 * wg.shape[1] else 64
    tile_exp, tile_ok, tile_new, d_a, d_b, w_ab, p_rows = _route(comb_t, tg)
    src = _sorted_row_tokens(d_a, d_b, p_rows, MOE_GATHER_AHEAD * tg)
    y = _moe_experts(h, src, tile_exp, tile_ok, tile_new, wg, wu, wd, layer_w, tg, p_rows)
    none = lambda rows: jnp.where(rows == p_rows, 0, rows)
    return _moe_combine(y, none(d_a), none(d_b), w_ab, x, m, mod, layer)


def kernel(x, c, ctx, c_ctx, ada_w, ada_b, norm_mix_w, norm_ffn_w, dn_w_in, dn_conv_w, dn_A_log, dn_dt_bias, dn_norm_w, dn_w_out, fn_w_in, fn_w_out, router_w, router_bias, moe_w_gate, moe_w_up, moe_w_down, final_norm_w):
    batch, seq, d = x.shape
    ctx_len = ctx.shape[1]
    depth = ada_w.shape[0]
    nh = DN_HEADS
    dn_width = dn_w_out.shape[1]
    hd = dn_width // nh
    fn_width = fn_w_in.shape[2]
    m_lat, m_all = batch * seq, batch * (seq + ctx_len)
    assert batch * ctx_len <= seq and seq % ctx_len == 0

    mod_rows = -(-(batch + 1) // 8) * 8
    cond = jnp.zeros((mod_rows, d), F32).at[:batch].set(c).at[batch].set(c_ctx)
    table = _ada_table(cond, ada_w, ada_b)
    mod = _Mod(table.reshape(depth * mod_rows * N_MOD, 1, d), mod_rows, d, seq)

    router_wt = router_w.T
    xs = jnp.concatenate([x.reshape(m_lat, d), ctx.reshape(batch * ctx_len, d)], axis=0)

    for i in range(depth):
        use_dn = i % N_MIXERS == 0
        j = i // N_MIXERS
        ctx_out = i < depth - 1
        ctx_live = use_dn or ctx_out
        m = m_all if ctx_live else m_lat

        h = _modulate(xs, m, norm_mix_w[i], mod, i, 0, 1)
        if use_dn:
            p = _matmul(h, dn_w_in, j, BF16, 1024, n=4 * dn_width)
            ab = _matmul(h, dn_w_in, j, F32, LANES, col0=4 * dn_width, n=LANES)
            q = _dn_conv(p, dn_conv_w, j, 0, "q", m_lat, ctx_len, dn_width, hd)
            k = _dn_conv(p, dn_conv_w, j, 1, "k", m_lat, ctx_len, dn_width, hd)
            v = _dn_conv(p, dn_conv_w, j, 2, "v", m_lat, ctx_len, dn_width, hd)
            g, beta = _dn_gates(ab, dn_A_log[j].reshape(-1), dn_dt_bias[j].reshape(-1), nh)
            g_rows = g.reshape(m // CHUNK, CHUNK, 2 * nh).transpose(0, 2, 1)
            o_f, o_b = _dn_scan(q, k, v, g, g_rows, beta, batch, m_lat, nh, hd)
            y = _dn_out(o_f, o_b, p, dn_norm_w[j], dn_width, hd)
            xs = _matmul_gated_residual(y, dn_w_out, j, xs, mod, i, 2)
        else:
            w_fold = _fold_channel_dft(fn_w_in, j, FN_GROUPS)
            u = _matmul(h, w_fold, 0, BF16, 1024)
            mixed = _position_dft_real(u, 0, batch, seq, fn_width)
            if ctx_out:
                mixed = jnp.concatenate([mixed, _position_dft_real(u, m_lat, batch, ctx_len, fn_width)], axis=0)
            xs = _matmul_gated_residual(mixed, fn_w_out, j, xs, mod, i, 2)

        hf, comb_t = _ffn_prep(xs, m, norm_ffn_w[i], mod, i, router_wt, router_bias)
        xs = _moe(hf, comb_t, moe_w_gate, moe_w_up, moe_w_down, i, xs, m, mod, i)

    return _final_norm(xs, m_lat, final_norm_w).reshape(batch, seq, d)
```

```python
import functools
import math

import numpy as np
import jax
import jax.numpy as jnp
from jax import lax
from jax.experimental import pallas as pl
from jax.experimental.pallas import tpu as pltpu

F32 = jnp.float32
BF16 = jnp.bfloat16
I32 = jnp.int32

GRID_W = 64
CHUNK = 64
CONV_W = 5
DN_HEADS = 16
FN_GROUPS = 4
N_GROUPS = 4
GROUP_SCORE_K = 2
TOP_K = 2
N_MOD = 6
N_MIXERS = 2
EPS = 1e-6

VMEM_LIMIT = 56 * 1024 * 1024
LANES = 128
MOE_GATHER_AHEAD = 3
SCAN_GROUP = 32


def _cparams(*sem):
    return pltpu.CompilerParams(dimension_semantics=sem, vmem_limit_bytes=VMEM_LIMIT)


def _dot(a, b):
    return jnp.dot(a, b, preferred_element_type=F32)


def _dot_nt(a, b):
    return lax.dot_general(a, b, (((1,), (1,)), ((), ())), preferred_element_type=F32)


def _dot_tn(a, b):
    return lax.dot_general(a, b, (((0,), (0,)), ((), ())), preferred_element_type=F32)


def _split2(x):
    hi = x.astype(BF16)
    lo = (x - hi.astype(F32)).astype(BF16)
    return hi, lo


def _split3(x):
    hi = x.astype(BF16)
    r = x - hi.astype(F32)
    mid = r.astype(BF16)
    lo = (r - mid.astype(F32)).astype(BF16)
    return hi, mid, lo


def _silu(x):
    return x / (1.0 + jnp.exp(-x))


def _sigmoid(x):
    return 1.0 / (1.0 + jnp.exp(-x))


def _tile(n, pref, mult=8):
    if n <= pref:
        return n
    t = (pref // mult) * mult
    while t >= mult:
        if n % t == 0:
            return t
        t -= mult
    return n


def _log2(n):
    assert n & (n - 1) == 0, n
    return n.bit_length() - 1


def _ada_kernel(c_ref, w_ref, b_ref, o_ref):
    s_hi, s_lo = _split2(_silu(c_ref[...]))
    w_hi, w_lo = _split2(w_ref[0])
    acc = _dot(s_hi, w_hi) + _dot(s_lo, w_hi) + _dot(s_hi, w_lo)
    o_ref[0] = acc + b_ref[0]


def _ada_table(cond, ada_w, ada_b):
    depth, d, n = ada_w.shape
    r = cond.shape[0]
    tn = _tile(n, 512, 128)
    return pl.pallas_call(
        _ada_kernel,
        grid=(depth, n // tn),
        in_specs=[
            pl.BlockSpec((r, d), lambda l, j: (0, 0)),
            pl.BlockSpec((1, d, tn), lambda l, j: (l, 0, j)),
            pl.BlockSpec((1, 1, tn), lambda l, j: (l, 0, j)),
        ],
        out_specs=pl.BlockSpec((1, r, tn), lambda l, j: (l, 0, j)),
        out_shape=jax.ShapeDtypeStruct((depth, r, n), F32),
        compiler_params=_cparams("parallel", "parallel"),
        name="ada_table",
    )(cond, ada_w, ada_b.reshape(depth, 1, n))


class _Mod:
    def __init__(self, table3, mod_rows, d, rows_per_mod):
        self.table = table3
        self.mod_rows = mod_rows
        self.d = d
        self.rows_per_mod = rows_per_mod

    def spec(self, layer, k, tm, row_axis=0, width=None, col=None):
        assert self.rows_per_mod % tm == 0
        width = self.d if width is None else width
        rows, mr = self.rows_per_mod, self.mod_rows

        def index_map(*g):
            row = (g[row_axis] * tm) // rows
            return ((layer * mr + row) * N_MOD + k, 0, 0 if col is None else col(*g))
        return pl.BlockSpec((1, 1, width), index_map)


def _modulated(x, nw, shift, scale):
    ms = jnp.mean(x * x, axis=-1, keepdims=True)
    return (x * lax.rsqrt(ms + EPS) * nw) * (1.0 + scale) + shift


def _modulate_kernel(x_ref, nw_ref, sh_ref, sc_ref, o_ref):
    o_ref[...] = _modulated(x_ref[...], nw_ref[...], sh_ref[0], sc_ref[0]).astype(o_ref.dtype)


def _modulate(x, m, nw, mod, layer, k_shift, k_scale):
    d = x.shape[1]
    tm = _tile(math.gcd(m, mod.rows_per_mod), 512)
    return pl.pallas_call(
        _modulate_kernel,
        grid=(m // tm,),
        in_specs=[
            pl.BlockSpec((tm, d), lambda i: (i, 0)),
            pl.BlockSpec((1, d), lambda i: (0, 0)),
            mod.spec(layer, k_shift, tm),
            mod.spec(layer, k_scale, tm),
        ],
        out_specs=pl.BlockSpec((tm, d), lambda i: (i, 0)),
        out_shape=jax.ShapeDtypeStruct((m, d), BF16),
        compiler_params=_cparams("parallel"),
        name="modulate",
    )(x, nw.reshape(1, d), mod.table, mod.table)


def _final_norm_kernel(x_ref, nw_ref, o_ref):
    x = x_ref[...]
    ms = jnp.mean(x * x, axis=-1, keepdims=True)
    o_ref[...] = x * lax.rsqrt(ms + EPS) * nw_ref[...]


def _final_norm(x, m, nw):
    d = x.shape[1]
    tm = _tile(m, 512)
    return pl.pallas_call(
        _final_norm_kernel,
        grid=(m // tm,),
        in_specs=[pl.BlockSpec((tm, d), lambda i: (i, 0)), pl.BlockSpec((1, d), lambda i: (0, 0))],
        out_specs=pl.BlockSpec((tm, d), lambda i: (i, 0)),
        out_shape=jax.ShapeDtypeStruct((m, d), F32),
        compiler_params=_cparams("parallel"),
        name="final_norm",
    )(x, nw.reshape(1, d))


def _matmul_kernel(a_ref, w_ref, o_ref, wb_ref, *, valid_cols):
    @pl.when(pl.program_id(1) == 0)
    def _():
        w = w_ref[0]
        if valid_cols < w.shape[1]:
            w = jnp.where(lax.broadcasted_iota(I32, w.shape, 1) < valid_cols, w, 0.0)
        wb_ref[...] = w.astype(BF16)

    o_ref[...] = _dot(a_ref[...], wb_ref[...]).astype(o_ref.dtype)


def _matmul(a, w, layer, out_dtype, tn_pref, col0=0, n=None):
    m, k = a.shape
    n = w.shape[2] if n is None else n
    tm, tn = _tile(m, 1024), _tile(n, tn_pref, LANES)
    assert col0 % tn == 0
    cb = col0 // tn
    valid_cols = tn if col0 + n <= w.shape[2] else w.shape[2] - col0
    assert valid_cols == tn or n == tn
    return pl.pallas_call(
        functools.partial(_matmul_kernel, valid_cols=valid_cols),
        grid=(n // tn, m // tm),
        in_specs=[pl.BlockSpec((tm, k), lambda j, i: (i, 0)),
                  pl.BlockSpec((1, k, tn), lambda j, i: (layer, 0, cb + j))],
        out_specs=pl.BlockSpec((tm, tn), lambda j, i: (i, j)),
        out_shape=jax.ShapeDtypeStruct((m, n), out_dtype),
        scratch_shapes=[pltpu.VMEM((k, tn), BF16)],
        compiler_params=_cparams("parallel", "arbitrary"),
        name="matmul",
    )(a, w)


def _matmul_res_kernel(a_ref, w_ref, x_ref, g_ref, o_ref, wb_ref):
    @pl.when(pl.program_id(1) == 0)
    def _():
        wb_ref[...] = w_ref[0].astype(BF16)

    o_ref[...] = x_ref[...] + g_ref[0] * _dot(a_ref[...], wb_ref[...])


def _matmul_gated_residual(a, w, layer_w, x, mod, layer, k_gate, tn_pref=1024):
    m, k = a.shape
    n = w.shape[2]
    tm = _tile(math.gcd(m, mod.rows_per_mod), 1024)
    tn = _tile(n, tn_pref, LANES)
    return pl.pallas_call(
        _matmul_res_kernel,
        grid=(n // tn, m // tm),
        in_specs=[
            pl.BlockSpec((tm, k), lambda j, i: (i, 0)),
            pl.BlockSpec((1, k, tn), lambda j, i: (layer_w, 0, j)),
            pl.BlockSpec((tm, tn), lambda j, i: (i, j)),
            mod.spec(layer, k_gate, tm, row_axis=1, width=tn, col=lambda j, i: j),
        ],
        out_specs=pl.BlockSpec((tm, tn), lambda j, i: (i, j)),
        out_shape=jax.ShapeDtypeStruct((m, n), F32),
        scratch_shapes=[pltpu.VMEM((k, tn), BF16)],
        compiler_params=_cparams("parallel", "arbitrary"),
        name="matmul_gated_residual",
    )(a, w, x, mod.table)


def _head_sumsq(y, ones_bd):
    sq_hi, sq_lo = _split2(y * y)
    return _dot(sq_hi, ones_bd) + _dot(sq_lo, ones_bd)


def _dn_conv_kernel(u_ref, cw_ref, e_ref, o_ref, coef_ref, *, lat_tiles, lat_len, ctx_len, mode, head_dim):
    i = pl.program_id(1)
    tm = u_ref.shape[0]
    pad = CONV_W // 2
    cw = cw_ref[0]

    @pl.when((i == 0) | (i == lat_tiles))
    def _():
        conv_len = jnp.where(i < lat_tiles, lat_len, ctx_len)
        pos = lax.broadcasted_iota(I32, (tm, 1), 0) & (conv_len - 1)
        for j in range(CONV_W):
            d = j - pad
            inside = ((pos + d >= 0) & (pos + d < conv_len)).astype(F32)
            coef_ref[j] = inside * cw[j:j + 1, :]

    u = u_ref[...].astype(F32)
    acc = u * cw[pad:pad + 1, :]
    for j in range(CONV_W):
        d = j - pad
        if d != 0:
            acc = acc + pltpu.roll(u, (-d) % tm, 0) * coef_ref[j]
    y = _silu(acc)
    if mode != "v":
        y = y * lax.rsqrt(_head_sumsq(y, e_ref[...]) + EPS)
        if mode == "q":
            y = y * (head_dim ** -0.5)
    o_ref[...] = y.astype(o_ref.dtype)


def _dn_conv(p, conv_w, layer, section, mode, m_lat, ctx_len, width, head_dim):
    m = p.shape[0]
    tc = _tile(width, 512, LANES)
    tm = max(GRID_W, ctx_len, 256)
    assert m_lat % tm == 0 and m % tm == 0 and tm % GRID_W == 0 and tm % ctx_len == 0
    ones_bd = jnp.asarray(np.kron(np.eye(tc // head_dim), np.ones((head_dim, head_dim))), BF16)
    nsec = width // tc
    return pl.pallas_call(
        functools.partial(_dn_conv_kernel, lat_tiles=m_lat // tm, lat_len=GRID_W, ctx_len=ctx_len, mode=mode,
                          head_dim=head_dim),
        grid=(nsec, m // tm),
        in_specs=[
            pl.BlockSpec((tm, tc), lambda j, i: (i, section * nsec + j)),
            pl.BlockSpec((1, CONV_W, tc), lambda j, i: (layer, 0, section * nsec + j)),
            pl.BlockSpec((tc, tc), lambda j, i: (0, 0)),
        ],
        out_specs=pl.BlockSpec((tm, tc), lambda j, i: (i, j)),
        out_shape=jax.ShapeDtypeStruct((m, width), BF16),
        scratch_shapes=[pltpu.VMEM((CONV_W, tm, tc), F32)],
        compiler_params=_cparams("parallel", "arbitrary"),
        name="dn_conv_" + mode,
    )(p, conv_w, ones_bd)


def _dn_gates_kernel(ab_ref, al_ref, dt_ref, tl_ref, tu_ref, g_ref, beta_ref, *, nh):
    ab = ab_ref[:, :4 * nh]
    a = ab[:, :2 * nh] + dt_ref[...]
    softplus = jnp.maximum(a, 0.0) + jnp.log1p(jnp.exp(-jnp.abs(a)))
    la = -jnp.exp(al_ref[...]) * softplus
    beta_ref[...] = _sigmoid(ab[:, 2 * nh:])
    parts = _split3(la)
    g_pre = sum(_dot(tl_ref[...], p) for p in parts)
    g_suf = sum(_dot(tu_ref[...], p) for p in parts)
    col = lax.broadcasted_iota(I32, la.shape, 1)
    g_ref[...] = jnp.where(col < nh, g_pre, g_suf)


def _dn_gates(ab, a_log, dt_bias, nh):
    m = ab.shape[0]
    tm = _tile(m, 256, CHUNK)
    r = np.arange(tm)
    same = (r[:, None] // CHUNK) == (r[None, :] // CHUNK)
    tri_l = jnp.asarray(same & (r[None, :] <= r[:, None]), BF16)
    tri_u = jnp.asarray(same & (r[None, :] >= r[:, None]), BF16)
    return pl.pallas_call(
        functools.partial(_dn_gates_kernel, nh=nh),
        grid=(m // tm,),
        in_specs=[
            pl.BlockSpec((tm, ab.shape[1]), lambda i: (i, 0)),
            pl.BlockSpec((1, 2 * nh), lambda i: (0, 0)),
            pl.BlockSpec((1, 2 * nh), lambda i: (0, 0)),
            pl.BlockSpec((tm, tm), lambda i: (0, 0)),
            pl.BlockSpec((tm, tm), lambda i: (0, 0)),
        ],
        out_specs=[pl.BlockSpec((tm, 2 * nh), lambda i: (i, 0))] * 2,
        out_shape=[jax.ShapeDtypeStruct((m, 2 * nh), F32)] * 2,
        compiler_params=_cparams("parallel"),
        name="dn_gates",
    )(ab, a_log.reshape(1, 2 * nh), dt_bias.reshape(1, 2 * nh), tri_l, tri_u)


def _unit_tri_inverses(lms):
    shape = lms[0].shape
    ri = lax.broadcasted_iota(I32, shape, 0)
    ci = lax.broadcasted_iota(I32, shape, 1)
    eye = (ri == ci).astype(F32)
    pair = (ri >> 1) == (ci >> 1)
    ts = [eye - jnp.where(pair, lm, 0.0) for lm in lms]
    for lvl in range(1, _log2(shape[0])):
        cross = ((ri >> (lvl + 1)) == (ci >> (lvl + 1))) & ((ri >> lvl) != (ci >> lvl))
        offs = [jnp.where(cross, lm, 0.0).astype(BF16) for lm in lms]
        tbs = [t.astype(BF16) for t in ts]
        ps = [_dot(tb, off).astype(BF16) for tb, off in zip(tbs, offs)]
        ts = [t - _dot(p, tb) for t, p, tb in zip(ts, ps, tbs)]
    return ts


def _dn_scan_kernel(*refs, ns, nh, hd):
    ins, o_refs, s_refs = refs[:6 * ns], refs[6 * ns:7 * ns], refs[7 * ns:]
    q_refs, k_refs, v_refs = ins[0::6], ins[1::6], ins[2::6]

    @pl.when(pl.program_id(1) == 0)
    def _():
        for s_ref in s_refs:
            s_ref[...] = jnp.zeros_like(s_ref)

    n = CHUNK
    ri = lax.broadcasted_iota(I32, (n, n), 0)
    ci = lax.broadcasted_iota(I32, (n, n), 1)
    causal_d = (ci <= ri, ci >= ri)
    strict_d = (ci < ri, ci > ri)
    last_d = (n - 1, 0)
    gc_s = [r[...] for r in ins[3::6]]
    gr_s = [r[0] for r in ins[4::6]]
    b_s = [r[...] for r in ins[5::6]]

    def group(items):
        idx = range(len(items))
        col = [(s % 2) * nh + h for s, h in items]
        dirs = [s % 2 for s, h in items]
        gcol = [gc_s[s][:, col[j]:col[j] + 1] for j, (s, h) in enumerate(items)]
        grow = [gr_s[s][col[j]:col[j] + 1, :] for j, (s, h) in enumerate(items)]
        bcol = [b_s[s][:, col[j]:col[j] + 1] for j, (s, h) in enumerate(items)]
        glast = [gcol[j][last_d[dirs[j]]:last_d[dirs[j]] + 1, :] for j in idx]
        q = [q_refs[s][:, h * hd:(h + 1) * hd] for s, h in items]
        k = [k_refs[s][:, h * hd:(h + 1) * hd] for s, h in items]
        v = [v_refs[s][:, h * hd:(h + 1) * hd] for s, h in items]
        kq_k = [_dot_nt(jnp.concatenate([k[j], q[j]], axis=0), k[j]) for j in idx]
        decay = [jnp.exp(jnp.where(causal_d[dirs[j]], gcol[j] - grow[j], -jnp.inf)) for j in idx]
        lm = [jnp.where(strict_d[dirs[j]], bcol[j] * kq_k[j][:n] * decay[j], 0.0) for j in idx]
        intra = [(kq_k[j][n:] * decay[j]).astype(BF16) for j in idx]
        eg = [jnp.exp(g) for g in gcol]
        rhs = [jnp.concatenate([v[j].astype(F32) * bcol[j], k[j].astype(F32) * (bcol[j] * eg[j])],
                               axis=1).astype(BF16) for j in idx]
        t = _unit_tri_inverses(lm)
        uw = [_dot(t[j].astype(BF16), rhs[j]) for j in idx]
        st = [s_refs[s][h] for s, h in items]
        wq = [jnp.concatenate([uw[j][:, hd:], q[j].astype(F32) * eg[j]], axis=0).astype(BF16) for j in idx]
        ws = [_dot(wq[j], st[j].astype(BF16)) for j in idx]
        vnew = [(uw[j][:, :hd] - ws[j][:n]).astype(BF16) for j in idx]
        kg = [(k[j].astype(F32) * jnp.exp(glast[j] - gcol[j])).astype(BF16) for j in idx]
        for j, (s, h) in enumerate(items):
            o_refs[s][:, h * hd:(h + 1) * hd] = (ws[j][n:] + _dot(intra[j], vnew[j])).astype(o_refs[s].dtype)
        for j, (s, h) in enumerate(items):
            s_refs[s][h] = st[j] * jnp.exp(glast[j]) + _dot_tn(kg[j], vnew[j])

    items = [(s, h) for h in range(nh) for s in range(ns)]
    for i0 in range(0, len(items), SCAN_GROUP):
        group(items[i0:i0 + SCAN_GROUP])


def _dn_scan(q, k, v, g, g_rows, beta, batch, m_lat, nh, hd):
    m = q.shape[0]
    nl = m_lat // batch // CHUNK
    nc = (m - m_lat) // batch // CHUNK
    ctx0 = m_lat // CHUNK

    def specs(backward):
        def block(b, c):
            ctx_blk = ctx0 + b * nc + (nc - 1 - c if backward else c)
            lat_blk = b * nl + (nl - 1 - (c - nc) if backward else c - nc)
            return jnp.where(c < nc, ctx_blk, lat_blk)

        tok = lambda b, c: (block(b, c), 0)
        tok3 = lambda b, c: (block(b, c), 0, 0)
        wide = pl.BlockSpec((CHUNK, nh * hd), tok)
        gate = pl.BlockSpec((CHUNK, 2 * nh), tok)
        return [wide, wide, wide, gate, pl.BlockSpec((1, 2 * nh, CHUNK), tok3), gate], wide

    per_stream = [specs(False), specs(True)]
    return pl.pallas_call(
        functools.partial(_dn_scan_kernel, ns=2, nh=nh, hd=hd),
        grid=(batch, nc + nl),
        in_specs=[spec for ins, _ in per_stream for spec in ins],
        out_specs=[out for _, out in per_stream],
        out_shape=[jax.ShapeDtypeStruct((m, nh * hd), BF16)] * 2,
        scratch_shapes=[pltpu.VMEM((nh, hd, hd), F32)] * 2,
        compiler_params=_cparams("parallel", "arbitrary"),
        name="dn_scan",
    )(*([q, k, v, g, g_rows, beta] * 2))


def _dn_out_kernel(of_ref, ob_ref, gate_ref, nw_ref, e_ref, y_ref, *, hd):
    o = of_ref[...].astype(F32) + ob_ref[...].astype(F32)
    ms = _head_sumsq(o, e_ref[...]) * (1.0 / hd)
    y = o * lax.rsqrt(ms + EPS) * nw_ref[...]
    y_ref[...] = (y * _silu(gate_ref[...].astype(F32))).astype(y_ref.dtype)


def _dn_out(o_f, o_b, p, norm_w, width, hd):
    m = o_f.shape[0]
    tc = _tile(width, 512, LANES)
    tm = _tile(m, 512)
    nsec = width // tc
    ones_bd = jnp.asarray(np.kron(np.eye(tc // hd), np.ones((hd, hd))), BF16)
    nw = jnp.tile(norm_w.astype(F32), tc // hd).reshape(1, tc)
    return pl.pallas_call(
        functools.partial(_dn_out_kernel, hd=hd),
        grid=(m // tm, nsec),
        in_specs=[
            pl.BlockSpec((tm, tc), lambda i, j: (i, j)),
            pl.BlockSpec((tm, tc), lambda i, j: (i, j)),
            pl.BlockSpec((tm, tc), lambda i, j: (i, 3 * nsec + j)),
            pl.BlockSpec((1, tc), lambda i, j: (0, 0)),
            pl.BlockSpec((tc, tc), lambda i, j: (0, 0)),
        ],
        out_specs=pl.BlockSpec((tm, tc), lambda i, j: (i, j)),
        out_shape=jax.ShapeDtypeStruct((m, width), BF16),
        compiler_params=_cparams("parallel", "parallel"),
        name="dn_out",
    )(o_f, o_b, p, nw, ones_bd)


def _fold_kernel(w_ref, t_ref, o_ref):
    w_hi, w_lo = _split2(w_ref[0])
    t_hi, t_lo = _split2(t_ref[0])
    o_ref[0] = (_dot(w_hi, t_hi) + _dot(w_lo, t_hi) + _dot(w_hi, t_lo)).astype(o_ref.dtype)


def _fold_channel_dft(w_in, layer, groups):
    _, d, width = w_in.shape
    c = width // groups
    idx = np.arange(c)
    ang = 2.0 * np.pi * ((idx[:, None] * idx[None, :]) % c) / c
    table = jnp.asarray(np.stack([np.cos(ang), -np.sin(ang)]) / math.sqrt(c), F32)
    return pl.pallas_call(
        _fold_kernel,
        grid=(2, groups),
        in_specs=[pl.BlockSpec((1, d, c), lambda s, g: (layer, 0, g)),
                  pl.BlockSpec((1, c, c), lambda s, g: (s, 0, 0))],
        out_specs=pl.BlockSpec((1, d, c), lambda s, g: (0, 0, s * groups + g)),
        out_shape=jax.ShapeDtypeStruct((1, d, 2 * width), BF16),
        compiler_params=_cparams("parallel", "parallel"),
        name="fold_channel_dft",
    )(w_in, table)


def _dft_kernel(zr_ref, zi_ref, f_ref, g_ref, *rest, n1, n2):
    o_ref, zf, yf, of = rest[-4:]
    slabs = range(of.shape[0])
    lanes = lambda l: slice(l * LANES, (l + 1) * LANES)
    for l in slabs:
        zf[0, l] = zr_ref[:, lanes(l)].astype(F32)
        zf[1, l] = zi_ref[:, lanes(l)].astype(F32)

    def stage1(t2, carry):
        z = jnp.concatenate(
            [jnp.concatenate([zf[part, l, pl.ds(t2, n1, stride=n2), :] for l in slabs], axis=1) for part in (0, 1)],
            axis=0)
        y = _dot(f_ref[...], z.astype(BF16))
        for l in slabs:
            yf[l, pl.ds(pl.multiple_of(t2 * 2 * n1, 2 * n1), 2 * n1), :] = y[:, lanes(l)]
        return carry

    lax.fori_loop(0, n2, stage1, 0, unroll=8)

    def stage2(f1, carry):
        z = jnp.concatenate(
            [jnp.concatenate([yf[l, pl.ds(off + f1, n2, stride=2 * n1), :] for l in slabs], axis=1)
             for off in (0, n1)], axis=0)
        r = _dot(g_ref[f1], z.astype(BF16))
        for l in slabs:
            of[l, pl.ds(f1, n2, stride=n1), :] = r[:, lanes(l)]
        return carry

    lax.fori_loop(0, n1, stage2, 0, unroll=8)
    for l in slabs:
        o_ref[:, lanes(l)] = of[l].astype(o_ref.dtype)


def _position_dft_real(u, row0, batch, t_len, width, into=None):
    n1 = 1 << (_log2(t_len) // 2)
    n2 = t_len // n1
    a1 = 2.0 * np.pi * ((np.arange(n1)[:, None] * np.arange(n1)[None, :]) % n1) / n1
    c1, s1 = np.cos(a1), np.sin(a1)
    f1m = jnp.asarray(np.block([[c1, s1], [-s1, c1]]) / math.sqrt(n1), BF16)
    f1i, f2i, t2i = np.arange(n1)[:, None, None], np.arange(n2)[None, :, None], np.arange(n2)[None, None, :]
    theta = 2.0 * np.pi * (((t2i * f1i) % t_len) / t_len + ((t2i * f2i) % n2) / n2)
    g2m = jnp.asarray(np.concatenate([np.cos(theta), np.sin(theta)], axis=2) / math.sqrt(n2), BF16)

    tc = _tile(width, 256, LANES)
    nj = width // tc
    assert row0 % t_len == 0
    b0 = row0 // t_len
    return pl.pallas_call(
        functools.partial(_dft_kernel, n1=n1, n2=n2),
        grid=(batch, nj),
        in_specs=[
            pl.BlockSpec((t_len, tc), lambda b, j: (b0 + b, j)),
            pl.BlockSpec((t_len, tc), lambda b, j: (b0 + b, nj + j)),
            pl.BlockSpec((2 * n1, 2 * n1), lambda b, j: (0, 0)),
            pl.BlockSpec((n1, n2, 2 * n2), lambda b, j: (0, 0, 0)),
        ] + ([] if into is None else [pl.BlockSpec(memory_space=pl.ANY)]),
        input_output_aliases={} if into is None else {4: 0},
        out_specs=pl.BlockSpec((t_len, tc), lambda b, j: (b0 + b, j)),
        out_shape=jax.ShapeDtypeStruct((u.shape[0], width), BF16),
        scratch_shapes=[pltpu.VMEM((2, tc // LANES, t_len, LANES), F32),
                        pltpu.VMEM((tc // LANES, n2 * 2 * n1, LANES), F32),
                        pltpu.VMEM((tc // LANES, t_len, LANES), F32)],
        compiler_params=_cparams("parallel", "parallel"),
        name="position_dft",
    )(u, u, f1m, g2m, *([] if into is None else [into]))


def _ffn_prep_kernel(x_ref, nw_ref, sh_ref, sc_ref, rw_ref, rb_ref, h_ref, comb_ref, *, n_exp):
    h = _modulated(x_ref[...], nw_ref[...], sh_ref[0], sc_ref[0])
    h_ref[...] = h.astype(h_ref.dtype)
    h_hi, h_lo = _split2(h)
    r_hi, r_lo = _split2(rw_ref[...])
    logits = _dot_nt(r_hi, h_hi) + _dot_nt(r_lo, h_hi) + _dot_nt(r_hi, h_lo)
    s = _sigmoid(logits)
    sel = s + rb_ref[...]
    srow = [s[e:e + 1, :] for e in range(n_exp)]
    row = [sel[e:e + 1, :] for e in range(n_exp)]
    epg = n_exp // N_GROUPS

    def beats(a, ia, b, ib):
        return (a >= b) if ia < ib else (a > b)

    rank, gscore = {}, []
    for g in range(N_GROUPS):
        members = range(g * epg, (g + 1) * epg)
        for i in members:
            rank[i] = sum(beats(row[j], j, row[i], i).astype(F32) for j in members if j != i)
        gscore.append(sum(jnp.where(rank[i] < GROUP_SCORE_K, row[i], 0.0) for i in members))
    picked = []
    for g in range(N_GROUPS):
        grank = sum(beats(gscore[j], j, gscore[g], g).astype(F32) for j in range(N_GROUPS) if j != g)
        for i in range(g * epg, (g + 1) * epg):
            picked.append(jnp.where((grank < 1.0) & (rank[i] < TOP_K), srow[i], 0.0))
    denom = sum(picked)
    comb_ref[...] = jnp.concatenate([p / denom for p in picked], axis=0)


def _ffn_prep(x, m, nw, mod, layer, router_wt, router_b):
    d = x.shape[1]
    n_exp = router_wt.shape[0]
    tm = _tile(math.gcd(m, mod.rows_per_mod), 512, LANES)
    return pl.pallas_call(
        functools.partial(_ffn_prep_kernel, n_exp=n_exp),
        grid=(m // tm,),
        in_specs=[
            pl.BlockSpec((tm, d), lambda i: (i, 0)),
            pl.BlockSpec((1, d), lambda i: (0, 0)),
            mod.spec(layer, 3, tm),
            mod.spec(layer, 4, tm),
            pl.BlockSpec((n_exp, d), lambda i: (0, 0)),
            pl.BlockSpec((n_exp, 1), lambda i: (0, 0)),
        ],
        out_specs=[pl.BlockSpec((tm, d), lambda i: (i, 0)), pl.BlockSpec((n_exp, tm), lambda i: (0, i))],
        out_shape=[jax.ShapeDtypeStruct((m, d), F32), jax.ShapeDtypeStruct((n_exp, m), F32)],
        compiler_params=_cparams("parallel"),
        name="ffn_prep",
    )(x, nw.reshape(1, d), mod.table, mod.table, router_wt, router_b.reshape(n_exp, 1))


def _route(comb_t, tg):
    assert TOP_K == 2
    n_exp, n = comb_t.shape
    mask = comb_t > 0.0
    cnt = jnp.sum(mask.astype(I32), axis=1)
    padded = ((cnt + tg - 1) // tg) * tg
    ends = jnp.cumsum(padded)
    dest = (ends - padded)[:, None] + jnp.cumsum(mask.astype(I32), axis=1) - 1
    eidx = jnp.arange(n_exp, dtype=I32)[:, None]
    first = jnp.min(jnp.where(mask, eidx, n_exp), axis=0)
    final = jnp.max(jnp.where(mask, eidx, -1), axis=0)
    is_a = mask & (eidx == first)
    is_b = mask & (eidx == final) & (final != first)
    pick = lambda sel, val: jnp.sum(jnp.where(sel, val, 0), axis=0)
    w_ab = jnp.stack([pick(is_a, comb_t), pick(is_b, comb_t)], axis=1)
    p_rows = TOP_K * n + n_exp * tg
    d_a = jnp.where(jnp.any(is_a, axis=0), pick(is_a, dest), p_rows).astype(I32)
    d_b = jnp.where(jnp.any(is_b, axis=0), pick(is_b, dest), p_rows).astype(I32)
    tile_start = jnp.arange(p_rows // tg, dtype=I32) * tg
    tile_exp = jnp.minimum(jnp.sum((tile_start[:, None] >= ends[None, :]).astype(I32), axis=1), n_exp - 1)
    tile_ok = (tile_start < ends[-1]).astype(I32)
    tile_new = jnp.concatenate([jnp.ones((1,), I32), (tile_exp[1:] != tile_exp[:-1]).astype(I32)])
    return tile_exp, tile_ok, tile_new, d_a, d_b, w_ab, p_rows


def _src_kernel(da_ref, db_ref, src_ref, *, n_tok):
    def zero(r, carry):
        src_ref[r] = 0
        return carry
    lax.fori_loop(0, src_ref.shape[0], zero, 0, unroll=8)

    def put(t, carry):
        src_ref[da_ref[t]] = t
        src_ref[db_ref[t]] = t
        return carry
    lax.fori_loop(0, n_tok, put, 0, unroll=8)


def _sorted_row_tokens(d_a, d_b, p_rows, spare):
    n_tok = d_a.shape[0]
    smem = pl.BlockSpec(memory_space=pltpu.SMEM)
    return pl.pallas_call(
        functools.partial(_src_kernel, n_tok=n_tok),
        in_specs=[smem, smem],
        out_specs=smem,
        out_shape=jax.ShapeDtypeStruct((p_rows + spare,), I32),
        name="sorted_row_tokens",
    )(d_a, d_b)


def _row_copies(idx_ref, base, n_rows, src_hbm, dst_buf, sem, wait):
    def body(r, carry):
        cp = pltpu.make_async_copy(src_hbm.at[pl.ds(idx_ref[base + r], 1)], dst_buf.at[pl.ds(r, 1)], sem)
        if wait:
            cp.wait()
        else:
            cp.start()
        return carry
    lax.fori_loop(0, n_rows, body, 0, unroll=8)


def _moe_expert_kernel(texp_ref, tval_ref, tnew_ref, src_ref, h_hbm, wg_ref, wu_ref, wd_ref, y_ref,
                       xbuf, sem, wgb, wub, wdb, *, tg, n_tiles):
    i = pl.program_id(0)
    nbuf = MOE_GATHER_AHEAD + 1
    bufs = [xbuf.at[s] for s in range(nbuf)]
    valid = tval_ref[i] != 0

    def requested(tile):
        return (tile < MOE_GATHER_AHEAD) | (tval_ref[jnp.maximum(tile - MOE_GATHER_AHEAD, 0)] != 0)

    @pl.when(i == 0)
    def _():
        for t in range(MOE_GATHER_AHEAD):
            _row_copies(src_ref, t * tg, tg, h_hbm, bufs[t], sem.at[t], wait=False)

    @pl.when(valid & (tnew_ref[i] != 0))
    def _():
        wgb[...] = wg_ref[0, 0].astype(BF16)
        wub[...] = wu_ref[0, 0].astype(BF16)
        wdb[...] = wd_ref[0, 0].astype(BF16)

    for slot in range(nbuf):
        @pl.when(requested(i) & (i % nbuf == slot))
        def _():
            _row_copies(src_ref, i * tg, tg, h_hbm, bufs[slot], sem.at[slot], wait=True)

        @pl.when(valid & (i % nbuf == slot))
        def _():
            ahead = (slot + MOE_GATHER_AHEAD) % nbuf

            def request(rows):
                for r in rows:
                    pltpu.make_async_copy(h_hbm.at[pl.ds(src_ref[(i + MOE_GATHER_AHEAD) * tg + r], 1)],
                                          bufs[ahead].at[pl.ds(r, 1)], sem.at[ahead]).start(priority=r % 2)

            third = tg // 3
            x = bufs[slot][...].astype(BF16)
            request(range(0, third))
            a = _dot(x, wgb[...])
            request(range(third, 2 * third))
            b = _dot(x, wub[...])
            request(range(2 * third, tg))
            act = (_silu(a) * b).astype(BF16)
            y_ref[...] = _dot(act, wdb[...])

    @pl.when(i == n_tiles - 1)
    def _():
        for t in range(n_tiles, n_tiles + MOE_GATHER_AHEAD):
            @pl.when(tval_ref[t - MOE_GATHER_AHEAD] != 0)
            def _():
                _row_copies(src_ref, t * tg, tg, h_hbm, bufs[t % nbuf], sem.at[t % nbuf], wait=True)

    @pl.when(jnp.logical_not(valid))
    def _():
        y_ref[...] = jnp.zeros_like(y_ref)


def _moe_experts(h, src, tile_exp, tile_ok, tile_new, wg, wu, wd, layer, tg, p_rows):
    n, d = h.shape
    f = wg.shape[3]
    wmap = lambda i, te, ok, new, s: (layer, te[i], 0, 0)
    return pl.pallas_call(
        functools.partial(_moe_expert_kernel, tg=tg, n_tiles=p_rows // tg),
        grid_spec=pltpu.PrefetchScalarGridSpec(
            num_scalar_prefetch=4,
            grid=(p_rows // tg,),
            in_specs=[
                pl.BlockSpec(memory_space=pl.ANY),
                pl.BlockSpec((1, 1, d, f), wmap),
                pl.BlockSpec((1, 1, d, f), wmap),
                pl.BlockSpec((1, 1, f, d), wmap),
            ],
            out_specs=pl.BlockSpec((tg, d), lambda i, te, ok, new, s: (i, 0)),
            scratch_shapes=[pltpu.VMEM((MOE_GATHER_AHEAD + 1, tg, d), F32),
                            pltpu.SemaphoreType.DMA((MOE_GATHER_AHEAD + 1,)),
                            pltpu.VMEM((d, f), BF16), pltpu.VMEM((d, f), BF16), pltpu.VMEM((f, d), BF16)],
        ),
        out_shape=jax.ShapeDtypeStruct((p_rows, d), F32),
        compiler_params=_cparams("arbitrary"),
        name="moe_experts",
    )(tile_exp, tile_ok, tile_new, src, h, wg, wu, wd)


def _moe_combine_kernel(da_ref, db_ref, y_hbm, x_ref, w_ref, g_ref, o_ref, ya, yb, sem, *, tm):
    i = pl.program_id(0)

    def fetch(tile, slot, wait):
        _row_copies(da_ref, tile * tm, tm, y_hbm, ya.at[slot], sem.at[0, slot], wait)
        _row_copies(db_ref, tile * tm, tm, y_hbm, yb.at[slot], sem.at[1, slot], wait)

    @pl.when(i == 0)
    def _():
        fetch(0, 0, False)

    @pl.when(i + 1 < pl.num_programs(0))
    def _():
        slot = (i + 1) % 2
        for r in range(tm):
            for k, (idx_ref, buf) in enumerate(((da_ref, ya), (db_ref, yb))):
                pltpu.make_async_copy(y_hbm.at[pl.ds(idx_ref[(i + 1) * tm + r], 1)],
                                      buf.at[slot, pl.ds(r, 1)], sem.at[k, slot]).start(priority=k)

    fetch(i, i % 2, True)
    w = w_ref[...]
    mix = w[:, 0:1] * ya[i % 2] + w[:, 1:2] * yb[i % 2]
    o_ref[...] = x_ref[...] + g_ref[0] * mix


def _moe_combine(y, row_a, row_b, w_ab, x, m, mod, layer):
    d = x.shape[1]
    tm = _tile(math.gcd(m, mod.rows_per_mod), 256)
    return pl.pallas_call(
        functools.partial(_moe_combine_kernel, tm=tm),
        grid_spec=pltpu.PrefetchScalarGridSpec(
            num_scalar_prefetch=2,
            grid=(m // tm,),
            in_specs=[
                pl.BlockSpec(memory_space=pl.ANY),
                pl.BlockSpec((tm, d), lambda i, a, b: (i, 0)),
                pl.BlockSpec((tm, 2), lambda i, a, b: (i, 0)),
                mod.spec(layer, 5, tm),
            ],
            out_specs=pl.BlockSpec((tm, d), lambda i, a, b: (i, 0)),
            scratch_shapes=[pltpu.VMEM((2, tm, d), F32), pltpu.VMEM((2, tm, d), F32),
                            pltpu.SemaphoreType.DMA((2, 2))],
        ),
        out_shape=jax.ShapeDtypeStruct((m, d), F32),
        compiler_params=_cparams("arbitrary"),
        name="moe_combine",
    )(row_a, row_b, y, x, w_ab, mod.table)


def _moe(h, comb_t, wg, wu, wd, layer_w, x, m, mod, layer):
    tg = 256 if m * TOP_K >= 256 * wg.shape[1] else 64
    tile_exp, tile_ok, tile_new, d_a, d_b, w_ab, p_rows = _route(comb_t, tg)
    src = _sorted_row_tokens(d_a, d_b, p_rows, MOE_GATHER_AHEAD * tg)
    y = _moe_experts(h, src, tile_exp, tile_ok, tile_new, wg, wu, wd, layer_w, tg, p_rows)
    none = lambda rows: jnp.where(rows == p_rows, 0, rows)
    return _moe_combine(y, none(d_a), none(d_b), w_ab, x, m, mod, layer)


def kernel(x, c, ctx, c_ctx, ada_w, ada_b, norm_mix_w, norm_ffn_w, dn_w_in, dn_conv_w, dn_A_log, dn_dt_bias, dn_norm_w, dn_w_out, fn_w_in, fn_w_out, router_w, router_bias, moe_w_gate, moe_w_up, moe_w_down, final_norm_w):
    batch, seq, d = x.shape
    ctx_len = ctx.shape[1]
    depth = ada_w.shape[0]
    nh = DN_HEADS
    dn_width = dn_w_out.shape[1]
    hd = dn_width // nh
    fn_width = fn_w_in.shape[2]
    m_lat, m_all = batch * seq, batch * (seq + ctx_len)
    assert batch * ctx_len <= seq and seq % ctx_len == 0

    mod_rows = -(-(batch + 1) // 8) * 8
    cond = jnp.zeros((mod_rows, d), F32).at[:batch].set(c).at[batch].set(c_ctx)
    table = _ada_table(cond, ada_w, ada_b)
    mod = _Mod(table.reshape(depth * mod_rows * N_MOD, 1, d), mod_rows, d, seq)

    router_wt = router_w.T
    xs = jnp.concatenate([x.reshape(m_lat, d), ctx.reshape(batch * ctx_len, d)], axis=0)

    for i in range(depth):
        use_dn = i % N_MIXERS == 0
        j = i // N_MIXERS
        ctx_out = i < depth - 1
        ctx_live = use_dn or ctx_out
        m = m_all if ctx_live else m_lat

        h = _modulate(xs, m, norm_mix_w[i], mod, i, 0, 1)
        if use_dn:
            p = _matmul(h, dn_w_in, j, BF16, 1024, n=4 * dn_width)
            ab = _matmul(h, dn_w_in, j, F32, LANES, col0=4 * dn_width, n=LANES)
            q = _dn_conv(p, dn_conv_w, j, 0, "q", m_lat, ctx_len, dn_width, hd)
            k = _dn_conv(p, dn_conv_w, j, 1, "k", m_lat, ctx_len, dn_width, hd)
            v = _dn_conv(p, dn_conv_w, j, 2, "v", m_lat, ctx_len, dn_width, hd)
            g, beta = _dn_gates(ab, dn_A_log[j].reshape(-1), dn_dt_bias[j].reshape(-1), nh)
            g_rows = g.reshape(m // CHUNK, CHUNK, 2 * nh).transpose(0, 2, 1)
            o_f, o_b = _dn_scan(q, k, v, g, g_rows, beta, batch, m_lat, nh, hd)
            y = _dn_out(o_f, o_b, p, dn_norm_w[j], dn_width, hd)
            xs = _matmul_gated_residual(y, dn_w_out, j, xs, mod, i, 2)
        else:
            w_fold = _fold_channel_dft(fn_w_in, j, FN_GROUPS)
            u = _matmul(h, w_fold, 0, BF16, 1024)
            mixed = _position_dft_real(u, 0, batch, seq, fn_width)
            if ctx_out:
                mixed = _position_dft_real(u, m_lat, batch, ctx_len, fn_width, into=mixed)
            xs = _matmul_gated_residual(mixed, fn_w_out, j, xs, mod, i, 2)

        hf, comb_t = _ffn_prep(xs, m, norm_ffn_w[i], mod, i, router_wt, router_bias)
        xs = _moe(hf, comb_t, moe_w_gate, moe_w_up, moe_w_down, i, xs, m, mod, i)

    return _final_norm(xs, m_lat, final_norm_w).reshape(batch, seq, d)
```

```python
import functools
import math

import numpy as np
import jax
import jax.numpy as jnp
from jax import lax
from jax.experimental import pallas as pl
from jax.experimental.pallas import tpu as pltpu

F32 = jnp.float32
BF16 = jnp.bfloat16
I32 = jnp.int32

GRID_W = 64
CHUNK = 64
CONV_W = 5
DN_HEADS = 16
FN_GROUPS = 4
N_GROUPS = 4
GROUP_SCORE_K = 2
TOP_K = 2
N_MOD = 6
N_MIXERS = 2
EPS = 1e-6

VMEM_LIMIT = 56 * 1024 * 1024
LANES = 128
MOE_GATHER_AHEAD = 3
SCAN_GROUP = 32


def _cparams(*sem):
    return pltpu.CompilerParams(dimension_semantics=sem, vmem_limit_bytes=VMEM_LIMIT)


def _dot(a, b):
    return jnp.dot(a, b, preferred_element_type=F32)


def _dot_nt(a, b):
    return lax.dot_general(a, b, (((1,), (1,)), ((), ())), preferred_element_type=F32)


def _dot_tn(a, b):
    return lax.dot_general(a, b, (((0,), (0,)), ((), ())), preferred_element_type=F32)


def _split2(x):
    hi = x.astype(BF16)
    lo = (x - hi.astype(F32)).astype(BF16)
    return hi, lo


def _split3(x):
    hi = x.astype(BF16)
    r = x - hi.astype(F32)
    mid = r.astype(BF16)
    lo = (r - mid.astype(F32)).astype(BF16)
    return hi, mid, lo


def _silu(x):
    return x / (1.0 + jnp.exp(-x))


def _sigmoid(x):
    return 1.0 / (1.0 + jnp.exp(-x))


def _tile(n, pref, mult=8):
    if n <= pref:
        return n
    t = (pref // mult) * mult
    while t >= mult:
        if n % t == 0:
            return t
        t -= mult
    return n


def _log2(n):
    assert n & (n - 1) == 0, n
    return n.bit_length() - 1


def _ada_kernel(c_ref, w_ref, b_ref, o_ref):
    s_hi, s_lo = _split2(_silu(c_ref[...]))
    w_hi, w_lo = _split2(w_ref[0])
    acc = _dot(s_hi, w_hi) + _dot(s_lo, w_hi) + _dot(s_hi, w_lo)
    o_ref[0] = acc + b_ref[0]


def _ada_table(cond, ada_w, ada_b):
    depth, d, n = ada_w.shape
    r = cond.shape[0]
    tn = _tile(n, 512, 128)
    return pl.pallas_call(
        _ada_kernel,
        grid=(depth, n // tn),
        in_specs=[
            pl.BlockSpec((r, d), lambda l, j: (0, 0)),
            pl.BlockSpec((1, d, tn), lambda l, j: (l, 0, j)),
            pl.BlockSpec((1, 1, tn), lambda l, j: (l, 0, j)),
        ],
        out_specs=pl.BlockSpec((1, r, tn), lambda l, j: (l, 0, j)),
        out_shape=jax.ShapeDtypeStruct((depth, r, n), F32),
        compiler_params=_cparams("parallel", "parallel"),
        name="ada_table",
    )(cond, ada_w, ada_b.reshape(depth, 1, n))


class _Mod:
    def __init__(self, table3, mod_rows, d, rows_per_mod):
        self.table = table3
        self.mod_rows = mod_rows
        self.d = d
        self.rows_per_mod = rows_per_mod

    def spec(self, layer, k, tm, row_axis=0, width=None, col=None):
        assert self.rows_per_mod % tm == 0
        width = self.d if width is None else width
        rows, mr = self.rows_per_mod, self.mod_rows

        def index_map(*g):
            row = (g[row_axis] * tm) // rows
            return ((layer * mr + row) * N_MOD + k, 0, 0 if col is None else col(*g))
        return pl.BlockSpec((1, 1, width), index_map)


def _modulated(x, nw, shift, scale):
    ms = jnp.mean(x * x, axis=-1, keepdims=True)
    return (x * lax.rsqrt(ms + EPS) * nw) * (1.0 + scale) + shift


def _modulate_kernel(x_ref, nw_ref, sh_ref, sc_ref, o_ref):
    o_ref[...] = _modulated(x_ref[...], nw_ref[...], sh_ref[0], sc_ref[0]).astype(o_ref.dtype)


def _modulate(x, m, nw, mod, layer, k_shift, k_scale):
    d = x.shape[1]
    tm = _tile(math.gcd(m, mod.rows_per_mod), 512)
    return pl.pallas_call(
        _modulate_kernel,
        grid=(m // tm,),
        in_specs=[
            pl.BlockSpec((tm, d), lambda i: (i, 0)),
            pl.BlockSpec((1, d), lambda i: (0, 0)),
            mod.spec(layer, k_shift, tm),
            mod.spec(layer, k_scale, tm),
        ],
        out_specs=pl.BlockSpec((tm, d), lambda i: (i, 0)),
        out_shape=jax.ShapeDtypeStruct((m, d), BF16),
        compiler_params=_cparams("parallel"),
        name="modulate",
    )(x, nw.reshape(1, d), mod.table, mod.table)


def _final_norm_kernel(x_ref, nw_ref, o_ref):
    x = x_ref[...]
    ms = jnp.mean(x * x, axis=-1, keepdims=True)
    o_ref[...] = x * lax.rsqrt(ms + EPS) * nw_ref[...]


def _final_norm(x, m, nw):
    d = x.shape[1]
    tm = _tile(m, 512)
    return pl.pallas_call(
        _final_norm_kernel,
        grid=(m // tm,),
        in_specs=[pl.BlockSpec((tm, d), lambda i: (i, 0)), pl.BlockSpec((1, d), lambda i: (0, 0))],
        out_specs=pl.BlockSpec((tm, d), lambda i: (i, 0)),
        out_shape=jax.ShapeDtypeStruct((m, d), F32),
        compiler_params=_cparams("parallel"),
        name="final_norm",
    )(x, nw.reshape(1, d))


def _matmul_kernel(a_ref, w_ref, o_ref, wb_ref, *, valid_cols):
    @pl.when(pl.program_id(1) == 0)
    def _():
        w = w_ref[0]
        if valid_cols < w.shape[1]:
            w = jnp.where(lax.broadcasted_iota(I32, w.shape, 1) < valid_cols, w, 0.0)
        wb_ref[...] = w.astype(BF16)

    o_ref[...] = _dot(a_ref[...], wb_ref[...]).astype(o_ref.dtype)


def _matmul(a, w, layer, out_dtype, tn_pref, col0=0, n=None):
    m, k = a.shape
    n = w.shape[2] if n is None else n
    tm, tn = _tile(m, 1024), _tile(n, tn_pref, LANES)
    assert col0 % tn == 0
    cb = col0 // tn
    valid_cols = tn if col0 + n <= w.shape[2] else w.shape[2] - col0
    assert valid_cols == tn or n == tn
    return pl.pallas_call(
        functools.partial(_matmul_kernel, valid_cols=valid_cols),
        grid=(n // tn, m // tm),
        in_specs=[pl.BlockSpec((tm, k), lambda j, i: (i, 0)),
                  pl.BlockSpec((1, k, tn), lambda j, i: (layer, 0, cb + j))],
        out_specs=pl.BlockSpec((tm, tn), lambda j, i: (i, j)),
        out_shape=jax.ShapeDtypeStruct((m, n), out_dtype),
        scratch_shapes=[pltpu.VMEM((k, tn), BF16)],
        compiler_params=_cparams("parallel", "arbitrary"),
        name="matmul",
    )(a, w)


def _matmul_res_kernel(a_ref, w_ref, x_ref, g_ref, o_ref, wb_ref):
    @pl.when(pl.program_id(1) == 0)
    def _():
        wb_ref[...] = w_ref[0].astype(BF16)

    o_ref[...] = x_ref[...] + g_ref[0] * _dot(a_ref[...], wb_ref[...])


def _matmul_gated_residual(a, w, layer_w, x, mod, layer, k_gate, tn_pref=1024):
    m, k = a.shape
    n = w.shape[2]
    tm = _tile(math.gcd(m, mod.rows_per_mod), 1024)
    tn = _tile(n, tn_pref, LANES)
    return pl.pallas_call(
        _matmul_res_kernel,
        grid=(n // tn, m // tm),
        in_specs=[
            pl.BlockSpec((tm, k), lambda j, i: (i, 0)),
            pl.BlockSpec((1, k, tn), lambda j, i: (layer_w, 0, j)),
            pl.BlockSpec((tm, tn), lambda j, i: (i, j)),
            mod.spec(layer, k_gate, tm, row_axis=1, width=tn, col=lambda j, i: j),
        ],
        out_specs=pl.BlockSpec((tm, tn), lambda j, i: (i, j)),
        out_shape=jax.ShapeDtypeStruct((m, n), F32),
        scratch_shapes=[pltpu.VMEM((k, tn), BF16)],
        compiler_params=_cparams("parallel", "arbitrary"),
        name="matmul_gated_residual",
    )(a, w, x, mod.table)


def _head_sumsq(y, ones_bd):
    sq_hi, sq_lo = _split2(y * y)
    return _dot(sq_hi, ones_bd) + _dot(sq_lo, ones_bd)


def _dn_conv_kernel(u_ref, cw_ref, e_ref, o_ref, coef_ref, *, lat_tiles, lat_len, ctx_len, mode, head_dim):
    i = pl.program_id(1)
    tm = u_ref.shape[0]
    pad = CONV_W // 2
    cw = cw_ref[0]

    @pl.when((i == 0) | (i == lat_tiles))
    def _():
        conv_len = jnp.where(i < lat_tiles, lat_len, ctx_len)
        pos = lax.broadcasted_iota(I32, (tm, 1), 0) & (conv_len - 1)
        for j in range(CONV_W):
            d = j - pad
            inside = ((pos + d >= 0) & (pos + d < conv_len)).astype(F32)
            coef_ref[j] = inside * cw[j:j + 1, :]

    u = u_ref[...].astype(F32)
    acc = u * cw[pad:pad + 1, :]
    for j in range(CONV_W):
        d = j - pad
        if d != 0:
            acc = acc + pltpu.roll(u, (-d) % tm, 0) * coef_ref[j]
    y = _silu(acc)
    if mode != "v":
        y = y * lax.rsqrt(_head_sumsq(y, e_ref[...]) + EPS)
        if mode == "q":
            y = y * (head_dim ** -0.5)
    o_ref[...] = y.astype(o_ref.dtype)


def _dn_conv(p, conv_w, layer, section, mode, m_lat, ctx_len, width, head_dim):
    m = p.shape[0]
    tc = _tile(width, 512, LANES)
    tm = _tile(math.gcd(m_lat, m - m_lat), 1024, max(GRID_W, ctx_len))
    assert m_lat % tm == 0 and m % tm == 0 and tm % GRID_W == 0 and tm % ctx_len == 0
    ones_bd = jnp.asarray(np.kron(np.eye(tc // head_dim), np.ones((head_dim, head_dim))), BF16)
    nsec = width // tc
    return pl.pallas_call(
        functools.partial(_dn_conv_kernel, lat_tiles=m_lat // tm, lat_len=GRID_W, ctx_len=ctx_len, mode=mode,
                          head_dim=head_dim),
        grid=(nsec, m // tm),
        in_specs=[
            pl.BlockSpec((tm, tc), lambda j, i: (i, section * nsec + j)),
            pl.BlockSpec((1, CONV_W, tc), lambda j, i: (layer, 0, section * nsec + j)),
            pl.BlockSpec((tc, tc), lambda j, i: (0, 0)),
        ],
        out_specs=pl.BlockSpec((tm, tc), lambda j, i: (i, j)),
        out_shape=jax.ShapeDtypeStruct((m, width), BF16),
        scratch_shapes=[pltpu.VMEM((CONV_W, tm, tc), F32)],
        compiler_params=_cparams("parallel", "arbitrary"),
        name="dn_conv_" + mode,
    )(p, conv_w, ones_bd)


def _dn_gates_kernel(ab_ref, al_ref, dt_ref, tl_ref, tu_ref, g_ref, beta_ref, *, nh):
    ab = ab_ref[:, :4 * nh]
    a = ab[:, :2 * nh] + dt_ref[...]
    softplus = jnp.maximum(a, 0.0) + jnp.log1p(jnp.exp(-jnp.abs(a)))
    la = -jnp.exp(al_ref[...]) * softplus
    beta_ref[...] = _sigmoid(ab[:, 2 * nh:])
    parts = _split3(la)
    g_pre = sum(_dot(tl_ref[...], p) for p in parts)
    g_suf = sum(_dot(tu_ref[...], p) for p in parts)
    col = lax.broadcasted_iota(I32, la.shape, 1)
    g_ref[...] = jnp.where(col < nh, g_pre, g_suf)


def _dn_gates(ab, a_log, dt_bias, nh):
    m = ab.shape[0]
    tm = _tile(m, 256, CHUNK)
    r = np.arange(tm)
    same = (r[:, None] // CHUNK) == (r[None, :] // CHUNK)
    tri_l = jnp.asarray(same & (r[None, :] <= r[:, None]), BF16)
    tri_u = jnp.asarray(same & (r[None, :] >= r[:, None]), BF16)
    return pl.pallas_call(
        functools.partial(_dn_gates_kernel, nh=nh),
        grid=(m // tm,),
        in_specs=[
            pl.BlockSpec((tm, ab.shape[1]), lambda i: (i, 0)),
            pl.BlockSpec((1, 2 * nh), lambda i: (0, 0)),
            pl.BlockSpec((1, 2 * nh), lambda i: (0, 0)),
            pl.BlockSpec((tm, tm), lambda i: (0, 0)),
            pl.BlockSpec((tm, tm), lambda i: (0, 0)),
        ],
        out_specs=[pl.BlockSpec((tm, 2 * nh), lambda i: (i, 0))] * 2,
        out_shape=[jax.ShapeDtypeStruct((m, 2 * nh), F32)] * 2,
        compiler_params=_cparams("parallel"),
        name="dn_gates",
    )(ab, a_log.reshape(1, 2 * nh), dt_bias.reshape(1, 2 * nh), tri_l, tri_u)


def _unit_tri_inverses(lms):
    shape = lms[0].shape
    ri = lax.broadcasted_iota(I32, shape, 0)
    ci = lax.broadcasted_iota(I32, shape, 1)
    eye = (ri == ci).astype(F32)
    pair = (ri >> 1) == (ci >> 1)
    ts = [eye - jnp.where(pair, lm, 0.0) for lm in lms]
    for lvl in range(1, _log2(shape[0])):
        cross = ((ri >> (lvl + 1)) == (ci >> (lvl + 1))) & ((ri >> lvl) != (ci >> lvl))
        offs = [jnp.where(cross, lm, 0.0).astype(BF16) for lm in lms]
        tbs = [t.astype(BF16) for t in ts]
        ps = [_dot(tb, off).astype(BF16) for tb, off in zip(tbs, offs)]
        ts = [t - _dot(p, tb) for t, p, tb in zip(ts, ps, tbs)]
    return ts


def _dn_scan_kernel(*refs, ns, nh, hd):
    ins, o_refs, s_refs = refs[:6 * ns], refs[6 * ns:7 * ns], refs[7 * ns:]
    q_refs, k_refs, v_refs = ins[0::6], ins[1::6], ins[2::6]

    @pl.when(pl.program_id(1) == 0)
    def _():
        for s_ref in s_refs:
            s_ref[...] = jnp.zeros_like(s_ref)

    n = CHUNK
    ri = lax.broadcasted_iota(I32, (n, n), 0)
    ci = lax.broadcasted_iota(I32, (n, n), 1)
    causal_d = (ci <= ri, ci >= ri)
    strict_d = (ci < ri, ci > ri)
    last_d = (n - 1, 0)
    gc_s = [r[...] for r in ins[3::6]]
    gr_s = [r[0] for r in ins[4::6]]
    b_s = [r[...] for r in ins[5::6]]

    def group(items):
        idx = range(len(items))
        col = [(s % 2) * nh + h for s, h in items]
        dirs = [s % 2 for s, h in items]
        gcol = [gc_s[s][:, col[j]:col[j] + 1] for j, (s, h) in enumerate(items)]
        grow = [gr_s[s][col[j]:col[j] + 1, :] for j, (s, h) in enumerate(items)]
        bcol = [b_s[s][:, col[j]:col[j] + 1] for j, (s, h) in enumerate(items)]
        glast = [gcol[j][last_d[dirs[j]]:last_d[dirs[j]] + 1, :] for j in idx]
        q = [q_refs[s][:, h * hd:(h + 1) * hd] for s, h in items]
        k = [k_refs[s][:, h * hd:(h + 1) * hd] for s, h in items]
        v = [v_refs[s][:, h * hd:(h + 1) * hd] for s, h in items]
        kq_k = [_dot_nt(jnp.concatenate([k[j], q[j]], axis=0), k[j]) for j in idx]
        decay = [jnp.exp(jnp.where(causal_d[dirs[j]], gcol[j] - grow[j], -jnp.inf)) for j in idx]
        lm = [jnp.where(strict_d[dirs[j]], bcol[j] * kq_k[j][:n] * decay[j], 0.0) for j in idx]
        intra = [(kq_k[j][n:] * decay[j]).astype(BF16) for j in idx]
        eg = [jnp.exp(g) for g in gcol]
        rhs = [jnp.concatenate([v[j].astype(F32) * bcol[j], k[j].astype(F32) * (bcol[j] * eg[j])],
                               axis=1).astype(BF16) for j in idx]
        t = _unit_tri_inverses(lm)
        uw = [_dot(t[j].astype(BF16), rhs[j]) for j in idx]
        st = [s_refs[s][h] for s, h in items]
        wq = [jnp.concatenate([uw[j][:, hd:], q[j].astype(F32) * eg[j]], axis=0).astype(BF16) for j in idx]
        ws = [_dot(wq[j], st[j].astype(BF16)) for j in idx]
        vnew = [(uw[j][:, :hd] - ws[j][:n]).astype(BF16) for j in idx]
        kg = [(k[j].astype(F32) * jnp.exp(glast[j] - gcol[j])).astype(BF16) for j in idx]
        for j, (s, h) in enumerate(items):
            o_refs[s][:, h * hd:(h + 1) * hd] = (ws[j][n:] + _dot(intra[j], vnew[j])).astype(o_refs[s].dtype)
        for j, (s, h) in enumerate(items):
            s_refs[s][h] = st[j] * jnp.exp(glast[j]) + _dot_tn(kg[j], vnew[j])

    items = [(s, h) for h in range(nh) for s in range(ns)]
    for i0 in range(0, len(items), SCAN_GROUP):
        group(items[i0:i0 + SCAN_GROUP])


def _dn_scan(q, k, v, g, g_rows, beta, batch, m_lat, nh, hd):
    m = q.shape[0]
    nl = m_lat // batch // CHUNK
    nc = (m - m_lat) // batch // CHUNK
    ctx0 = m_lat // CHUNK

    def specs(backward):
        def block(b, c):
            ctx_blk = ctx0 + b * nc + (nc - 1 - c if backward else c)
            lat_blk = b * nl + (nl - 1 - (c - nc) if backward else c - nc)
            return jnp.where(c < nc, ctx_blk, lat_blk)

        tok = lambda b, c: (block(b, c), 0)
        tok3 = lambda b, c: (block(b, c), 0, 0)
        wide = pl.BlockSpec((CHUNK, nh * hd), tok)
        gate = pl.BlockSpec((CHUNK, 2 * nh), tok)
        return [wide, wide, wide, gate, pl.BlockSpec((1, 2 * nh, CHUNK), tok3), gate], wide

    per_stream = [specs(False), specs(True)]
    return pl.pallas_call(
        functools.partial(_dn_scan_kernel, ns=2, nh=nh, hd=hd),
        grid=(batch, nc + nl),
        in_specs=[spec for ins, _ in per_stream for spec in ins],
        out_specs=[out for _, out in per_stream],
        out_shape=[jax.ShapeDtypeStruct((m, nh * hd), BF16)] * 2,
        scratch_shapes=[pltpu.VMEM((nh, hd, hd), F32)] * 2,
        compiler_params=_cparams("parallel", "arbitrary"),
        name="dn_scan",
    )(*([q, k, v, g, g_rows, beta] * 2))


def _dn_out_kernel(of_ref, ob_ref, gate_ref, nw_ref, e_ref, y_ref, *, hd):
    o = of_ref[...].astype(F32) + ob_ref[...].astype(F32)
    ms = _head_sumsq(o, e_ref[...]) * (1.0 / hd)
    y = o * lax.rsqrt(ms + EPS) * nw_ref[...]
    y_ref[...] = (y * _silu(gate_ref[...].astype(F32))).astype(y_ref.dtype)


def _dn_out(o_f, o_b, p, norm_w, width, hd):
    m = o_f.shape[0]
    tc = _tile(width, 512, LANES)
    tm = _tile(m, 1024)
    nsec = width // tc
    ones_bd = jnp.asarray(np.kron(np.eye(tc // hd), np.ones((hd, hd))), BF16)
    nw = jnp.tile(norm_w.astype(F32), tc // hd).reshape(1, tc)
    return pl.pallas_call(
        functools.partial(_dn_out_kernel, hd=hd),
        grid=(m // tm, nsec),
        in_specs=[
            pl.BlockSpec((tm, tc), lambda i, j: (i, j)),
            pl.BlockSpec((tm, tc), lambda i, j: (i, j)),
            pl.BlockSpec((tm, tc), lambda i, j: (i, 3 * nsec + j)),
            pl.BlockSpec((1, tc), lambda i, j: (0, 0)),
            pl.BlockSpec((tc, tc), lambda i, j: (0, 0)),
        ],
        out_specs=pl.BlockSpec((tm, tc), lambda i, j: (i, j)),
        out_shape=jax.ShapeDtypeStruct((m, width), BF16),
        compiler_params=_cparams("parallel", "parallel"),
        name="dn_out",
    )(o_f, o_b, p, nw, ones_bd)


def _fold_kernel(w_ref, t_ref, o_ref):
    w_hi, w_lo = _split2(w_ref[0])
    t_hi, t_lo = _split2(t_ref[0])
    o_ref[0] = (_dot(w_hi, t_hi) + _dot(w_lo, t_hi) + _dot(w_hi, t_lo)).astype(o_ref.dtype)


def _fold_channel_dft(w_in, layer, groups):
    _, d, width = w_in.shape
    c = width // groups
    idx = np.arange(c)
    ang = 2.0 * np.pi * ((idx[:, None] * idx[None, :]) % c) / c
    table = jnp.asarray(np.stack([np.cos(ang), -np.sin(ang)]) / math.sqrt(c), F32)
    return pl.pallas_call(
        _fold_kernel,
        grid=(2, groups),
        in_specs=[pl.BlockSpec((1, d, c), lambda s, g: (layer, 0, g)),
                  pl.BlockSpec((1, c, c), lambda s, g: (s, 0, 0))],
        out_specs=pl.BlockSpec((1, d, c), lambda s, g: (0, 0, s * groups + g)),
        out_shape=jax.ShapeDtypeStruct((1, d, 2 * width), BF16),
        compiler_params=_cparams("parallel", "parallel"),
        name="fold_channel_dft",
    )(w_in, table)


def _dft_kernel(zr_ref, zi_ref, f_ref, g_ref, *rest, n1, n2):
    o_ref, zf, yf, of = rest[-4:]
    slabs = range(of.shape[0])
    lanes = lambda l: slice(l * LANES, (l + 1) * LANES)
    for l in slabs:
        zf[0, l] = zr_ref[:, lanes(l)].astype(F32)
        zf[1, l] = zi_ref[:, lanes(l)].astype(F32)

    def stage1(t2, carry):
        z = jnp.concatenate(
            [jnp.concatenate([zf[part, l, pl.ds(t2, n1, stride=n2), :] for l in slabs], axis=1) for part in (0, 1)],
            axis=0)
        y = _dot(f_ref[...], z.astype(BF16))
        for l in slabs:
            yf[l, pl.ds(pl.multiple_of(t2 * 2 * n1, 2 * n1), 2 * n1), :] = y[:, lanes(l)]
        return carry

    lax.fori_loop(0, n2, stage1, 0, unroll=8)

    def stage2(f1, carry):
        z = jnp.concatenate(
            [jnp.concatenate([yf[l, pl.ds(off + f1, n2, stride=2 * n1), :] for l in slabs], axis=1)
             for off in (0, n1)], axis=0)
        r = _dot(g_ref[f1], z.astype(BF16))
        for l in slabs:
            of[l, pl.ds(f1, n2, stride=n1), :] = r[:, lanes(l)]
        return carry

    lax.fori_loop(0, n1, stage2, 0, unroll=8)
    for l in slabs:
        o_ref[:, lanes(l)] = of[l].astype(o_ref.dtype)


def _position_dft_real(u, row0, batch, t_len, width, into=None):
    n1 = 1 << (_log2(t_len) // 2)
    n2 = t_len // n1
    a1 = 2.0 * np.pi * ((np.arange(n1)[:, None] * np.arange(n1)[None, :]) % n1) / n1
    c1, s1 = np.cos(a1), np.sin(a1)
    f1m = jnp.asarray(np.block([[c1, s1], [-s1, c1]]) / math.sqrt(n1), BF16)
    f1i, f2i, t2i = np.arange(n1)[:, None, None], np.arange(n2)[None, :, None], np.arange(n2)[None, None, :]
    theta = 2.0 * np.pi * (((t2i * f1i) % t_len) / t_len + ((t2i * f2i) % n2) / n2)
    g2m = jnp.asarray(np.concatenate([np.cos(theta), np.sin(theta)], axis=2) / math.sqrt(n2), BF16)

    tc = _tile(width, 256, LANES)
    nj = width // tc
    assert row0 % t_len == 0
    b0 = row0 // t_len
    return pl.pallas_call(
        functools.partial(_dft_kernel, n1=n1, n2=n2),
        grid=(batch, nj),
        in_specs=[
            pl.BlockSpec((t_len, tc), lambda b, j: (b0 + b, j)),
            pl.BlockSpec((t_len, tc), lambda b, j: (b0 + b, nj + j)),
            pl.BlockSpec((2 * n1, 2 * n1), lambda b, j: (0, 0)),
            pl.BlockSpec((n1, n2, 2 * n2), lambda b, j: (0, 0, 0)),
        ] + ([] if into is None else [pl.BlockSpec(memory_space=pl.ANY)]),
        input_output_aliases={} if into is None else {4: 0},
        out_specs=pl.BlockSpec((t_len, tc), lambda b, j: (b0 + b, j)),
        out_shape=jax.ShapeDtypeStruct((u.shape[0], width), BF16),
        scratch_shapes=[pltpu.VMEM((2, tc // LANES, t_len, LANES), F32),
                        pltpu.VMEM((tc // LANES, n2 * 2 * n1, LANES), F32),
                        pltpu.VMEM((tc // LANES, t_len, LANES), F32)],
        compiler_params=_cparams("parallel", "parallel"),
        name="position_dft",
    )(u, u, f1m, g2m, *([] if into is None else [into]))


def _ffn_prep_kernel(x_ref, nw_ref, sh_ref, sc_ref, rw_ref, rb_ref, h_ref, comb_ref, *, n_exp):
    h = _modulated(x_ref[...], nw_ref[...], sh_ref[0], sc_ref[0])
    h_ref[...] = h.astype(h_ref.dtype)
    h_hi, h_lo = _split2(h)
    r_hi, r_lo = _split2(rw_ref[...])
    logits = _dot_nt(r_hi, h_hi) + _dot_nt(r_lo, h_hi) + _dot_nt(r_hi, h_lo)
    s = _sigmoid(logits)
    sel = s + rb_ref[...]
    srow = [s[e:e + 1, :] for e in range(n_exp)]
    row = [sel[e:e + 1, :] for e in range(n_exp)]
    epg = n_exp // N_GROUPS

    def beats(a, ia, b, ib):
        return (a >= b) if ia < ib else (a > b)

    rank, gscore = {}, []
    for g in range(N_GROUPS):
        members = range(g * epg, (g + 1) * epg)
        for i in members:
            rank[i] = sum(beats(row[j], j, row[i], i).astype(F32) for j in members if j != i)
        gscore.append(sum(jnp.where(rank[i] < GROUP_SCORE_K, row[i], 0.0) for i in members))
    picked = []
    for g in range(N_GROUPS):
        grank = sum(beats(gscore[j], j, gscore[g], g).astype(F32) for j in range(N_GROUPS) if j != g)
        for i in range(g * epg, (g + 1) * epg):
            picked.append(jnp.where((grank < 1.0) & (rank[i] < TOP_K), srow[i], 0.0))
    denom = sum(picked)
    comb_ref[...] = jnp.concatenate([p / denom for p in picked], axis=0)


def _ffn_prep(x, m, nw, mod, layer, router_wt, router_b):
    d = x.shape[1]
    n_exp = router_wt.shape[0]
    tm = _tile(math.gcd(m, mod.rows_per_mod), 512, LANES)
    return pl.pallas_call(
        functools.partial(_ffn_prep_kernel, n_exp=n_exp),
        grid=(m // tm,),
        in_specs=[
            pl.BlockSpec((tm, d), lambda i: (i, 0)),
            pl.BlockSpec((1, d), lambda i: (0, 0)),
            mod.spec(layer, 3, tm),
            mod.spec(layer, 4, tm),
            pl.BlockSpec((n_exp, d), lambda i: (0, 0)),
            pl.BlockSpec((n_exp, 1), lambda i: (0, 0)),
        ],
        out_specs=[pl.BlockSpec((tm, d), lambda i: (i, 0)), pl.BlockSpec((n_exp, tm), lambda i: (0, i))],
        out_shape=[jax.ShapeDtypeStruct((m, d), F32), jax.ShapeDtypeStruct((n_exp, m), F32)],
        compiler_params=_cparams("parallel"),
        name="ffn_prep",
    )(x, nw.reshape(1, d), mod.table, mod.table, router_wt, router_b.reshape(n_exp, 1))


def _route(comb_t, tg):
    assert TOP_K == 2
    n_exp, n = comb_t.shape
    mask = comb_t > 0.0
    cnt = jnp.sum(mask.astype(I32), axis=1)
    padded = ((cnt + tg - 1) // tg) * tg
    ends = jnp.cumsum(padded)
    dest = (ends - padded)[:, None] + jnp.cumsum(mask.astype(I32), axis=1) - 1
    eidx = jnp.arange(n_exp, dtype=I32)[:, None]
    first = jnp.min(jnp.where(mask, eidx, n_exp), axis=0)
    final = jnp.max(jnp.where(mask, eidx, -1), axis=0)
    is_a = mask & (eidx == first)
    is_b = mask & (eidx == final) & (final != first)
    pick = lambda sel, val: jnp.sum(jnp.where(sel, val, 0), axis=0)
    w_ab = jnp.stack([pick(is_a, comb_t), pick(is_b, comb_t)], axis=1)
    p_rows = TOP_K * n + n_exp * tg
    d_a = jnp.where(jnp.any(is_a, axis=0), pick(is_a, dest), p_rows).astype(I32)
    d_b = jnp.where(jnp.any(is_b, axis=0), pick(is_b, dest), p_rows).astype(I32)
    tile_start = jnp.arange(p_rows // tg, dtype=I32) * tg
    tile_exp = jnp.minimum(jnp.sum((tile_start[:, None] >= ends[None, :]).astype(I32), axis=1), n_exp - 1)
    tile_ok = (tile_start < ends[-1]).astype(I32)
    tile_new = jnp.concatenate([jnp.ones((1,), I32), (tile_exp[1:] != tile_exp[:-1]).astype(I32)])
    return tile_exp, tile_ok, tile_new, d_a, d_b, w_ab, p_rows


def _src_kernel(da_ref, db_ref, src_ref, *, n_tok):
    def zero(r, carry):
        src_ref[r] = 0
        return carry
    lax.fori_loop(0, src_ref.shape[0], zero, 0, unroll=8)

    def put(t, carry):
        src_ref[da_ref[t]] = t
        src_ref[db_ref[t]] = t
        return carry
    lax.fori_loop(0, n_tok, put, 0, unroll=8)


def _sorted_row_tokens(d_a, d_b, p_rows, spare):
    n_tok = d_a.shape[0]
    smem = pl.BlockSpec(memory_space=pltpu.SMEM)
    return pl.pallas_call(
        functools.partial(_src_kernel, n_tok=n_tok),
        in_specs=[smem, smem],
        out_specs=smem,
        out_shape=jax.ShapeDtypeStruct((p_rows + spare,), I32),
        name="sorted_row_tokens",
    )(d_a, d_b)


def _row_copies(idx_ref, base, n_rows, src_hbm, dst_buf, sem, wait):
    def body(r, carry):
        cp = pltpu.make_async_copy(src_hbm.at[pl.ds(idx_ref[base + r], 1)], dst_buf.at[pl.ds(r, 1)], sem)
        if wait:
            cp.wait()
        else:
            cp.start()
        return carry
    lax.fori_loop(0, n_rows, body, 0, unroll=8)


def _moe_expert_kernel(texp_ref, tval_ref, tnew_ref, src_ref, h_hbm, wg_ref, wu_ref, wd_ref, y_ref,
                       xbuf, sem, wgb, wub, wdb, *, tg, n_tiles):
    i = pl.program_id(0)
    nbuf = MOE_GATHER_AHEAD + 1
    bufs = [xbuf.at[s] for s in range(nbuf)]
    valid = tval_ref[i] != 0

    def requested(tile):
        return (tile < MOE_GATHER_AHEAD) | (tval_ref[jnp.maximum(tile - MOE_GATHER_AHEAD, 0)] != 0)

    @pl.when(i == 0)
    def _():
        for t in range(MOE_GATHER_AHEAD):
            _row_copies(src_ref, t * tg, tg, h_hbm, bufs[t], sem.at[t], wait=False)

    @pl.when(valid & (tnew_ref[i] != 0))
    def _():
        wgb[...] = wg_ref[0, 0].astype(BF16)
        wub[...] = wu_ref[0, 0].astype(BF16)
        wdb[...] = wd_ref[0, 0].astype(BF16)

    for slot in range(nbuf):
        @pl.when(requested(i) & (i % nbuf == slot))
        def _():
            _row_copies(src_ref, i * tg, tg, h_hbm, bufs[slot], sem.at[slot], wait=True)

        @pl.when(valid & (i % nbuf == slot))
        def _():
            ahead = (slot + MOE_GATHER_AHEAD) % nbuf

            def request(rows):
                for r in rows:
                    pltpu.make_async_copy(h_hbm.at[pl.ds(src_ref[(i + MOE_GATHER_AHEAD) * tg + r], 1)],
                                          bufs[ahead].at[pl.ds(r, 1)], sem.at[ahead]).start(priority=r % 2)

            third = tg // 3
            x = bufs[slot][...].astype(BF16)
            request(range(0, third))
            a = _dot(x, wgb[...])
            request(range(third, 2 * third))
            b = _dot(x, wub[...])
            request(range(2 * third, tg))
            act = (_silu(a) * b).astype(BF16)
            y_ref[...] = _dot(act, wdb[...])

    @pl.when(i == n_tiles - 1)
    def _():
        for t in range(n_tiles, n_tiles + MOE_GATHER_AHEAD):
            @pl.when(tval_ref[t - MOE_GATHER_AHEAD] != 0)
            def _():
                _row_copies(src_ref, t * tg, tg, h_hbm, bufs[t % nbuf], sem.at[t % nbuf], wait=True)

    @pl.when(jnp.logical_not(valid))
    def _():
        y_ref[...] = jnp.zeros_like(y_ref)


def _moe_experts(h, src, tile_exp, tile_ok, tile_new, wg, wu, wd, layer, tg, p_rows):
    n, d = h.shape
    f = wg.shape[3]
    wmap = lambda i, te, ok, new, s: (layer, te[i], 0, 0)
    return pl.pallas_call(
        functools.partial(_moe_expert_kernel, tg=tg, n_tiles=p_rows // tg),
        grid_spec=pltpu.PrefetchScalarGridSpec(
            num_scalar_prefetch=4,
            grid=(p_rows // tg,),
            in_specs=[
                pl.BlockSpec(memory_space=pl.ANY),
                pl.BlockSpec((1, 1, d, f), wmap),
                pl.BlockSpec((1, 1, d, f), wmap),
                pl.BlockSpec((1, 1, f, d), wmap),
            ],
            out_specs=pl.BlockSpec((tg, d), lambda i, te, ok, new, s: (i, 0)),
            scratch_shapes=[pltpu.VMEM((MOE_GATHER_AHEAD + 1, tg, d), F32),
                            pltpu.SemaphoreType.DMA((MOE_GATHER_AHEAD + 1,)),
                            pltpu.VMEM((d, f), BF16), pltpu.VMEM((d, f), BF16), pltpu.VMEM((f, d), BF16)],
        ),
        out_shape=jax.ShapeDtypeStruct((p_rows, d), F32),
        compiler_params=_cparams("arbitrary"),
        name="moe_experts",
    )(tile_exp, tile_ok, tile_new, src, h, wg, wu, wd)


def _moe_combine_kernel(da_ref, db_ref, y_hbm, x_ref, w_ref, g_ref, o_ref, ya, yb, sem, *, tm):
    i = pl.program_id(0)

    def fetch(tile, slot, wait):
        _row_copies(da_ref, tile * tm, tm, y_hbm, ya.at[slot], sem.at[0, slot], wait)
        _row_copies(db_ref, tile * tm, tm, y_hbm, yb.at[slot], sem.at[1, slot], wait)

    @pl.when(i == 0)
    def _():
        fetch(0, 0, False)

    @pl.when(i + 1 < pl.num_programs(0))
    def _():
        slot = (i + 1) % 2
        for r in range(tm):
            for k, (idx_ref, buf) in enumerate(((da_ref, ya), (db_ref, yb))):
                pltpu.make_async_copy(y_hbm.at[pl.ds(idx_ref[(i + 1) * tm + r], 1)],
                                      buf.at[slot, pl.ds(r, 1)], sem.at[k, slot]).start(priority=k)

    fetch(i, i % 2, True)
    w = w_ref[...]
    mix = w[:, 0:1] * ya[i % 2] + w[:, 1:2] * yb[i % 2]
    o_ref[...] = x_ref[...] + g_ref[0] * mix


def _moe_combine(y, row_a, row_b, w_ab, x, m, mod, layer):
    d = x.shape[1]
    tm = _tile(math.gcd(m, mod.rows_per_mod), 256)
    return pl.pallas_call(
        functools.partial(_moe_combine_kernel, tm=tm),
        grid_spec=pltpu.PrefetchScalarGridSpec(
            num_scalar_prefetch=2,
            grid=(m // tm,),
            in_specs=[
                pl.BlockSpec(memory_space=pl.ANY),
                pl.BlockSpec((tm, d), lambda i, a, b: (i, 0)),
                pl.BlockSpec((tm, 2), lambda i, a, b: (i, 0)),
                mod.spec(layer, 5, tm),
            ],
            out_specs=pl.BlockSpec((tm, d), lambda i, a, b: (i, 0)),
            scratch_shapes=[pltpu.VMEM((2, tm, d), F32), pltpu.VMEM((2, tm, d), F32),
                            pltpu.SemaphoreType.DMA((2, 2))],
        ),
        out_shape=jax.ShapeDtypeStruct((m, d), F32),
        compiler_params=_cparams("arbitrary"),
        name="moe_combine",
    )(row_a, row_b, y, x, w_ab, mod.table)


def _moe(h, comb_t, wg, wu, wd, layer_w, x, m, mod, layer):
    tg = 256 if m * TOP_K >= 256 * wg.shape[1] else 64
    tile_exp, tile_ok, tile_new, d_a, d_b, w_ab, p_rows = _route(comb_t, tg)
    src = _sorted_row_tokens(d_a, d_b, p_rows, MOE_GATHER_AHEAD * tg)
    y = _moe_experts(h, src, tile_exp, tile_ok, tile_new, wg, wu, wd, layer_w, tg, p_rows)
    none = lambda rows: jnp.where(rows == p_rows, 0, rows)
    return _moe_combine(y, none(d_a), none(d_b), w_ab, x, m, mod, layer)


def kernel(x, c, ctx, c_ctx, ada_w, ada_b, norm_mix_w, norm_ffn_w, dn_w_in, dn_conv_w, dn_A_log, dn_dt_bias, dn_norm_w, dn_w_out, fn_w_in, fn_w_out, router_w, router_bias, moe_w_gate, moe_w_up, moe_w_down, final_norm_w):
    batch, seq, d = x.shape
    ctx_len = ctx.shape[1]
    depth = ada_w.shape[0]
    nh = DN_HEADS
    dn_width = dn_w_out.shape[1]
    hd = dn_width // nh
    fn_width = fn_w_in.shape[2]
    m_lat, m_all = batch * seq, batch * (seq + ctx_len)
    assert batch * ctx_len <= seq and seq % ctx_len == 0

    mod_rows = -(-(batch + 1) // 8) * 8
    cond = jnp.zeros((mod_rows, d), F32).at[:batch].set(c).at[batch].set(c_ctx)
    table = _ada_table(cond, ada_w, ada_b)
    mod = _Mod(table.reshape(depth * mod_rows * N_MOD, 1, d), mod_rows, d, seq)

    router_wt = router_w.T
    xs = jnp.concatenate([x.reshape(m_lat, d), ctx.reshape(batch * ctx_len, d)], axis=0)

    for i in range(depth):
        use_dn = i % N_MIXERS == 0
        j = i // N_MIXERS
        ctx_out = i < depth - 1
        ctx_live = use_dn or ctx_out
        m = m_all if ctx_live else m_lat

        h = _modulate(xs, m, norm_mix_w[i], mod, i, 0, 1)
        if use_dn:
            p = _matmul(h, dn_w_in, j, BF16, 1024, n=4 * dn_width)
            ab = _matmul(h, dn_w_in, j, F32, LANES, col0=4 * dn_width, n=LANES)
            q = _dn_conv(p, dn_conv_w, j, 0, "q", m_lat, ctx_len, dn_width, hd)
            k = _dn_conv(p, dn_conv_w, j, 1, "k", m_lat, ctx_len, dn_width, hd)
            v = _dn_conv(p, dn_conv_w, j, 2, "v", m_lat, ctx_len, dn_width, hd)
            g, beta = _dn_gates(ab, dn_A_log[j].reshape(-1), dn_dt_bias[j].reshape(-1), nh)
            g_rows = g.reshape(m // CHUNK, CHUNK, 2 * nh).transpose(0, 2, 1)
            o_f, o_b = _dn_scan(q, k, v, g, g_rows, beta, batch, m_lat, nh, hd)
            y = _dn_out(o_f, o_b, p, dn_norm_w[j], dn_width, hd)
            xs = _matmul_gated_residual(y, dn_w_out, j, xs, mod, i, 2)
        else:
            w_fold = _fold_channel_dft(fn_w_in, j, FN_GROUPS)
            u = _matmul(h, w_fold, 0, BF16, 1024)
            mixed = _position_dft_real(u, 0, batch, seq, fn_width)
            if ctx_out:
                mixed = _position_dft_real(u, m_lat, batch, ctx_len, fn_width, into=mixed)
            xs = _matmul_gated_residual(mixed, fn_w_out, j, xs, mod, i, 2)

        hf, comb_t = _ffn_prep(xs, m, norm_ffn_w[i], mod, i, router_wt, router_bias)
        xs = _moe(hf, comb_t, moe_w_gate, moe_w_up, moe_w_down, i, xs, m, mod, i)

    return _final_norm(xs, m_lat, final_norm_w).reshape(batch, seq, d)
```

```python
import functools
import math

import numpy as np
import jax
import jax.numpy as jnp
from jax import lax
from jax.experimental import pallas as pl
from jax.experimental.pallas import tpu as pltpu

F32 = jnp.float32
BF16 = jnp.bfloat16
I32 = jnp.int32

GRID_W = 64
CHUNK = 64
CONV_W = 5
DN_HEADS = 16
FN_GROUPS = 4
N_GROUPS = 4
GROUP_SCORE_K = 2
TOP_K = 2
N_MOD = 6
N_MIXERS = 2
EPS = 1e-6

VMEM_LIMIT = 56 * 1024 * 1024
LANES = 128
MOE_GATHER_AHEAD = 3
SCAN_GROUP = 32


def _cparams(*sem):
    return pltpu.CompilerParams(dimension_semantics=sem, vmem_limit_bytes=VMEM_LIMIT)


def _dot(a, b):
    return jnp.dot(a, b, preferred_element_type=F32)


def _dot_nt(a, b):
    return lax.dot_general(a, b, (((1,), (1,)), ((), ())), preferred_element_type=F32)


def _dot_tn(a, b):
    return lax.dot_general(a, b, (((0,), (0,)), ((), ())), preferred_element_type=F32)


def _split2(x):
    hi = x.astype(BF16)
    lo = (x - hi.astype(F32)).astype(BF16)
    return hi, lo


def _split3(x):
    hi = x.astype(BF16)
    r = x - hi.astype(F32)
    mid = r.astype(BF16)
    lo = (r - mid.astype(F32)).astype(BF16)
    return hi, mid, lo


def _silu(x):
    return x / (1.0 + jnp.exp(-x))


def _sigmoid(x):
    return 1.0 / (1.0 + jnp.exp(-x))


def _tile(n, pref, mult=8):
    if n <= pref:
        return n
    t = (pref // mult) * mult
    while t >= mult:
        if n % t == 0:
            return t
        t -= mult
    return n


def _log2(n):
    assert n & (n - 1) == 0, n
    return n.bit_length() - 1


def _ada_kernel(c_ref, w_ref, b_ref, o_ref):
    s_hi, s_lo = _split2(_silu(c_ref[...]))
    w_hi, w_lo = _split2(w_ref[0])
    acc = _dot(s_hi, w_hi) + _dot(s_lo, w_hi) + _dot(s_hi, w_lo)
    o_ref[0] = acc + b_ref[0]


def _ada_table(cond, ada_w, ada_b):
    depth, d, n = ada_w.shape
    r = cond.shape[0]
    tn = _tile(n, 1024, LANES)
    return pl.pallas_call(
        _ada_kernel,
        grid=(depth, n // tn),
        in_specs=[
            pl.BlockSpec((r, d), lambda l, j: (0, 0)),
            pl.BlockSpec((1, d, tn), lambda l, j: (l, 0, j)),
            pl.BlockSpec((1, 1, tn), lambda l, j: (l, 0, j)),
        ],
        out_specs=pl.BlockSpec((1, r, tn), lambda l, j: (l, 0, j)),
        out_shape=jax.ShapeDtypeStruct((depth, r, n), F32),
        compiler_params=_cparams("parallel", "parallel"),
        name="ada_table",
    )(cond, ada_w, ada_b.reshape(depth, 1, n))


class _Mod:
    def __init__(self, table3, mod_rows, d, rows_per_mod):
        self.table = table3
        self.mod_rows = mod_rows
        self.d = d
        self.rows_per_mod = rows_per_mod

    def spec(self, layer, k, tm, row_axis=0, width=None, col=None):
        assert self.rows_per_mod % tm == 0
        width = self.d if width is None else width
        rows, mr = self.rows_per_mod, self.mod_rows

        def index_map(*g):
            row = (g[row_axis] * tm) // rows
            return ((layer * mr + row) * N_MOD + k, 0, 0 if col is None else col(*g))
        return pl.BlockSpec((1, 1, width), index_map)


def _modulated(x, nw, shift, scale):
    ms = jnp.mean(x * x, axis=-1, keepdims=True)
    return (x * lax.rsqrt(ms + EPS) * nw) * (1.0 + scale) + shift


def _modulate_kernel(x_ref, nw_ref, sh_ref, sc_ref, o_ref):
    o_ref[...] = _modulated(x_ref[...], nw_ref[...], sh_ref[0], sc_ref[0]).astype(o_ref.dtype)


def _modulate(x, m, nw, mod, layer, k_shift, k_scale):
    d = x.shape[1]
    tm = _tile(math.gcd(m, mod.rows_per_mod), 1024)
    return pl.pallas_call(
        _modulate_kernel,
        grid=(m // tm,),
        in_specs=[
            pl.BlockSpec((tm, d), lambda i: (i, 0)),
            pl.BlockSpec((1, d), lambda i: (0, 0)),
            mod.spec(layer, k_shift, tm),
            mod.spec(layer, k_scale, tm),
        ],
        out_specs=pl.BlockSpec((tm, d), lambda i: (i, 0)),
        out_shape=jax.ShapeDtypeStruct((m, d), BF16),
        compiler_params=_cparams("parallel"),
        name="modulate",
    )(x, nw.reshape(1, d), mod.table, mod.table)


def _final_norm_kernel(x_ref, nw_ref, o_ref):
    x = x_ref[...]
    ms = jnp.mean(x * x, axis=-1, keepdims=True)
    o_ref[...] = x * lax.rsqrt(ms + EPS) * nw_ref[...]


def _final_norm(x, m, nw):
    d = x.shape[1]
    tm = _tile(m, 1024)
    return pl.pallas_call(
        _final_norm_kernel,
        grid=(m // tm,),
        in_specs=[pl.BlockSpec((tm, d), lambda i: (i, 0)), pl.BlockSpec((1, d), lambda i: (0, 0))],
        out_specs=pl.BlockSpec((tm, d), lambda i: (i, 0)),
        out_shape=jax.ShapeDtypeStruct((m, d), F32),
        compiler_params=_cparams("parallel"),
        name="final_norm",
    )(x, nw.reshape(1, d))


def _matmul_kernel(a_ref, w_ref, o_ref, wb_ref, *, valid_cols):
    @pl.when(pl.program_id(1) == 0)
    def _():
        w = w_ref[0]
        if valid_cols < w.shape[1]:
            w = jnp.where(lax.broadcasted_iota(I32, w.shape, 1) < valid_cols, w, 0.0)
        wb_ref[...] = w.astype(BF16)

    o_ref[...] = _dot(a_ref[...], wb_ref[...]).astype(o_ref.dtype)


def _matmul(a, w, layer, out_dtype, tn_pref, col0=0, n=None):
    m, k = a.shape
    n = w.shape[2] if n is None else n
    tm, tn = _tile(m, 1024), _tile(n, tn_pref, LANES)
    assert col0 % tn == 0
    cb = col0 // tn
    valid_cols = tn if col0 + n <= w.shape[2] else w.shape[2] - col0
    assert valid_cols == tn or n == tn
    return pl.pallas_call(
        functools.partial(_matmul_kernel, valid_cols=valid_cols),
        grid=(n // tn, m // tm),
        in_specs=[pl.BlockSpec((tm, k), lambda j, i: (i, 0)),
                  pl.BlockSpec((1, k, tn), lambda j, i: (layer, 0, cb + j))],
        out_specs=pl.BlockSpec((tm, tn), lambda j, i: (i, j)),
        out_shape=jax.ShapeDtypeStruct((m, n), out_dtype),
        scratch_shapes=[pltpu.VMEM((k, tn), BF16)],
        compiler_params=_cparams("parallel", "arbitrary"),
        name="matmul",
    )(a, w)


def _matmul_res_kernel(a_ref, w_ref, x_ref, g_ref, o_ref, wb_ref):
    @pl.when(pl.program_id(1) == 0)
    def _():
        wb_ref[...] = w_ref[0].astype(BF16)

    o_ref[...] = x_ref[...] + g_ref[0] * _dot(a_ref[...], wb_ref[...])


def _matmul_gated_residual(a, w, layer_w, x, mod, layer, k_gate, tn_pref=1024):
    m, k = a.shape
    n = w.shape[2]
    tm = _tile(math.gcd(m, mod.rows_per_mod), 1024)
    tn = _tile(n, tn_pref, LANES)
    return pl.pallas_call(
        _matmul_res_kernel,
        grid=(n // tn, m // tm),
        in_specs=[
            pl.BlockSpec((tm, k), lambda j, i: (i, 0)),
            pl.BlockSpec((1, k, tn), lambda j, i: (layer_w, 0, j)),
            pl.BlockSpec((tm, tn), lambda j, i: (i, j)),
            mod.spec(layer, k_gate, tm, row_axis=1, width=tn, col=lambda j, i: j),
        ],
        out_specs=pl.BlockSpec((tm, tn), lambda j, i: (i, j)),
        out_shape=jax.ShapeDtypeStruct((m, n), F32),
        scratch_shapes=[pltpu.VMEM((k, tn), BF16)],
        compiler_params=_cparams("parallel", "arbitrary"),
        name="matmul_gated_residual",
    )(a, w, x, mod.table)


def _head_sumsq(y, ones_bd):
    sq_hi, sq_lo = _split2(y * y)
    return _dot(sq_hi, ones_bd) + _dot(sq_lo, ones_bd)


def _dn_conv_kernel(u_ref, cw_ref, e_ref, o_ref, coef_ref, *, lat_tiles, lat_len, ctx_len, mode, head_dim):
    i = pl.program_id(1)
    tm = u_ref.shape[0]
    pad = CONV_W // 2
    cw = cw_ref[0]

    @pl.when((i == 0) | (i == lat_tiles))
    def _():
        conv_len = jnp.where(i < lat_tiles, lat_len, ctx_len)
        pos = lax.broadcasted_iota(I32, (tm, 1), 0) & (conv_len - 1)
        for j in range(CONV_W):
            d = j - pad
            inside = ((pos + d >= 0) & (pos + d < conv_len)).astype(F32)
            coef_ref[j] = inside * cw[j:j + 1, :]

    u = u_ref[...].astype(F32)
    acc = u * cw[pad:pad + 1, :]
    for j in range(CONV_W):
        d = j - pad
        if d != 0:
            acc = acc + pltpu.roll(u, (-d) % tm, 0) * coef_ref[j]
    y = _silu(acc)
    if mode != "v":
        y = y * lax.rsqrt(_head_sumsq(y, e_ref[...]) + EPS)
        if mode == "q":
            y = y * (head_dim ** -0.5)
    o_ref[...] = y.astype(o_ref.dtype)


def _dn_conv(p, conv_w, layer, section, mode, m_lat, ctx_len, width, head_dim):
    m = p.shape[0]
    tc = _tile(width, 512, LANES)
    tm = _tile(math.gcd(m_lat, m - m_lat), 1024, max(GRID_W, ctx_len))
    assert m_lat % tm == 0 and m % tm == 0 and tm % GRID_W == 0 and tm % ctx_len == 0
    ones_bd = jnp.asarray(np.kron(np.eye(tc // head_dim), np.ones((head_dim, head_dim))), BF16)
    nsec = width // tc
    return pl.pallas_call(
        functools.partial(_dn_conv_kernel, lat_tiles=m_lat // tm, lat_len=GRID_W, ctx_len=ctx_len, mode=mode,
                          head_dim=head_dim),
        grid=(nsec, m // tm),
        in_specs=[
            pl.BlockSpec((tm, tc), lambda j, i: (i, section * nsec + j)),
            pl.BlockSpec((1, CONV_W, tc), lambda j, i: (layer, 0, section * nsec + j)),
            pl.BlockSpec((tc, tc), lambda j, i: (0, 0)),
        ],
        out_specs=pl.BlockSpec((tm, tc), lambda j, i: (i, j)),
        out_shape=jax.ShapeDtypeStruct((m, width), BF16),
        scratch_shapes=[pltpu.VMEM((CONV_W, tm, tc), F32)],
        compiler_params=_cparams("parallel", "arbitrary"),
        name="dn_conv_" + mode,
    )(p, conv_w, ones_bd)


def _dn_gates_kernel(ab_ref, al_ref, dt_ref, tl_ref, tu_ref, g_ref, beta_ref, *, nh):
    ab = ab_ref[:, :4 * nh]
    a = ab[:, :2 * nh] + dt_ref[...]
    softplus = jnp.maximum(a, 0.0) + jnp.log1p(jnp.exp(-jnp.abs(a)))
    la = -jnp.exp(al_ref[...]) * softplus
    beta_ref[...] = _sigmoid(ab[:, 2 * nh:])
    parts = _split3(la)
    g_pre = sum(_dot(tl_ref[...], p) for p in parts)
    g_suf = sum(_dot(tu_ref[...], p) for p in parts)
    col = lax.broadcasted_iota(I32, la.shape, 1)
    g_ref[...] = jnp.where(col < nh, g_pre, g_suf)


def _dn_gates(ab, a_log, dt_bias, nh):
    m = ab.shape[0]
    tm = _tile(m, 1024, CHUNK)
    r = np.arange(tm)
    same = (r[:, None] // CHUNK) == (r[None, :] // CHUNK)
    tri_l = jnp.asarray(same & (r[None, :] <= r[:, None]), BF16)
    tri_u = jnp.asarray(same & (r[None, :] >= r[:, None]), BF16)
    return pl.pallas_call(
        functools.partial(_dn_gates_kernel, nh=nh),
        grid=(m // tm,),
        in_specs=[
            pl.BlockSpec((tm, ab.shape[1]), lambda i: (i, 0)),
            pl.BlockSpec((1, 2 * nh), lambda i: (0, 0)),
            pl.BlockSpec((1, 2 * nh), lambda i: (0, 0)),
            pl.BlockSpec((tm, tm), lambda i: (0, 0)),
            pl.BlockSpec((tm, tm), lambda i: (0, 0)),
        ],
        out_specs=[pl.BlockSpec((tm, 2 * nh), lambda i: (i, 0))] * 2,
        out_shape=[jax.ShapeDtypeStruct((m, 2 * nh), F32)] * 2,
        compiler_params=_cparams("parallel"),
        name="dn_gates",
    )(ab, a_log.reshape(1, 2 * nh), dt_bias.reshape(1, 2 * nh), tri_l, tri_u)


def _unit_tri_inverses(lms):
    shape = lms[0].shape
    ri = lax.broadcasted_iota(I32, shape, 0)
    ci = lax.broadcasted_iota(I32, shape, 1)
    eye = (ri == ci).astype(F32)
    pair = (ri >> 1) == (ci >> 1)
    ts = [eye - jnp.where(pair, lm, 0.0) for lm in lms]
    for lvl in range(1, _log2(shape[0])):
        cross = ((ri >> (lvl + 1)) == (ci >> (lvl + 1))) & ((ri >> lvl) != (ci >> lvl))
        offs = [jnp.where(cross, lm, 0.0).astype(BF16) for lm in lms]
        tbs = [t.astype(BF16) for t in ts]
        ps = [_dot(tb, off).astype(BF16) for tb, off in zip(tbs, offs)]
        ts = [t - _dot(p, tb) for t, p, tb in zip(ts, ps, tbs)]
    return ts


def _dn_scan_kernel(*refs, ns, nh, hd):
    ins, o_refs, s_refs = refs[:6 * ns], refs[6 * ns:7 * ns], refs[7 * ns:]
    q_refs, k_refs, v_refs = ins[0::6], ins[1::6], ins[2::6]

    @pl.when(pl.program_id(1) == 0)
    def _():
        for s_ref in s_refs:
            s_ref[...] = jnp.zeros_like(s_ref)

    n = CHUNK
    ri = lax.broadcasted_iota(I32, (n, n), 0)
    ci = lax.broadcasted_iota(I32, (n, n), 1)
    causal_d = (ci <= ri, ci >= ri)
    strict_d = (ci < ri, ci > ri)
    last_d = (n - 1, 0)
    gc_s = [r[...] for r in ins[3::6]]
    gr_s = [r[0] for r in ins[4::6]]
    b_s = [r[...] for r in ins[5::6]]

    def group(items):
        idx = range(len(items))
        col = [(s % 2) * nh + h for s, h in items]
        dirs = [s % 2 for s, h in items]
        gcol = [gc_s[s][:, col[j]:col[j] + 1] for j, (s, h) in enumerate(items)]
        grow = [gr_s[s][col[j]:col[j] + 1, :] for j, (s, h) in enumerate(items)]
        bcol = [b_s[s][:, col[j]:col[j] + 1] for j, (s, h) in enumerate(items)]
        glast = [gcol[j][last_d[dirs[j]]:last_d[dirs[j]] + 1, :] for j in idx]
        q = [q_refs[s][:, h * hd:(h + 1) * hd] for s, h in items]
        k = [k_refs[s][:, h * hd:(h + 1) * hd] for s, h in items]
        v = [v_refs[s][:, h * hd:(h + 1) * hd] for s, h in items]
        kq_k = [_dot_nt(jnp.concatenate([k[j], q[j]], axis=0), k[j]) for j in idx]
        decay = [jnp.exp(jnp.where(causal_d[dirs[j]], gcol[j] - grow[j], -jnp.inf)) for j in idx]
        lm = [jnp.where(strict_d[dirs[j]], bcol[j] * kq_k[j][:n] * decay[j], 0.0) for j in idx]
        intra = [(kq_k[j][n:] * decay[j]).astype(BF16) for j in idx]
        eg = [jnp.exp(g) for g in gcol]
        rhs = [jnp.concatenate([v[j].astype(F32) * bcol[j], k[j].astype(F32) * (bcol[j] * eg[j])],
                               axis=1).astype(BF16) for j in idx]
        t = _unit_tri_inverses(lm)
        uw = [_dot(t[j].astype(BF16), rhs[j]) for j in idx]
        st = [s_refs[s][h] for s, h in items]
        wq = [jnp.concatenate([uw[j][:, hd:], q[j].astype(F32) * eg[j]], axis=0).astype(BF16) for j in idx]
        ws = [_dot(wq[j], st[j].astype(BF16)) for j in idx]
        vnew = [(uw[j][:, :hd] - ws[j][:n]).astype(BF16) for j in idx]
        kg = [(k[j].astype(F32) * jnp.exp(glast[j] - gcol[j])).astype(BF16) for j in idx]
        for j, (s, h) in enumerate(items):
            o_refs[s][:, h * hd:(h + 1) * hd] = (ws[j][n:] + _dot(intra[j], vnew[j])).astype(o_refs[s].dtype)
        for j, (s, h) in enumerate(items):
            s_refs[s][h] = st[j] * jnp.exp(glast[j]) + _dot_tn(kg[j], vnew[j])

    items = [(s, h) for h in range(nh) for s in range(ns)]
    for i0 in range(0, len(items), SCAN_GROUP):
        group(items[i0:i0 + SCAN_GROUP])


def _dn_scan(q, k, v, g, g_rows, beta, batch, m_lat, nh, hd):
    m = q.shape[0]
    nl = m_lat // batch // CHUNK
    nc = (m - m_lat) // batch // CHUNK
    ctx0 = m_lat // CHUNK

    def specs(backward):
        def block(b, c):
            ctx_blk = ctx0 + b * nc + (nc - 1 - c if backward else c)
            lat_blk = b * nl + (nl - 1 - (c - nc) if backward else c - nc)
            return jnp.where(c < nc, ctx_blk, lat_blk)

        tok = lambda b, c: (block(b, c), 0)
        tok3 = lambda b, c: (block(b, c), 0, 0)
        wide = pl.BlockSpec((CHUNK, nh * hd), tok)
        gate = pl.BlockSpec((CHUNK, 2 * nh), tok)
        return [wide, wide, wide, gate, pl.BlockSpec((1, 2 * nh, CHUNK), tok3), gate], wide

    per_stream = [specs(False), specs(True)]
    return pl.pallas_call(
        functools.partial(_dn_scan_kernel, ns=2, nh=nh, hd=hd),
        grid=(batch, nc + nl),
        in_specs=[spec for ins, _ in per_stream for spec in ins],
        out_specs=[out for _, out in per_stream],
        out_shape=[jax.ShapeDtypeStruct((m, nh * hd), BF16)] * 2,
        scratch_shapes=[pltpu.VMEM((nh, hd, hd), F32)] * 2,
        compiler_params=_cparams("parallel", "arbitrary"),
        name="dn_scan",
    )(*([q, k, v, g, g_rows, beta] * 2))


def _dn_out_kernel(of_ref, ob_ref, gate_ref, nw_ref, e_ref, y_ref, *, hd):
    o = of_ref[...].astype(F32) + ob_ref[...].astype(F32)
    ms = _head_sumsq(o, e_ref[...]) * (1.0 / hd)
    y = o * lax.rsqrt(ms + EPS) * nw_ref[...]
    y_ref[...] = (y * _silu(gate_ref[...].astype(F32))).astype(y_ref.dtype)


def _dn_out(o_f, o_b, p, norm_w, width, hd):
    m = o_f.shape[0]
    tc = _tile(width, 512, LANES)
    tm = _tile(m, 1024)
    nsec = width // tc
    ones_bd = jnp.asarray(np.kron(np.eye(tc // hd), np.ones((hd, hd))), BF16)
    nw = jnp.tile(norm_w.astype(F32), tc // hd).reshape(1, tc)
    return pl.pallas_call(
        functools.partial(_dn_out_kernel, hd=hd),
        grid=(m // tm, nsec),
        in_specs=[
            pl.BlockSpec((tm, tc), lambda i, j: (i, j)),
            pl.BlockSpec((tm, tc), lambda i, j: (i, j)),
            pl.BlockSpec((tm, tc), lambda i, j: (i, 3 * nsec + j)),
            pl.BlockSpec((1, tc), lambda i, j: (0, 0)),
            pl.BlockSpec((tc, tc), lambda i, j: (0, 0)),
        ],
        out_specs=pl.BlockSpec((tm, tc), lambda i, j: (i, j)),
        out_shape=jax.ShapeDtypeStruct((m, width), BF16),
        compiler_params=_cparams("parallel", "parallel"),
        name="dn_out",
    )(o_f, o_b, p, nw, ones_bd)


def _fold_kernel(w_ref, t_ref, o_ref):
    w_hi, w_lo = _split2(w_ref[0])
    t_hi, t_lo = _split2(t_ref[0])
    o_ref[0] = (_dot(w_hi, t_hi) + _dot(w_lo, t_hi) + _dot(w_hi, t_lo)).astype(o_ref.dtype)


def _fold_channel_dft(w_in, layer, groups):
    _, d, width = w_in.shape
    c = width // groups
    idx = np.arange(c)
    ang = 2.0 * np.pi * ((idx[:, None] * idx[None, :]) % c) / c
    table = jnp.asarray(np.stack([np.cos(ang), -np.sin(ang)]) / math.sqrt(c), F32)
    return pl.pallas_call(
        _fold_kernel,
        grid=(2, groups),
        in_specs=[pl.BlockSpec((1, d, c), lambda s, g: (layer, 0, g)),
                  pl.BlockSpec((1, c, c), lambda s, g: (s, 0, 0))],
        out_specs=pl.BlockSpec((1, d, c), lambda s, g: (0, 0, s * groups + g)),
        out_shape=jax.ShapeDtypeStruct((1, d, 2 * width), BF16),
        compiler_params=_cparams("parallel", "parallel"),
        name="fold_channel_dft",
    )(w_in, table)


def _dft_kernel(zr_ref, zi_ref, f_ref, g_ref, *rest, n1, n2):
    o_ref, zf, yf, of = rest[-4:]
    slabs = range(of.shape[0])
    lanes = lambda l: slice(l * LANES, (l + 1) * LANES)
    for l in slabs:
        zf[0, l] = zr_ref[:, lanes(l)].astype(F32)
        zf[1, l] = zi_ref[:, lanes(l)].astype(F32)

    def stage1(t2, carry):
        z = jnp.concatenate(
            [jnp.concatenate([zf[part, l, pl.ds(t2, n1, stride=n2), :] for l in slabs], axis=1) for part in (0, 1)],
            axis=0)
        y = _dot(f_ref[...], z.astype(BF16))
        for l in slabs:
            yf[l, pl.ds(pl.multiple_of(t2 * 2 * n1, 2 * n1), 2 * n1), :] = y[:, lanes(l)]
        return carry

    lax.fori_loop(0, n2, stage1, 0, unroll=8)

    def stage2(f1, carry):
        z = jnp.concatenate(
            [jnp.concatenate([yf[l, pl.ds(off + f1, n2, stride=2 * n1), :] for l in slabs], axis=1)
             for off in (0, n1)], axis=0)
        r = _dot(g_ref[f1], z.astype(BF16))
        for l in slabs:
            of[l, pl.ds(f1, n2, stride=n1), :] = r[:, lanes(l)]
        return carry

    lax.fori_loop(0, n1, stage2, 0, unroll=8)
    for l in slabs:
        o_ref[:, lanes(l)] = of[l].astype(o_ref.dtype)


def _position_dft_real(u, row0, batch, t_len, width, into=None):
    n1 = 1 << (_log2(t_len) // 2)
    n2 = t_len // n1
    a1 = 2.0 * np.pi * ((np.arange(n1)[:, None] * np.arange(n1)[None, :]) % n1) / n1
    c1, s1 = np.cos(a1), np.sin(a1)
    f1m = jnp.asarray(np.block([[c1, s1], [-s1, c1]]) / math.sqrt(n1), BF16)
    f1i, f2i, t2i = np.arange(n1)[:, None, None], np.arange(n2)[None, :, None], np.arange(n2)[None, None, :]
    theta = 2.0 * np.pi * (((t2i * f1i) % t_len) / t_len + ((t2i * f2i) % n2) / n2)
    g2m = jnp.asarray(np.concatenate([np.cos(theta), np.sin(theta)], axis=2) / math.sqrt(n2), BF16)

    tc = _tile(width, 256, LANES)
    nj = width // tc
    assert row0 % t_len == 0
    b0 = row0 // t_len
    return pl.pallas_call(
        functools.partial(_dft_kernel, n1=n1, n2=n2),
        grid=(batch, nj),
        in_specs=[
            pl.BlockSpec((t_len, tc), lambda b, j: (b0 + b, j)),
            pl.BlockSpec((t_len, tc), lambda b, j: (b0 + b, nj + j)),
            pl.BlockSpec((2 * n1, 2 * n1), lambda b, j: (0, 0)),
            pl.BlockSpec((n1, n2, 2 * n2), lambda b, j: (0, 0, 0)),
        ] + ([] if into is None else [pl.BlockSpec(memory_space=pl.ANY)]),
        input_output_aliases={} if into is None else {4: 0},
        out_specs=pl.BlockSpec((t_len, tc), lambda b, j: (b0 + b, j)),
        out_shape=jax.ShapeDtypeStruct((u.shape[0], width), BF16),
        scratch_shapes=[pltpu.VMEM((2, tc // LANES, t_len, LANES), F32),
                        pltpu.VMEM((tc // LANES, n2 * 2 * n1, LANES), F32),
                        pltpu.VMEM((tc // LANES, t_len, LANES), F32)],
        compiler_params=_cparams("parallel", "parallel"),
        name="position_dft",
    )(u, u, f1m, g2m, *([] if into is None else [into]))


def _ffn_prep_kernel(x_ref, nw_ref, sh_ref, sc_ref, rw_ref, rb_ref, h_ref, comb_ref, *, n_exp):
    h = _modulated(x_ref[...], nw_ref[...], sh_ref[0], sc_ref[0])
    h_ref[...] = h.astype(h_ref.dtype)
    h_hi, h_lo = _split2(h)
    r_hi, r_lo = _split2(rw_ref[...])
    logits = _dot_nt(r_hi, h_hi) + _dot_nt(r_lo, h_hi) + _dot_nt(r_hi, h_lo)
    s = _sigmoid(logits)
    sel = s + rb_ref[...]
    srow = [s[e:e + 1, :] for e in range(n_exp)]
    row = [sel[e:e + 1, :] for e in range(n_exp)]
    epg = n_exp // N_GROUPS

    def beats(a, ia, b, ib):
        return (a >= b) if ia < ib else (a > b)

    rank, gscore = {}, []
    for g in range(N_GROUPS):
        members = range(g * epg, (g + 1) * epg)
        for i in members:
            rank[i] = sum(beats(row[j], j, row[i], i).astype(F32) for j in members if j != i)
        gscore.append(sum(jnp.where(rank[i] < GROUP_SCORE_K, row[i], 0.0) for i in members))
    picked = []
    for g in range(N_GROUPS):
        grank = sum(beats(gscore[j], j, gscore[g], g).astype(F32) for j in range(N_GROUPS) if j != g)
        for i in range(g * epg, (g + 1) * epg):
            picked.append(jnp.where((grank < 1.0) & (rank[i] < TOP_K), srow[i], 0.0))
    denom = sum(picked)
    comb_ref[...] = jnp.concatenate([p / denom for p in picked], axis=0)


def _ffn_prep(x, m, nw, mod, layer, router_wt, router_b):
    d = x.shape[1]
    n_exp = router_wt.shape[0]
    tm = _tile(math.gcd(m, mod.rows_per_mod), 512, LANES)
    return pl.pallas_call(
        functools.partial(_ffn_prep_kernel, n_exp=n_exp),
        grid=(m // tm,),
        in_specs=[
            pl.BlockSpec((tm, d), lambda i: (i, 0)),
            pl.BlockSpec((1, d), lambda i: (0, 0)),
            mod.spec(layer, 3, tm),
            mod.spec(layer, 4, tm),
            pl.BlockSpec((n_exp, d), lambda i: (0, 0)),
            pl.BlockSpec((n_exp, 1), lambda i: (0, 0)),
        ],
        out_specs=[pl.BlockSpec((tm, d), lambda i: (i, 0)), pl.BlockSpec((n_exp, tm), lambda i: (0, i))],
        out_shape=[jax.ShapeDtypeStruct((m, d), F32), jax.ShapeDtypeStruct((n_exp, m), F32)],
        compiler_params=_cparams("parallel"),
        name="ffn_prep",
    )(x, nw.reshape(1, d), mod.table, mod.table, router_wt, router_b.reshape(n_exp, 1))


def _route(comb_t, tg):
    assert TOP_K == 2
    n_exp, n = comb_t.shape
    mask = comb_t > 0.0
    cnt = jnp.sum(mask.astype(I32), axis=1)
    padded = ((cnt + tg - 1) // tg) * tg
    ends = jnp.cumsum(padded)
    dest = (ends - padded)[:, None] + jnp.cumsum(mask.astype(I32), axis=1) - 1
    eidx = jnp.arange(n_exp, dtype=I32)[:, None]
    first = jnp.min(jnp.where(mask, eidx, n_exp), axis=0)
    final = jnp.max(jnp.where(mask, eidx, -1), axis=0)
    is_a = mask & (eidx == first)
    is_b = mask & (eidx == final) & (final != first)
    pick = lambda sel, val: jnp.sum(jnp.where(sel, val, 0), axis=0)
    w_ab = jnp.stack([pick(is_a, comb_t), pick(is_b, comb_t)], axis=1)
    p_rows = TOP_K * n + n_exp * tg
    d_a = jnp.where(jnp.any(is_a, axis=0), pick(is_a, dest), p_rows).astype(I32)
    d_b = jnp.where(jnp.any(is_b, axis=0), pick(is_b, dest), p_rows).astype(I32)
    tile_start = jnp.arange(p_rows // tg, dtype=I32) * tg
    tile_exp = jnp.minimum(jnp.sum((tile_start[:, None] >= ends[None, :]).astype(I32), axis=1), n_exp - 1)
    tile_ok = (tile_start < ends[-1]).astype(I32)
    tile_new = jnp.concatenate([jnp.ones((1,), I32), (tile_exp[1:] != tile_exp[:-1]).astype(I32)])
    return tile_exp, tile_ok, tile_new, d_a, d_b, w_ab, p_rows


def _src_kernel(da_ref, db_ref, src_ref, *, n_tok):
    def zero(r, carry):
        src_ref[r] = 0
        return carry
    lax.fori_loop(0, src_ref.shape[0], zero, 0, unroll=8)

    def put(t, carry):
        src_ref[da_ref[t]] = t
        src_ref[db_ref[t]] = t
        return carry
    lax.fori_loop(0, n_tok, put, 0, unroll=8)


def _sorted_row_tokens(d_a, d_b, p_rows, spare):
    n_tok = d_a.shape[0]
    smem = pl.BlockSpec(memory_space=pltpu.SMEM)
    return pl.pallas_call(
        functools.partial(_src_kernel, n_tok=n_tok),
        in_specs=[smem, smem],
        out_specs=smem,
        out_shape=jax.ShapeDtypeStruct((p_rows + spare,), I32),
        name="sorted_row_tokens",
    )(d_a, d_b)


def _row_copies(idx_ref, base, n_rows, src_hbm, dst_buf, sem, wait):
    def body(r, carry):
        cp = pltpu.make_async_copy(src_hbm.at[pl.ds(idx_ref[base + r], 1)], dst_buf.at[pl.ds(r, 1)], sem)
        if wait:
            cp.wait()
        else:
            cp.start()
        return carry
    lax.fori_loop(0, n_rows, body, 0, unroll=8)


def _moe_expert_kernel(texp_ref, tval_ref, tnew_ref, src_ref, h_hbm, wg_ref, wu_ref, wd_ref, y_ref,
                       xbuf, sem, wgb, wub, wdb, *, tg, n_tiles):
    i = pl.program_id(0)
    nbuf = MOE_GATHER_AHEAD + 1
    bufs = [xbuf.at[s] for s in range(nbuf)]
    valid = tval_ref[i] != 0

    def requested(tile):
        return (tile < MOE_GATHER_AHEAD) | (tval_ref[jnp.maximum(tile - MOE_GATHER_AHEAD, 0)] != 0)

    @pl.when(i == 0)
    def _():
        for t in range(MOE_GATHER_AHEAD):
            _row_copies(src_ref, t * tg, tg, h_hbm, bufs[t], sem.at[t], wait=False)

    @pl.when(valid & (tnew_ref[i] != 0))
    def _():
        wgb[...] = wg_ref[0, 0].astype(BF16)
        wub[...] = wu_ref[0, 0].astype(BF16)
        wdb[...] = wd_ref[0, 0].astype(BF16)

    for slot in range(nbuf):
        @pl.when(requested(i) & (i % nbuf == slot))
        def _():
            _row_copies(src_ref, i * tg, tg, h_hbm, bufs[slot], sem.at[slot], wait=True)

        @pl.when(valid & (i % nbuf == slot))
        def _():
            ahead = (slot + MOE_GATHER_AHEAD) % nbuf

            def request(rows):
                for r in rows:
                    pltpu.make_async_copy(h_hbm.at[pl.ds(src_ref[(i + MOE_GATHER_AHEAD) * tg + r], 1)],
                                          bufs[ahead].at[pl.ds(r, 1)], sem.at[ahead]).start(priority=r % 2)

            third = tg // 3
            x = bufs[slot][...].astype(BF16)
            request(range(0, third))
            a = _dot(x, wgb[...])
            request(range(third, 2 * third))
            b = _dot(x, wub[...])
            request(range(2 * third, tg))
            act = (_silu(a) * b).astype(BF16)
            y_ref[...] = _dot(act, wdb[...])

    @pl.when(i == n_tiles - 1)
    def _():
        for t in range(n_tiles, n_tiles + MOE_GATHER_AHEAD):
            @pl.when(tval_ref[t - MOE_GATHER_AHEAD] != 0)
            def _():
                _row_copies(src_ref, t * tg, tg, h_hbm, bufs[t % nbuf], sem.at[t % nbuf], wait=True)

    @pl.when(jnp.logical_not(valid))
    def _():
        y_ref[...] = jnp.zeros_like(y_ref)


def _moe_experts(h, src, tile_exp, tile_ok, tile_new, wg, wu, wd, layer, tg, p_rows):
    n, d = h.shape
    f = wg.shape[3]
    wmap = lambda i, te, ok, new, s: (layer, te[i], 0, 0)
    return pl.pallas_call(
        functools.partial(_moe_expert_kernel, tg=tg, n_tiles=p_rows // tg),
        grid_spec=pltpu.PrefetchScalarGridSpec(
            num_scalar_prefetch=4,
            grid=(p_rows // tg,),
            in_specs=[
                pl.BlockSpec(memory_space=pl.ANY),
                pl.BlockSpec((1, 1, d, f), wmap),
                pl.BlockSpec((1, 1, d, f), wmap),
                pl.BlockSpec((1, 1, f, d), wmap),
            ],
            out_specs=pl.BlockSpec((tg, d), lambda i, te, ok, new, s: (i, 0)),
            scratch_shapes=[pltpu.VMEM((MOE_GATHER_AHEAD + 1, tg, d), F32),
                            pltpu.SemaphoreType.DMA((MOE_GATHER_AHEAD + 1,)),
                            pltpu.VMEM((d, f), BF16), pltpu.VMEM((d, f), BF16), pltpu.VMEM((f, d), BF16)],
        ),
        out_shape=jax.ShapeDtypeStruct((p_rows, d), F32),
        compiler_params=_cparams("arbitrary"),
        name="moe_experts",
    )(tile_exp, tile_ok, tile_new, src, h, wg, wu, wd)


def _moe_combine_kernel(da_ref, db_ref, y_hbm, x_ref, w_ref, g_ref, o_ref, ya, yb, sem, *, tm):
    i = pl.program_id(0)

    def fetch(tile, slot, wait):
        _row_copies(da_ref, tile * tm, tm, y_hbm, ya.at[slot], sem.at[0, slot], wait)
        _row_copies(db_ref, tile * tm, tm, y_hbm, yb.at[slot], sem.at[1, slot], wait)

    @pl.when(i == 0)
    def _():
        fetch(0, 0, False)

    @pl.when(i + 1 < pl.num_programs(0))
    def _():
        slot = (i + 1) % 2
        for r in range(tm):
            for k, (idx_ref, buf) in enumerate(((da_ref, ya), (db_ref, yb))):
                pltpu.make_async_copy(y_hbm.at[pl.ds(idx_ref[(i + 1) * tm + r], 1)],
                                      buf.at[slot, pl.ds(r, 1)], sem.at[k, slot]).start(priority=k)

    fetch(i, i % 2, True)
    w = w_ref[...]
    mix = w[:, 0:1] * ya[i % 2] + w[:, 1:2] * yb[i % 2]
    o_ref[...] = x_ref[...] + g_ref[0] * mix


def _moe_combine(y, row_a, row_b, w_ab, x, m, mod, layer):
    d = x.shape[1]
    tm = _tile(math.gcd(m, mod.rows_per_mod), 512)
    return pl.pallas_call(
        functools.partial(_moe_combine_kernel, tm=tm),
        grid_spec=pltpu.PrefetchScalarGridSpec(
            num_scalar_prefetch=2,
            grid=(m // tm,),
            in_specs=[
                pl.BlockSpec(memory_space=pl.ANY),
                pl.BlockSpec((tm, d), lambda i, a, b: (i, 0)),
                pl.BlockSpec((tm, 2), lambda i, a, b: (i, 0)),
                mod.spec(layer, 5, tm),
            ],
            out_specs=pl.BlockSpec((tm, d), lambda i, a, b: (i, 0)),
            scratch_shapes=[pltpu.VMEM((2, tm, d), F32), pltpu.VMEM((2, tm, d), F32),
                            pltpu.SemaphoreType.DMA((2, 2))],
        ),
        out_shape=jax.ShapeDtypeStruct((m, d), F32),
        compiler_params=_cparams("arbitrary"),
        name="moe_combine",
    )(row_a, row_b, y, x, w_ab, mod.table)


def _moe(h, comb_t, wg, wu, wd, layer_w, x, m, mod, layer):
    tg = 256 if m * TOP_K >= 256 * wg.shape[1] else 64
    tile_exp, tile_ok, tile_new, d_a, d_b, w_ab, p_rows = _route(comb_t, tg)
    src = _sorted_row_tokens(d_a, d_b, p_rows, MOE_GATHER_AHEAD * tg)
    y = _moe_experts(h, src, tile_exp, tile_ok, tile_new, wg, wu, wd, layer_w, tg, p_rows)
    none = lambda rows: jnp.where(rows == p_rows, 0, rows)
    return _moe_combine(y, none(d_a), none(d_b), w_ab, x, m, mod, layer)


def kernel(x, c, ctx, c_ctx, ada_w, ada_b, norm_mix_w, norm_ffn_w, dn_w_in, dn_conv_w, dn_A_log, dn_dt_bias, dn_norm_w, dn_w_out, fn_w_in, fn_w_out, router_w, router_bias, moe_w_gate, moe_w_up, moe_w_down, final_norm_w):
    batch, seq, d = x.shape
    ctx_len = ctx.shape[1]
    depth = ada_w.shape[0]
    nh = DN_HEADS
    dn_width = dn_w_out.shape[1]
    hd = dn_width // nh
    fn_width = fn_w_in.shape[2]
    m_lat, m_all = batch * seq, batch * (seq + ctx_len)
    assert batch * ctx_len <= seq and seq % ctx_len == 0

    mod_rows = -(-(batch + 1) // 8) * 8
    cond = jnp.zeros((mod_rows, d), F32).at[:batch].set(c).at[batch].set(c_ctx)
    table = _ada_table(cond, ada_w, ada_b)
    mod = _Mod(table.reshape(depth * mod_rows * N_MOD, 1, d), mod_rows, d, seq)

    router_wt = router_w.T
    xs = jnp.concatenate([x.reshape(m_lat, d), ctx.reshape(batch * ctx_len, d)], axis=0)

    for i in range(depth):
        use_dn = i % N_MIXERS == 0
        j = i // N_MIXERS
        ctx_out = i < depth - 1
        ctx_live = use_dn or ctx_out
        m = m_all if ctx_live else m_lat

        h = _modulate(xs, m, norm_mix_w[i], mod, i, 0, 1)
        if use_dn:
            p = _matmul(h, dn_w_in, j, BF16, 1024, n=4 * dn_width)
            ab = _matmul(h, dn_w_in, j, F32, LANES, col0=4 * dn_width, n=LANES)
            q = _dn_conv(p, dn_conv_w, j, 0, "q", m_lat, ctx_len, dn_width, hd)
            k = _dn_conv(p, dn_conv_w, j, 1, "k", m_lat, ctx_len, dn_width, hd)
            v = _dn_conv(p, dn_conv_w, j, 2, "v", m_lat, ctx_len, dn_width, hd)
            g, beta = _dn_gates(ab, dn_A_log[j].reshape(-1), dn_dt_bias[j].reshape(-1), nh)
            g_rows = g.reshape(m // CHUNK, CHUNK, 2 * nh).transpose(0, 2, 1)
            o_f, o_b = _dn_scan(q, k, v, g, g_rows, beta, batch, m_lat, nh, hd)
            y = _dn_out(o_f, o_b, p, dn_norm_w[j], dn_width, hd)
            xs = _matmul_gated_residual(y, dn_w_out, j, xs, mod, i, 2)
        else:
            w_fold = _fold_channel_dft(fn_w_in, j, FN_GROUPS)
            u = _matmul(h, w_fold, 0, BF16, 1024)
            mixed = _position_dft_real(u, 0, batch, seq, fn_width)
            if ctx_out:
                mixed = _position_dft_real(u, m_lat, batch, ctx_len, fn_width, into=mixed)
            xs = _matmul_gated_residual(mixed, fn_w_out, j, xs, mod, i, 2)

        hf, comb_t = _ffn_prep(xs, m, norm_ffn_w[i], mod, i, router_wt, router_bias)
        xs = _moe(hf, comb_t, moe_w_gate, moe_w_up, moe_w_down, i, xs, m, mod, i)

    return _final_norm(xs, m_lat, final_norm_w).reshape(batch, seq, d)
```

```python
import functools
import math

import numpy as np
import jax
import jax.numpy as jnp
from jax import lax
from jax.experimental import pallas as pl
from jax.experimental.pallas import tpu as pltpu

F32 = jnp.float32
BF16 = jnp.bfloat16
I32 = jnp.int32

GRID_W = 64
CHUNK = 64
CONV_W = 5
DN_HEADS = 16
FN_GROUPS = 4
N_GROUPS = 4
GROUP_SCORE_K = 2
TOP_K = 2
N_MOD = 6
N_MIXERS = 2
EPS = 1e-6

VMEM_LIMIT = 56 * 1024 * 1024
LANES = 128
MOE_GATHER_AHEAD = 3
SCAN_GROUP = 32


def _cparams(*sem):
    return pltpu.CompilerParams(dimension_semantics=sem, vmem_limit_bytes=VMEM_LIMIT)


def _dot(a, b):
    return jnp.dot(a, b, preferred_element_type=F32)


def _dot_nt(a, b):
    return lax.dot_general(a, b, (((1,), (1,)), ((), ())), preferred_element_type=F32)


def _dot_tn(a, b):
    return lax.dot_general(a, b, (((0,), (0,)), ((), ())), preferred_element_type=F32)


def _split2(x):
    hi = x.astype(BF16)
    lo = (x - hi.astype(F32)).astype(BF16)
    return hi, lo


def _split3(x):
    hi = x.astype(BF16)
    r = x - hi.astype(F32)
    mid = r.astype(BF16)
    lo = (r - mid.astype(F32)).astype(BF16)
    return hi, mid, lo


def _silu(x):
    return x / (1.0 + jnp.exp(-x))


def _sigmoid(x):
    return 1.0 / (1.0 + jnp.exp(-x))


def _tile(n, pref, mult=8):
    if n <= pref:
        return n
    t = (pref // mult) * mult
    while t >= mult:
        if n % t == 0:
            return t
        t -= mult
    return n


def _log2(n):
    assert n & (n - 1) == 0, n
    return n.bit_length() - 1


def _ada_kernel(c_ref, w_ref, b_ref, o_ref):
    s_hi, s_lo = _split2(_silu(c_ref[...]))
    w_hi, w_lo = _split2(w_ref[0])
    acc = _dot(s_hi, w_hi) + _dot(s_lo, w_hi) + _dot(s_hi, w_lo)
    o_ref[0] = acc + b_ref[0]


def _ada_table(cond, ada_w, ada_b):
    depth, d, n = ada_w.shape
    r = cond.shape[0]
    tn = _tile(n, 1024, LANES)
    return pl.pallas_call(
        _ada_kernel,
        grid=(depth, n // tn),
        in_specs=[
            pl.BlockSpec((r, d), lambda l, j: (0, 0)),
            pl.BlockSpec((1, d, tn), lambda l, j: (l, 0, j)),
            pl.BlockSpec((1, 1, tn), lambda l, j: (l, 0, j)),
        ],
        out_specs=pl.BlockSpec((1, r, tn), lambda l, j: (l, 0, j)),
        out_shape=jax.ShapeDtypeStruct((depth, r, n), F32),
        compiler_params=_cparams("parallel", "parallel"),
        name="ada_table",
    )(cond, ada_w, ada_b.reshape(depth, 1, n))


class _Mod:
    def __init__(self, table3, mod_rows, d, rows_per_mod):
        self.table = table3
        self.mod_rows = mod_rows
        self.d = d
        self.rows_per_mod = rows_per_mod

    def spec(self, layer, k, tm, row_axis=0, width=None, col=None):
        assert self.rows_per_mod % tm == 0
        width = self.d if width is None else width
        rows, mr = self.rows_per_mod, self.mod_rows

        def index_map(*g):
            row = (g[row_axis] * tm) // rows
            return ((layer * mr + row) * N_MOD + k, 0, 0 if col is None else col(*g))
        return pl.BlockSpec((1, 1, width), index_map)


def _modulated(x, nw, shift, scale):
    ms = jnp.mean(x * x, axis=-1, keepdims=True)
    return (x * lax.rsqrt(ms + EPS) * nw) * (1.0 + scale) + shift


def _modulate_kernel(x_ref, nw_ref, sh_ref, sc_ref, o_ref):
    o_ref[...] = _modulated(x_ref[...], nw_ref[...], sh_ref[0], sc_ref[0]).astype(o_ref.dtype)


def _modulate(x, m, nw, mod, layer, k_shift, k_scale):
    d = x.shape[1]
    tm = _tile(math.gcd(m, mod.rows_per_mod), 1024)
    return pl.pallas_call(
        _modulate_kernel,
        grid=(m // tm,),
        in_specs=[
            pl.BlockSpec((tm, d), lambda i: (i, 0)),
            pl.BlockSpec((1, d), lambda i: (0, 0)),
            mod.spec(layer, k_shift, tm),
            mod.spec(layer, k_scale, tm),
        ],
        out_specs=pl.BlockSpec((tm, d), lambda i: (i, 0)),
        out_shape=jax.ShapeDtypeStruct((m, d), BF16),
        compiler_params=_cparams("parallel"),
        name="modulate",
    )(x, nw.reshape(1, d), mod.table, mod.table)


def _final_norm_kernel(x_ref, nw_ref, o_ref):
    x = x_ref[...]
    ms = jnp.mean(x * x, axis=-1, keepdims=True)
    o_ref[...] = x * lax.rsqrt(ms + EPS) * nw_ref[...]


def _final_norm(x, m, nw):
    d = x.shape[1]
    tm = _tile(m, 1024)
    return pl.pallas_call(
        _final_norm_kernel,
        grid=(m // tm,),
        in_specs=[pl.BlockSpec((tm, d), lambda i: (i, 0)), pl.BlockSpec((1, d), lambda i: (0, 0))],
        out_specs=pl.BlockSpec((tm, d), lambda i: (i, 0)),
        out_shape=jax.ShapeDtypeStruct((m, d), F32),
        compiler_params=_cparams("parallel"),
        name="final_norm",
    )(x, nw.reshape(1, d))


def _matmul_kernel(a_ref, w_ref, o_ref, wb_ref, *, valid_cols):
    @pl.when(pl.program_id(1) == 0)
    def _():
        w = w_ref[0]
        if valid_cols < w.shape[1]:
            w = jnp.where(lax.broadcasted_iota(I32, w.shape, 1) < valid_cols, w, 0.0)
        wb_ref[...] = w.astype(BF16)

    o_ref[...] = _dot(a_ref[...], wb_ref[...]).astype(o_ref.dtype)


def _matmul(a, w, layer, out_dtype, tn_pref, col0=0, n=None):
    m, k = a.shape
    n = w.shape[2] if n is None else n
    tm, tn = _tile(m, 1024), _tile(n, tn_pref, LANES)
    assert col0 % tn == 0
    cb = col0 // tn
    valid_cols = tn if col0 + n <= w.shape[2] else w.shape[2] - col0
    assert valid_cols == tn or n == tn
    return pl.pallas_call(
        functools.partial(_matmul_kernel, valid_cols=valid_cols),
        grid=(n // tn, m // tm),
        in_specs=[pl.BlockSpec((tm, k), lambda j, i: (i, 0)),
                  pl.BlockSpec((1, k, tn), lambda j, i: (layer, 0, cb + j))],
        out_specs=pl.BlockSpec((tm, tn), lambda j, i: (i, j)),
        out_shape=jax.ShapeDtypeStruct((m, n), out_dtype),
        scratch_shapes=[pltpu.VMEM((k, tn), BF16)],
        compiler_params=_cparams("parallel", "arbitrary"),
        name="matmul",
    )(a, w)


def _matmul_res_kernel(a_ref, w_ref, x_ref, g_ref, o_ref, wb_ref):
    @pl.when(pl.program_id(1) == 0)
    def _():
        wb_ref[...] = w_ref[0].astype(BF16)

    o_ref[...] = x_ref[...] + g_ref[0] * _dot(a_ref[...], wb_ref[...])


def _matmul_gated_residual(a, w, layer_w, x, mod, layer, k_gate, tn_pref=1024):
    m, k = a.shape
    n = w.shape[2]
    tm = _tile(math.gcd(m, mod.rows_per_mod), 1024)
    tn = _tile(n, tn_pref, LANES)
    return pl.pallas_call(
        _matmul_res_kernel,
        grid=(n // tn, m // tm),
        in_specs=[
            pl.BlockSpec((tm, k), lambda j, i: (i, 0)),
            pl.BlockSpec((1, k, tn), lambda j, i: (layer_w, 0, j)),
            pl.BlockSpec((tm, tn), lambda j, i: (i, j)),
            mod.spec(layer, k_gate, tm, row_axis=1, width=tn, col=lambda j, i: j),
        ],
        out_specs=pl.BlockSpec((tm, tn), lambda j, i: (i, j)),
        out_shape=jax.ShapeDtypeStruct((m, n), F32),
        scratch_shapes=[pltpu.VMEM((k, tn), BF16)],
        compiler_params=_cparams("parallel", "arbitrary"),
        name="matmul_gated_residual",
    )(a, w, x, mod.table)


def _head_sumsq(y, ones_bd):
    sq_hi, sq_lo = _split2(y * y)
    return _dot(sq_hi, ones_bd) + _dot(sq_lo, ones_bd)


def _dn_conv_kernel(u_ref, cw_ref, e_ref, o_ref, coef_ref, *, lat_tiles, lat_len, ctx_len, mode, head_dim):
    i = pl.program_id(1)
    tm = u_ref.shape[0]
    pad = CONV_W // 2
    cw = cw_ref[0]

    @pl.when((i == 0) | (i == lat_tiles))
    def _():
        conv_len = jnp.where(i < lat_tiles, lat_len, ctx_len)
        pos = lax.broadcasted_iota(I32, (tm, 1), 0) & (conv_len - 1)
        for j in range(CONV_W):
            d = j - pad
            inside = ((pos + d >= 0) & (pos + d < conv_len)).astype(F32)
            coef_ref[j] = inside * cw[j:j + 1, :]

    u = u_ref[...].astype(F32)
    acc = u * cw[pad:pad + 1, :]
    for j in range(CONV_W):
        d = j - pad
        if d != 0:
            acc = acc + pltpu.roll(u, (-d) % tm, 0) * coef_ref[j]
    y = _silu(acc)
    if mode != "v":
        y = y * lax.rsqrt(_head_sumsq(y, e_ref[...]) + EPS)
        if mode == "q":
            y = y * (head_dim ** -0.5)
    o_ref[...] = y.astype(o_ref.dtype)


def _dn_conv(p, conv_w, layer, section, mode, m_lat, ctx_len, width, head_dim):
    m = p.shape[0]
    tc = _tile(width, 512, LANES)
    tm = _tile(math.gcd(m_lat, m - m_lat), 1024, max(GRID_W, ctx_len))
    assert m_lat % tm == 0 and m % tm == 0 and tm % GRID_W == 0 and tm % ctx_len == 0
    ones_bd = jnp.asarray(np.kron(np.eye(tc // head_dim), np.ones((head_dim, head_dim))), BF16)
    nsec = width // tc
    return pl.pallas_call(
        functools.partial(_dn_conv_kernel, lat_tiles=m_lat // tm, lat_len=GRID_W, ctx_len=ctx_len, mode=mode,
                          head_dim=head_dim),
        grid=(nsec, m // tm),
        in_specs=[
            pl.BlockSpec((tm, tc), lambda j, i: (i, section * nsec + j)),
            pl.BlockSpec((1, CONV_W, tc), lambda j, i: (layer, 0, section * nsec + j)),
            pl.BlockSpec((tc, tc), lambda j, i: (0, 0)),
        ],
        out_specs=pl.BlockSpec((tm, tc), lambda j, i: (i, j)),
        out_shape=jax.ShapeDtypeStruct((m, width), BF16),
        scratch_shapes=[pltpu.VMEM((CONV_W, tm, tc), F32)],
        compiler_params=_cparams("parallel", "arbitrary"),
        name="dn_conv_" + mode,
    )(p, conv_w, ones_bd)


def _dn_gates_kernel(ab_ref, al_ref, dt_ref, tl_ref, tu_ref, g_ref, beta_ref, *, nh):
    ab = ab_ref[:, :4 * nh]
    a = ab[:, :2 * nh] + dt_ref[...]
    softplus = jnp.maximum(a, 0.0) + jnp.log1p(jnp.exp(-jnp.abs(a)))
    la = -jnp.exp(al_ref[...]) * softplus
    beta_ref[...] = _sigmoid(ab[:, 2 * nh:])
    parts = _split3(la)
    g_pre = sum(_dot(tl_ref[...], p) for p in parts)
    g_suf = sum(_dot(tu_ref[...], p) for p in parts)
    col = lax.broadcasted_iota(I32, la.shape, 1)
    g_ref[...] = jnp.where(col < nh, g_pre, g_suf)


def _dn_gates(ab, a_log, dt_bias, nh):
    m = ab.shape[0]
    tm = _tile(m, 256, CHUNK)
    r = np.arange(tm)
    same = (r[:, None] // CHUNK) == (r[None, :] // CHUNK)
    tri_l = jnp.asarray(same & (r[None, :] <= r[:, None]), BF16)
    tri_u = jnp.asarray(same & (r[None, :] >= r[:, None]), BF16)
    return pl.pallas_call(
        functools.partial(_dn_gates_kernel, nh=nh),
        grid=(m // tm,),
        in_specs=[
            pl.BlockSpec((tm, ab.shape[1]), lambda i: (i, 0)),
            pl.BlockSpec((1, 2 * nh), lambda i: (0, 0)),
            pl.BlockSpec((1, 2 * nh), lambda i: (0, 0)),
            pl.BlockSpec((tm, tm), lambda i: (0, 0)),
            pl.BlockSpec((tm, tm), lambda i: (0, 0)),
        ],
        out_specs=[pl.BlockSpec((tm, 2 * nh), lambda i: (i, 0))] * 2,
        out_shape=[jax.ShapeDtypeStruct((m, 2 * nh), F32)] * 2,
        compiler_params=_cparams("parallel"),
        name="dn_gates",
    )(ab, a_log.reshape(1, 2 * nh), dt_bias.reshape(1, 2 * nh), tri_l, tri_u)


def _unit_tri_inverses(lms):
    shape = lms[0].shape
    ri = lax.broadcasted_iota(I32, shape, 0)
    ci = lax.broadcasted_iota(I32, shape, 1)
    eye = (ri == ci).astype(F32)
    pair = (ri >> 1) == (ci >> 1)
    ts = [eye - jnp.where(pair, lm, 0.0) for lm in lms]
    for lvl in range(1, _log2(shape[0])):
        cross = ((ri >> (lvl + 1)) == (ci >> (lvl + 1))) & ((ri >> lvl) != (ci >> lvl))
        offs = [jnp.where(cross, lm, 0.0).astype(BF16) for lm in lms]
        tbs = [t.astype(BF16) for t in ts]
        ps = [_dot(tb, off).astype(BF16) for tb, off in zip(tbs, offs)]
        ts = [t - _dot(p, tb) for t, p, tb in zip(ts, ps, tbs)]
    return ts


def _dn_scan_kernel(*refs, ns, nh, hd):
    ins, o_refs, s_refs = refs[:6 * ns], refs[6 * ns:7 * ns], refs[7 * ns:]
    q_refs, k_refs, v_refs = ins[0::6], ins[1::6], ins[2::6]

    @pl.when(pl.program_id(1) == 0)
    def _():
        for s_ref in s_refs:
            s_ref[...] = jnp.zeros_like(s_ref)

    n = CHUNK
    ri = lax.broadcasted_iota(I32, (n, n), 0)
    ci = lax.broadcasted_iota(I32, (n, n), 1)
    causal_d = (ci <= ri, ci >= ri)
    strict_d = (ci < ri, ci > ri)
    last_d = (n - 1, 0)
    gc_s = [r[...] for r in ins[3::6]]
    gr_s = [r[0] for r in ins[4::6]]
    b_s = [r[...] for r in ins[5::6]]

    def group(items):
        idx = range(len(items))
        col = [(s % 2) * nh + h for s, h in items]
        dirs = [s % 2 for s, h in items]
        gcol = [gc_s[s][:, col[j]:col[j] + 1] for j, (s, h) in enumerate(items)]
        grow = [gr_s[s][col[j]:col[j] + 1, :] for j, (s, h) in enumerate(items)]
        bcol = [b_s[s][:, col[j]:col[j] + 1] for j, (s, h) in enumerate(items)]
        glast = [gcol[j][last_d[dirs[j]]:last_d[dirs[j]] + 1, :] for j in idx]
        q = [q_refs[s][:, h * hd:(h + 1) * hd] for s, h in items]
        k = [k_refs[s][:, h * hd:(h + 1) * hd] for s, h in items]
        v = [v_refs[s][:, h * hd:(h + 1) * hd] for s, h in items]
        kq_k = [_dot_nt(jnp.concatenate([k[j], q[j]], axis=0), k[j]) for j in idx]
        decay = [jnp.exp(jnp.where(causal_d[dirs[j]], gcol[j] - grow[j], -jnp.inf)) for j in idx]
        lm = [jnp.where(strict_d[dirs[j]], bcol[j] * kq_k[j][:n] * decay[j], 0.0) for j in idx]
        intra = [(kq_k[j][n:] * decay[j]).astype(BF16) for j in idx]
        eg = [jnp.exp(g) for g in gcol]
        rhs = [jnp.concatenate([v[j].astype(F32) * bcol[j], k[j].astype(F32) * (bcol[j] * eg[j])],
                               axis=1).astype(BF16) for j in idx]
        t = _unit_tri_inverses(lm)
        uw = [_dot(t[j].astype(BF16), rhs[j]) for j in idx]
        st = [s_refs[s][h] for s, h in items]
        wq = [jnp.concatenate([uw[j][:, hd:], q[j].astype(F32) * eg[j]], axis=0).astype(BF16) for j in idx]
        ws = [_dot(wq[j], st[j].astype(BF16)) for j in idx]
        vnew = [(uw[j][:, :hd] - ws[j][:n]).astype(BF16) for j in idx]
        kg = [(k[j].astype(F32) * jnp.exp(glast[j] - gcol[j])).astype(BF16) for j in idx]
        for j, (s, h) in enumerate(items):
            o_refs[s][:, h * hd:(h + 1) * hd] = (ws[j][n:] + _dot(intra[j], vnew[j])).astype(o_refs[s].dtype)
        for j, (s, h) in enumerate(items):
            s_refs[s][h] = st[j] * jnp.exp(glast[j]) + _dot_tn(kg[j], vnew[j])

    items = [(s, h) for h in range(nh) for s in range(ns)]
    for i0 in range(0, len(items), SCAN_GROUP):
        group(items[i0:i0 + SCAN_GROUP])


def _dn_scan(q, k, v, g, g_rows, beta, batch, m_lat, nh, hd):
    m = q.shape[0]
    nl = m_lat // batch // CHUNK
    nc = (m - m_lat) // batch // CHUNK
    ctx0 = m_lat // CHUNK

    def specs(backward):
        def block(b, c):
            ctx_blk = ctx0 + b * nc + (nc - 1 - c if backward else c)
            lat_blk = b * nl + (nl - 1 - (c - nc) if backward else c - nc)
            return jnp.where(c < nc, ctx_blk, lat_blk)

        tok = lambda b, c: (block(b, c), 0)
        tok3 = lambda b, c: (block(b, c), 0, 0)
        wide = pl.BlockSpec((CHUNK, nh * hd), tok)
        gate = pl.BlockSpec((CHUNK, 2 * nh), tok)
        return [wide, wide, wide, gate, pl.BlockSpec((1, 2 * nh, CHUNK), tok3), gate], wide

    per_stream = [specs(False), specs(True)]
    return pl.pallas_call(
        functools.partial(_dn_scan_kernel, ns=2, nh=nh, hd=hd),
        grid=(batch, nc + nl),
        in_specs=[spec for ins, _ in per_stream for spec in ins],
        out_specs=[out for _, out in per_stream],
        out_shape=[jax.ShapeDtypeStruct((m, nh * hd), BF16)] * 2,
        scratch_shapes=[pltpu.VMEM((nh, hd, hd), F32)] * 2,
        compiler_params=_cparams("parallel", "arbitrary"),
        name="dn_scan",
    )(*([q, k, v, g, g_rows, beta] * 2))


def _dn_out_kernel(of_ref, ob_ref, gate_ref, nw_ref, e_ref, y_ref, *, hd):
    o = of_ref[...].astype(F32) + ob_ref[...].astype(F32)
    ms = _head_sumsq(o, e_ref[...]) * (1.0 / hd)
    y = o * lax.rsqrt(ms + EPS) * nw_ref[...]
    y_ref[...] = (y * _silu(gate_ref[...].astype(F32))).astype(y_ref.dtype)


def _dn_out(o_f, o_b, p, norm_w, width, hd):
    m = o_f.shape[0]
    tc = _tile(width, 512, LANES)
    tm = _tile(m, 1024)
    nsec = width // tc
    ones_bd = jnp.asarray(np.kron(np.eye(tc // hd), np.ones((hd, hd))), BF16)
    nw = jnp.tile(norm_w.astype(F32), tc // hd).reshape(1, tc)
    return pl.pallas_call(
        functools.partial(_dn_out_kernel, hd=hd),
        grid=(m // tm, nsec),
        in_specs=[
            pl.BlockSpec((tm, tc), lambda i, j: (i, j)),
            pl.BlockSpec((tm, tc), lambda i, j: (i, j)),
            pl.BlockSpec((tm, tc), lambda i, j: (i, 3 * nsec + j)),
            pl.BlockSpec((1, tc), lambda i, j: (0, 0)),
            pl.BlockSpec((tc, tc), lambda i, j: (0, 0)),
        ],
        out_specs=pl.BlockSpec((tm, tc), lambda i, j: (i, j)),
        out_shape=jax.ShapeDtypeStruct((m, width), BF16),
        compiler_params=_cparams("parallel", "parallel"),
        name="dn_out",
    )(o_f, o_b, p, nw, ones_bd)


def _fold_kernel(w_ref, t_ref, o_ref):
    w_hi, w_lo = _split2(w_ref[0])
    t_hi, t_lo = _split2(t_ref[0])
    o_ref[0] = (_dot(w_hi, t_hi) + _dot(w_lo, t_hi) + _dot(w_hi, t_lo)).astype(o_ref.dtype)


def _fold_channel_dft(w_in, layer, groups):
    _, d, width = w_in.shape
    c = width // groups
    idx = np.arange(c)
    ang = 2.0 * np.pi * ((idx[:, None] * idx[None, :]) % c) / c
    table = jnp.asarray(np.stack([np.cos(ang), -np.sin(ang)]) / math.sqrt(c), F32)
    return pl.pallas_call(
        _fold_kernel,
        grid=(2, groups),
        in_specs=[pl.BlockSpec((1, d, c), lambda s, g: (layer, 0, g)),
                  pl.BlockSpec((1, c, c), lambda s, g: (s, 0, 0))],
        out_specs=pl.BlockSpec((1, d, c), lambda s, g: (0, 0, s * groups + g)),
        out_shape=jax.ShapeDtypeStruct((1, d, 2 * width), BF16),
        compiler_params=_cparams("parallel", "parallel"),
        name="fold_channel_dft",
    )(w_in, table)


def _dft_kernel(zr_ref, zi_ref, f_ref, g_ref, *rest, n1, n2):
    o_ref, zf, yf, of = rest[-4:]
    slabs = range(of.shape[0])
    lanes = lambda l: slice(l * LANES, (l + 1) * LANES)
    for l in slabs:
        zf[0, l] = zr_ref[:, lanes(l)].astype(F32)
        zf[1, l] = zi_ref[:, lanes(l)].astype(F32)

    def stage1(t2, carry):
        z = jnp.concatenate(
            [jnp.concatenate([zf[part, l, pl.ds(t2, n1, stride=n2), :] for l in slabs], axis=1) for part in (0, 1)],
            axis=0)
        y = _dot(f_ref[...], z.astype(BF16))
        for l in slabs:
            yf[l, pl.ds(pl.multiple_of(t2 * 2 * n1, 2 * n1), 2 * n1), :] = y[:, lanes(l)]
        return carry

    lax.fori_loop(0, n2, stage1, 0, unroll=8)

    def stage2(f1, carry):
        z = jnp.concatenate(
            [jnp.concatenate([yf[l, pl.ds(off + f1, n2, stride=2 * n1), :] for l in slabs], axis=1)
             for off in (0, n1)], axis=0)
        r = _dot(g_ref[f1], z.astype(BF16))
        for l in slabs:
            of[l, pl.ds(f1, n2, stride=n1), :] = r[:, lanes(l)]
        return carry

    lax.fori_loop(0, n1, stage2, 0, unroll=8)
    for l in slabs:
        o_ref[:, lanes(l)] = of[l].astype(o_ref.dtype)


def _position_dft_real(u, row0, batch, t_len, width, into=None):
    n1 = 1 << (_log2(t_len) // 2)
    n2 = t_len // n1
    a1 = 2.0 * np.pi * ((np.arange(n1)[:, None] * np.arange(n1)[None, :]) % n1) / n1
    c1, s1 = np.cos(a1), np.sin(a1)
    f1m = jnp.asarray(np.block([[c1, s1], [-s1, c1]]) / math.sqrt(n1), BF16)
    f1i, f2i, t2i = np.arange(n1)[:, None, None], np.arange(n2)[None, :, None], np.arange(n2)[None, None, :]
    theta = 2.0 * np.pi * (((t2i * f1i) % t_len) / t_len + ((t2i * f2i) % n2) / n2)
    g2m = jnp.asarray(np.concatenate([np.cos(theta), np.sin(theta)], axis=2) / math.sqrt(n2), BF16)

    tc = _tile(width, 256, LANES)
    nj = width // tc
    assert row0 % t_len == 0
    b0 = row0 // t_len
    return pl.pallas_call(
        functools.partial(_dft_kernel, n1=n1, n2=n2),
        grid=(batch, nj),
        in_specs=[
            pl.BlockSpec((t_len, tc), lambda b, j: (b0 + b, j)),
            pl.BlockSpec((t_len, tc), lambda b, j: (b0 + b, nj + j)),
            pl.BlockSpec((2 * n1, 2 * n1), lambda b, j: (0, 0)),
            pl.BlockSpec((n1, n2, 2 * n2), lambda b, j: (0, 0, 0)),
        ] + ([] if into is None else [pl.BlockSpec(memory_space=pl.ANY)]),
        input_output_aliases={} if into is None else {4: 0},
        out_specs=pl.BlockSpec((t_len, tc), lambda b, j: (b0 + b, j)),
        out_shape=jax.ShapeDtypeStruct((u.shape[0], width), BF16),
        scratch_shapes=[pltpu.VMEM((2, tc // LANES, t_len, LANES), F32),
                        pltpu.VMEM((tc // LANES, n2 * 2 * n1, LANES), F32),
                        pltpu.VMEM((tc // LANES, t_len, LANES), F32)],
        compiler_params=_cparams("parallel", "parallel"),
        name="position_dft",
    )(u, u, f1m, g2m, *([] if into is None else [into]))


def _ffn_prep_kernel(x_ref, nw_ref, sh_ref, sc_ref, rw_ref, rb_ref, h_ref, comb_ref, *, n_exp):
    h = _modulated(x_ref[...], nw_ref[...], sh_ref[0], sc_ref[0])
    h_ref[...] = h.astype(h_ref.dtype)
    h_hi, h_lo = _split2(h)
    r_hi, r_lo = _split2(rw_ref[...])
    logits = _dot_nt(r_hi, h_hi) + _dot_nt(r_lo, h_hi) + _dot_nt(r_hi, h_lo)
    s = _sigmoid(logits)
    sel = s + rb_ref[...]
    srow = [s[e:e + 1, :] for e in range(n_exp)]
    row = [sel[e:e + 1, :] for e in range(n_exp)]
    epg = n_exp // N_GROUPS

    def beats(a, ia, b, ib):
        return (a >= b) if ia < ib else (a > b)

    rank, gscore = {}, []
    for g in range(N_GROUPS):
        members = range(g * epg, (g + 1) * epg)
        for i in members:
            rank[i] = sum(beats(row[j], j, row[i], i).astype(F32) for j in members if j != i)
        gscore.append(sum(jnp.where(rank[i] < GROUP_SCORE_K, row[i], 0.0) for i in members))
    picked = []
    for g in range(N_GROUPS):
        grank = sum(beats(gscore[j], j, gscore[g], g).astype(F32) for j in range(N_GROUPS) if j != g)
        for i in range(g * epg, (g + 1) * epg):
            picked.append(jnp.where((grank < 1.0) & (rank[i] < TOP_K), srow[i], 0.0))
    denom = sum(picked)
    comb_ref[...] = jnp.concatenate([p / denom for p in picked], axis=0)


def _ffn_prep(x, m, nw, mod, layer, router_wt, router_b):
    d = x.shape[1]
    n_exp = router_wt.shape[0]
    tm = _tile(math.gcd(m, mod.rows_per_mod), 512, LANES)
    return pl.pallas_call(
        functools.partial(_ffn_prep_kernel, n_exp=n_exp),
        grid=(m // tm,),
        in_specs=[
            pl.BlockSpec((tm, d), lambda i: (i, 0)),
            pl.BlockSpec((1, d), lambda i: (0, 0)),
            mod.spec(layer, 3, tm),
            mod.spec(layer, 4, tm),
            pl.BlockSpec((n_exp, d), lambda i: (0, 0)),
            pl.BlockSpec((n_exp, 1), lambda i: (0, 0)),
        ],
        out_specs=[pl.BlockSpec((tm, d), lambda i: (i, 0)), pl.BlockSpec((n_exp, tm), lambda i: (0, i))],
        out_shape=[jax.ShapeDtypeStruct((m, d), F32), jax.ShapeDtypeStruct((n_exp, m), F32)],
        compiler_params=_cparams("parallel"),
        name="ffn_prep",
    )(x, nw.reshape(1, d), mod.table, mod.table, router_wt, router_b.reshape(n_exp, 1))


def _route(comb_t, tg):
    assert TOP_K == 2
    n_exp, n = comb_t.shape
    mask = comb_t > 0.0
    cnt = jnp.sum(mask.astype(I32), axis=1)
    padded = ((cnt + tg - 1) // tg) * tg
    ends = jnp.cumsum(padded)
    dest = (ends - padded)[:, None] + jnp.cumsum(mask.astype(I32), axis=1) - 1
    eidx = jnp.arange(n_exp, dtype=I32)[:, None]
    first = jnp.min(jnp.where(mask, eidx, n_exp), axis=0)
    final = jnp.max(jnp.where(mask, eidx, -1), axis=0)
    is_a = mask & (eidx == first)
    is_b = mask & (eidx == final) & (final != first)
    pick = lambda sel, val: jnp.sum(jnp.where(sel, val, 0), axis=0)
    w_ab = jnp.stack([pick(is_a, comb_t), pick(is_b, comb_t)], axis=1)
    p_rows = TOP_K * n + n_exp * tg
    d_a = jnp.where(jnp.any(is_a, axis=0), pick(is_a, dest), p_rows).astype(I32)
    d_b = jnp.where(jnp.any(is_b, axis=0), pick(is_b, dest), p_rows).astype(I32)
    tile_start = jnp.arange(p_rows // tg, dtype=I32) * tg
    tile_exp = jnp.minimum(jnp.sum((tile_start[:, None] >= ends[None, :]).astype(I32), axis=1), n_exp - 1)
    tile_ok = (tile_start < ends[-1]).astype(I32)
    tile_new = jnp.concatenate([jnp.ones((1,), I32), (tile_exp[1:] != tile_exp[:-1]).astype(I32)])
    return tile_exp, tile_ok, tile_new, d_a, d_b, w_ab, p_rows


def _src_kernel(da_ref, db_ref, src_ref, *, n_tok):
    def zero(r, carry):
        src_ref[r] = 0
        return carry
    lax.fori_loop(0, src_ref.shape[0], zero, 0, unroll=8)

    def put(t, carry):
        src_ref[da_ref[t]] = t
        src_ref[db_ref[t]] = t
        return carry
    lax.fori_loop(0, n_tok, put, 0, unroll=8)


def _sorted_row_tokens(d_a, d_b, p_rows, spare):
    n_tok = d_a.shape[0]
    smem = pl.BlockSpec(memory_space=pltpu.SMEM)
    return pl.pallas_call(
        functools.partial(_src_kernel, n_tok=n_tok),
        in_specs=[smem, smem],
        out_specs=smem,
        out_shape=jax.ShapeDtypeStruct((p_rows + spare,), I32),
        name="sorted_row_tokens",
    )(d_a, d_b)


def _row_copies(idx_ref, base, n_rows, src_hbm, dst_buf, sem, wait):
    def body(r, carry):
        cp = pltpu.make_async_copy(src_hbm.at[pl.ds(idx_ref[base + r], 1)], dst_buf.at[pl.ds(r, 1)], sem)
        if wait:
            cp.wait()
        else:
            cp.start()
        return carry
    lax.fori_loop(0, n_rows, body, 0, unroll=8)


def _moe_expert_kernel(texp_ref, tval_ref, tnew_ref, src_ref, h_hbm, wg_ref, wu_ref, wd_ref, y_ref,
                       xbuf, sem, wgb, wub, wdb, *, tg, n_tiles):
    i = pl.program_id(0)
    nbuf = MOE_GATHER_AHEAD + 1
    bufs = [xbuf.at[s] for s in range(nbuf)]
    valid = tval_ref[i] != 0

    def requested(tile):
        return (tile < MOE_GATHER_AHEAD) | (tval_ref[jnp.maximum(tile - MOE_GATHER_AHEAD, 0)] != 0)

    @pl.when(i == 0)
    def _():
        for t in range(MOE_GATHER_AHEAD):
            _row_copies(src_ref, t * tg, tg, h_hbm, bufs[t], sem.at[t], wait=False)

    @pl.when(valid & (tnew_ref[i] != 0))
    def _():
        wgb[...] = wg_ref[0, 0].astype(BF16)
        wub[...] = wu_ref[0, 0].astype(BF16)
        wdb[...] = wd_ref[0, 0].astype(BF16)

    for slot in range(nbuf):
        @pl.when(requested(i) & (i % nbuf == slot))
        def _():
            _row_copies(src_ref, i * tg, tg, h_hbm, bufs[slot], sem.at[slot], wait=True)

        @pl.when(valid & (i % nbuf == slot))
        def _():
            ahead = (slot + MOE_GATHER_AHEAD) % nbuf

            def request(rows):
                for r in rows:
                    pltpu.make_async_copy(h_hbm.at[pl.ds(src_ref[(i + MOE_GATHER_AHEAD) * tg + r], 1)],
                                          bufs[ahead].at[pl.ds(r, 1)], sem.at[ahead]).start(priority=r % 2)

            third = tg // 3
            x = bufs[slot][...].astype(BF16)
            request(range(0, third))
            a = _dot(x, wgb[...])
            request(range(third, 2 * third))
            b = _dot(x, wub[...])
            request(range(2 * third, tg))
            act = (_silu(a) * b).astype(BF16)
            y_ref[...] = _dot(act, wdb[...])

    @pl.when(i == n_tiles - 1)
    def _():
        for t in range(n_tiles, n_tiles + MOE_GATHER_AHEAD):
            @pl.when(tval_ref[t - MOE_GATHER_AHEAD] != 0)
            def _():
                _row_copies(src_ref, t * tg, tg, h_hbm, bufs[t % nbuf], sem.at[t % nbuf], wait=True)

    @pl.when(jnp.logical_not(valid))
    def _():
        y_ref[...] = jnp.zeros_like(y_ref)


def _moe_experts(h, src, tile_exp, tile_ok, tile_new, wg, wu, wd, layer, tg, p_rows):
    n, d = h.shape
    f = wg.shape[3]
    wmap = lambda i, te, ok, new, s: (layer, te[i], 0, 0)
    return pl.pallas_call(
        functools.partial(_moe_expert_kernel, tg=tg, n_tiles=p_rows // tg),
        grid_spec=pltpu.PrefetchScalarGridSpec(
            num_scalar_prefetch=4,
            grid=(p_rows // tg,),
            in_specs=[
                pl.BlockSpec(memory_space=pl.ANY),
                pl.BlockSpec((1, 1, d, f), wmap),
                pl.BlockSpec((1, 1, d, f), wmap),
                pl.BlockSpec((1, 1, f, d), wmap),
            ],
            out_specs=pl.BlockSpec((tg, d), lambda i, te, ok, new, s: (i, 0)),
            scratch_shapes=[pltpu.VMEM((MOE_GATHER_AHEAD + 1, tg, d), F32),
                            pltpu.SemaphoreType.DMA((MOE_GATHER_AHEAD + 1,)),
                            pltpu.VMEM((d, f), BF16), pltpu.VMEM((d, f), BF16), pltpu.VMEM((f, d), BF16)],
        ),
        out_shape=jax.ShapeDtypeStruct((p_rows, d), F32),
        compiler_params=_cparams("arbitrary"),
        name="moe_experts",
    )(tile_exp, tile_ok, tile_new, src, h, wg, wu, wd)


def _moe_combine_kernel(da_ref, db_ref, y_hbm, x_ref, w_ref, g_ref, o_ref, ya, yb, sem, *, tm):
    i = pl.program_id(0)

    def fetch(tile, slot, wait):
        _row_copies(da_ref, tile * tm, tm, y_hbm, ya.at[slot], sem.at[0, slot], wait)
        _row_copies(db_ref, tile * tm, tm, y_hbm, yb.at[slot], sem.at[1, slot], wait)

    @pl.when(i == 0)
    def _():
        fetch(0, 0, False)

    @pl.when(i + 1 < pl.num_programs(0))
    def _():
        slot = (i + 1) % 2
        for r in range(tm):
            for k, (idx_ref, buf) in enumerate(((da_ref, ya), (db_ref, yb))):
                pltpu.make_async_copy(y_hbm.at[pl.ds(idx_ref[(i + 1) * tm + r], 1)],
                                      buf.at[slot, pl.ds(r, 1)], sem.at[k, slot]).start(priority=k)

    fetch(i, i % 2, True)
    w = w_ref[...]
    mix = w[:, 0:1] * ya[i % 2] + w[:, 1:2] * yb[i % 2]
    o_ref[...] = x_ref[...] + g_ref[0] * mix


def _moe_combine(y, row_a, row_b, w_ab, x, m, mod, layer):
    d = x.shape[1]
    tm = _tile(math.gcd(m, mod.rows_per_mod), 256)
    return pl.pallas_call(
        functools.partial(_moe_combine_kernel, tm=tm),
        grid_spec=pltpu.PrefetchScalarGridSpec(
            num_scalar_prefetch=2,
            grid=(m // tm,),
            in_specs=[
                pl.BlockSpec(memory_space=pl.ANY),
                pl.BlockSpec((tm, d), lambda i, a, b: (i, 0)),
                pl.BlockSpec((tm, 2), lambda i, a, b: (i, 0)),
                mod.spec(layer, 5, tm),
            ],
            out_specs=pl.BlockSpec((tm, d), lambda i, a, b: (i, 0)),
            scratch_shapes=[pltpu.VMEM((2, tm, d), F32), pltpu.VMEM((2, tm, d), F32),
                            pltpu.SemaphoreType.DMA((2, 2))],
        ),
        out_shape=jax.ShapeDtypeStruct((m, d), F32),
        compiler_params=_cparams("arbitrary"),
        name="moe_combine",
    )(row_a, row_b, y, x, w_ab, mod.table)


def _moe(h, comb_t, wg, wu, wd, layer_w, x, m, mod, layer):
    tg = 256 if m * TOP_K >= 256 * wg.shape[1] else 64
    tile_exp, tile_ok, tile_new, d_a, d_b, w_ab, p_rows = _route(comb_t, tg)
    src = _sorted_row_tokens(d_a, d_b, p_rows, MOE_GATHER_AHEAD * tg)
    y = _moe_experts(h, src, tile_exp, tile_ok, tile_new, wg, wu, wd, layer_w, tg, p_rows)
    none = lambda rows: jnp.where(rows == p_rows, 0, rows)
    return _moe_combine(y, none(d_a), none(d_b), w_ab, x, m, mod, layer)


def kernel(x, c, ctx, c_ctx, ada_w, ada_b, norm_mix_w, norm_ffn_w, dn_w_in, dn_conv_w, dn_A_log, dn_dt_bias, dn_norm_w, dn_w_out, fn_w_in, fn_w_out, router_w, router_bias, moe_w_gate, moe_w_up, moe_w_down, final_norm_w):
    batch, seq, d = x.shape
    ctx_len = ctx.shape[1]
    depth = ada_w.shape[0]
    nh = DN_HEADS
    dn_width = dn_w_out.shape[1]
    hd = dn_width // nh
    fn_width = fn_w_in.shape[2]
    m_lat, m_all = batch * seq, batch * (seq + ctx_len)
    assert batch * ctx_len <= seq and seq % ctx_len == 0

    mod_rows = -(-(batch + 1) // 8) * 8
    cond = jnp.zeros((mod_rows, d), F32).at[:batch].set(c).at[batch].set(c_ctx)
    table = _ada_table(cond, ada_w, ada_b)
    mod = _Mod(table.reshape(depth * mod_rows * N_MOD, 1, d), mod_rows, d, seq)

    router_wt = router_w.T
    xs = jnp.concatenate([x.reshape(m_lat, d), ctx.reshape(batch * ctx_len, d)], axis=0)

    for i in range(depth):
        use_dn = i % N_MIXERS == 0
        j = i // N_MIXERS
        ctx_out = i < depth - 1
        ctx_live = use_dn or ctx_out
        m = m_all if ctx_live else m_lat

        h = _modulate(xs, m, norm_mix_w[i], mod, i, 0, 1)
        if use_dn:
            p = _matmul(h, dn_w_in, j, BF16, 1024, n=4 * dn_width)
            ab = _matmul(h, dn_w_in, j, F32, LANES, col0=4 * dn_width, n=LANES)
            q = _dn_conv(p, dn_conv_w, j, 0, "q", m_lat, ctx_len, dn_width, hd)
            k = _dn_conv(p, dn_conv_w, j, 1, "k", m_lat, ctx_len, dn_width, hd)
            v = _dn_conv(p, dn_conv_w, j, 2, "v", m_lat, ctx_len, dn_width, hd)
            g, beta = _dn_gates(ab, dn_A_log[j].reshape(-1), dn_dt_bias[j].reshape(-1), nh)
            g_rows = g.reshape(m // CHUNK, CHUNK, 2 * nh).transpose(0, 2, 1)
            o_f, o_b = _dn_scan(q, k, v, g, g_rows, beta, batch, m_lat, nh, hd)
            y = _dn_out(o_f, o_b, p, dn_norm_w[j], dn_width, hd)
            xs = _matmul_gated_residual(y, dn_w_out, j, xs, mod, i, 2)
        else:
            w_fold = _fold_channel_dft(fn_w_in, j, FN_GROUPS)
            u = _matmul(h, w_fold, 0, BF16, 1024)
            mixed = _position_dft_real(u, 0, batch, seq, fn_width)
            if ctx_out:
                mixed = _position_dft_real(u, m_lat, batch, ctx_len, fn_width, into=mixed)
            xs = _matmul_gated_residual(mixed, fn_w_out, j, xs, mod, i, 2)

        hf, comb_t = _ffn_prep(xs, m, norm_ffn_w[i], mod, i, router_wt, router_bias)
        xs = _moe(hf, comb_t, moe_w_gate, moe_w_up, moe_w_down, i, xs, m, mod, i)

    return _final_norm(xs, m_lat, final_norm_w).reshape(batch, seq, d)
```

```python
import functools
import math

import numpy as np
import jax
import jax.numpy as jnp
from jax import lax
from jax.experimental import pallas as pl
from jax.experimental.pallas import tpu as pltpu

F32 = jnp.float32
BF16 = jnp.bfloat16
I32 = jnp.int32

GRID_W = 64
CHUNK = 64
CONV_W = 5
DN_HEADS = 16
FN_GROUPS = 4
N_GROUPS = 4
GROUP_SCORE_K = 2
TOP_K = 2
N_MOD = 6
N_MIXERS = 2
EPS = 1e-6

VMEM_LIMIT = 56 * 1024 * 1024
LANES = 128
MOE_GATHER_AHEAD = 3
SCAN_GROUP = 32


def _cparams(*sem):
    return pltpu.CompilerParams(dimension_semantics=sem, vmem_limit_bytes=VMEM_LIMIT)


def _dot(a, b):
    return jnp.dot(a, b, preferred_element_type=F32)


def _dot_nt(a, b):
    return lax.dot_general(a, b, (((1,), (1,)), ((), ())), preferred_element_type=F32)


def _dot_tn(a, b):
    return lax.dot_general(a, b, (((0,), (0,)), ((), ())), preferred_element_type=F32)


def _split2(x):
    hi = x.astype(BF16)
    lo = (x - hi.astype(F32)).astype(BF16)
    return hi, lo


def _split3(x):
    hi = x.astype(BF16)
    r = x - hi.astype(F32)
    mid = r.astype(BF16)
    lo = (r - mid.astype(F32)).astype(BF16)
    return hi, mid, lo


def _silu(x):
    return x / (1.0 + jnp.exp(-x))


def _sigmoid(x):
    return 1.0 / (1.0 + jnp.exp(-x))


def _tile(n, pref, mult=8):
    if n <= pref:
        return n
    t = (pref // mult) * mult
    while t >= mult:
        if n % t == 0:
            return t
        t -= mult
    return n


def _log2(n):
    assert n & (n - 1) == 0, n
    return n.bit_length() - 1


def _ada_kernel(c_ref, w_ref, b_ref, o_ref):
    s_hi, s_lo = _split2(_silu(c_ref[...]))
    w_hi, w_lo = _split2(w_ref[0])
    acc = _dot(s_hi, w_hi) + _dot(s_lo, w_hi) + _dot(s_hi, w_lo)
    o_ref[0] = acc + b_ref[0]


def _ada_table(cond, ada_w, ada_b):
    depth, d, n = ada_w.shape
    r = cond.shape[0]
    tn = _tile(n, 1024, LANES)
    return pl.pallas_call(
        _ada_kernel,
        grid=(depth, n // tn),
        in_specs=[
            pl.BlockSpec((r, d), lambda l, j: (0, 0)),
            pl.BlockSpec((1, d, tn), lambda l, j: (l, 0, j)),
            pl.BlockSpec((1, 1, tn), lambda l, j: (l, 0, j)),
        ],
        out_specs=pl.BlockSpec((1, r, tn), lambda l, j: (l, 0, j)),
        out_shape=jax.ShapeDtypeStruct((depth, r, n), F32),
        compiler_params=_cparams("parallel", "parallel"),
        name="ada_table",
    )(cond, ada_w, ada_b.reshape(depth, 1, n))


class _Mod:
    def __init__(self, table3, mod_rows, d, rows_per_mod):
        self.table = table3
        self.mod_rows = mod_rows
        self.d = d
        self.rows_per_mod = rows_per_mod

    def spec(self, layer, k, tm, row_axis=0, width=None, col=None):
        assert self.rows_per_mod % tm == 0
        width = self.d if width is None else width
        rows, mr = self.rows_per_mod, self.mod_rows

        def index_map(*g):
            row = (g[row_axis] * tm) // rows
            return ((layer * mr + row) * N_MOD + k, 0, 0 if col is None else col(*g))
        return pl.BlockSpec((1, 1, width), index_map)


def _modulated(x, nw, shift, scale):
    ms = jnp.mean(x * x, axis=-1, keepdims=True)
    return (x * lax.rsqrt(ms + EPS) * nw) * (1.0 + scale) + shift


def _modulate_kernel(x_ref, nw_ref, sh_ref, sc_ref, o_ref):
    o_ref[...] = _modulated(x_ref[...], nw_ref[...], sh_ref[0], sc_ref[0]).astype(o_ref.dtype)


def _modulate(x, m, nw, mod, layer, k_shift, k_scale):
    d = x.shape[1]
    tm = _tile(math.gcd(m, mod.rows_per_mod), 1024)
    return pl.pallas_call(
        _modulate_kernel,
        grid=(m // tm,),
        in_specs=[
            pl.BlockSpec((tm, d), lambda i: (i, 0)),
            pl.BlockSpec((1, d), lambda i: (0, 0)),
            mod.spec(layer, k_shift, tm),
            mod.spec(layer, k_scale, tm),
        ],
        out_specs=pl.BlockSpec((tm, d), lambda i: (i, 0)),
        out_shape=jax.ShapeDtypeStruct((m, d), BF16),
        compiler_params=_cparams("parallel"),
        name="modulate",
    )(x, nw.reshape(1, d), mod.table, mod.table)


def _final_norm_kernel(x_ref, nw_ref, o_ref):
    x = x_ref[...]
    ms = jnp.mean(x * x, axis=-1, keepdims=True)
    o_ref[...] = x * lax.rsqrt(ms + EPS) * nw_ref[...]


def _final_norm(x, m, nw):
    d = x.shape[1]
    tm = _tile(m, 1024)
    return pl.pallas_call(
        _final_norm_kernel,
        grid=(m // tm,),
        in_specs=[pl.BlockSpec((tm, d), lambda i: (i, 0)), pl.BlockSpec((1, d), lambda i: (0, 0))],
        out_specs=pl.BlockSpec((tm, d), lambda i: (i, 0)),
        out_shape=jax.ShapeDtypeStruct((m, d), F32),
        compiler_params=_cparams("parallel"),
        name="final_norm",
    )(x, nw.reshape(1, d))


def _matmul_kernel(a_ref, w_ref, o_ref, wb_ref, *, valid_cols):
    @pl.when(pl.program_id(1) == 0)
    def _():
        w = w_ref[0]
        if valid_cols < w.shape[1]:
            w = jnp.where(lax.broadcasted_iota(I32, w.shape, 1) < valid_cols, w, 0.0)
        wb_ref[...] = w.astype(BF16)

    o_ref[...] = _dot(a_ref[...], wb_ref[...]).astype(o_ref.dtype)


def _matmul(a, w, layer, out_dtype, tn_pref, col0=0, n=None):
    m, k = a.shape
    n = w.shape[2] if n is None else n
    tm, tn = _tile(m, 1024), _tile(n, tn_pref, LANES)
    assert col0 % tn == 0
    cb = col0 // tn
    valid_cols = tn if col0 + n <= w.shape[2] else w.shape[2] - col0
    assert valid_cols == tn or n == tn
    return pl.pallas_call(
        functools.partial(_matmul_kernel, valid_cols=valid_cols),
        grid=(n // tn, m // tm),
        in_specs=[pl.BlockSpec((tm, k), lambda j, i: (i, 0)),
                  pl.BlockSpec((1, k, tn), lambda j, i: (layer, 0, cb + j))],
        out_specs=pl.BlockSpec((tm, tn), lambda j, i: (i, j)),
        out_shape=jax.ShapeDtypeStruct((m, n), out_dtype),
        scratch_shapes=[pltpu.VMEM((k, tn), BF16)],
        compiler_params=_cparams("parallel", "arbitrary"),
        name="matmul",
    )(a, w)


def _matmul_res_kernel(a_ref, w_ref, x_ref, g_ref, o_ref, wb_ref):
    @pl.when(pl.program_id(1) == 0)
    def _():
        wb_ref[...] = w_ref[0].astype(BF16)

    o_ref[...] = x_ref[...] + g_ref[0] * _dot(a_ref[...], wb_ref[...])


def _matmul_gated_residual(a, w, layer_w, x, mod, layer, k_gate, tn_pref=1024):
    m, k = a.shape
    n = w.shape[2]
    tm = _tile(math.gcd(m, mod.rows_per_mod), 1024)
    tn = _tile(n, tn_pref, LANES)
    return pl.pallas_call(
        _matmul_res_kernel,
        grid=(n // tn, m // tm),
        in_specs=[
            pl.BlockSpec((tm, k), lambda j, i: (i, 0)),
            pl.BlockSpec((1, k, tn), lambda j, i: (layer_w, 0, j)),
            pl.BlockSpec((tm, tn), lambda j, i: (i, j)),
            mod.spec(layer, k_gate, tm, row_axis=1, width=tn, col=lambda j, i: j),
        ],
        out_specs=pl.BlockSpec((tm, tn), lambda j, i: (i, j)),
        out_shape=jax.ShapeDtypeStruct((m, n), F32),
        scratch_shapes=[pltpu.VMEM((k, tn), BF16)],
        compiler_params=_cparams("parallel", "arbitrary"),
        name="matmul_gated_residual",
    )(a, w, x, mod.table)


def _head_sumsq(y, ones_bd):
    sq_hi, sq_lo = _split2(y * y)
    return _dot(sq_hi, ones_bd) + _dot(sq_lo, ones_bd)


def _dn_conv_kernel(u_ref, cw_ref, e_ref, o_ref, coef_ref, *, lat_tiles, lat_len, ctx_len, mode, head_dim):
    i = pl.program_id(1)
    tm = u_ref.shape[0]
    pad = CONV_W // 2
    cw = cw_ref[0]

    @pl.when((i == 0) | (i == lat_tiles))
    def _():
        conv_len = jnp.where(i < lat_tiles, lat_len, ctx_len)
        pos = lax.broadcasted_iota(I32, (tm, 1), 0) & (conv_len - 1)
        for j in range(CONV_W):
            d = j - pad
            inside = ((pos + d >= 0) & (pos + d < conv_len)).astype(F32)
            coef_ref[j] = inside * cw[j:j + 1, :]

    u = u_ref[...].astype(F32)
    acc = u * cw[pad:pad + 1, :]
    for j in range(CONV_W):
        d = j - pad
        if d != 0:
            acc = acc + pltpu.roll(u, (-d) % tm, 0) * coef_ref[j]
    y = _silu(acc)
    if mode != "v":
        y = y * lax.rsqrt(_head_sumsq(y, e_ref[...]) + EPS)
        if mode == "q":
            y = y * (head_dim ** -0.5)
    o_ref[...] = y.astype(o_ref.dtype)


def _dn_conv(p, conv_w, layer, section, mode, m_lat, ctx_len, width, head_dim):
    m = p.shape[0]
    tc = _tile(width, 512, LANES)
    tm = _tile(math.gcd(m_lat, m - m_lat), 1024, max(GRID_W, ctx_len))
    assert m_lat % tm == 0 and m % tm == 0 and tm % GRID_W == 0 and tm % ctx_len == 0
    ones_bd = jnp.asarray(np.kron(np.eye(tc // head_dim), np.ones((head_dim, head_dim))), BF16)
    nsec = width // tc
    return pl.pallas_call(
        functools.partial(_dn_conv_kernel, lat_tiles=m_lat // tm, lat_len=GRID_W, ctx_len=ctx_len, mode=mode,
                          head_dim=head_dim),
        grid=(nsec, m // tm),
        in_specs=[
            pl.BlockSpec((tm, tc), lambda j, i: (i, section * nsec + j)),
            pl.BlockSpec((1, CONV_W, tc), lambda j, i: (layer, 0, section * nsec + j)),
            pl.BlockSpec((tc, tc), lambda j, i: (0, 0)),
        ],
        out_specs=pl.BlockSpec((tm, tc), lambda j, i: (i, j)),
        out_shape=jax.ShapeDtypeStruct((m, width), BF16),
        scratch_shapes=[pltpu.VMEM((CONV_W, tm, tc), F32)],
        compiler_params=_cparams("parallel", "arbitrary"),
        name="dn_conv_" + mode,
    )(p, conv_w, ones_bd)


def _dn_gates_kernel(ab_ref, al_ref, dt_ref, tl_ref, tu_ref, g_ref, beta_ref, *, nh):
    ab = ab_ref[:, :4 * nh]
    a = ab[:, :2 * nh] + dt_ref[...]
    softplus = jnp.maximum(a, 0.0) + jnp.log1p(jnp.exp(-jnp.abs(a)))
    la = -jnp.exp(al_ref[...]) * softplus
    beta_ref[...] = _sigmoid(ab[:, 2 * nh:])
    parts = _split3(la)
    g_pre = sum(_dot(tl_ref[...], p) for p in parts)
    g_suf = sum(_dot(tu_ref[...], p) for p in parts)
    col = lax.broadcasted_iota(I32, la.shape, 1)
    g_ref[...] = jnp.where(col < nh, g_pre, g_suf)


def _dn_gates(ab, a_log, dt_bias, nh):
    m = ab.shape[0]
    tm = _tile(m, 256, CHUNK)
    r = np.arange(tm)
    same = (r[:, None] // CHUNK) == (r[None, :] // CHUNK)
    tri_l = jnp.asarray(same & (r[None, :] <= r[:, None]), BF16)
    tri_u = jnp.asarray(same & (r[None, :] >= r[:, None]), BF16)
    return pl.pallas_call(
        functools.partial(_dn_gates_kernel, nh=nh),
        grid=(m // tm,),
        in_specs=[
            pl.BlockSpec((tm, ab.shape[1]), lambda i: (i, 0)),
            pl.BlockSpec((1, 2 * nh), lambda i: (0, 0)),
            pl.BlockSpec((1, 2 * nh), lambda i: (0, 0)),
            pl.BlockSpec((tm, tm), lambda i: (0, 0)),
            pl.BlockSpec((tm, tm), lambda i: (0, 0)),
        ],
        out_specs=[pl.BlockSpec((tm, 2 * nh), lambda i: (i, 0))] * 2,
        out_shape=[jax.ShapeDtypeStruct((m, 2 * nh), F32)] * 2,
        compiler_params=_cparams("parallel"),
        name="dn_gates",
    )(ab, a_log.reshape(1, 2 * nh), dt_bias.reshape(1, 2 * nh), tri_l, tri_u)


def _unit_tri_inverses(lms):
    shape = lms[0].shape
    ri = lax.broadcasted_iota(I32, shape, 0)
    ci = lax.broadcasted_iota(I32, shape, 1)
    eye = (ri == ci).astype(F32)
    pair = (ri >> 1) == (ci >> 1)
    ts = [eye - jnp.where(pair, lm, 0.0) for lm in lms]
    for lvl in range(1, _log2(shape[0])):
        cross = ((ri >> (lvl + 1)) == (ci >> (lvl + 1))) & ((ri >> lvl) != (ci >> lvl))
        offs = [jnp.where(cross, lm, 0.0).astype(BF16) for lm in lms]
        tbs = [t.astype(BF16) for t in ts]
        ps = [_dot(tb, off).astype(BF16) for tb, off in zip(tbs, offs)]
        ts = [t - _dot(p, tb) for t, p, tb in zip(ts, ps, tbs)]
    return ts


def _dn_scan_kernel(*refs, ns, nh, hd):
    ins, o_refs, s_refs = refs[:6 * ns], refs[6 * ns:7 * ns], refs[7 * ns:]
    q_refs, k_refs, v_refs = ins[0::6], ins[1::6], ins[2::6]

    @pl.when(pl.program_id(1) == 0)
    def _():
        for s_ref in s_refs:
            s_ref[...] = jnp.zeros_like(s_ref)

    n = CHUNK
    ri = lax.broadcasted_iota(I32, (n, n), 0)
    ci = lax.broadcasted_iota(I32, (n, n), 1)
    causal_d = (ci <= ri, ci >= ri)
    strict_d = (ci < ri, ci > ri)
    last_d = (n - 1, 0)
    gc_s = [r[...] for r in ins[3::6]]
    gr_s = [r[0] for r in ins[4::6]]
    b_s = [r[...] for r in ins[5::6]]

    def group(items):
        idx = range(len(items))
        col = [(s % 2) * nh + h for s, h in items]
        dirs = [s % 2 for s, h in items]
        gcol = [gc_s[s][:, col[j]:col[j] + 1] for j, (s, h) in enumerate(items)]
        grow = [gr_s[s][col[j]:col[j] + 1, :] for j, (s, h) in enumerate(items)]
        bcol = [b_s[s][:, col[j]:col[j] + 1] for j, (s, h) in enumerate(items)]
        glast = [gcol[j][last_d[dirs[j]]:last_d[dirs[j]] + 1, :] for j in idx]
        q = [q_refs[s][:, h * hd:(h + 1) * hd] for s, h in items]
        k = [k_refs[s][:, h * hd:(h + 1) * hd] for s, h in items]
        v = [v_refs[s][:, h * hd:(h + 1) * hd] for s, h in items]
        kq_k = [_dot_nt(jnp.concatenate([k[j], q[j]], axis=0), k[j]) for j in idx]
        decay = [jnp.exp(jnp.where(causal_d[dirs[j]], gcol[j] - grow[j], -jnp.inf)) for j in idx]
        lm = [jnp.where(strict_d[dirs[j]], bcol[j] * kq_k[j][:n] * decay[j], 0.0) for j in idx]
        intra = [(kq_k[j][n:] * decay[j]).astype(BF16) for j in idx]
        eg = [jnp.exp(g) for g in gcol]
        rhs = [jnp.concatenate([v[j].astype(F32) * bcol[j], k[j].astype(F32) * (bcol[j] * eg[j])],
                               axis=1).astype(BF16) for j in idx]
        t = _unit_tri_inverses(lm)
        uw = [_dot(t[j].astype(BF16), rhs[j]) for j in idx]
        st = [s_refs[s][h] for s, h in items]
        wq = [jnp.concatenate([uw[j][:, hd:], q[j].astype(F32) * eg[j]], axis=0).astype(BF16) for j in idx]
        ws = [_dot(wq[j], st[j].astype(BF16)) for j in idx]
        vnew = [(uw[j][:, :hd] - ws[j][:n]).astype(BF16) for j in idx]
        kg = [(k[j].astype(F32) * jnp.exp(glast[j] - gcol[j])).astype(BF16) for j in idx]
        for j, (s, h) in enumerate(items):
            o_refs[s][:, h * hd:(h + 1) * hd] = (ws[j][n:] + _dot(intra[j], vnew[j])).astype(o_refs[s].dtype)
        for j, (s, h) in enumerate(items):
            s_refs[s][h] = st[j] * jnp.exp(glast[j]) + _dot_tn(kg[j], vnew[j])

    items = [(s, h) for h in range(nh) for s in range(ns)]
    for i0 in range(0, len(items), SCAN_GROUP):
        group(items[i0:i0 + SCAN_GROUP])


def _dn_scan(q, k, v, g, g_rows, beta, batch, m_lat, nh, hd):
    m = q.shape[0]
    nl = m_lat // batch // CHUNK
    nc = (m - m_lat) // batch // CHUNK
    ctx0 = m_lat // CHUNK

    def specs(backward):
        def block(b, c):
            ctx_blk = ctx0 + b * nc + (nc - 1 - c if backward else c)
            lat_blk = b * nl + (nl - 1 - (c - nc) if backward else c - nc)
            return jnp.where(c < nc, ctx_blk, lat_blk)

        tok = lambda b, c: (block(b, c), 0)
        tok3 = lambda b, c: (block(b, c), 0, 0)
        wide = pl.BlockSpec((CHUNK, nh * hd), tok)
        gate = pl.BlockSpec((CHUNK, 2 * nh), tok)
        return [wide, wide, wide, gate, pl.BlockSpec((1, 2 * nh, CHUNK), tok3), gate], wide

    per_stream = [specs(False), specs(True)]
    return pl.pallas_call(
        functools.partial(_dn_scan_kernel, ns=2, nh=nh, hd=hd),
        grid=(batch, nc + nl),
        in_specs=[spec for ins, _ in per_stream for spec in ins],
        out_specs=[out for _, out in per_stream],
        out_shape=[jax.ShapeDtypeStruct((m, nh * hd), BF16)] * 2,
        scratch_shapes=[pltpu.VMEM((nh, hd, hd), F32)] * 2,
        compiler_params=_cparams("parallel", "arbitrary"),
        name="dn_scan",
    )(*([q, k, v, g, g_rows, beta] * 2))


def _dn_out_kernel(of_ref, ob_ref, gate_ref, nw_ref, e_ref, y_ref, *, hd):
    o = of_ref[...].astype(F32) + ob_ref[...].astype(F32)
    ms = _head_sumsq(o, e_ref[...]) * (1.0 / hd)
    y = o * lax.rsqrt(ms + EPS) * nw_ref[...]
    y_ref[...] = (y * _silu(gate_ref[...].astype(F32))).astype(y_ref.dtype)


def _dn_out(o_f, o_b, p, norm_w, width, hd):
    m = o_f.shape[0]
    tc = _tile(width, 512, LANES)
    tm = _tile(m, 1024)
    nsec = width // tc
    ones_bd = jnp.asarray(np.kron(np.eye(tc // hd), np.ones((hd, hd))), BF16)
    nw = jnp.tile(norm_w.astype(F32), tc // hd).reshape(1, tc)
    return pl.pallas_call(
        functools.partial(_dn_out_kernel, hd=hd),
        grid=(m // tm, nsec),
        in_specs=[
            pl.BlockSpec((tm, tc), lambda i, j: (i, j)),
            pl.BlockSpec((tm, tc), lambda i, j: (i, j)),
            pl.BlockSpec((tm, tc), lambda i, j: (i, 3 * nsec + j)),
            pl.BlockSpec((1, tc), lambda i, j: (0, 0)),
            pl.BlockSpec((tc, tc), lambda i, j: (0, 0)),
        ],
        out_specs=pl.BlockSpec((tm, tc), lambda i, j: (i, j)),
        out_shape=jax.ShapeDtypeStruct((m, width), BF16),
        compiler_params=_cparams("parallel", "parallel"),
        name="dn_out",
    )(o_f, o_b, p, nw, ones_bd)


def _fold_kernel(w_ref, t_ref, o_ref):
    w_hi, w_lo = _split2(w_ref[0])
    t_hi, t_lo = _split2(t_ref[0])
    o_ref[0] = (_dot(w_hi, t_hi) + _dot(w_lo, t_hi) + _dot(w_hi, t_lo)).astype(o_ref.dtype)


def _fold_channel_dft(w_in, layer, groups):
    _, d, width = w_in.shape
    c = width // groups
    idx = np.arange(c)
    ang = 2.0 * np.pi * ((idx[:, None] * idx[None, :]) % c) / c
    table = jnp.asarray(np.stack([np.cos(ang), -np.sin(ang)]) / math.sqrt(c), F32)
    return pl.pallas_call(
        _fold_kernel,
        grid=(2, groups),
        in_specs=[pl.BlockSpec((1, d, c), lambda s, g: (layer, 0, g)),
                  pl.BlockSpec((1, c, c), lambda s, g: (s, 0, 0))],
        out_specs=pl.BlockSpec((1, d, c), lambda s, g: (0, 0, s * groups + g)),
        out_shape=jax.ShapeDtypeStruct((1, d, 2 * width), BF16),
        compiler_params=_cparams("parallel", "parallel"),
        name="fold_channel_dft",
    )(w_in, table)


def _dft_kernel(zr_ref, zi_ref, f_ref, g_ref, *rest, n1, n2):
    o_ref, zf, yf, of = rest[-4:]
    slabs = range(of.shape[0])
    lanes = lambda l: slice(l * LANES, (l + 1) * LANES)
    for l in slabs:
        zf[0, l] = zr_ref[:, lanes(l)].astype(F32)
        zf[1, l] = zi_ref[:, lanes(l)].astype(F32)

    def stage1(t2, carry):
        z = jnp.concatenate(
            [jnp.concatenate([zf[part, l, pl.ds(t2, n1, stride=n2), :] for l in slabs], axis=1) for part in (0, 1)],
            axis=0)
        y = _dot(f_ref[...], z.astype(BF16))
        for l in slabs:
            yf[l, pl.ds(pl.multiple_of(t2 * 2 * n1, 2 * n1), 2 * n1), :] = y[:, lanes(l)]
        return carry

    lax.fori_loop(0, n2, stage1, 0, unroll=8)

    def stage2(f1, carry):
        z = jnp.concatenate(
            [jnp.concatenate([yf[l, pl.ds(off + f1, n2, stride=2 * n1), :] for l in slabs], axis=1)
             for off in (0, n1)], axis=0)
        r = _dot(g_ref[f1], z.astype(BF16))
        for l in slabs:
            of[l, pl.ds(f1, n2, stride=n1), :] = r[:, lanes(l)]
        return carry

    lax.fori_loop(0, n1, stage2, 0, unroll=8)
    for l in slabs:
        o_ref[:, lanes(l)] = of[l].astype(o_ref.dtype)


def _position_dft_real(u, row0, batch, t_len, width, into=None):
    n1 = 1 << (_log2(t_len) // 2)
    n2 = t_len // n1
    a1 = 2.0 * np.pi * ((np.arange(n1)[:, None] * np.arange(n1)[None, :]) % n1) / n1
    c1, s1 = np.cos(a1), np.sin(a1)
    f1m = jnp.asarray(np.block([[c1, s1], [-s1, c1]]) / math.sqrt(n1), BF16)
    f1i, f2i, t2i = np.arange(n1)[:, None, None], np.arange(n2)[None, :, None], np.arange(n2)[None, None, :]
    theta = 2.0 * np.pi * (((t2i * f1i) % t_len) / t_len + ((t2i * f2i) % n2) / n2)
    g2m = jnp.asarray(np.concatenate([np.cos(theta), np.sin(theta)], axis=2) / math.sqrt(n2), BF16)

    tc = _tile(width, 256, LANES)
    nj = width // tc
    assert row0 % t_len == 0
    b0 = row0 // t_len
    return pl.pallas_call(
        functools.partial(_dft_kernel, n1=n1, n2=n2),
        grid=(batch, nj),
        in_specs=[
            pl.BlockSpec((t_len, tc), lambda b, j: (b0 + b, j)),
            pl.BlockSpec((t_len, tc), lambda b, j: (b0 + b, nj + j)),
            pl.BlockSpec((2 * n1, 2 * n1), lambda b, j: (0, 0)),
            pl.BlockSpec((n1, n2, 2 * n2), lambda b, j: (0, 0, 0)),
        ] + ([] if into is None else [pl.BlockSpec(memory_space=pl.ANY)]),
        input_output_aliases={} if into is None else {4: 0},
        out_specs=pl.BlockSpec((t_len, tc), lambda b, j: (b0 + b, j)),
        out_shape=jax.ShapeDtypeStruct((u.shape[0], width), BF16),
        scratch_shapes=[pltpu.VMEM((2, tc // LANES, t_len, LANES), F32),
                        pltpu.VMEM((tc // LANES, n2 * 2 * n1, LANES), F32),
                        pltpu.VMEM((tc // LANES, t_len, LANES), F32)],
        compiler_params=_cparams("parallel", "parallel"),
        name="position_dft",
    )(u, u, f1m, g2m, *([] if into is None else [into]))


def _ffn_prep_kernel(x_ref, nw_ref, sh_ref, sc_ref, rw_ref, rb_ref, h_ref, comb_ref, *, n_exp):
    h = _modulated(x_ref[...], nw_ref[...], sh_ref[0], sc_ref[0])
    h_ref[...] = h.astype(h_ref.dtype)
    h_hi, h_lo = _split2(h)
    r_hi, r_lo = _split2(rw_ref[...])
    logits = _dot_nt(r_hi, h_hi) + _dot_nt(r_lo, h_hi) + _dot_nt(r_hi, h_lo)
    s = _sigmoid(logits)
    sel = s + rb_ref[...]
    srow = [s[e:e + 1, :] for e in range(n_exp)]
    row = [sel[e:e + 1, :] for e in range(n_exp)]
    epg = n_exp // N_GROUPS

    def beats(a, ia, b, ib):
        return (a >= b) if ia < ib else (a > b)

    rank, gscore = {}, []
    for g in range(N_GROUPS):
        members = range(g * epg, (g + 1) * epg)
        for i in members:
            rank[i] = sum(beats(row[j], j, row[i], i).astype(F32) for j in members if j != i)
        gscore.append(sum(jnp.where(rank[i] < GROUP_SCORE_K, row[i], 0.0) for i in members))
    picked = []
    for g in range(N_GROUPS):
        grank = sum(beats(gscore[j], j, gscore[g], g).astype(F32) for j in range(N_GROUPS) if j != g)
        for i in range(g * epg, (g + 1) * epg):
            picked.append(jnp.where((grank < 1.0) & (rank[i] < TOP_K), srow[i], 0.0))
    denom = sum(picked)
    comb_ref[...] = jnp.concatenate([p / denom for p in picked], axis=0)


def _ffn_prep(x, m, nw, mod, layer, router_wt, router_b):
    d = x.shape[1]
    n_exp = router_wt.shape[0]
    tm = _tile(math.gcd(m, mod.rows_per_mod), 512, LANES)
    return pl.pallas_call(
        functools.partial(_ffn_prep_kernel, n_exp=n_exp),
        grid=(m // tm,),
        in_specs=[
            pl.BlockSpec((tm, d), lambda i: (i, 0)),
            pl.BlockSpec((1, d), lambda i: (0, 0)),
            mod.spec(layer, 3, tm),
            mod.spec(layer, 4, tm),
            pl.BlockSpec((n_exp, d), lambda i: (0, 0)),
            pl.BlockSpec((n_exp, 1), lambda i: (0, 0)),
        ],
        out_specs=[pl.BlockSpec((tm, d), lambda i: (i, 0)), pl.BlockSpec((n_exp, tm), lambda i: (0, i))],
        out_shape=[jax.ShapeDtypeStruct((m, d), F32), jax.ShapeDtypeStruct((n_exp, m), F32)],
        compiler_params=_cparams("parallel"),
        name="ffn_prep",
    )(x, nw.reshape(1, d), mod.table, mod.table, router_wt, router_b.reshape(n_exp, 1))


def _route(comb_t, tg):
    assert TOP_K == 2
    n_exp, n = comb_t.shape
    mask = comb_t > 0.0
    cnt = jnp.sum(mask.astype(I32), axis=1)
    padded = ((cnt + tg - 1) // tg) * tg
    ends = jnp.cumsum(padded)
    dest = (ends - padded)[:, None] + jnp.cumsum(mask.astype(I32), axis=1) - 1
    eidx = jnp.arange(n_exp, dtype=I32)[:, None]
    first = jnp.min(jnp.where(mask, eidx, n_exp), axis=0)
    final = jnp.max(jnp.where(mask, eidx, -1), axis=0)
    is_a = mask & (eidx == first)
    is_b = mask & (eidx == final) & (final != first)
    pick = lambda sel, val: jnp.sum(jnp.where(sel, val, 0), axis=0)
    w_ab = jnp.stack([pick(is_a, comb_t), pick(is_b, comb_t)], axis=1)
    p_rows = TOP_K * n + n_exp * tg
    d_a = jnp.where(jnp.any(is_a, axis=0), pick(is_a, dest), p_rows).astype(I32)
    d_b = jnp.where(jnp.any(is_b, axis=0), pick(is_b, dest), p_rows).astype(I32)
    tile_start = jnp.arange(p_rows // tg, dtype=I32) * tg
    tile_exp = jnp.minimum(jnp.sum((tile_start[:, None] >= ends[None, :]).astype(I32), axis=1), n_exp - 1)
    tile_ok = (tile_start < ends[-1]).astype(I32)
    tile_new = jnp.concatenate([jnp.ones((1,), I32), (tile_exp[1:] != tile_exp[:-1]).astype(I32)])
    return tile_exp, tile_ok, tile_new, d_a, d_b, w_ab, p_rows


def _src_kernel(da_ref, db_ref, src_ref, *, n_tok):
    def zero(r, carry):
        src_ref[r] = 0
        return carry
    lax.fori_loop(0, src_ref.shape[0], zero, 0, unroll=8)

    def put(t, carry):
        src_ref[da_ref[t]] = t
        src_ref[db_ref[t]] = t
        return carry
    lax.fori_loop(0, n_tok, put, 0, unroll=8)


def _sorted_row_tokens(d_a, d_b, p_rows, spare):
    n_tok = d_a.shape[0]
    smem = pl.BlockSpec(memory_space=pltpu.SMEM)
    return pl.pallas_call(
        functools.partial(_src_kernel, n_tok=n_tok),
        in_specs=[smem, smem],
        out_specs=smem,
        out_shape=jax.ShapeDtypeStruct((p_rows + spare,), I32),
        name="sorted_row_tokens",
    )(d_a, d_b)


def _row_copies(idx_ref, base, n_rows, src_hbm, dst_buf, sem, wait):
    def body(r, carry):
        cp = pltpu.make_async_copy(src_hbm.at[pl.ds(idx_ref[base + r], 1)], dst_buf.at[pl.ds(r, 1)], sem)
        if wait:
            cp.wait()
        else:
            cp.start()
        return carry
    lax.fori_loop(0, n_rows, body, 0, unroll=8)


def _moe_expert_kernel(texp_ref, tval_ref, tnew_ref, src_ref, h_hbm, wg_ref, wu_ref, wd_ref, y_ref,
                       xbuf, sem, wgb, wub, wdb, *, tg, n_tiles):
    i = pl.program_id(0)
    nbuf = MOE_GATHER_AHEAD + 1
    bufs = [xbuf.at[s] for s in range(nbuf)]
    valid = tval_ref[i] != 0

    def requested(tile):
        return (tile < MOE_GATHER_AHEAD) | (tval_ref[jnp.maximum(tile - MOE_GATHER_AHEAD, 0)] != 0)

    @pl.when(i == 0)
    def _():
        for t in range(MOE_GATHER_AHEAD):
            _row_copies(src_ref, t * tg, tg, h_hbm, bufs[t], sem.at[t], wait=False)

    @pl.when(valid & (tnew_ref[i] != 0))
    def _():
        wgb[...] = wg_ref[0, 0].astype(BF16)
        wub[...] = wu_ref[0, 0].astype(BF16)
        wdb[...] = wd_ref[0, 0].astype(BF16)

    for slot in range(nbuf):
        @pl.when(requested(i) & (i % nbuf == slot))
        def _():
            _row_copies(src_ref, i * tg, tg, h_hbm, bufs[slot], sem.at[slot], wait=True)

        @pl.when(valid & (i % nbuf == slot))
        def _():
            ahead = (slot + MOE_GATHER_AHEAD) % nbuf

            def request(rows):
                for r in rows:
                    pltpu.make_async_copy(h_hbm.at[pl.ds(src_ref[(i + MOE_GATHER_AHEAD) * tg + r], 1)],
                                          bufs[ahead].at[pl.ds(r, 1)], sem.at[ahead]).start(priority=r % 2)

            third = tg // 3
            x = bufs[slot][...].astype(BF16)
            request(range(0, third))
            a = _dot(x, wgb[...])
            request(range(third, 2 * third))
            b = _dot(x, wub[...])
            request(range(2 * third, tg))
            act = (_silu(a) * b).astype(BF16)
            y_ref[...] = _dot(act, wdb[...])

    @pl.when(i == n_tiles - 1)
    def _():
        for t in range(n_tiles, n_tiles + MOE_GATHER_AHEAD):
            @pl.when(tval_ref[t - MOE_GATHER_AHEAD] != 0)
            def _():
                _row_copies(src_ref, t * tg, tg, h_hbm, bufs[t % nbuf], sem.at[t % nbuf], wait=True)

    @pl.when(jnp.logical_not(valid))
    def _():
        y_ref[...] = jnp.zeros_like(y_ref)


def _moe_experts(h, src, tile_exp, tile_ok, tile_new, wg, wu, wd, layer, tg, p_rows):
    n, d = h.shape
    f = wg.shape[3]
    wmap = lambda i, te, ok, new, s: (layer, te[i], 0, 0)
    return pl.pallas_call(
        functools.partial(_moe_expert_kernel, tg=tg, n_tiles=p_rows // tg),
        grid_spec=pltpu.PrefetchScalarGridSpec(
            num_scalar_prefetch=4,
            grid=(p_rows // tg,),
            in_specs=[
                pl.BlockSpec(memory_space=pl.ANY),
                pl.BlockSpec((1, 1, d, f), wmap),
                pl.BlockSpec((1, 1, d, f), wmap),
                pl.BlockSpec((1, 1, f, d), wmap),
            ],
            out_specs=pl.BlockSpec((tg, d), lambda i, te, ok, new, s: (i, 0)),
            scratch_shapes=[pltpu.VMEM((MOE_GATHER_AHEAD + 1, tg, d), F32),
                            pltpu.SemaphoreType.DMA((MOE_GATHER_AHEAD + 1,)),
                            pltpu.VMEM((d, f), BF16), pltpu.VMEM((d, f), BF16), pltpu.VMEM((f, d), BF16)],
        ),
        out_shape=jax.ShapeDtypeStruct((p_rows, d), F32),
        compiler_params=_cparams("arbitrary"),
        name="moe_experts",
    )(tile_exp, tile_ok, tile_new, src, h, wg, wu, wd)


def _moe_combine_kernel(da_ref, db_ref, y_hbm, x_ref, w_ref, g_ref, nw_ref, o_ref, ya, yb, sem, *, tm, final_norm):
    i = pl.program_id(0)

    def fetch(tile, slot, wait):
        _row_copies(da_ref, tile * tm, tm, y_hbm, ya.at[slot], sem.at[0, slot], wait)
        _row_copies(db_ref, tile * tm, tm, y_hbm, yb.at[slot], sem.at[1, slot], wait)

    @pl.when(i == 0)
    def _():
        fetch(0, 0, False)

    @pl.when(i + 1 < pl.num_programs(0))
    def _():
        slot = (i + 1) % 2
        for r in range(tm):
            for k, (idx_ref, buf) in enumerate(((da_ref, ya), (db_ref, yb))):
                pltpu.make_async_copy(y_hbm.at[pl.ds(idx_ref[(i + 1) * tm + r], 1)],
                                      buf.at[slot, pl.ds(r, 1)], sem.at[k, slot]).start(priority=k)

    fetch(i, i % 2, True)
    w = w_ref[...]
    mix = w[:, 0:1] * ya[i % 2] + w[:, 1:2] * yb[i % 2]
    res = x_ref[...] + g_ref[0] * mix
    if final_norm:
        ms = jnp.mean(res * res, axis=-1, keepdims=True)
        res = res * lax.rsqrt(ms + EPS) * nw_ref[...]
    o_ref[...] = res


def _moe_combine(y, row_a, row_b, w_ab, x, m, mod, layer, final_nw, final_norm):
    d = x.shape[1]
    tm = _tile(math.gcd(m, mod.rows_per_mod), 256)
    return pl.pallas_call(
        functools.partial(_moe_combine_kernel, tm=tm, final_norm=final_norm),
        grid_spec=pltpu.PrefetchScalarGridSpec(
            num_scalar_prefetch=2,
            grid=(m // tm,),
            in_specs=[
                pl.BlockSpec(memory_space=pl.ANY),
                pl.BlockSpec((tm, d), lambda i, a, b: (i, 0)),
                pl.BlockSpec((tm, 2), lambda i, a, b: (i, 0)),
                mod.spec(layer, 5, tm),
                pl.BlockSpec((1, d), lambda i, a, b: (0, 0)),
            ],
            out_specs=pl.BlockSpec((tm, d), lambda i, a, b: (i, 0)),
            scratch_shapes=[pltpu.VMEM((2, tm, d), F32), pltpu.VMEM((2, tm, d), F32),
                            pltpu.SemaphoreType.DMA((2, 2))],
        ),
        out_shape=jax.ShapeDtypeStruct((m, d), F32),
        compiler_params=_cparams("arbitrary"),
        name="moe_combine",
    )(row_a, row_b, y, x, w_ab, mod.table, final_nw.reshape(1, d))


def _moe(h, comb_t, wg, wu, wd, layer_w, x, m, mod, layer, final_nw, final_norm):
    tg = 256 if m * TOP_K >= 256 * wg.shape[1] else 64
    tile_exp, tile_ok, tile_new, d_a, d_b, w_ab, p_rows = _route(comb_t, tg)
    src = _sorted_row_tokens(d_a, d_b, p_rows, MOE_GATHER_AHEAD * tg)
    y = _moe_experts(h, src, tile_exp, tile_ok, tile_new, wg, wu, wd, layer_w, tg, p_rows)
    none = lambda rows: jnp.where(rows == p_rows, 0, rows)
    return _moe_combine(y, none(d_a), none(d_b), w_ab, x, m, mod, layer, final_nw, final_norm)


def kernel(x, c, ctx, c_ctx, ada_w, ada_b, norm_mix_w, norm_ffn_w, dn_w_in, dn_conv_w, dn_A_log, dn_dt_bias, dn_norm_w, dn_w_out, fn_w_in, fn_w_out, router_w, router_bias, moe_w_gate, moe_w_up, moe_w_down, final_norm_w):
    batch, seq, d = x.shape
    ctx_len = ctx.shape[1]
    depth = ada_w.shape[0]
    nh = DN_HEADS
    dn_width = dn_w_out.shape[1]
    hd = dn_width // nh
    fn_width = fn_w_in.shape[2]
    m_lat, m_all = batch * seq, batch * (seq + ctx_len)
    assert batch * ctx_len <= seq and seq % ctx_len == 0

    mod_rows = -(-(batch + 1) // 8) * 8
    cond = jnp.zeros((mod_rows, d), F32).at[:batch].set(c).at[batch].set(c_ctx)
    table = _ada_table(cond, ada_w, ada_b)
    mod = _Mod(table.reshape(depth * mod_rows * N_MOD, 1, d), mod_rows, d, seq)

    router_wt = router_w.T
    xs = jnp.concatenate([x.reshape(m_lat, d), ctx.reshape(batch * ctx_len, d)], axis=0)

    for i in range(depth):
        use_dn = i % N_MIXERS == 0
        j = i // N_MIXERS
        ctx_out = i < depth - 1
        ctx_live = use_dn or ctx_out
        m = m_all if ctx_live else m_lat

        h = _modulate(xs, m, norm_mix_w[i], mod, i, 0, 1)
        if use_dn:
            p = _matmul(h, dn_w_in, j, BF16, 1024, n=4 * dn_width)
            ab = _matmul(h, dn_w_in, j, F32, LANES, col0=4 * dn_width, n=LANES)
            q = _dn_conv(p, dn_conv_w, j, 0, "q", m_lat, ctx_len, dn_width, hd)
            k = _dn_conv(p, dn_conv_w, j, 1, "k", m_lat, ctx_len, dn_width, hd)
            v = _dn_conv(p, dn_conv_w, j, 2, "v", m_lat, ctx_len, dn_width, hd)
            g, beta = _dn_gates(ab, dn_A_log[j].reshape(-1), dn_dt_bias[j].reshape(-1), nh)
            g_rows = g.reshape(m // CHUNK, CHUNK, 2 * nh).transpose(0, 2, 1)
            o_f, o_b = _dn_scan(q, k, v, g, g_rows, beta, batch, m_lat, nh, hd)
            y = _dn_out(o_f, o_b, p, dn_norm_w[j], dn_width, hd)
            xs = _matmul_gated_residual(y, dn_w_out, j, xs, mod, i, 2)
        else:
            w_fold = _fold_channel_dft(fn_w_in, j, FN_GROUPS)
            u = _matmul(h, w_fold, 0, BF16, 1024)
            mixed = _position_dft_real(u, 0, batch, seq, fn_width)
            if ctx_out:
                mixed = _position_dft_real(u, m_lat, batch, ctx_len, fn_width, into=mixed)
            xs = _matmul_gated_residual(mixed, fn_w_out, j, xs, mod, i, 2)

        hf, comb_t = _ffn_prep(xs, m, norm_ffn_w[i], mod, i, router_wt, router_bias)
        last = i == depth - 1 and m == m_lat
        xs = _moe(hf, comb_t, moe_w_gate, moe_w_up, moe_w_down, i, xs, m, mod, i, final_norm_w, last)

    if not last:
        xs = _final_norm(xs, m_lat, final_norm_w)
    return xs.reshape(batch, seq, d)
```
